```python
import jax, jax.numpy as jnp
from jax import lax
import numpy as np

D_MODEL = 2048
BATCH = 1
SEQ = 8192
DEPTH = 1
DEC_BATCH = 8
DEC_SEQ = 64
PAST_LEN = 4096

CHUNK = 64
WINDOW = 128
N_BAND = WINDOW // CHUNK
HEAD_DIM = 64
N_Q_HEADS = 16
N_KV_HEADS = 4
GROUP = N_Q_HEADS // N_KV_HEADS
ATTN_WIDTH = N_Q_HEADS * HEAD_DIM
KV_WIDTH = N_KV_HEADS * HEAD_DIM
CONV_CH = 1024
CONV_WIDTH = 31
OFF_K = ATTN_WIDTH
OFF_V = OFF_K + KV_WIDTH
OFF_GLU = OFF_V + KV_WIDTH
OFF_GA = OFF_GLU + 2 * CONV_CH
OFF_GC = OFF_GA + D_MODEL
IN_WIDTH = OFF_GC + D_MODEL
N_GROUPS = 8
EXPERTS_PER_GROUP = 4
N_EXPERTS = N_GROUPS * EXPERTS_PER_GROUP
TOP_K = 2
D_EXPERT = 512
ROW_BLOCK = 128
EPS = 1e-6
NEG = -1e30

kernel_name = "hybrid_swa_sink_conformer_hmoe_step"


def _rmsnorm(x, g):
    xf = x.astype(jnp.float32)
    y = xf * lax.rsqrt(jnp.mean(xf * xf, axis=-1, keepdims=True) + EPS)
    return (y * g.astype(jnp.float32)).astype(x.dtype)


def _layernorm(x, g, b):
    xf = x.astype(jnp.float32)
    mu = jnp.mean(xf, axis=-1, keepdims=True)
    var = jnp.mean(jnp.square(xf - mu), axis=-1, keepdims=True)
    y = (xf - mu) * lax.rsqrt(var + EPS)
    return (y * g.astype(jnp.float32) + b.astype(jnp.float32)).astype(x.dtype)


def _split_proj(xn, w_in, b_in):
    p = xn @ w_in + b_in
    lead = p.shape[:-1]
    q = p[..., :OFF_K].reshape(lead + (N_KV_HEADS, GROUP, HEAD_DIM))
    k = p[..., OFF_K:OFF_V].reshape(lead + (N_KV_HEADS, HEAD_DIM))
    v = p[..., OFF_V:OFF_GLU].reshape(lead + (N_KV_HEADS, HEAD_DIM))
    u = p[..., OFF_GLU:OFF_GLU + CONV_CH] * jax.nn.sigmoid(p[..., OFF_GLU + CONV_CH:OFF_GA])
    g_attn = jax.nn.sigmoid(p[..., OFF_GA:OFF_GC])
    g_conv = jax.nn.sigmoid(p[..., OFF_GC:])
    return q, k, v, u, g_attn, g_conv


def _sink_attend(q, k, v, sinks, mask):
    s = jnp.einsum('...qhgd,...khd->...hgqk', q, k).astype(jnp.float32) * (HEAD_DIM ** -0.5)
    if mask is not None:
        s = jnp.where(mask, s, NEG)
    sink = sinks.astype(jnp.float32).reshape(N_KV_HEADS, GROUP)[:, :, None, None]
    m = jnp.maximum(jnp.max(s, axis=-1, keepdims=True), sink)
    p = jnp.exp(s - m)
    denom = jnp.sum(p, axis=-1, keepdims=True) + jnp.exp(sink - m)
    o = jnp.einsum('...hgqk,...khd->...qhgd', p / denom, v.astype(jnp.float32))
    return o.astype(q.dtype)


def _attn_prompt(q, k, v, sinks):
    b, s = q.shape[:2]
    nc = s // CHUNK
    qb = q.reshape(b, nc, CHUNK, N_KV_HEADS, GROUP, HEAD_DIM)
    pad = jnp.zeros((b, WINDOW, N_KV_HEADS, HEAD_DIM), k.dtype)
    kb = jnp.concatenate([pad, k], axis=1).reshape(b, nc + N_BAND, CHUNK, N_KV_HEADS, HEAD_DIM)
    vb = jnp.concatenate([pad, v], axis=1).reshape(b, nc + N_BAND, CHUNK, N_KV_HEADS, HEAD_DIM)
    kband = jnp.concatenate([kb[:, j:j + nc] for j in range(N_BAND + 1)], axis=2)
    vband = jnp.concatenate([vb[:, j:j + nc] for j in range(N_BAND + 1)], axis=2)
    key_pos = (jnp.arange(nc)[:, None] * CHUNK - WINDOW
               + jnp.arange((N_BAND + 1) * CHUNK)[None, :])
    mask = (key_pos >= 0)[None, :, None, None, None, :]
    o = _sink_attend(qb, kband, vband, sinks, mask)
    return o.reshape(b, s, ATTN_WIDTH), k[:, -WINDOW:], v[:, -WINDOW:]


def _attn_sample(q, k, v, cache_k, cache_v, sinks):
    b, l = q.shape[:2]
    kf = jnp.concatenate([cache_k.astype(k.dtype), k], axis=1)
    vf = jnp.concatenate([cache_v.astype(v.dtype), v], axis=1)
    o = _sink_attend(q, kf, vf, sinks, None)
    return o.reshape(b, l, ATTN_WIDTH), kf[:, -WINDOW:], vf[:, -WINDOW:]


def _conv_branch(u, hist, conv_dw, conv_dw_b, ln_g, ln_b, w_conv_o):
    full = jnp.concatenate([hist.astype(u.dtype), u], axis=1)
    y = lax.conv_general_dilated(full, conv_dw[:, None, :].astype(u.dtype), (1,), 'VALID',
                                 dimension_numbers=('NWC', 'WIO', 'NWC'),
                                 feature_group_count=CONV_CH) + conv_dw_b
    y = jax.nn.silu(_layernorm(y, ln_g, ln_b))
    return y @ w_conv_o, full[:, -(CONV_WIDTH - 1):]


def _hier_moe(xn, w_rg, b_rg, w_re, b_re, w_g, w_u, w_d):
    lead = xn.shape[:-1]
    xt = xn.reshape(-1, D_MODEL)
    t = xt.shape[0]
    gl = (xt @ w_rg + b_rg).astype(jnp.float32)
    gp = jax.nn.softmax(gl, axis=-1)
    grp = jnp.argmax(gl, axis=-1)
    p_grp = jnp.take_along_axis(gp, grp[:, None], axis=1)[:, 0]
    el = (xt @ w_re + b_re).astype(jnp.float32).reshape(t, N_GROUPS, EXPERTS_PER_GROUP)
    el = jnp.take_along_axis(el, grp[:, None, None], axis=1)[:, 0]
    top_p, top_i = lax.top_k(jax.nn.softmax(el, axis=-1), TOP_K)
    wts = p_grp[:, None] * top_p / jnp.sum(top_p, axis=-1, keepdims=True)
    eid = grp[:, None] * EXPERTS_PER_GROUP + top_i
    n = t * TOP_K
    eid_f = eid.reshape(-1).astype(jnp.int32)
    wts_f = wts.reshape(-1)
    tok_f = jnp.arange(n, dtype=jnp.int32) // TOP_K
    order = jnp.argsort(eid_f)
    se = eid_f[order]
    counts = jnp.zeros((N_EXPERTS,), jnp.int32).at[eid_f].add(1)
    padded = (counts + ROW_BLOCK - 1) // ROW_BLOCK * ROW_BLOCK
    start = jnp.cumsum(counts) - counts
    pend = jnp.cumsum(padded)
    pstart = pend - padded
    dest = pstart[se] + jnp.arange(n, dtype=jnp.int32) - start[se]
    n_rows = -(-(n + N_EXPERTS * (ROW_BLOCK - 1)) // ROW_BLOCK) * ROW_BLOCK
    n_blk = n_rows // ROW_BLOCK
    row_tok = jnp.zeros((n_rows,), jnp.int32).at[dest].set(tok_f[order])
    row_w = jnp.zeros((n_rows,), jnp.float32).at[dest].set(wts_f[order])
    blk_e = jnp.minimum(jnp.searchsorted(pend, jnp.arange(n_blk, dtype=jnp.int32) * ROW_BLOCK,
                                         side='right'), N_EXPERTS - 1)

    def run_block(args):
        tb, e = args
        xb = xt[tb]
        h = jax.nn.silu(xb @ w_g[e]) * (xb @ w_u[e])
        return h @ w_d[e]

    out = lax.map(run_block, (row_tok.reshape(n_blk, ROW_BLOCK), blk_e))
    y = jnp.zeros((t, D_MODEL), jnp.float32).at[row_tok].add(
        out.reshape(n_rows, D_MODEL).astype(jnp.float32) * row_w[:, None])
    return y.astype(xn.dtype).reshape(lead + (D_MODEL,))


def _layer(x, cache_k, cache_v, conv_hist, norm1_g, w_in, b_in, attn_sinks, w_attn_o,
           conv_dw, conv_dw_b, conv_ln_g, conv_ln_b, w_conv_o, w_out, norm2_g,
           w_router_group, b_router_group, w_router_expert, b_router_expert,
           w_e_gate, w_e_up, w_e_down):
    xn = _rmsnorm(x, norm1_g)
    q, k, v, u, g_attn, g_conv = _split_proj(xn, w_in, b_in)
    if cache_k is None:
        a, new_k, new_v = _attn_prompt(q, k, v, attn_sinks)
        hist = jnp.zeros((x.shape[0], CONV_WIDTH - 1, CONV_CH), u.dtype)
    else:
        a, new_k, new_v = _attn_sample(q, k, v, cache_k, cache_v, attn_sinks)
        hist = conv_hist
    c, new_conv = _conv_branch(u, hist, conv_dw, conv_dw_b, conv_ln_g, conv_ln_b, w_conv_o)
    x = x + (g_attn * (a @ w_attn_o) + g_conv * c) @ w_out
    x = x + _hier_moe(_rmsnorm(x, norm2_g), w_router_group, b_router_group,
                      w_router_expert, b_router_expert, w_e_gate, w_e_up, w_e_down)
    return x, new_k, new_v, new_conv


def setup_inputs(seed: int = 0) -> dict:
    key = jax.random.key(seed)
    ks = jax.random.split(key, 32)
    f = jnp.float32

    def nrm(k, shape, scale):
        return jax.random.normal(k, shape, f) * scale

    return {
        "x_prompt": nrm(ks[0], (BATCH, SEQ, D_MODEL), 1.0),
        "x_sample": nrm(ks[1], (DEC_BATCH, DEC_SEQ, D_MODEL), 1.0),
        "cache_k": nrm(ks[2], (DEPTH, DEC_BATCH, WINDOW, N_KV_HEADS, HEAD_DIM), 1.0),
        "cache_v": nrm(ks[3], (DEPTH, DEC_BATCH, WINDOW, N_KV_HEADS, HEAD_DIM), 1.0),
        "state_conv": nrm(ks[4], (DEPTH, DEC_BATCH, CONV_WIDTH - 1, CONV_CH), 0.5),
        "norm1_g": 1.0 + nrm(ks[5], (DEPTH, D_MODEL), 0.01),
        "w_in": nrm(ks[6], (DEPTH, D_MODEL, IN_WIDTH), D_MODEL ** -0.5),
        "b_in": nrm(ks[7], (DEPTH, IN_WIDTH), 0.01),
        "attn_sinks": nrm(ks[8], (DEPTH, N_Q_HEADS), 1.0),
        "w_attn_o": nrm(ks[9], (DEPTH, ATTN_WIDTH, D_MODEL), ATTN_WIDTH ** -0.5),
        "conv_dw": nrm(ks[10], (DEPTH, CONV_WIDTH, CONV_CH), CONV_WIDTH ** -0.5),
        "conv_dw_b": nrm(ks[11], (DEPTH, CONV_CH), 0.01),
        "conv_ln_g": 1.0 + nrm(ks[12], (DEPTH, CONV_CH), 0.01),
        "conv_ln_b": nrm(ks[13], (DEPTH, CONV_CH), 0.01),
        "w_conv_o": nrm(ks[14], (DEPTH, CONV_CH, D_MODEL), CONV_CH ** -0.5),
        "w_out": nrm(ks[15], (DEPTH, D_MODEL, D_MODEL), D_MODEL ** -0.5),
        "norm2_g": 1.0 + nrm(ks[16], (DEPTH, D_MODEL), 0.01),
        "w_router_group": nrm(ks[17], (DEPTH, D_MODEL, N_GROUPS), D_MODEL ** -0.5),
        "b_router_group": nrm(ks[18], (DEPTH, N_GROUPS), 0.01),
        "w_router_expert": nrm(ks[19], (DEPTH, D_MODEL, N_EXPERTS), D_MODEL ** -0.5),
        "b_router_expert": nrm(ks[20], (DEPTH, N_EXPERTS), 0.01),
        "w_e_gate": nrm(ks[21], (DEPTH, N_EXPERTS, D_MODEL, D_EXPERT), D_MODEL ** -0.5),
        "w_e_up": nrm(ks[22], (DEPTH, N_EXPERTS, D_MODEL, D_EXPERT), D_MODEL ** -0.5),
        "w_e_down": nrm(ks[23], (DEPTH, N_EXPERTS, D_EXPERT, D_MODEL), D_EXPERT ** -0.5),
        "final_norm_g": 1.0 + nrm(ks[24], (D_MODEL,), 0.01),
    }


def reference(x_prompt, x_sample, cache_k, cache_v, state_conv, norm1_g, w_in, b_in,
              attn_sinks, w_attn_o, conv_dw, conv_dw_b, conv_ln_g, conv_ln_b, w_conv_o,
              w_out, norm2_g, w_router_group, b_router_group, w_router_expert,
              b_router_expert, w_e_gate, w_e_up, w_e_down, final_norm_g):
    xp, xs = x_prompt, x_sample
    kp_l, vp_l, cp_l, ks_l, vs_l, cs_l = [], [], [], [], [], []
    for l in range(DEPTH):
        lw = (norm1_g[l], w_in[l], b_in[l], attn_sinks[l], w_attn_o[l], conv_dw[l],
              conv_dw_b[l], conv_ln_g[l], conv_ln_b[l], w_conv_o[l], w_out[l], norm2_g[l],
              w_router_group[l], b_router_group[l], w_router_expert[l], b_router_expert[l],
              w_e_gate[l], w_e_up[l], w_e_down[l])
        xp, kp, vp, cp = _layer(xp, None, None, None, *lw)
        xs, ks, vs, cs = _layer(xs, cache_k[l], cache_v[l], state_conv[l], *lw)
        kp_l.append(kp); vp_l.append(vp); cp_l.append(cp)
        ks_l.append(ks); vs_l.append(vs); cs_l.append(cs)
    y_prompt = _rmsnorm(xp, final_norm_g)
    y_sample = _rmsnorm(xs, final_norm_g)
    new_k_prompt = jnp.stack(kp_l)
    new_v_prompt = jnp.stack(vp_l)
    new_conv_prompt = jnp.stack(cp_l)
    new_k_sample = jnp.stack(ks_l)
    new_v_sample = jnp.stack(vs_l)
    new_conv_sample = jnp.stack(cs_l)
    return (y_prompt, y_sample, new_k_prompt, new_v_prompt, new_conv_prompt,
            new_k_sample, new_v_sample, new_conv_sample)
```

```python
import functools

import jax
import jax.numpy as jnp
from jax import lax
from jax.experimental import pallas as pl
from jax.experimental.pallas import tpu as pltpu

F32 = jnp.float32
BF16 = jnp.bfloat16

D_MODEL = 2048
T_P = 8192
N_STREAMS = 8
L_S = 64
T_S = N_STREAMS * L_S
T = T_P + T_S
CHUNK = 64
WINDOW = 128
HEAD_DIM = 64
N_Q_HEADS = 16
N_KV_HEADS = 4
GROUP = N_Q_HEADS // N_KV_HEADS
ATTN_WIDTH = N_Q_HEADS * HEAD_DIM
KV_WIDTH = N_KV_HEADS * HEAD_DIM
CONV_CH = 1024
CONV_WIDTH = 31
HIST = CONV_WIDTH - 1
HIST_PAD = 32
IN_WIDTH = ATTN_WIDTH + 2 * KV_WIDTH + 2 * CONV_CH + 2 * D_MODEL
N_GROUPS = 8
EXPERTS_PER_GROUP = 4
N_EXPERTS = N_GROUPS * EXPERTS_PER_GROUP
TOP_K = 2
D_EXPERT = 512
N_ROUTER = N_GROUPS + N_EXPERTS
ROUTER_PAD = 128
EPS = 1e-6
NEG = -1e30

VMEM_LIMIT = 56 * 1024 * 1024

TM_IN, TN_IN = 512, 512
N_TILES_IN = IN_WIDTH // TN_IN
TQ_ATTN = 256
TR_CONV = 256
SUB_CONV = 32
TM_MIX = 256
BM_MOE = 256
TM_OUT = 256
N_SLOTS = T * TOP_K
N_BLK_MOE = -(-(N_SLOTS + N_EXPERTS * (BM_MOE - 1)) // BM_MOE)


def _sigmoid(x):
    return 1.0 / (1.0 + jnp.exp(-x))


def _params(n_axes):
    return pltpu.CompilerParams(dimension_semantics=("arbitrary",) * n_axes,
                                vmem_limit_bytes=VMEM_LIMIT)


def _in_proj_body(x_ref, g_ref, w_ref, b_ref, q_ref, kv_ref, u_ref, gate_ref, xn_sc, a_sc):
    n = pl.program_id(1)

    @pl.when(n == 0)
    def _():
        x = x_ref[...]
        ms = jnp.mean(x * x, axis=-1, keepdims=True)
        xn_sc[...] = (x * lax.rsqrt(ms + EPS) * g_ref[...]).astype(BF16)

    acc = jnp.dot(xn_sc[...], w_ref[...], preferred_element_type=F32) + b_ref[...]

    @pl.when(n < 2)
    def _():
        q_ref[...] = (acc * (HEAD_DIM ** -0.5)).astype(BF16)

    @pl.when(n == 2)
    def _():
        kv_ref[...] = acc

    @pl.when((n >= 3) & (n < 5))
    def _():
        a_sc[n - 3] = acc

    @pl.when((n >= 5) & (n < 7))
    def _():
        u_ref[...] = a_sc[n - 5] * _sigmoid(acc)

    @pl.when(n >= 7)
    def _():
        gate_ref[...] = _sigmoid(acc).astype(BF16)


def _in_proj(x, norm_g, w_bf16, b):
    grid = (T // TM_IN, N_TILES_IN)
    return pl.pallas_call(
        _in_proj_body,
        name="in_proj",
        grid=grid,
        in_specs=[
            pl.BlockSpec((TM_IN, D_MODEL), lambda m, n: (m, 0)),
            pl.BlockSpec((1, D_MODEL), lambda m, n: (0, 0)),
            pl.BlockSpec((D_MODEL, TN_IN), lambda m, n: (0, n)),
            pl.BlockSpec((1, TN_IN), lambda m, n: (0, n)),
        ],
        out_specs=[
            pl.BlockSpec((TM_IN, TN_IN), lambda m, n: (m, jnp.minimum(n, 1))),
            pl.BlockSpec((TM_IN, 2 * KV_WIDTH), lambda m, n: (m, 0)),
            pl.BlockSpec((TM_IN, TN_IN), lambda m, n: (m, jnp.clip(n - 5, 0, 1))),
            pl.BlockSpec((TM_IN, TN_IN), lambda m, n: (m, jnp.clip(n - 7, 0, 7))),
        ],
        out_shape=[
            jax.ShapeDtypeStruct((T, ATTN_WIDTH), BF16),
            jax.ShapeDtypeStruct((T, 2 * KV_WIDTH), F32),
            jax.ShapeDtypeStruct((T, CONV_CH), F32),
            jax.ShapeDtypeStruct((T, 2 * D_MODEL), BF16),
        ],
        scratch_shapes=[
            pltpu.VMEM((TM_IN, D_MODEL), BF16),
            pltpu.VMEM((2, TM_IN, TN_IN), F32),
        ],
        compiler_params=_params(2),
    )(x, norm_g, w_bf16, b)


def _attend_chunk(qc, kw, vw, sink_ref, key_pos0, a_ref, row0):
    n_keys = kw.shape[0]
    for h in range(N_KV_HEADS):
        kh = kw[:, h * HEAD_DIM:(h + 1) * HEAD_DIM]
        vh = vw[:, h * HEAD_DIM:(h + 1) * HEAD_DIM]
        qg = jnp.concatenate(
            [qc[:, (h * GROUP + g) * HEAD_DIM:(h * GROUP + g + 1) * HEAD_DIM] for g in range(GROUP)],
            axis=0)
        s = lax.dot_general(qg, kh, (((1,), (1,)), ((), ())), preferred_element_type=F32)
        if key_pos0 is not None:
            col = lax.broadcasted_iota(jnp.int32, (GROUP * CHUNK, n_keys), 1)
            s = jnp.where(col + key_pos0 >= 0, s, NEG)
        sink = jnp.concatenate(
            [jnp.full((CHUNK, 1), sink_ref[h * GROUP + g], F32) for g in range(GROUP)], axis=0)
        m = jnp.maximum(jnp.max(s, axis=1, keepdims=True), sink)
        p = jnp.exp(s - m)
        denom = jnp.sum(p, axis=1, keepdims=True) + jnp.exp(sink - m)
        o = jnp.dot(p.astype(BF16), vh, preferred_element_type=F32) / denom
        for g in range(GROUP):
            c0 = (h * GROUP + g) * HEAD_DIM
            a_ref[row0:row0 + CHUNK, c0:c0 + HEAD_DIM] = o[g * CHUNK:(g + 1) * CHUNK].astype(BF16)


def _attn_prompt_body(sink_ref, q_ref, kvp_ref, kvc_ref, a_ref):
    i = pl.program_id(0)
    kv = jnp.concatenate([kvp_ref[...], kvc_ref[...]], axis=0).astype(BF16)
    for c in range(TQ_ATTN // CHUNK):
        kw = kv[c * CHUNK:c * CHUNK + WINDOW + CHUNK, :KV_WIDTH]
        vw = kv[c * CHUNK:c * CHUNK + WINDOW + CHUNK, KV_WIDTH:]
        _attend_chunk(q_ref[c * CHUNK:(c + 1) * CHUNK, :], kw, vw, sink_ref,
                      i * TQ_ATTN + c * CHUNK - WINDOW, a_ref, c * CHUNK)


def _attn_prompt(sinks, q, kv):
    r = TQ_ATTN // WINDOW
    return pl.pallas_call(
        _attn_prompt_body,
        name="attn_prompt",
        grid=(T_P // TQ_ATTN,),
        in_specs=[
            pl.BlockSpec(memory_space=pltpu.SMEM),
            pl.BlockSpec((TQ_ATTN, ATTN_WIDTH), lambda i: (i, 0)),
            pl.BlockSpec((WINDOW, 2 * KV_WIDTH), lambda i: (jnp.maximum(i * r - 1, 0), 0)),
            pl.BlockSpec((TQ_ATTN, 2 * KV_WIDTH), lambda i: (i, 0)),
        ],
        out_specs=pl.BlockSpec((TQ_ATTN, ATTN_WIDTH), lambda i: (i, 0)),
        out_shape=jax.ShapeDtypeStruct((T_P, ATTN_WIDTH), BF16),
        compiler_params=_params(1),
    )(sinks, q, kv, kv)


def _attn_sample_body(sink_ref, q_ref, kvc_ref, ck_ref, cv_ref, a_ref):
    kvc = kvc_ref[...]
    kw = jnp.concatenate([ck_ref[0], kvc[:, :KV_WIDTH]], axis=0).astype(BF16)
    vw = jnp.concatenate([cv_ref[0], kvc[:, KV_WIDTH:]], axis=0).astype(BF16)
    _attend_chunk(q_ref[...], kw, vw, sink_ref, None, a_ref, 0)


def _attn_sample(sinks, q, kv, cache_k, cache_v):
    first = T_P // L_S
    return pl.pallas_call(
        _attn_sample_body,
        name="attn_sample",
        grid=(N_STREAMS,),
        in_specs=[
            pl.BlockSpec(memory_space=pltpu.SMEM),
            pl.BlockSpec((L_S, ATTN_WIDTH), lambda b: (first + b, 0)),
            pl.BlockSpec((L_S, 2 * KV_WIDTH), lambda b: (first + b, 0)),
            pl.BlockSpec((1, WINDOW, KV_WIDTH), lambda b: (b, 0, 0)),
            pl.BlockSpec((1, WINDOW, KV_WIDTH), lambda b: (b, 0, 0)),
        ],
        out_specs=pl.BlockSpec((L_S, ATTN_WIDTH), lambda b: (b, 0)),
        out_shape=jax.ShapeDtypeStruct((T_S, ATTN_WIDTH), BF16),
        compiler_params=_params(1),
    )(sinks, q, kv, cache_k, cache_v)


def _conv_tile(prev, cur_ref, w_ref, b_ref, lg_ref, lb_ref, o_ref, ext_sc, rows):
    ext_sc[0:HIST_PAD, :] = prev
    ext_sc[HIST_PAD:HIST_PAD + rows, :] = cur_ref[...]
    lead = HIST_PAD - HIST
    for r0 in range(0, rows, SUB_CONV):
        acc = jnp.zeros((SUB_CONV, CONV_CH), F32) + b_ref[...]
        for k in range(CONV_WIDTH):
            acc = acc + w_ref[k:k + 1, :] * ext_sc[r0 + lead + k:r0 + lead + k + SUB_CONV, :]
        mu = jnp.mean(acc, axis=-1, keepdims=True)
        d = acc - mu
        var = jnp.mean(d * d, axis=-1, keepdims=True)
        yn = d * lax.rsqrt(var + EPS) * lg_ref[...] + lb_ref[...]
        o_ref[r0:r0 + SUB_CONV, :] = (yn * _sigmoid(yn)).astype(BF16)


def _conv_prompt_body(prev_ref, cur_ref, w_ref, b_ref, lg_ref, lb_ref, o_ref, ext_sc):
    i = pl.program_id(0)
    prev = jnp.where(i > 0, prev_ref[...], 0.0)
    _conv_tile(prev, cur_ref, w_ref, b_ref, lg_ref, lb_ref, o_ref, ext_sc, TR_CONV)


def _conv_sample_body(prev_ref, cur_ref, w_ref, b_ref, lg_ref, lb_ref, o_ref, ext_sc):
    _conv_tile(prev_ref[0], cur_ref, w_ref, b_ref, lg_ref, lb_ref, o_ref, ext_sc, L_S)


def _conv_consts_specs(index_map):
    return [
        pl.BlockSpec((CONV_WIDTH, CONV_CH), index_map),
        pl.BlockSpec((1, CONV_CH), index_map),
        pl.BlockSpec((1, CONV_CH), index_map),
        pl.BlockSpec((1, CONV_CH), index_map),
    ]


def _conv_prompt(u, conv_dw, conv_b, ln_g, ln_b):
    r = TR_CONV // HIST_PAD
    return pl.pallas_call(
        _conv_prompt_body,
        name="conv_prompt",
        grid=(T_P // TR_CONV,),
        in_specs=[
            pl.BlockSpec((HIST_PAD, CONV_CH), lambda i: (jnp.maximum(i * r - 1, 0), 0)),
            pl.BlockSpec((TR_CONV, CONV_CH), lambda i: (i, 0)),
        ] + _conv_consts_specs(lambda i: (0, 0)),
        out_specs=pl.BlockSpec((TR_CONV, CONV_CH), lambda i: (i, 0)),
        out_shape=jax.ShapeDtypeStruct((T_P, CONV_CH), BF16),
        scratch_shapes=[pltpu.VMEM((HIST_PAD + TR_CONV, CONV_CH), F32)],
        compiler_params=_params(1),
    )(u, u, conv_dw, conv_b, ln_g, ln_b)


def _conv_sample(hist_pad, u, conv_dw, conv_b, ln_g, ln_b):
    first = T_P // L_S
    return pl.pallas_call(
        _conv_sample_body,
        name="conv_sample",
        grid=(N_STREAMS,),
        in_specs=[
            pl.BlockSpec((1, HIST_PAD, CONV_CH), lambda b: (b, 0, 0)),
            pl.BlockSpec((L_S, CONV_CH), lambda b: (first + b, 0)),
        ] + _conv_consts_specs(lambda b: (0, 0)),
        out_specs=pl.BlockSpec((L_S, CONV_CH), lambda b: (b, 0)),
        out_shape=jax.ShapeDtypeStruct((T_S, CONV_CH), BF16),
        scratch_shapes=[pltpu.VMEM((HIST_PAD + L_S, CONV_CH), F32)],
        compiler_params=_params(1),
    )(hist_pad, u, conv_dw, conv_b, ln_g, ln_b)


def _mix_body(ap_ref, as_ref, cp_ref, cs_ref, gate_ref, x_ref, wao_ref, wco_ref, wout_ref,
              g2_ref, wr_ref, br_ref, x1_ref, xn_ref, lg_ref):
    m = pl.program_id(0)
    is_sample = m >= T_P // TM_MIX
    a = jnp.where(is_sample, as_ref[...], ap_ref[...])
    c = jnp.where(is_sample, cs_ref[...], cp_ref[...])
    pa = jnp.dot(a, wao_ref[...], preferred_element_type=F32)
    pc = jnp.dot(c, wco_ref[...], preferred_element_type=F32)
    g_attn = gate_ref[:, :D_MODEL].astype(F32)
    g_conv = gate_ref[:, D_MODEL:].astype(F32)
    y = (g_attn * pa + g_conv * pc).astype(BF16)
    x1 = x_ref[...] + jnp.dot(y, wout_ref[...], preferred_element_type=F32)
    x1_ref[...] = x1
    ms = jnp.mean(x1 * x1, axis=-1, keepdims=True)
    xn = x1 * lax.rsqrt(ms + EPS) * g2_ref[...]
    xn_ref[...] = xn
    xn_hi = xn.astype(BF16)
    xn_lo = (xn - xn_hi.astype(F32)).astype(BF16)
    r_hi = jnp.dot(xn_hi, wr_ref[...], preferred_element_type=F32)
    r_lo = jnp.dot(xn_lo, wr_ref[...], preferred_element_type=F32)
    lg_ref[...] = r_hi + pltpu.roll(r_hi, ROUTER_PAD - N_ROUTER, axis=1) + r_lo + br_ref[...]


def _mix(a_p, a_s, c_p, c_s, gates, x, wao, wco, wout, g2, wr, br):
    n_p = T_P // TM_MIX
    n_s = T_S // TM_MIX
    prompt_map = lambda m: (jnp.minimum(m, n_p - 1), 0)
    sample_map = lambda m: (jnp.clip(m - n_p, 0, n_s - 1), 0)
    row_map = lambda m: (m, 0)
    const_map = lambda m: (0, 0)
    once = pl.Buffered(1)
    return pl.pallas_call(
        _mix_body,
        name="mix",
        grid=(T // TM_MIX,),
        in_specs=[
            pl.BlockSpec((TM_MIX, ATTN_WIDTH), prompt_map),
            pl.BlockSpec((TM_MIX, ATTN_WIDTH), sample_map),
            pl.BlockSpec((TM_MIX, CONV_CH), prompt_map),
            pl.BlockSpec((TM_MIX, CONV_CH), sample_map),
            pl.BlockSpec((TM_MIX, 2 * D_MODEL), row_map),
            pl.BlockSpec((TM_MIX, D_MODEL), row_map),
            pl.BlockSpec((ATTN_WIDTH, D_MODEL), const_map, pipeline_mode=once),
            pl.BlockSpec((CONV_CH, D_MODEL), const_map, pipeline_mode=once),
            pl.BlockSpec((D_MODEL, D_MODEL), const_map, pipeline_mode=once),
            pl.BlockSpec((1, D_MODEL), const_map),
            pl.BlockSpec((D_MODEL, ROUTER_PAD), const_map),
            pl.BlockSpec((1, ROUTER_PAD), const_map),
        ],
        out_specs=[
            pl.BlockSpec((TM_MIX, D_MODEL), row_map),
            pl.BlockSpec((TM_MIX, D_MODEL), row_map),
            pl.BlockSpec((TM_MIX, ROUTER_PAD), row_map),
        ],
        out_shape=[
            jax.ShapeDtypeStruct((T, D_MODEL), F32),
            jax.ShapeDtypeStruct((T, D_MODEL), F32),
            jax.ShapeDtypeStruct((T, ROUTER_PAD), F32),
        ],
        compiler_params=_params(1),
    )(a_p, a_s, c_p, c_s, gates, x, wao, wco, wout, g2, wr, br)


def _route(logits):
    gl = logits[:, :N_GROUPS]
    gp = jax.nn.softmax(gl, axis=-1)
    grp = jnp.argmax(gl, axis=-1)
    p_grp = jnp.take_along_axis(gp, grp[:, None], axis=1)[:, 0]
    el = logits[:, N_GROUPS:N_ROUTER].reshape(T, N_GROUPS, EXPERTS_PER_GROUP)
    el = jnp.take_along_axis(el, grp[:, None, None], axis=1)[:, 0]
    top_p, top_i = lax.top_k(jax.nn.softmax(el, axis=-1), TOP_K)
    wts = p_grp[:, None] * top_p / jnp.sum(top_p, axis=-1, keepdims=True)
    eid = (grp[:, None] * EXPERTS_PER_GROUP + top_i).astype(jnp.int32).reshape(-1)

    onehot = (eid[:, None] == jnp.arange(N_EXPERTS, dtype=jnp.int32)[None, :]).astype(jnp.int32)
    csum = jnp.cumsum(onehot, axis=0)
    counts = csum[-1]
    rank = jnp.sum(csum * onehot, axis=1) - 1
    nblk = (counts + BM_MOE - 1) // BM_MOE
    blk_end = jnp.cumsum(nblk)
    blk_start = blk_end - nblk
    row_start = jnp.cumsum(counts) - counts
    pos = (blk_start[eid] * BM_MOE + rank).astype(jnp.int32)

    order = jnp.argsort(eid, stable=True)
    tok_sorted = jnp.concatenate([(order // TOP_K).astype(jnp.int32),
                                  jnp.zeros((BM_MOE,), jnp.int32)])
    n_used = blk_end[-1]
    j = jnp.minimum(jnp.arange(N_BLK_MOE, dtype=jnp.int32), n_used - 1)
    blk_e = jnp.minimum(jnp.searchsorted(blk_end, j, side='right'), N_EXPERTS - 1).astype(jnp.int32)
    blk_src = (row_start[blk_e] + (j - blk_start[blk_e]) * BM_MOE).astype(jnp.int32)
    return wts, pos, tok_sorted, blk_e, blk_src, n_used.astype(jnp.int32).reshape(1)


def _moe_body(blk_e, blk_src, n_used, tok, x_hbm, wg_ref, wu_ref, wd_ref, o_ref, xb, sem):
    j = pl.program_id(0)

    def gather_row(jj, slot, r):
        t = tok[blk_src[jj] + r]
        return pltpu.make_async_copy(x_hbm.at[pl.ds(t, 1)], xb.at[slot, pl.ds(r, 1)], sem.at[slot])

    def issue(jj, slot):
        def body(r, carry):
            gather_row(jj, slot, r).start()
            return carry
        lax.fori_loop(0, BM_MOE, body, 0, unroll=8)

    @pl.when(j == 0)
    def _():
        issue(0, 0)

    @pl.when(j + 1 < n_used[0])
    def _():
        issue(j + 1, (j + 1) % 2)

    @pl.when(j < n_used[0])
    def _():
        slot = j % 2
        pltpu.make_async_copy(x_hbm.at[pl.ds(0, BM_MOE)], xb.at[slot], sem.at[slot]).wait()
        x = xb[slot].astype(BF16)
        g = jnp.dot(x, wg_ref[0].astype(BF16), preferred_element_type=F32)
        u = jnp.dot(x, wu_ref[0].astype(BF16), preferred_element_type=F32)
        h = (g * _sigmoid(g) * u).astype(BF16)
        o_ref[...] = jnp.dot(h, wd_ref[0].astype(BF16), preferred_element_type=F32)

    @pl.when(j >= n_used[0])
    def _():
        o_ref[...] = jnp.zeros_like(o_ref)


def _moe(blk_e, blk_src, n_used, tok_sorted, xn, w_g, w_u, w_d):
    w_map = lambda j, be, bs, nu, tk: (be[j], 0, 0)
    grid_spec = pltpu.PrefetchScalarGridSpec(
        num_scalar_prefetch=4,
        grid=(N_BLK_MOE,),
        in_specs=[
            pl.BlockSpec(memory_space=pl.ANY),
            pl.BlockSpec((1, D_MODEL, D_EXPERT), w_map),
            pl.BlockSpec((1, D_MODEL, D_EXPERT), w_map),
            pl.BlockSpec((1, D_EXPERT, D_MODEL), w_map),
        ],
        out_specs=pl.BlockSpec((BM_MOE, D_MODEL), lambda j, be, bs, nu, tk: (j, 0)),
        scratch_shapes=[
            pltpu.VMEM((2, BM_MOE, D_MODEL), F32),
            pltpu.SemaphoreType.DMA((2,)),
        ],
    )
    return pl.pallas_call(
        _moe_body,
        name="experts",
        grid_spec=grid_spec,
        out_shape=jax.ShapeDtypeStruct((N_BLK_MOE * BM_MOE, D_MODEL), F32),
        compiler_params=_params(1),
    )(blk_e, blk_src, n_used, tok_sorted, xn, w_g, w_u, w_d)


def _combine_body(pos, x1_ref, w_ref, gf_ref, o_hbm, yp_ref, ys_ref, rb, sem):
    m = pl.program_id(0)
    n_m = pl.num_programs(0)

    def issue(mm, slot):
        base = mm * (TM_OUT * TOP_K)

        def body(r, carry):
            for k in range(TOP_K):
                p = pos[base + r * TOP_K + k]
                pltpu.make_async_copy(o_hbm.at[pl.ds(p, 1)], rb.at[slot, k, pl.ds(r, 1)],
                                      sem.at[slot]).start()
            return carry
        lax.fori_loop(0, TM_OUT, body, 0, unroll=4)

    @pl.when(m == 0)
    def _():
        issue(0, 0)

    @pl.when(m + 1 < n_m)
    def _():
        issue(m + 1, (m + 1) % 2)

    slot = m % 2
    for k in range(TOP_K):
        pltpu.make_async_copy(o_hbm.at[pl.ds(0, TM_OUT)], rb.at[slot, k], sem.at[slot]).wait()
    y = x1_ref[...] + w_ref[:, 0:1] * rb[slot, 0] + w_ref[:, 1:2] * rb[slot, 1]
    ms = jnp.mean(y * y, axis=-1, keepdims=True)
    out = y * lax.rsqrt(ms + EPS) * gf_ref[...]

    @pl.when(m < T_P // TM_OUT)
    def _():
        yp_ref[...] = out

    @pl.when(m >= T_P // TM_OUT)
    def _():
        ys_ref[...] = out


def _combine(pos, x1, wts, gf, out_sorted):
    n_p = T_P // TM_OUT
    n_s = T_S // TM_OUT
    grid_spec = pltpu.PrefetchScalarGridSpec(
        num_scalar_prefetch=1,
        grid=(T // TM_OUT,),
        in_specs=[
            pl.BlockSpec((TM_OUT, D_MODEL), lambda m, p: (m, 0)),
            pl.BlockSpec((TM_OUT, TOP_K), lambda m, p: (m, 0)),
            pl.BlockSpec((1, D_MODEL), lambda m, p: (0, 0)),
            pl.BlockSpec(memory_space=pl.ANY),
        ],
        out_specs=[
            pl.BlockSpec((TM_OUT, D_MODEL), lambda m, p: (jnp.minimum(m, n_p - 1), 0)),
            pl.BlockSpec((TM_OUT, D_MODEL), lambda m, p: (jnp.clip(m - n_p, 0, n_s - 1), 0)),
        ],
        scratch_shapes=[
            pltpu.VMEM((2, TOP_K, TM_OUT, D_MODEL), F32),
            pltpu.SemaphoreType.DMA((2,)),
        ],
    )
    return pl.pallas_call(
        _combine_body,
        name="combine",
        grid_spec=grid_spec,
        out_shape=[
            jax.ShapeDtypeStruct((T_P, D_MODEL), F32),
            jax.ShapeDtypeStruct((T_S, D_MODEL), F32),
        ],
        compiler_params=_params(1),
    )(pos, x1, wts, gf, out_sorted)


def kernel(x_prompt, x_sample, cache_k, cache_v, state_conv, norm1_g, w_in, b_in, attn_sinks,
           w_attn_o, conv_dw, conv_dw_b, conv_ln_g, conv_ln_b, w_conv_o, w_out, norm2_g,
           w_router_group, b_router_group, w_router_expert, b_router_expert, w_e_gate, w_e_up,
           w_e_down, final_norm_g):
    x = jnp.concatenate([x_prompt.reshape(T_P, D_MODEL), x_sample.reshape(T_S, D_MODEL)], axis=0)

    q, kv, u, gates = _in_proj(x, norm1_g[0][None, :], w_in[0].astype(BF16), b_in[0][None, :])

    sinks = attn_sinks[0]
    a_p = _attn_prompt(sinks, q, kv)
    ck = cache_k[0].reshape(N_STREAMS, WINDOW, KV_WIDTH)
    cv = cache_v[0].reshape(N_STREAMS, WINDOW, KV_WIDTH)
    a_s = _attn_sample(sinks, q, kv, ck, cv)

    conv_consts = (conv_dw[0], conv_dw_b[0][None, :], conv_ln_g[0][None, :], conv_ln_b[0][None, :])
    c_p = _conv_prompt(u, *conv_consts)
    hist_pad = jnp.pad(state_conv[0], ((0, 0), (HIST_PAD - HIST, 0), (0, 0)))
    c_s = _conv_sample(hist_pad, u, *conv_consts)

    w_r = jnp.concatenate([w_router_group[0], w_router_expert[0]], axis=1)
    w_r_hi = w_r.astype(BF16)
    w_r_lo = (w_r - w_r_hi.astype(F32)).astype(BF16)
    w_r_cat = jnp.concatenate(
        [w_r_hi, w_r_lo, jnp.zeros((D_MODEL, ROUTER_PAD - 2 * N_ROUTER), BF16)], axis=1)
    b_r = jnp.concatenate([b_router_group[0], b_router_expert[0],
                           jnp.zeros((ROUTER_PAD - N_ROUTER,), F32)])[None, :]

    x1, xn2, logits = _mix(a_p, a_s, c_p, c_s, gates, x,
                           w_attn_o[0].astype(BF16), w_conv_o[0].astype(BF16), w_out[0].astype(BF16),
                           norm2_g[0][None, :], w_r_cat, b_r)

    wts, pos, tok_sorted, blk_e, blk_src, n_used = _route(logits)
    out_sorted = _moe(blk_e, blk_src, n_used, tok_sorted, xn2, w_e_gate[0], w_e_up[0], w_e_down[0])
    y_p, y_s = _combine(pos, x1, wts, final_norm_g[None, :], out_sorted)

    k_new = kv[:, :KV_WIDTH]
    v_new = kv[:, KV_WIDTH:]
    kv_shape = (1, -1, WINDOW, N_KV_HEADS, HEAD_DIM)
    new_k_prompt = k_new[T_P - WINDOW:T_P].reshape(1, 1, WINDOW, N_KV_HEADS, HEAD_DIM)
    new_v_prompt = v_new[T_P - WINDOW:T_P].reshape(1, 1, WINDOW, N_KV_HEADS, HEAD_DIM)
    new_conv_prompt = u[T_P - HIST:T_P].reshape(1, 1, HIST, CONV_CH)
    k_s = k_new[T_P:].reshape(N_STREAMS, L_S, KV_WIDTH)
    v_s = v_new[T_P:].reshape(N_STREAMS, L_S, KV_WIDTH)
    new_k_sample = jnp.concatenate([ck[:, L_S:], k_s], axis=1).reshape(kv_shape)
    new_v_sample = jnp.concatenate([cv[:, L_S:], v_s], axis=1).reshape(kv_shape)
    u_s = u[T_P:].reshape(N_STREAMS, L_S, CONV_CH)
    new_conv_sample = u_s[:, L_S - HIST:].reshape(1, N_STREAMS, HIST, CONV_CH)

    return (y_p.reshape(1, T_P, D_MODEL), y_s.reshape(N_STREAMS, L_S, D_MODEL),
            new_k_prompt, new_v_prompt, new_conv_prompt,
            new_k_sample, new_v_sample, new_conv_sample)
```

```python
import functools

import jax
import jax.numpy as jnp
from jax import lax
from jax.experimental import pallas as pl
from jax.experimental.pallas import tpu as pltpu

F32 = jnp.float32
BF16 = jnp.bfloat16

D_MODEL = 2048
T_P = 8192
N_STREAMS = 8
L_S = 64
T_S = N_STREAMS * L_S
T = T_P + T_S
CHUNK = 64
WINDOW = 128
HEAD_DIM = 64
N_Q_HEADS = 16
N_KV_HEADS = 4
GROUP = N_Q_HEADS // N_KV_HEADS
ATTN_WIDTH = N_Q_HEADS * HEAD_DIM
KV_WIDTH = N_KV_HEADS * HEAD_DIM
CONV_CH = 1024
CONV_WIDTH = 31
HIST = CONV_WIDTH - 1
HIST_PAD = 32
SUBLANES = 8
OFF_KV = ATTN_WIDTH
OFF_GLU = OFF_KV + 2 * KV_WIDTH
OFF_GATES = OFF_GLU + 2 * CONV_CH
IN_WIDTH = OFF_GATES + 2 * D_MODEL
N_GROUPS = 8
EXPERTS_PER_GROUP = 4
N_EXPERTS = N_GROUPS * EXPERTS_PER_GROUP
TOP_K = 2
D_EXPERT = 512
N_ROUTER = N_GROUPS + N_EXPERTS
ROUTER_PAD = 128
EPS = 1e-6
NEG = -1e30

VMEM_LIMIT = 56 * 1024 * 1024
MXU_N = 256

TM_IN = 1024
TM_WIDE = 2048
TN_IN = 512
TN_GATES = TN_IN
TQ_ATTN = 256
PAIR = 2 * CHUNK
TR_CONV = 256
SUB_CONV = 32
TM_MIX = 256
BM_MOE = 256
TM_OUT = 256
N_SLOTS = T * TOP_K
N_BLK_MOE = -(-(N_SLOTS + N_EXPERTS * (BM_MOE - 1)) // BM_MOE)


def _sigmoid(x):
    return 1.0 / (1.0 + jnp.exp(-x))


def _params(n_axes):
    return pltpu.CompilerParams(dimension_semantics=("arbitrary",) * n_axes,
                                vmem_limit_bytes=VMEM_LIMIT)


def _chunk_dot(xn, w_ref, b_ref, c):
    w = w_ref[:, c * MXU_N:(c + 1) * MXU_N].astype(BF16)
    return jnp.dot(xn, w, preferred_element_type=F32) + b_ref[:, c * MXU_N:(c + 1) * MXU_N]


def _qkv_body(x_ref, g_ref, w_ref, b_ref, xn_ref, q_ref, kv_ref):
    n = pl.program_id(1)

    @pl.when(n == 0)
    def _():
        x = x_ref[...]
        ms = jnp.mean(x * x, axis=-1, keepdims=True)
        xn_ref[...] = (x * lax.rsqrt(ms + EPS) * g_ref[...]).astype(BF16)

    @pl.when(n < ATTN_WIDTH // TN_IN)
    def _():
        xn = xn_ref[...]
        for c in range(TN_IN // MXU_N):
            q_ref[:, c * MXU_N:(c + 1) * MXU_N] = (
                _chunk_dot(xn, w_ref, b_ref, c) * (HEAD_DIM ** -0.5)).astype(BF16)

    @pl.when(n == ATTN_WIDTH // TN_IN)
    def _():
        xn = xn_ref[...]
        for c in range(TN_IN // MXU_N):
            kv_ref[:, c * MXU_N:(c + 1) * MXU_N] = _chunk_dot(xn, w_ref, b_ref, c)


def _qkv(x, norm_g, w_in, b_in, tm):
    t = x.shape[0]
    n_q = ATTN_WIDTH // TN_IN
    return pl.pallas_call(
        _qkv_body,
        name="qkv",
        grid=(t // tm, n_q + 1),
        in_specs=[
            pl.BlockSpec((tm, D_MODEL), lambda m, n: (m, 0)),
            pl.BlockSpec((1, D_MODEL), lambda m, n: (0, 0)),
            pl.BlockSpec((D_MODEL, TN_IN), lambda m, n: (0, n)),
            pl.BlockSpec((1, TN_IN), lambda m, n: (0, n)),
        ],
        out_specs=[
            pl.BlockSpec((tm, D_MODEL), lambda m, n: (m, 0)),
            pl.BlockSpec((tm, TN_IN), lambda m, n: (m, jnp.minimum(n, n_q - 1))),
            pl.BlockSpec((tm, 2 * KV_WIDTH), lambda m, n: (m, 0)),
        ],
        out_shape=[
            jax.ShapeDtypeStruct((t, D_MODEL), BF16),
            jax.ShapeDtypeStruct((t, ATTN_WIDTH), BF16),
            jax.ShapeDtypeStruct((t, 2 * KV_WIDTH), F32),
        ],
        compiler_params=_params(2),
    )(x, norm_g, w_in, b_in)


def _glu_body(xn_ref, wa_ref, ba_ref, wb_ref, bb_ref, u_ref):
    xn = xn_ref[...]
    for c in range(TN_IN // MXU_N):
        a = _chunk_dot(xn, wa_ref, ba_ref, c)
        b = _chunk_dot(xn, wb_ref, bb_ref, c)
        u_ref[:, c * MXU_N:(c + 1) * MXU_N] = a * _sigmoid(b)


def _glu(xn, w_in, b_in, tm):
    t = xn.shape[0]
    a0 = OFF_GLU // TN_IN
    b0 = (OFF_GLU + CONV_CH) // TN_IN
    return pl.pallas_call(
        _glu_body,
        name="glu",
        grid=(t // tm, CONV_CH // TN_IN),
        in_specs=[
            pl.BlockSpec((tm, D_MODEL), lambda m, n: (m, 0)),
            pl.BlockSpec((D_MODEL, TN_IN), lambda m, n: (0, a0 + n)),
            pl.BlockSpec((1, TN_IN), lambda m, n: (0, a0 + n)),
            pl.BlockSpec((D_MODEL, TN_IN), lambda m, n: (0, b0 + n)),
            pl.BlockSpec((1, TN_IN), lambda m, n: (0, b0 + n)),
        ],
        out_specs=pl.BlockSpec((tm, TN_IN), lambda m, n: (m, n)),
        out_shape=jax.ShapeDtypeStruct((t, CONV_CH), F32),
        compiler_params=_params(2),
    )(xn, w_in, b_in, w_in, b_in)


def _gates_body(xn_ref, w_ref, b_ref, o_ref):
    xn = xn_ref[...]
    for c in range(TN_GATES // MXU_N):
        o_ref[:, c * MXU_N:(c + 1) * MXU_N] = _sigmoid(_chunk_dot(xn, w_ref, b_ref, c)).astype(BF16)


def _gates(xn, w_in, b_in, tm):
    t = xn.shape[0]
    n0 = OFF_GATES // TN_GATES
    return pl.pallas_call(
        _gates_body,
        name="gates",
        grid=(t // tm, 2 * D_MODEL // TN_GATES),
        in_specs=[
            pl.BlockSpec((tm, D_MODEL), lambda m, n: (m, 0)),
            pl.BlockSpec((D_MODEL, TN_GATES), lambda m, n: (0, n0 + n)),
            pl.BlockSpec((1, TN_GATES), lambda m, n: (0, n0 + n)),
        ],
        out_specs=pl.BlockSpec((tm, TN_GATES), lambda m, n: (m, n)),
        out_shape=jax.ShapeDtypeStruct((t, 2 * D_MODEL), BF16),
        compiler_params=_params(2),
    )(xn, w_in, b_in)


def _in_proj(x, norm_g, w_in, b_in, tm):
    xn, q, kv = _qkv(x, norm_g, w_in, b_in, tm)
    tw = min(TM_WIDE, x.shape[0])
    return q, kv, _glu(xn, w_in, b_in, tw), _gates(xn, w_in, b_in, tw)


def _attn_prompt_body(sink_ref, q_ref, kvp_ref, kvc_ref, a_ref, at_sc):
    i = pl.program_id(0)
    kv = jnp.concatenate([kvp_ref[...], kvc_ref[...]], axis=0)
    k = kv[:, :KV_WIDTH].astype(BF16)
    vt = kv[:, KV_WIDTH:].T.astype(BF16)
    qt = q_ref[...].astype(F32).T.astype(BF16)
    n_cols = GROUP * PAIR
    n_keys = WINDOW + PAIR
    for p in range(TQ_ATTN // PAIR):
        w0 = p * PAIR
        row = lax.broadcasted_iota(jnp.int32, (n_keys, n_cols), 0)
        col = lax.broadcasted_iota(jnp.int32, (n_keys, n_cols), 1)
        first = jnp.where((col & (PAIR - 1)) >= CHUNK, CHUNK, 0)
        pos = row + (i * TQ_ATTN + w0 - WINDOW)
        ok = (row >= first) & (row < first + WINDOW + CHUNK) & (pos >= 0)
        bias = jnp.where(ok, 0.0, NEG)
        for h in range(N_KV_HEADS):
            kh = k[w0:w0 + n_keys, h * HEAD_DIM:(h + 1) * HEAD_DIM]
            rhs = jnp.concatenate(
                [qt[(h * GROUP + g) * HEAD_DIM:(h * GROUP + g + 1) * HEAD_DIM, w0:w0 + PAIR]
                 for g in range(GROUP)], axis=1)
            st = jnp.dot(kh, rhs, preferred_element_type=F32) + bias
            gcol = lax.shift_right_logical(lax.broadcasted_iota(jnp.int32, (1, n_cols), 1),
                                           PAIR.bit_length() - 1)
            sink = jnp.full((1, n_cols), sink_ref[h * GROUP], F32)
            for g in range(1, GROUP):
                sink = jnp.where(gcol == g, sink_ref[h * GROUP + g], sink)
            m = jnp.maximum(jnp.max(st, axis=0, keepdims=True), sink)
            pt = jnp.exp(st - m)
            denom = jnp.sum(pt, axis=0, keepdims=True) + jnp.exp(sink - m)
            ot = jnp.dot(vt[h * HEAD_DIM:(h + 1) * HEAD_DIM, w0:w0 + n_keys], pt.astype(BF16),
                         preferred_element_type=F32) / denom
            for g in range(GROUP):
                r0 = (h * GROUP + g) * HEAD_DIM
                at_sc[r0:r0 + HEAD_DIM, w0:w0 + PAIR] = ot[:, g * PAIR:(g + 1) * PAIR]
    a_ref[...] = at_sc[...].T.astype(BF16)


def _attn_prompt(sinks, q, kv):
    r = TQ_ATTN // WINDOW
    return pl.pallas_call(
        _attn_prompt_body,
        name="attn_prompt",
        grid=(T_P // TQ_ATTN,),
        in_specs=[
            pl.BlockSpec(memory_space=pltpu.SMEM),
            pl.BlockSpec((TQ_ATTN, ATTN_WIDTH), lambda i: (i, 0)),
            pl.BlockSpec((WINDOW, 2 * KV_WIDTH), lambda i: (jnp.maximum(i * r - 1, 0), 0)),
            pl.BlockSpec((TQ_ATTN, 2 * KV_WIDTH), lambda i: (i, 0)),
        ],
        out_specs=pl.BlockSpec((TQ_ATTN, ATTN_WIDTH), lambda i: (i, 0)),
        out_shape=jax.ShapeDtypeStruct((T_P, ATTN_WIDTH), BF16),
        scratch_shapes=[pltpu.VMEM((ATTN_WIDTH, TQ_ATTN), F32)],
        compiler_params=_params(1),
    )(sinks, q, kv, kv)


def _attn_sample_body(sink_ref, q_ref, kvc_ref, ck_ref, cv_ref, a_ref):
    kvc = kvc_ref[...]
    kw = jnp.concatenate([ck_ref[0], kvc[:, :KV_WIDTH]], axis=0).astype(BF16)
    vw = jnp.concatenate([cv_ref[0], kvc[:, KV_WIDTH:]], axis=0).astype(BF16)
    qc = q_ref[...]
    for h in range(N_KV_HEADS):
        kh = kw[:, h * HEAD_DIM:(h + 1) * HEAD_DIM]
        vh = vw[:, h * HEAD_DIM:(h + 1) * HEAD_DIM]
        qg = jnp.concatenate(
            [qc[:, (h * GROUP + g) * HEAD_DIM:(h * GROUP + g + 1) * HEAD_DIM] for g in range(GROUP)],
            axis=0)
        s = lax.dot_general(qg, kh, (((1,), (1,)), ((), ())), preferred_element_type=F32)
        sink = jnp.concatenate(
            [jnp.full((L_S, 1), sink_ref[h * GROUP + g], F32) for g in range(GROUP)], axis=0)
        m = jnp.maximum(jnp.max(s, axis=1, keepdims=True), sink)
        p = jnp.exp(s - m)
        denom = jnp.sum(p, axis=1, keepdims=True) + jnp.exp(sink - m)
        o = jnp.dot(p.astype(BF16), vh, preferred_element_type=F32) / denom
        for g in range(GROUP):
            c0 = (h * GROUP + g) * HEAD_DIM
            a_ref[:, c0:c0 + HEAD_DIM] = o[g * L_S:(g + 1) * L_S].astype(BF16)


def _attn_sample(sinks, q, kv, cache_k, cache_v):
    return pl.pallas_call(
        _attn_sample_body,
        name="attn_sample",
        grid=(N_STREAMS,),
        in_specs=[
            pl.BlockSpec(memory_space=pltpu.SMEM),
            pl.BlockSpec((L_S, ATTN_WIDTH), lambda b: (b, 0)),
            pl.BlockSpec((L_S, 2 * KV_WIDTH), lambda b: (b, 0)),
            pl.BlockSpec((1, WINDOW, KV_WIDTH), lambda b: (b, 0, 0)),
            pl.BlockSpec((1, WINDOW, KV_WIDTH), lambda b: (b, 0, 0)),
        ],
        out_specs=pl.BlockSpec((L_S, ATTN_WIDTH), lambda b: (b, 0)),
        out_shape=jax.ShapeDtypeStruct((T_S, ATTN_WIDTH), BF16),
        compiler_params=_params(1),
    )(sinks, q, kv, cache_k, cache_v)


def _conv_tile(prev, cur_ref, w_ref, b_ref, lg_ref, lb_ref, o_ref, ext_sc, sh_sc, rows):
    ext_sc[0:HIST_PAD, :] = prev
    ext_sc[HIST_PAD:HIST_PAD + rows, :] = cur_ref[...]
    ext_sc[HIST_PAD + rows:, :] = jnp.zeros((SUBLANES, CONV_CH), F32)
    lead = HIST_PAD - HIST
    span = rows + HIST_PAD - SUBLANES
    for s in range(SUBLANES):
        sh_sc[s, 0:span, :] = ext_sc[lead + s:lead + s + span, :]
    for r0 in range(0, rows, SUB_CONV):
        acc = jnp.zeros((SUB_CONV, CONV_CH), F32) + b_ref[...]
        for k in range(CONV_WIDTH):
            a, s = divmod(k, SUBLANES)
            r = r0 + a * SUBLANES
            acc = acc + w_ref[k:k + 1, :] * sh_sc[s, r:r + SUB_CONV, :]
        mu = jnp.mean(acc, axis=-1, keepdims=True)
        d = acc - mu
        var = jnp.mean(d * d, axis=-1, keepdims=True)
        yn = d * lax.rsqrt(var + EPS) * lg_ref[...] + lb_ref[...]
        o_ref[r0:r0 + SUB_CONV, :] = (yn * _sigmoid(yn)).astype(BF16)


def _conv_prompt_body(prev_ref, cur_ref, w_ref, b_ref, lg_ref, lb_ref, o_ref, ext_sc, sh_sc):
    i = pl.program_id(0)
    prev = jnp.where(i > 0, prev_ref[...], 0.0)
    _conv_tile(prev, cur_ref, w_ref, b_ref, lg_ref, lb_ref, o_ref, ext_sc, sh_sc, TR_CONV)


def _conv_sample_body(prev_ref, cur_ref, w_ref, b_ref, lg_ref, lb_ref, o_ref, ext_sc, sh_sc):
    _conv_tile(prev_ref[0], cur_ref, w_ref, b_ref, lg_ref, lb_ref, o_ref, ext_sc, sh_sc, L_S)


def _conv_consts_specs(index_map):
    return [
        pl.BlockSpec((CONV_WIDTH, CONV_CH), index_map),
        pl.BlockSpec((1, CONV_CH), index_map),
        pl.BlockSpec((1, CONV_CH), index_map),
        pl.BlockSpec((1, CONV_CH), index_map),
    ]


def _conv_scratch(rows):
    return [pltpu.VMEM((HIST_PAD + rows + SUBLANES, CONV_CH), F32),
            pltpu.VMEM((SUBLANES, HIST_PAD + rows, CONV_CH), F32)]


def _conv_prompt(u, conv_dw, conv_b, ln_g, ln_b):
    r = TR_CONV // HIST_PAD
    return pl.pallas_call(
        _conv_prompt_body,
        name="conv_prompt",
        grid=(T_P // TR_CONV,),
        in_specs=[
            pl.BlockSpec((HIST_PAD, CONV_CH), lambda i: (jnp.maximum(i * r - 1, 0), 0)),
            pl.BlockSpec((TR_CONV, CONV_CH), lambda i: (i, 0)),
        ] + _conv_consts_specs(lambda i: (0, 0)),
        out_specs=pl.BlockSpec((TR_CONV, CONV_CH), lambda i: (i, 0)),
        out_shape=jax.ShapeDtypeStruct((T_P, CONV_CH), BF16),
        scratch_shapes=_conv_scratch(TR_CONV),
        compiler_params=_params(1),
    )(u, u, conv_dw, conv_b, ln_g, ln_b)


def _conv_sample(hist_pad, u, conv_dw, conv_b, ln_g, ln_b):
    return pl.pallas_call(
        _conv_sample_body,
        name="conv_sample",
        grid=(N_STREAMS,),
        in_specs=[
            pl.BlockSpec((1, HIST_PAD, CONV_CH), lambda b: (b, 0, 0)),
            pl.BlockSpec((L_S, CONV_CH), lambda b: (b, 0)),
        ] + _conv_consts_specs(lambda b: (0, 0)),
        out_specs=pl.BlockSpec((L_S, CONV_CH), lambda b: (b, 0)),
        out_shape=jax.ShapeDtypeStruct((T_S, CONV_CH), BF16),
        scratch_shapes=_conv_scratch(L_S),
        compiler_params=_params(1),
    )(hist_pad, u, conv_dw, conv_b, ln_g, ln_b)


def _mix_body(ap_ref, as_ref, cp_ref, cs_ref, gp_ref, gs_ref, xp_ref, xs_ref, wao_ref, wco_ref,
              wout_ref, g2_ref, wr_ref, br_ref, x1_ref, xn_ref, lg_ref):
    m = pl.program_id(0)
    is_sample = m >= T_P // TM_MIX
    a = jnp.where(is_sample, as_ref[...], ap_ref[...])
    c = jnp.where(is_sample, cs_ref[...], cp_ref[...])
    gate = jnp.where(is_sample, gs_ref[...], gp_ref[...])
    x = jnp.where(is_sample, xs_ref[...], xp_ref[...])
    pa = jnp.dot(a, wao_ref[...], preferred_element_type=F32)
    pc = jnp.dot(c, wco_ref[...], preferred_element_type=F32)
    y = (gate[:, :D_MODEL].astype(F32) * pa + gate[:, D_MODEL:].astype(F32) * pc).astype(BF16)
    x1 = x + jnp.dot(y, wout_ref[...], preferred_element_type=F32)
    x1_ref[...] = x1
    ms = jnp.mean(x1 * x1, axis=-1, keepdims=True)
    xn = x1 * lax.rsqrt(ms + EPS) * g2_ref[...]
    xn_ref[...] = xn
    xn_hi = xn.astype(BF16)
    xn_lo = (xn - xn_hi.astype(F32)).astype(BF16)
    r_hi = jnp.dot(xn_hi, wr_ref[...], preferred_element_type=F32)
    r_lo = jnp.dot(xn_lo, wr_ref[...], preferred_element_type=F32)
    lg_ref[...] = r_hi + pltpu.roll(r_hi, ROUTER_PAD - N_ROUTER, axis=1) + r_lo + br_ref[...]


def _mix(a_p, a_s, c_p, c_s, g_p, g_s, x_p, x_s, wao, wco, wout, g2, wr, br):
    n_p = T_P // TM_MIX
    n_s = T_S // TM_MIX
    prompt_map = lambda m: (jnp.minimum(m, n_p - 1), 0)
    sample_map = lambda m: (jnp.clip(m - n_p, 0, n_s - 1), 0)
    row_map = lambda m: (m, 0)
    const_map = lambda m: (0, 0)
    once = pl.Buffered(1)

    def pair(width):
        return [pl.BlockSpec((TM_MIX, width), prompt_map), pl.BlockSpec((TM_MIX, width), sample_map)]

    return pl.pallas_call(
        _mix_body,
        name="mix",
        grid=(T // TM_MIX,),
        in_specs=pair(ATTN_WIDTH) + pair(CONV_CH) + pair(2 * D_MODEL) + pair(D_MODEL) + [
            pl.BlockSpec((ATTN_WIDTH, D_MODEL), const_map, pipeline_mode=once),
            pl.BlockSpec((CONV_CH, D_MODEL), const_map, pipeline_mode=once),
            pl.BlockSpec((D_MODEL, D_MODEL), const_map, pipeline_mode=once),
            pl.BlockSpec((1, D_MODEL), const_map),
            pl.BlockSpec((D_MODEL, ROUTER_PAD), const_map),
            pl.BlockSpec((1, ROUTER_PAD), const_map),
        ],
        out_specs=[
            pl.BlockSpec((TM_MIX, D_MODEL), row_map),
            pl.BlockSpec((TM_MIX, D_MODEL), row_map),
            pl.BlockSpec((TM_MIX, ROUTER_PAD), row_map),
        ],
        out_shape=[
            jax.ShapeDtypeStruct((T, D_MODEL), F32),
            jax.ShapeDtypeStruct((T, D_MODEL), F32),
            jax.ShapeDtypeStruct((T, ROUTER_PAD), F32),
        ],
        compiler_params=_params(1),
    )(a_p, a_s, c_p, c_s, g_p, g_s, x_p, x_s, wao, wco, wout, g2, wr, br)


def _route(logits):
    gl = logits[:, :N_GROUPS]
    gp = jax.nn.softmax(gl, axis=-1)
    grp = jnp.argmax(gl, axis=-1)
    p_grp = jnp.take_along_axis(gp, grp[:, None], axis=1)[:, 0]
    el = logits[:, N_GROUPS:N_ROUTER].reshape(T, N_GROUPS, EXPERTS_PER_GROUP)
    el = jnp.take_along_axis(el, grp[:, None, None], axis=1)[:, 0]
    top_p, top_i = lax.top_k(jax.nn.softmax(el, axis=-1), TOP_K)
    wts = p_grp[:, None] * top_p / jnp.sum(top_p, axis=-1, keepdims=True)
    eid = (grp[:, None] * EXPERTS_PER_GROUP + top_i).astype(jnp.int32).reshape(-1)

    onehot = (eid[:, None] == jnp.arange(N_EXPERTS, dtype=jnp.int32)[None, :]).astype(jnp.int32)
    csum = jnp.cumsum(onehot, axis=0)
    counts = csum[-1]
    rank = jnp.sum(csum * onehot, axis=1) - 1
    nblk = (counts + BM_MOE - 1) // BM_MOE
    blk_end = jnp.cumsum(nblk)
    blk_start = blk_end - nblk
    row_start = jnp.cumsum(counts) - counts
    pos = (blk_start[eid] * BM_MOE + rank).astype(jnp.int32)

    order = jnp.argsort(eid, stable=True)
    tok_sorted = jnp.concatenate([(order // TOP_K).astype(jnp.int32),
                                  jnp.zeros((BM_MOE,), jnp.int32)])
    n_used = blk_end[-1]
    j = jnp.minimum(jnp.arange(N_BLK_MOE, dtype=jnp.int32), n_used - 1)
    blk_e = jnp.minimum(jnp.searchsorted(blk_end, j, side='right'), N_EXPERTS - 1).astype(jnp.int32)
    blk_src = (row_start[blk_e] + (j - blk_start[blk_e]) * BM_MOE).astype(jnp.int32)
    return wts, pos, tok_sorted, blk_e, blk_src, n_used.astype(jnp.int32).reshape(1)


def _moe_body(blk_e, blk_src, n_used, tok, x_hbm, wg_ref, wu_ref, wd_ref, o_ref, xb, sem):
    j = pl.program_id(0)
    nu = n_used[0]

    def row_copy(src, slot, r):
        return pltpu.make_async_copy(x_hbm.at[pl.ds(tok[src + r], 1)], xb.at[slot, pl.ds(r, 1)],
                                     sem.at[slot])

    def block_wait(slot):
        pltpu.make_async_copy(x_hbm.at[pl.ds(0, BM_MOE)], xb.at[slot], sem.at[slot]).wait()

    @pl.when(j == 0)
    def _():
        src = blk_src[0]

        def body(r, carry):
            row_copy(src, 0, r).start()
            return carry
        lax.fori_loop(0, BM_MOE, body, 0, unroll=8)

    @pl.when(j < nu)
    def _():
        slot = j % 2
        block_wait(slot)
        nsrc = blk_src[jnp.minimum(j + 1, N_BLK_MOE - 1)]
        x = xb[slot].astype(BF16)
        n_chunks = D_EXPERT // MXU_N
        rows_per_chunk = BM_MOE // n_chunks
        acc = None
        for c in range(n_chunks):
            for r in range(c * rows_per_chunk, (c + 1) * rows_per_chunk):
                row_copy(nsrc, 1 - slot, r).start()
            cols = slice(c * MXU_N, (c + 1) * MXU_N)
            g = jnp.dot(x, wg_ref[0, :, cols].astype(BF16), preferred_element_type=F32)
            u = jnp.dot(x, wu_ref[0, :, cols].astype(BF16), preferred_element_type=F32)
            h = (g * _sigmoid(g) * u).astype(BF16)
            part = jnp.dot(h, wd_ref[0, cols, :].astype(BF16), preferred_element_type=F32)
            acc = part if acc is None else acc + part
        o_ref[...] = acc

    @pl.when(j == nu)
    def _():
        block_wait(j % 2)

    @pl.when(j >= nu)
    def _():
        o_ref[...] = jnp.zeros_like(o_ref)


def _moe(blk_e, blk_src, n_used, tok_sorted, xn, w_g, w_u, w_d):
    last = N_BLK_MOE - 1
    w_map = lambda j, be, bs, nu, tk: (be[jnp.minimum(j, last)], 0, 0)
    grid_spec = pltpu.PrefetchScalarGridSpec(
        num_scalar_prefetch=4,
        grid=(N_BLK_MOE + 1,),
        in_specs=[
            pl.BlockSpec(memory_space=pl.ANY),
            pl.BlockSpec((1, D_MODEL, D_EXPERT), w_map),
            pl.BlockSpec((1, D_MODEL, D_EXPERT), w_map),
            pl.BlockSpec((1, D_EXPERT, D_MODEL), w_map),
        ],
        out_specs=pl.BlockSpec((BM_MOE, D_MODEL), lambda j, be, bs, nu, tk: (j, 0)),
        scratch_shapes=[
            pltpu.VMEM((2, BM_MOE, D_MODEL), F32),
            pltpu.SemaphoreType.DMA((2,)),
        ],
    )
    return pl.pallas_call(
        _moe_body,
        name="experts",
        grid_spec=grid_spec,
        out_shape=jax.ShapeDtypeStruct(((N_BLK_MOE + 1) * BM_MOE, D_MODEL), F32),
        compiler_params=_params(1),
    )(blk_e, blk_src, n_used, tok_sorted, xn, w_g, w_u, w_d)


def _combine_body(pos, x1_ref, w_ref, gf_ref, o_hbm, y_ref, rb, sem, *, first_tile):
    m = pl.program_id(0)
    last = pl.num_programs(0) - 1

    def row_copy(base, slot, r, k):
        return pltpu.make_async_copy(o_hbm.at[pl.ds(pos[base + r * TOP_K + k], 1)],
                                     rb.at[slot, k, pl.ds(r, 1)], sem.at[slot])

    def tile_wait(slot):
        for k in range(TOP_K):
            pltpu.make_async_copy(o_hbm.at[pl.ds(0, TM_OUT)], rb.at[slot, k], sem.at[slot]).wait()

    def tile_base(mm):
        return (first_tile + mm) * (TM_OUT * TOP_K)

    @pl.when(m == 0)
    def _():
        base = tile_base(0)

        def body(r, carry):
            for k in range(TOP_K):
                row_copy(base, 0, r, k).start()
            return carry
        lax.fori_loop(0, TM_OUT, body, 0, unroll=4)

    slot = m % 2
    tile_wait(slot)
    nbase = tile_base(jnp.minimum(m + 1, last))
    n_chunks = 4
    rows = TM_OUT // n_chunks
    for c in range(n_chunks):
        for r in range(c * rows, (c + 1) * rows):
            for k in range(TOP_K):
                row_copy(nbase, 1 - slot, r, k).start()
        rs = slice(c * rows, (c + 1) * rows)
        y = x1_ref[rs, :] + w_ref[rs, 0:1] * rb[slot, 0, rs, :] + w_ref[rs, 1:2] * rb[slot, 1, rs, :]
        ms = jnp.mean(y * y, axis=-1, keepdims=True)
        y_ref[rs, :] = y * lax.rsqrt(ms + EPS) * gf_ref[...]

    @pl.when(m == last)
    def _():
        tile_wait(1 - slot)


def _combine(pos, x1, wts, gf, out_sorted, first_tile, n_tiles):
    grid_spec = pltpu.PrefetchScalarGridSpec(
        num_scalar_prefetch=1,
        grid=(n_tiles,),
        in_specs=[
            pl.BlockSpec((TM_OUT, D_MODEL), lambda m, p: (first_tile + m, 0)),
            pl.BlockSpec((TM_OUT, TOP_K), lambda m, p: (first_tile + m, 0)),
            pl.BlockSpec((1, D_MODEL), lambda m, p: (0, 0)),
            pl.BlockSpec(memory_space=pl.ANY),
        ],
        out_specs=pl.BlockSpec((TM_OUT, D_MODEL), lambda m, p: (m, 0)),
        scratch_shapes=[
            pltpu.VMEM((2, TOP_K, TM_OUT, D_MODEL), F32),
            pltpu.SemaphoreType.DMA((2,)),
        ],
    )
    return pl.pallas_call(
        functools.partial(_combine_body, first_tile=first_tile),
        name="combine",
        grid_spec=grid_spec,
        out_shape=jax.ShapeDtypeStruct((n_tiles * TM_OUT, D_MODEL), F32),
        compiler_params=_params(1),
    )(pos, x1, wts, gf, out_sorted)


def kernel(x_prompt, x_sample, cache_k, cache_v, state_conv, norm1_g, w_in, b_in, attn_sinks,
           w_attn_o, conv_dw, conv_dw_b, conv_ln_g, conv_ln_b, w_conv_o, w_out, norm2_g,
           w_router_group, b_router_group, w_router_expert, b_router_expert, w_e_gate, w_e_up,
           w_e_down, final_norm_g):
    x_p = x_prompt.reshape(T_P, D_MODEL)
    x_s = x_sample.reshape(T_S, D_MODEL)
    g1 = norm1_g[0][None, :]
    b1 = b_in[0][None, :]
    q_p, kv_p, u_p, gates_p = _in_proj(x_p, g1, w_in[0], b1, TM_IN)
    q_s, kv_s, u_s, gates_s = _in_proj(x_s, g1, w_in[0], b1, T_S)

    sinks = attn_sinks[0]
    a_p = _attn_prompt(sinks, q_p, kv_p)
    ck = cache_k[0].reshape(N_STREAMS, WINDOW, KV_WIDTH)
    cv = cache_v[0].reshape(N_STREAMS, WINDOW, KV_WIDTH)
    a_s = _attn_sample(sinks, q_s, kv_s, ck, cv)

    conv_consts = (conv_dw[0], conv_dw_b[0][None, :], conv_ln_g[0][None, :], conv_ln_b[0][None, :])
    c_p = _conv_prompt(u_p, *conv_consts)
    hist_pad = jnp.pad(state_conv[0], ((0, 0), (HIST_PAD - HIST, 0), (0, 0)))
    c_s = _conv_sample(hist_pad, u_s, *conv_consts)

    w_r = jnp.concatenate([w_router_group[0], w_router_expert[0]], axis=1)
    w_r_hi = w_r.astype(BF16)
    w_r_lo = (w_r - w_r_hi.astype(F32)).astype(BF16)
    w_r_cat = jnp.concatenate(
        [w_r_hi, w_r_lo, jnp.zeros((D_MODEL, ROUTER_PAD - 2 * N_ROUTER), BF16)], axis=1)
    b_r = jnp.concatenate([b_router_group[0], b_router_expert[0],
                           jnp.zeros((ROUTER_PAD - N_ROUTER,), F32)])[None, :]

    x1, xn2, logits = _mix(a_p, a_s, c_p, c_s, gates_p, gates_s, x_p, x_s,
                           w_attn_o[0].astype(BF16), w_conv_o[0].astype(BF16), w_out[0].astype(BF16),
                           norm2_g[0][None, :], w_r_cat, b_r)

    wts, pos, tok_sorted, blk_e, blk_src, n_used = _route(logits)
    out_sorted = _moe(blk_e, blk_src, n_used, tok_sorted, xn2, w_e_gate[0], w_e_up[0], w_e_down[0])
    gf = final_norm_g[None, :]
    y_p = _combine(pos, x1, wts, gf, out_sorted, 0, T_P // TM_OUT)
    y_s = _combine(pos, x1, wts, gf, out_sorted, T_P // TM_OUT, T_S // TM_OUT)

    kv_shape = (1, -1, WINDOW, N_KV_HEADS, HEAD_DIM)
    new_k_prompt = kv_p[T_P - WINDOW:, :KV_WIDTH].reshape(kv_shape)
    new_v_prompt = kv_p[T_P - WINDOW:, KV_WIDTH:].reshape(kv_shape)
    new_conv_prompt = u_p[T_P - HIST:].reshape(1, 1, HIST, CONV_CH)
    k_s = kv_s[:, :KV_WIDTH].reshape(N_STREAMS, L_S, KV_WIDTH)
    v_s = kv_s[:, KV_WIDTH:].reshape(N_STREAMS, L_S, KV_WIDTH)
    new_k_sample = jnp.concatenate([ck[:, L_S:], k_s], axis=1).reshape(kv_shape)
    new_v_sample = jnp.concatenate([cv[:, L_S:], v_s], axis=1).reshape(kv_shape)
    new_conv_sample = u_s.reshape(N_STREAMS, L_S, CONV_CH)[:, L_S - HIST:].reshape(
        1, N_STREAMS, HIST, CONV_CH)

    return (y_p.reshape(1, T_P, D_MODEL), y_s.reshape(N_STREAMS, L_S, D_MODEL),
            new_k_prompt, new_v_prompt, new_conv_prompt,
            new_k_sample, new_v_sample, new_conv_sample)
```

```python
import functools

import numpy as np
import jax
import jax.numpy as jnp
from jax import lax
from jax.experimental import pallas as pl
from jax.experimental.pallas import tpu as pltpu

F32 = jnp.float32
BF16 = jnp.bfloat16
U32 = jnp.uint32

D_MODEL = 2048
T_P = 8192
N_STREAMS = 8
L_S = 64
T_S = N_STREAMS * L_S
T = T_P + T_S
CHUNK = 64
WINDOW = 128
HEAD_DIM = 64
N_Q_HEADS = 16
N_KV_HEADS = 4
GROUP = N_Q_HEADS // N_KV_HEADS
ATTN_WIDTH = N_Q_HEADS * HEAD_DIM
KV_WIDTH = N_KV_HEADS * HEAD_DIM
CONV_CH = 1024
CONV_WIDTH = 31
HIST = CONV_WIDTH - 1
HIST_PAD = 32
SUBLANES = 8
LANES = 128
ROW_TILE = SUBLANES
HALF_D = D_MODEL // 2
HI_MASK = np.uint32(0xFFFF0000)
OFF_KV = ATTN_WIDTH
OFF_GLU = OFF_KV + 2 * KV_WIDTH
OFF_GATES = OFF_GLU + 2 * CONV_CH
IN_WIDTH = OFF_GATES + 2 * D_MODEL
N_GROUPS = 8
EXPERTS_PER_GROUP = 4
N_EXPERTS = N_GROUPS * EXPERTS_PER_GROUP
TOP_K = 2
D_EXPERT = 512
N_ROUTER = N_GROUPS + N_EXPERTS
ROUTER_PAD = 128
EPS = 1e-6
NEG = -1e30

VMEM_LIMIT = 56 * 1024 * 1024
MXU_N = 256

TM_IN = 1024
TM_WIDE = 2048
TN_IN = 512
TN_GATES = TN_IN
TQ_ATTN = 256
PAIR = 2 * CHUNK
TR_CONV = 256
SUB_CONV = 32
TM_MIX = 256
BM_MOE = 256
TM_OUT = 256
N_SLOTS = T * TOP_K
N_BLK_MOE = -(-(N_SLOTS + N_EXPERTS * (BM_MOE - 1)) // BM_MOE)


def _sigmoid(x):
    return 1.0 / (1.0 + jnp.exp(-x))


def _params(n_axes):
    return pltpu.CompilerParams(dimension_semantics=("arbitrary",) * n_axes,
                                vmem_limit_bytes=VMEM_LIMIT)


def _store_packed_rows(ref, y):
    rows = y.shape[0]
    for s in range(ROW_TILE):
        lo = y[:, s * LANES:(s + 1) * LANES].astype(BF16).astype(F32)
        hi = y[:, HALF_D + s * LANES:HALF_D + (s + 1) * LANES].astype(BF16).astype(F32)
        word = (lax.bitcast_convert_type(hi, U32) & HI_MASK) | (lax.bitcast_convert_type(lo, U32) >> 16)
        ref[pl.ds(s, rows, stride=ROW_TILE), :] = word


def _row_tile(r):
    start = r * ROW_TILE
    return pl.ds(start if isinstance(r, int) else pl.multiple_of(start, ROW_TILE), ROW_TILE)


def _load_packed_rows(ref, rows, s):
    word = ref[pl.ds(s, rows, stride=ROW_TILE), :]
    lo = lax.bitcast_convert_type(word << 16, F32)
    hi = lax.bitcast_convert_type(word & HI_MASK, F32)
    return lo, hi


def _chunk_dot(xn, w_ref, b_ref, c):
    w = w_ref[:, c * MXU_N:(c + 1) * MXU_N].astype(BF16)
    return jnp.dot(xn, w, preferred_element_type=F32) + b_ref[:, c * MXU_N:(c + 1) * MXU_N]


def _qkv_body(x_ref, g_ref, w_ref, b_ref, xn_ref, q_ref, kv_ref):
    n = pl.program_id(1)

    @pl.when(n == 0)
    def _():
        x = x_ref[...]
        ms = jnp.mean(x * x, axis=-1, keepdims=True)
        xn_ref[...] = (x * lax.rsqrt(ms + EPS) * g_ref[...]).astype(BF16)

    @pl.when(n < ATTN_WIDTH // TN_IN)
    def _():
        xn = xn_ref[...]
        for c in range(TN_IN // MXU_N):
            q_ref[:, c * MXU_N:(c + 1) * MXU_N] = (
                _chunk_dot(xn, w_ref, b_ref, c) * (HEAD_DIM ** -0.5)).astype(BF16)

    @pl.when(n == ATTN_WIDTH // TN_IN)
    def _():
        xn = xn_ref[...]
        for c in range(TN_IN // MXU_N):
            kv_ref[:, c * MXU_N:(c + 1) * MXU_N] = _chunk_dot(xn, w_ref, b_ref, c)


def _qkv(x, norm_g, w_in, b_in, tm):
    t = x.shape[0]
    n_q = ATTN_WIDTH // TN_IN
    return pl.pallas_call(
        _qkv_body,
        name="qkv",
        grid=(t // tm, n_q + 1),
        in_specs=[
            pl.BlockSpec((tm, D_MODEL), lambda m, n: (m, 0)),
            pl.BlockSpec((1, D_MODEL), lambda m, n: (0, 0)),
            pl.BlockSpec((D_MODEL, TN_IN), lambda m, n: (0, n)),
            pl.BlockSpec((1, TN_IN), lambda m, n: (0, n)),
        ],
        out_specs=[
            pl.BlockSpec((tm, D_MODEL), lambda m, n: (m, 0)),
            pl.BlockSpec((tm, TN_IN), lambda m, n: (m, jnp.minimum(n, n_q - 1))),
            pl.BlockSpec((tm, 2 * KV_WIDTH), lambda m, n: (m, 0)),
        ],
        out_shape=[
            jax.ShapeDtypeStruct((t, D_MODEL), BF16),
            jax.ShapeDtypeStruct((t, ATTN_WIDTH), BF16),
            jax.ShapeDtypeStruct((t, 2 * KV_WIDTH), F32),
        ],
        compiler_params=_params(2),
    )(x, norm_g, w_in, b_in)


def _glu_body(xn_ref, wa_ref, ba_ref, wb_ref, bb_ref, u_ref):
    xn = xn_ref[...]
    for c in range(TN_IN // MXU_N):
        a = _chunk_dot(xn, wa_ref, ba_ref, c)
        b = _chunk_dot(xn, wb_ref, bb_ref, c)
        u_ref[:, c * MXU_N:(c + 1) * MXU_N] = a * _sigmoid(b)


def _glu(xn, w_in, b_in, tm):
    t = xn.shape[0]
    a0 = OFF_GLU // TN_IN
    b0 = (OFF_GLU + CONV_CH) // TN_IN
    return pl.pallas_call(
        _glu_body,
        name="glu",
        grid=(t // tm, CONV_CH // TN_IN),
        in_specs=[
            pl.BlockSpec((tm, D_MODEL), lambda m, n: (m, 0)),
            pl.BlockSpec((D_MODEL, TN_IN), lambda m, n: (0, a0 + n)),
            pl.BlockSpec((1, TN_IN), lambda m, n: (0, a0 + n)),
            pl.BlockSpec((D_MODEL, TN_IN), lambda m, n: (0, b0 + n)),
            pl.BlockSpec((1, TN_IN), lambda m, n: (0, b0 + n)),
        ],
        out_specs=pl.BlockSpec((tm, TN_IN), lambda m, n: (m, n)),
        out_shape=jax.ShapeDtypeStruct((t, CONV_CH), F32),
        compiler_params=_params(2),
    )(xn, w_in, b_in, w_in, b_in)


def _gates_body(xn_ref, w_ref, b_ref, o_ref):
    xn = xn_ref[...]
    for c in range(TN_GATES // MXU_N):
        o_ref[:, c * MXU_N:(c + 1) * MXU_N] = _sigmoid(_chunk_dot(xn, w_ref, b_ref, c)).astype(BF16)


def _gates(xn, w_in, b_in, tm):
    t = xn.shape[0]
    n0 = OFF_GATES // TN_GATES
    return pl.pallas_call(
        _gates_body,
        name="gates",
        grid=(t // tm, 2 * D_MODEL // TN_GATES),
        in_specs=[
            pl.BlockSpec((tm, D_MODEL), lambda m, n: (m, 0)),
            pl.BlockSpec((D_MODEL, TN_GATES), lambda m, n: (0, n0 + n)),
            pl.BlockSpec((1, TN_GATES), lambda m, n: (0, n0 + n)),
        ],
        out_specs=pl.BlockSpec((tm, TN_GATES), lambda m, n: (m, n)),
        out_shape=jax.ShapeDtypeStruct((t, 2 * D_MODEL), BF16),
        compiler_params=_params(2),
    )(xn, w_in, b_in)


def _in_proj(x, norm_g, w_in, b_in, tm):
    xn, q, kv = _qkv(x, norm_g, w_in, b_in, tm)
    tw = min(TM_WIDE, x.shape[0])
    return q, kv, _glu(xn, w_in, b_in, tw), _gates(xn, w_in, b_in, tw)


def _attn_prompt_body(sink_ref, q_ref, kvp_ref, kvc_ref, a_ref, at_sc):
    i = pl.program_id(0)
    kv = jnp.concatenate([kvp_ref[...], kvc_ref[...]], axis=0)
    k = kv[:, :KV_WIDTH].astype(BF16)
    vt = kv[:, KV_WIDTH:].T.astype(BF16)
    qt = q_ref[...].astype(F32).T.astype(BF16)
    n_cols = GROUP * PAIR
    n_keys = WINDOW + PAIR
    for p in range(TQ_ATTN // PAIR):
        w0 = p * PAIR
        row = lax.broadcasted_iota(jnp.int32, (n_keys, n_cols), 0)
        col = lax.broadcasted_iota(jnp.int32, (n_keys, n_cols), 1)
        first = jnp.where((col & (PAIR - 1)) >= CHUNK, CHUNK, 0)
        pos = row + (i * TQ_ATTN + w0 - WINDOW)
        ok = (row >= first) & (row < first + WINDOW + CHUNK) & (pos >= 0)
        bias = jnp.where(ok, 0.0, NEG)
        for h in range(N_KV_HEADS):
            kh = k[w0:w0 + n_keys, h * HEAD_DIM:(h + 1) * HEAD_DIM]
            rhs = jnp.concatenate(
                [qt[(h * GROUP + g) * HEAD_DIM:(h * GROUP + g + 1) * HEAD_DIM, w0:w0 + PAIR]
                 for g in range(GROUP)], axis=1)
            st = jnp.dot(kh, rhs, preferred_element_type=F32) + bias
            gcol = lax.shift_right_logical(lax.broadcasted_iota(jnp.int32, (1, n_cols), 1),
                                           PAIR.bit_length() - 1)
            sink = jnp.full((1, n_cols), sink_ref[h * GROUP], F32)
            for g in range(1, GROUP):
                sink = jnp.where(gcol == g, sink_ref[h * GROUP + g], sink)
            m = jnp.maximum(jnp.max(st, axis=0, keepdims=True), sink)
            pt = jnp.exp(st - m)
            denom = jnp.sum(pt, axis=0, keepdims=True) + jnp.exp(sink - m)
            ot = jnp.dot(vt[h * HEAD_DIM:(h + 1) * HEAD_DIM, w0:w0 + n_keys], pt.astype(BF16),
                         preferred_element_type=F32) / denom
            for g in range(GROUP):
                r0 = (h * GROUP + g) * HEAD_DIM
                at_sc[r0:r0 + HEAD_DIM, w0:w0 + PAIR] = ot[:, g * PAIR:(g + 1) * PAIR]
    a_ref[...] = at_sc[...].T.astype(BF16)


def _attn_prompt(sinks, q, kv):
    r = TQ_ATTN // WINDOW
    return pl.pallas_call(
        _attn_prompt_body,
        name="attn_prompt",
        grid=(T_P // TQ_ATTN,),
        in_specs=[
            pl.BlockSpec(memory_space=pltpu.SMEM),
            pl.BlockSpec((TQ_ATTN, ATTN_WIDTH), lambda i: (i, 0)),
            pl.BlockSpec((WINDOW, 2 * KV_WIDTH), lambda i: (jnp.maximum(i * r - 1, 0), 0)),
            pl.BlockSpec((TQ_ATTN, 2 * KV_WIDTH), lambda i: (i, 0)),
        ],
        out_specs=pl.BlockSpec((TQ_ATTN, ATTN_WIDTH), lambda i: (i, 0)),
        out_shape=jax.ShapeDtypeStruct((T_P, ATTN_WIDTH), BF16),
        scratch_shapes=[pltpu.VMEM((ATTN_WIDTH, TQ_ATTN), F32)],
        compiler_params=_params(1),
    )(sinks, q, kv, kv)


def _attn_sample_body(sink_ref, q_ref, kvc_ref, ck_ref, cv_ref, a_ref):
    kvc = kvc_ref[...]
    kw = jnp.concatenate([ck_ref[0], kvc[:, :KV_WIDTH]], axis=0).astype(BF16)
    vw = jnp.concatenate([cv_ref[0], kvc[:, KV_WIDTH:]], axis=0).astype(BF16)
    qc = q_ref[...]
    for h in range(N_KV_HEADS):
        kh = kw[:, h * HEAD_DIM:(h + 1) * HEAD_DIM]
        vh = vw[:, h * HEAD_DIM:(h + 1) * HEAD_DIM]
        qg = jnp.concatenate(
            [qc[:, (h * GROUP + g) * HEAD_DIM:(h * GROUP + g + 1) * HEAD_DIM] for g in range(GROUP)],
            axis=0)
        s = lax.dot_general(qg, kh, (((1,), (1,)), ((), ())), preferred_element_type=F32)
        sink = jnp.concatenate(
            [jnp.full((L_S, 1), sink_ref[h * GROUP + g], F32) for g in range(GROUP)], axis=0)
        m = jnp.maximum(jnp.max(s, axis=1, keepdims=True), sink)
        p = jnp.exp(s - m)
        denom = jnp.sum(p, axis=1, keepdims=True) + jnp.exp(sink - m)
        o = jnp.dot(p.astype(BF16), vh, preferred_element_type=F32) / denom
        for g in range(GROUP):
            c0 = (h * GROUP + g) * HEAD_DIM
            a_ref[:, c0:c0 + HEAD_DIM] = o[g * L_S:(g + 1) * L_S].astype(BF16)


def _attn_sample(sinks, q, kv, cache_k, cache_v):
    return pl.pallas_call(
        _attn_sample_body,
        name="attn_sample",
        grid=(N_STREAMS,),
        in_specs=[
            pl.BlockSpec(memory_space=pltpu.SMEM),
            pl.BlockSpec((L_S, ATTN_WIDTH), lambda b: (b, 0)),
            pl.BlockSpec((L_S, 2 * KV_WIDTH), lambda b: (b, 0)),
            pl.BlockSpec((1, WINDOW, KV_WIDTH), lambda b: (b, 0, 0)),
            pl.BlockSpec((1, WINDOW, KV_WIDTH), lambda b: (b, 0, 0)),
        ],
        out_specs=pl.BlockSpec((L_S, ATTN_WIDTH), lambda b: (b, 0)),
        out_shape=jax.ShapeDtypeStruct((T_S, ATTN_WIDTH), BF16),
        compiler_params=_params(1),
    )(sinks, q, kv, cache_k, cache_v)


def _conv_tile(prev, cur_ref, w_ref, b_ref, lg_ref, lb_ref, o_ref, ext_sc, sh_sc, rows):
    ext_sc[0:HIST_PAD, :] = prev
    ext_sc[HIST_PAD:HIST_PAD + rows, :] = cur_ref[...]
    ext_sc[HIST_PAD + rows:, :] = jnp.zeros((SUBLANES, CONV_CH), F32)
    lead = HIST_PAD - HIST
    span = rows + HIST_PAD - SUBLANES
    for s in range(SUBLANES):
        sh_sc[s, 0:span, :] = ext_sc[lead + s:lead + s + span, :]
    for r0 in range(0, rows, SUB_CONV):
        acc = jnp.zeros((SUB_CONV, CONV_CH), F32) + b_ref[...]
        for k in range(CONV_WIDTH):
            a, s = divmod(k, SUBLANES)
            r = r0 + a * SUBLANES
            acc = acc + w_ref[k:k + 1, :] * sh_sc[s, r:r + SUB_CONV, :]
        mu = jnp.mean(acc, axis=-1, keepdims=True)
        d = acc - mu
        var = jnp.mean(d * d, axis=-1, keepdims=True)
        yn = d * lax.rsqrt(var + EPS) * lg_ref[...] + lb_ref[...]
        o_ref[r0:r0 + SUB_CONV, :] = (yn * _sigmoid(yn)).astype(BF16)


def _conv_prompt_body(prev_ref, cur_ref, w_ref, b_ref, lg_ref, lb_ref, o_ref, ext_sc, sh_sc):
    i = pl.program_id(0)
    prev = jnp.where(i > 0, prev_ref[...], 0.0)
    _conv_tile(prev, cur_ref, w_ref, b_ref, lg_ref, lb_ref, o_ref, ext_sc, sh_sc, TR_CONV)


def _conv_sample_body(prev_ref, cur_ref, w_ref, b_ref, lg_ref, lb_ref, o_ref, ext_sc, sh_sc):
    _conv_tile(prev_ref[0], cur_ref, w_ref, b_ref, lg_ref, lb_ref, o_ref, ext_sc, sh_sc, L_S)


def _conv_consts_specs(index_map):
    return [
        pl.BlockSpec((CONV_WIDTH, CONV_CH), index_map),
        pl.BlockSpec((1, CONV_CH), index_map),
        pl.BlockSpec((1, CONV_CH), index_map),
        pl.BlockSpec((1, CONV_CH), index_map),
    ]


def _conv_scratch(rows):
    return [pltpu.VMEM((HIST_PAD + rows + SUBLANES, CONV_CH), F32),
            pltpu.VMEM((SUBLANES, HIST_PAD + rows, CONV_CH), F32)]


def _conv_prompt(u, conv_dw, conv_b, ln_g, ln_b):
    r = TR_CONV // HIST_PAD
    return pl.pallas_call(
        _conv_prompt_body,
        name="conv_prompt",
        grid=(T_P // TR_CONV,),
        in_specs=[
            pl.BlockSpec((HIST_PAD, CONV_CH), lambda i: (jnp.maximum(i * r - 1, 0), 0)),
            pl.BlockSpec((TR_CONV, CONV_CH), lambda i: (i, 0)),
        ] + _conv_consts_specs(lambda i: (0, 0)),
        out_specs=pl.BlockSpec((TR_CONV, CONV_CH), lambda i: (i, 0)),
        out_shape=jax.ShapeDtypeStruct((T_P, CONV_CH), BF16),
        scratch_shapes=_conv_scratch(TR_CONV),
        compiler_params=_params(1),
    )(u, u, conv_dw, conv_b, ln_g, ln_b)


def _conv_sample(hist_pad, u, conv_dw, conv_b, ln_g, ln_b):
    return pl.pallas_call(
        _conv_sample_body,
        name="conv_sample",
        grid=(N_STREAMS,),
        in_specs=[
            pl.BlockSpec((1, HIST_PAD, CONV_CH), lambda b: (b, 0, 0)),
            pl.BlockSpec((L_S, CONV_CH), lambda b: (b, 0)),
        ] + _conv_consts_specs(lambda b: (0, 0)),
        out_specs=pl.BlockSpec((L_S, CONV_CH), lambda b: (b, 0)),
        out_shape=jax.ShapeDtypeStruct((T_S, CONV_CH), BF16),
        scratch_shapes=_conv_scratch(L_S),
        compiler_params=_params(1),
    )(hist_pad, u, conv_dw, conv_b, ln_g, ln_b)


def _mix_body(ap_ref, as_ref, cp_ref, cs_ref, gp_ref, gs_ref, xp_ref, xs_ref, wao_ref, wco_ref,
              wout_ref, g2_ref, wr_ref, br_ref, x1_ref, xn_ref, lg_ref):
    m = pl.program_id(0)
    is_sample = m >= T_P // TM_MIX
    a = jnp.where(is_sample, as_ref[...], ap_ref[...])
    c = jnp.where(is_sample, cs_ref[...], cp_ref[...])
    gate = jnp.where(is_sample, gs_ref[...], gp_ref[...])
    x = jnp.where(is_sample, xs_ref[...], xp_ref[...])
    pa = jnp.dot(a, wao_ref[...], preferred_element_type=F32)
    pc = jnp.dot(c, wco_ref[...], preferred_element_type=F32)
    y = (gate[:, :D_MODEL].astype(F32) * pa + gate[:, D_MODEL:].astype(F32) * pc).astype(BF16)
    x1 = x + jnp.dot(y, wout_ref[...], preferred_element_type=F32)
    x1_ref[...] = x1
    ms = jnp.mean(x1 * x1, axis=-1, keepdims=True)
    xn = x1 * lax.rsqrt(ms + EPS) * g2_ref[...]
    _store_packed_rows(xn_ref, xn)
    xn_hi = xn.astype(BF16)
    xn_lo = (xn - xn_hi.astype(F32)).astype(BF16)
    r_hi = jnp.dot(xn_hi, wr_ref[...], preferred_element_type=F32)
    r_lo = jnp.dot(xn_lo, wr_ref[...], preferred_element_type=F32)
    lg_ref[...] = r_hi + pltpu.roll(r_hi, ROUTER_PAD - N_ROUTER, axis=1) + r_lo + br_ref[...]


def _mix(a_p, a_s, c_p, c_s, g_p, g_s, x_p, x_s, wao, wco, wout, g2, wr, br):
    n_p = T_P // TM_MIX
    n_s = T_S // TM_MIX
    prompt_map = lambda m: (jnp.minimum(m, n_p - 1), 0)
    sample_map = lambda m: (jnp.clip(m - n_p, 0, n_s - 1), 0)
    row_map = lambda m: (m, 0)
    const_map = lambda m: (0, 0)
    once = pl.Buffered(1)

    def pair(width):
        return [pl.BlockSpec((TM_MIX, width), prompt_map), pl.BlockSpec((TM_MIX, width), sample_map)]

    return pl.pallas_call(
        _mix_body,
        name="mix",
        grid=(T // TM_MIX,),
        in_specs=pair(ATTN_WIDTH) + pair(CONV_CH) + pair(2 * D_MODEL) + pair(D_MODEL) + [
            pl.BlockSpec((ATTN_WIDTH, D_MODEL), const_map, pipeline_mode=once),
            pl.BlockSpec((CONV_CH, D_MODEL), const_map, pipeline_mode=once),
            pl.BlockSpec((D_MODEL, D_MODEL), const_map, pipeline_mode=once),
            pl.BlockSpec((1, D_MODEL), const_map),
            pl.BlockSpec((D_MODEL, ROUTER_PAD), const_map),
            pl.BlockSpec((1, ROUTER_PAD), const_map),
        ],
        out_specs=[
            pl.BlockSpec((TM_MIX, D_MODEL), row_map),
            pl.BlockSpec((TM_MIX * ROW_TILE, LANES), row_map),
            pl.BlockSpec((TM_MIX, ROUTER_PAD), row_map),
        ],
        out_shape=[
            jax.ShapeDtypeStruct((T, D_MODEL), F32),
            jax.ShapeDtypeStruct((T * ROW_TILE, LANES), U32),
            jax.ShapeDtypeStruct((T, ROUTER_PAD), F32),
        ],
        compiler_params=_params(1),
    )(a_p, a_s, c_p, c_s, g_p, g_s, x_p, x_s, wao, wco, wout, g2, wr, br)


def _route(logits):
    gl = logits[:, :N_GROUPS]
    gp = jax.nn.softmax(gl, axis=-1)
    grp = jnp.argmax(gl, axis=-1)
    p_grp = jnp.take_along_axis(gp, grp[:, None], axis=1)[:, 0]
    el = logits[:, N_GROUPS:N_ROUTER].reshape(T, N_GROUPS, EXPERTS_PER_GROUP)
    el = jnp.take_along_axis(el, grp[:, None, None], axis=1)[:, 0]
    top_p, top_i = lax.top_k(jax.nn.softmax(el, axis=-1), TOP_K)
    wts = p_grp[:, None] * top_p / jnp.sum(top_p, axis=-1, keepdims=True)
    eid = (grp[:, None] * EXPERTS_PER_GROUP + top_i).astype(jnp.int32).reshape(-1)

    onehot = (eid[:, None] == jnp.arange(N_EXPERTS, dtype=jnp.int32)[None, :]).astype(jnp.int32)
    csum = jnp.cumsum(onehot, axis=0)
    counts = csum[-1]
    rank = jnp.sum(csum * onehot, axis=1) - 1
    nblk = (counts + BM_MOE - 1) // BM_MOE
    blk_end = jnp.cumsum(nblk)
    blk_start = blk_end - nblk
    row_start = jnp.cumsum(counts) - counts
    pos = ((blk_start[eid] * BM_MOE + rank) * ROW_TILE).astype(jnp.int32)

    order = jnp.argsort(eid, stable=True)
    tok_sorted = jnp.concatenate([((order // TOP_K) * ROW_TILE).astype(jnp.int32),
                                  jnp.zeros((BM_MOE,), jnp.int32)])
    n_used = blk_end[-1]
    j = jnp.minimum(jnp.arange(N_BLK_MOE, dtype=jnp.int32), n_used - 1)
    blk_e = jnp.minimum(jnp.searchsorted(blk_end, j, side='right'), N_EXPERTS - 1).astype(jnp.int32)
    blk_src = (row_start[blk_e] + (j - blk_start[blk_e]) * BM_MOE).astype(jnp.int32)
    e_ids = jnp.arange(N_EXPERTS, dtype=jnp.int32)
    used = nblk > 0
    e_slot = ((jnp.cumsum(used.astype(jnp.int32)) - 1) % 2).astype(jnp.int32)
    nxt = lax.cummin(jnp.where(used, e_ids, N_EXPERTS), reverse=True)
    nxt = jnp.concatenate([nxt[1:], jnp.full((1,), N_EXPERTS, jnp.int32)])
    e_next = jnp.where(nxt < N_EXPERTS, nxt, e_ids).astype(jnp.int32)
    blk_first = (j == blk_start[blk_e]).astype(jnp.int32)
    plan = jnp.stack([blk_e, blk_src, blk_first, e_slot[blk_e], e_next[blk_e]])
    return wts, pos, tok_sorted, plan, n_used.astype(jnp.int32).reshape(1)


PLAN_E, PLAN_SRC, PLAN_FIRST, PLAN_WSLOT, PLAN_NEXT_E = range(5)


def _moe_body(plan, n_used, tok, x_hbm, wg_hbm, wu_hbm, wd_hbm, o_ref,
              xb, wg_st, wu_st, wd_st, wg_bf, wu_bf, wd_bf, sem_x, sem_w):
    j = pl.program_id(0)
    nu = n_used[0]

    def row_copy(src, slot, r):
        t8 = pl.multiple_of(tok[src + r], ROW_TILE)
        return pltpu.make_async_copy(x_hbm.at[pl.ds(t8, ROW_TILE)], xb.at[slot, _row_tile(r)],
                                     sem_x.at[slot])

    def block_wait(slot):
        pltpu.make_async_copy(x_hbm.at[pl.ds(0, BM_MOE * ROW_TILE)], xb.at[slot],
                              sem_x.at[slot]).wait()

    def weight_copies(e, ws):
        return (pltpu.make_async_copy(wg_hbm.at[e], wg_st.at[ws], sem_w.at[ws, 0]),
                pltpu.make_async_copy(wu_hbm.at[e], wu_st.at[ws], sem_w.at[ws, 1]),
                pltpu.make_async_copy(wd_hbm.at[e], wd_st.at[ws], sem_w.at[ws, 2]))

    @pl.when(j == 0)
    def _():
        for cp in weight_copies(plan[PLAN_E, 0], 0):
            cp.start()
        src = plan[PLAN_SRC, 0]

        def body(r, carry):
            row_copy(src, 0, r).start()
            return carry
        lax.fori_loop(0, BM_MOE, body, 0, unroll=8)

    @pl.when((j < nu) & (plan[PLAN_FIRST, jnp.minimum(j, N_BLK_MOE - 1)] == 1))
    def _():
        e = plan[PLAN_E, j]
        ws = plan[PLAN_WSLOT, j]
        e_next = plan[PLAN_NEXT_E, j]

        @pl.when(e_next != e)
        def _():
            for cp in weight_copies(e_next, 1 - ws):
                cp.start()

        for cp in weight_copies(e, ws):
            cp.wait()
        wg_bf[...] = wg_st[ws].astype(BF16)
        wu_bf[...] = wu_st[ws].astype(BF16)
        wd_bf[...] = wd_st[ws].astype(BF16)

    @pl.when(j < nu)
    def _():
        slot = j % 2
        block_wait(slot)
        nsrc = plan[PLAN_SRC, jnp.minimum(j + 1, N_BLK_MOE - 1)]
        halves = [_load_packed_rows(xb.at[slot], BM_MOE, s) for s in range(ROW_TILE)]
        x = jnp.concatenate([lo for lo, _ in halves] + [hi for _, hi in halves], axis=1).astype(BF16)
        n_chunks = D_EXPERT // MXU_N
        rows_per_chunk = BM_MOE // n_chunks
        acc = None
        for c in range(n_chunks):
            for r in range(c * rows_per_chunk, (c + 1) * rows_per_chunk):
                row_copy(nsrc, 1 - slot, r).start()
            cols = slice(c * MXU_N, (c + 1) * MXU_N)
            g = jnp.dot(x, wg_bf[:, cols], preferred_element_type=F32)
            u = jnp.dot(x, wu_bf[:, cols], preferred_element_type=F32)
            h = (g * _sigmoid(g) * u).astype(BF16)
            part = jnp.dot(h, wd_bf[cols, :], preferred_element_type=F32)
            acc = part if acc is None else acc + part
        _store_packed_rows(o_ref, acc)

    @pl.when(j == nu)
    def _():
        block_wait(j % 2)

    @pl.when(j >= nu)
    def _():
        o_ref[...] = jnp.zeros_like(o_ref)


def _moe(plan, n_used, tok_sorted, xn_packed, w_g, w_u, w_d):
    grid_spec = pltpu.PrefetchScalarGridSpec(
        num_scalar_prefetch=3,
        grid=(N_BLK_MOE + 1,),
        in_specs=[pl.BlockSpec(memory_space=pl.ANY)] * 4,
        out_specs=pl.BlockSpec((BM_MOE * ROW_TILE, LANES), lambda j, pn, nu, tk: (j, 0)),
        scratch_shapes=[
            pltpu.VMEM((2, BM_MOE * ROW_TILE, LANES), U32),
            pltpu.VMEM((2, D_MODEL, D_EXPERT), F32),
            pltpu.VMEM((2, D_MODEL, D_EXPERT), F32),
            pltpu.VMEM((2, D_EXPERT, D_MODEL), F32),
            pltpu.VMEM((D_MODEL, D_EXPERT), BF16),
            pltpu.VMEM((D_MODEL, D_EXPERT), BF16),
            pltpu.VMEM((D_EXPERT, D_MODEL), BF16),
            pltpu.SemaphoreType.DMA((2,)),
            pltpu.SemaphoreType.DMA((2, 3)),
        ],
    )
    return pl.pallas_call(
        _moe_body,
        name="experts",
        grid_spec=grid_spec,
        out_shape=jax.ShapeDtypeStruct(((N_BLK_MOE + 1) * BM_MOE * ROW_TILE, LANES), U32),
        compiler_params=_params(1),
    )(plan, n_used, tok_sorted, xn_packed, w_g, w_u, w_d)


def _combine_body(pos, x1_ref, w_ref, gf_ref, o_hbm, y_ref, rb, sem, *, first_tile):
    m = pl.program_id(0)
    last = pl.num_programs(0) - 1

    def row_copy(base, slot, r, k):
        p8 = pl.multiple_of(pos[base + r * TOP_K + k], ROW_TILE)
        return pltpu.make_async_copy(o_hbm.at[pl.ds(p8, ROW_TILE)], rb.at[slot, k, _row_tile(r)],
                                     sem.at[slot])

    def tile_wait(slot):
        for k in range(TOP_K):
            pltpu.make_async_copy(o_hbm.at[pl.ds(0, TM_OUT * ROW_TILE)], rb.at[slot, k],
                                  sem.at[slot]).wait()

    def tile_base(mm):
        return (first_tile + mm) * (TM_OUT * TOP_K)

    @pl.when(m == 0)
    def _():
        base = tile_base(0)

        def body(r, carry):
            for k in range(TOP_K):
                row_copy(base, 0, r, k).start()
            return carry
        lax.fori_loop(0, TM_OUT, body, 0, unroll=4)

    slot = m % 2
    tile_wait(slot)
    nbase = tile_base(jnp.minimum(m + 1, last))
    rows = TM_OUT // ROW_TILE
    w0 = w_ref[:, 0:1]
    w1 = w_ref[:, 1:2]
    ss = jnp.zeros((TM_OUT, 1), F32)
    for s in range(ROW_TILE):
        for r in range(s * rows, (s + 1) * rows):
            for k in range(TOP_K):
                row_copy(nbase, 1 - slot, r, k).start()
        lo0, hi0 = _load_packed_rows(rb.at[slot, 0], TM_OUT, s)
        lo1, hi1 = _load_packed_rows(rb.at[slot, 1], TM_OUT, s)
        for off, r0, r1 in ((s * LANES, lo0, lo1), (HALF_D + s * LANES, hi0, hi1)):
            y = x1_ref[:, off:off + LANES] + w0 * r0 + w1 * r1
            ss = ss + jnp.sum(y * y, axis=-1, keepdims=True)
            y_ref[:, off:off + LANES] = y
    scale = lax.rsqrt(ss * (1.0 / D_MODEL) + EPS)
    y_ref[...] = y_ref[...] * scale * gf_ref[...]

    @pl.when(m == last)
    def _():
        tile_wait(1 - slot)


def _combine(pos, x1, wts, gf, out_sorted, first_tile, n_tiles):
    grid_spec = pltpu.PrefetchScalarGridSpec(
        num_scalar_prefetch=1,
        grid=(n_tiles,),
        in_specs=[
            pl.BlockSpec((TM_OUT, D_MODEL), lambda m, p: (first_tile + m, 0)),
            pl.BlockSpec((TM_OUT, TOP_K), lambda m, p: (first_tile + m, 0)),
            pl.BlockSpec((1, D_MODEL), lambda m, p: (0, 0)),
            pl.BlockSpec(memory_space=pl.ANY),
        ],
        out_specs=pl.BlockSpec((TM_OUT, D_MODEL), lambda m, p: (m, 0)),
        scratch_shapes=[
            pltpu.VMEM((2, TOP_K, TM_OUT * ROW_TILE, LANES), U32),
            pltpu.SemaphoreType.DMA((2,)),
        ],
    )
    return pl.pallas_call(
        functools.partial(_combine_body, first_tile=first_tile),
        name="combine",
        grid_spec=grid_spec,
        out_shape=jax.ShapeDtypeStruct((n_tiles * TM_OUT, D_MODEL), F32),
        compiler_params=_params(1),
    )(pos, x1, wts, gf, out_sorted)


def kernel(x_prompt, x_sample, cache_k, cache_v, state_conv, norm1_g, w_in, b_in, attn_sinks,
           w_attn_o, conv_dw, conv_dw_b, conv_ln_g, conv_ln_b, w_conv_o, w_out, norm2_g,
           w_router_group, b_router_group, w_router_expert, b_router_expert, w_e_gate, w_e_up,
           w_e_down, final_norm_g):
    x_p = x_prompt.reshape(T_P, D_MODEL)
    x_s = x_sample.reshape(T_S, D_MODEL)
    g1 = norm1_g[0][None, :]
    b1 = b_in[0][None, :]
    q_p, kv_p, u_p, gates_p = _in_proj(x_p, g1, w_in[0], b1, TM_IN)
    q_s, kv_s, u_s, gates_s = _in_proj(x_s, g1, w_in[0], b1, T_S)

    sinks = attn_sinks[0]
    a_p = _attn_prompt(sinks, q_p, kv_p)
    ck = cache_k[0].reshape(N_STREAMS, WINDOW, KV_WIDTH)
    cv = cache_v[0].reshape(N_STREAMS, WINDOW, KV_WIDTH)
    a_s = _attn_sample(sinks, q_s, kv_s, ck, cv)

    conv_consts = (conv_dw[0], conv_dw_b[0][None, :], conv_ln_g[0][None, :], conv_ln_b[0][None, :])
    c_p = _conv_prompt(u_p, *conv_consts)
    hist_pad = jnp.pad(state_conv[0], ((0, 0), (HIST_PAD - HIST, 0), (0, 0)))
    c_s = _conv_sample(hist_pad, u_s, *conv_consts)

    w_r = jnp.concatenate([w_router_group[0], w_router_expert[0]], axis=1)
    w_r_hi = w_r.astype(BF16)
    w_r_lo = (w_r - w_r_hi.astype(F32)).astype(BF16)
    w_r_cat = jnp.concatenate(
        [w_r_hi, w_r_lo, jnp.zeros((D_MODEL, ROUTER_PAD - 2 * N_ROUTER), BF16)], axis=1)
    b_r = jnp.concatenate([b_router_group[0], b_router_expert[0],
                           jnp.zeros((ROUTER_PAD - N_ROUTER,), F32)])[None, :]

    x1, xn2, logits = _mix(a_p, a_s, c_p, c_s, gates_p, gates_s, x_p, x_s,
                           w_attn_o[0].astype(BF16), w_conv_o[0].astype(BF16), w_out[0].astype(BF16),
                           norm2_g[0][None, :], w_r_cat, b_r)

    wts, pos, tok_sorted, plan, n_used = _route(logits)
    out_sorted = _moe(plan, n_used, tok_sorted, xn2, w_e_gate[0], w_e_up[0], w_e_down[0])
    gf = final_norm_g[None, :]
    y_p = _combine(pos, x1, wts, gf, out_sorted, 0, T_P // TM_OUT)
    y_s = _combine(pos, x1, wts, gf, out_sorted, T_P // TM_OUT, T_S // TM_OUT)

    kv_shape = (1, -1, WINDOW, N_KV_HEADS, HEAD_DIM)
    new_k_prompt = kv_p[T_P - WINDOW:, :KV_WIDTH].reshape(kv_shape)
    new_v_prompt = kv_p[T_P - WINDOW:, KV_WIDTH:].reshape(kv_shape)
    new_conv_prompt = u_p[T_P - HIST:].reshape(1, 1, HIST, CONV_CH)
    k_s = kv_s[:, :KV_WIDTH].reshape(N_STREAMS, L_S, KV_WIDTH)
    v_s = kv_s[:, KV_WIDTH:].reshape(N_STREAMS, L_S, KV_WIDTH)
    new_k_sample = jnp.concatenate([ck[:, L_S:], k_s], axis=1).reshape(kv_shape)
    new_v_sample = jnp.concatenate([cv[:, L_S:], v_s], axis=1).reshape(kv_shape)
    new_conv_sample = u_s.reshape(N_STREAMS, L_S, CONV_CH)[:, L_S - HIST:].reshape(
        1, N_STREAMS, HIST, CONV_CH)

    return (y_p.reshape(1, T_P, D_MODEL), y_s.reshape(N_STREAMS, L_S, D_MODEL),
            new_k_prompt, new_v_prompt, new_conv_prompt,
            new_k_sample, new_v_sample, new_conv_sample)
```

```python
import functools

import numpy as np
import jax
import jax.numpy as jnp
from jax import lax
from jax.experimental import pallas as pl
from jax.experimental.pallas import tpu as pltpu

F32 = jnp.float32
BF16 = jnp.bfloat16
U32 = jnp.uint32

D_MODEL = 2048
T_P = 8192
N_STREAMS = 8
L_S = 64
T_S = N_STREAMS * L_S
T = T_P + T_S
CHUNK = 64
WINDOW = 128
HEAD_DIM = 64
N_Q_HEADS = 16
N_KV_HEADS = 4
GROUP = N_Q_HEADS // N_KV_HEADS
ATTN_WIDTH = N_Q_HEADS * HEAD_DIM
KV_WIDTH = N_KV_HEADS * HEAD_DIM
CONV_CH = 1024
CONV_WIDTH = 31
HIST = CONV_WIDTH - 1
HIST_PAD = 32
SUBLANES = 8
LANES = 128
ROW_TILE = SUBLANES
HALF_D = D_MODEL // 2
HI_MASK = np.uint32(0xFFFF0000)
OFF_KV = ATTN_WIDTH
OFF_GLU = OFF_KV + 2 * KV_WIDTH
OFF_GATES = OFF_GLU + 2 * CONV_CH
IN_WIDTH = OFF_GATES + 2 * D_MODEL
N_GROUPS = 8
EXPERTS_PER_GROUP = 4
N_EXPERTS = N_GROUPS * EXPERTS_PER_GROUP
TOP_K = 2
D_EXPERT = 512
N_ROUTER = N_GROUPS + N_EXPERTS
ROUTER_PAD = 128
EPS = 1e-6
NEG = -1e30

VMEM_LIMIT = 56 * 1024 * 1024
MXU_N = 256

TM_IN = 1024
TM_WIDE = 2048
TN_IN = 512
TN_GATES = TN_IN
N_GATE_TILES = 2 * D_MODEL // TN_GATES
ROW_CHUNK = 256
W_STAGE_ROWS = 256
TQ_ATTN = 256
PAIR = 2 * CHUNK
SUB_CONV = 32
TM_MIX = 256
BM_MOE = 256
TM_OUT = 256
N_SLOTS = T * TOP_K
N_BLK_MOE = -(-(N_SLOTS + N_EXPERTS * (BM_MOE - 1)) // BM_MOE)


def _sigmoid(x):
    return 1.0 / (1.0 + jnp.exp(-x))


def _params(n_axes):
    return pltpu.CompilerParams(dimension_semantics=("arbitrary",) * n_axes,
                                vmem_limit_bytes=VMEM_LIMIT)


def _store_packed_rows(ref, y):
    rows = y.shape[0]
    for s in range(ROW_TILE):
        lo = y[:, s * LANES:(s + 1) * LANES].astype(BF16).astype(F32)
        hi = y[:, HALF_D + s * LANES:HALF_D + (s + 1) * LANES].astype(BF16).astype(F32)
        word = (lax.bitcast_convert_type(hi, U32) & HI_MASK) | (lax.bitcast_convert_type(lo, U32) >> 16)
        ref[pl.ds(s, rows, stride=ROW_TILE), :] = word


def _row_tile(r):
    start = r * ROW_TILE
    return pl.ds(start if isinstance(r, int) else pl.multiple_of(start, ROW_TILE), ROW_TILE)


def _load_packed_rows(ref, rows, s):
    word = ref[pl.ds(s, rows, stride=ROW_TILE), :]
    lo = lax.bitcast_convert_type(word << 16, F32)
    hi = lax.bitcast_convert_type(word & HI_MASK, F32)
    return lo, hi


def _chunk_dot(xn, w_ref, b_ref, c):
    w = w_ref[:, c * MXU_N:(c + 1) * MXU_N].astype(BF16)
    return jnp.dot(xn, w, preferred_element_type=F32) + b_ref[:, c * MXU_N:(c + 1) * MXU_N]


def _qkv_body(x_ref, g_ref, w_hbm, b_ref, xn_ref, q_ref, kv_ref, w_bf, stage, sem, *, tm):
    n_stage = D_MODEL // W_STAGE_ROWS

    @pl.when(pl.program_id(0) == 0)
    def _():
        def chunk_copy(i):
            return pltpu.make_async_copy(
                w_hbm.at[pl.ds(i * W_STAGE_ROWS, W_STAGE_ROWS), pl.ds(0, OFF_GLU)],
                stage.at[i % 2], sem.at[i % 2])
        chunk_copy(0).start()
        for i in range(n_stage):
            if i + 1 < n_stage:
                chunk_copy(i + 1).start()
            chunk_copy(i).wait()
            w_bf[i * W_STAGE_ROWS:(i + 1) * W_STAGE_ROWS, :] = stage[i % 2].astype(BF16)

    for r0 in range(0, tm, ROW_CHUNK):
        rows = slice(r0, r0 + ROW_CHUNK)
        x = x_ref[rows, :]
        ms = jnp.mean(x * x, axis=-1, keepdims=True)
        xn = (x * lax.rsqrt(ms + EPS) * g_ref[...]).astype(BF16)
        xn_ref[rows, :] = xn
        for c in range(OFF_GLU // MXU_N):
            cols = slice(c * MXU_N, (c + 1) * MXU_N)
            acc = jnp.dot(xn, w_bf[:, cols], preferred_element_type=F32) + b_ref[:, cols]
            if c < ATTN_WIDTH // MXU_N:
                q_ref[rows, cols] = (acc * (HEAD_DIM ** -0.5)).astype(BF16)
            else:
                kv_ref[rows, c * MXU_N - ATTN_WIDTH:(c + 1) * MXU_N - ATTN_WIDTH] = acc


def _qkv(x, norm_g, w_in, b_in, tm):
    t = x.shape[0]
    return pl.pallas_call(
        functools.partial(_qkv_body, tm=tm),
        name="qkv",
        grid=(t // tm,),
        in_specs=[
            pl.BlockSpec((tm, D_MODEL), lambda m: (m, 0)),
            pl.BlockSpec((1, D_MODEL), lambda m: (0, 0)),
            pl.BlockSpec(memory_space=pl.ANY),
            pl.BlockSpec((1, OFF_GLU), lambda m: (0, 0)),
        ],
        out_specs=[
            pl.BlockSpec((tm, D_MODEL), lambda m: (m, 0)),
            pl.BlockSpec((tm, ATTN_WIDTH), lambda m: (m, 0)),
            pl.BlockSpec((tm, 2 * KV_WIDTH), lambda m: (m, 0)),
        ],
        out_shape=[
            jax.ShapeDtypeStruct((t, D_MODEL), BF16),
            jax.ShapeDtypeStruct((t, ATTN_WIDTH), BF16),
            jax.ShapeDtypeStruct((t, 2 * KV_WIDTH), F32),
        ],
        scratch_shapes=[
            pltpu.VMEM((D_MODEL, OFF_GLU), BF16),
            pltpu.VMEM((2, W_STAGE_ROWS, OFF_GLU), F32),
            pltpu.SemaphoreType.DMA((2,)),
        ],
        compiler_params=_params(1),
    )(x, norm_g, w_in, b_in)


def _glu_body(xn_ref, wa_ref, ba_ref, wb_ref, bb_ref, u_ref):
    xn = xn_ref[...]
    for c in range(TN_IN // MXU_N):
        a = _chunk_dot(xn, wa_ref, ba_ref, c)
        b = _chunk_dot(xn, wb_ref, bb_ref, c)
        u_ref[:, c * MXU_N:(c + 1) * MXU_N] = a * _sigmoid(b)


def _glu(xn, w_in, b_in, tm):
    t = xn.shape[0]
    a0 = OFF_GLU // TN_IN
    b0 = (OFF_GLU + CONV_CH) // TN_IN
    return pl.pallas_call(
        _glu_body,
        name="glu",
        grid=(t // tm, CONV_CH // TN_IN),
        in_specs=[
            pl.BlockSpec((tm, D_MODEL), lambda m, n: (m, 0)),
            pl.BlockSpec((D_MODEL, TN_IN), lambda m, n: (0, a0 + n)),
            pl.BlockSpec((1, TN_IN), lambda m, n: (0, a0 + n)),
            pl.BlockSpec((D_MODEL, TN_IN), lambda m, n: (0, b0 + n)),
            pl.BlockSpec((1, TN_IN), lambda m, n: (0, b0 + n)),
        ],
        out_specs=pl.BlockSpec((tm, TN_IN), lambda m, n: (m, n)),
        out_shape=jax.ShapeDtypeStruct((t, CONV_CH), F32),
        compiler_params=_params(2),
    )(xn, w_in, b_in, w_in, b_in)


def _attn_prompt_body(sink_ref, q_ref, kvp_ref, kvc_ref, a_ref, at_sc):
    i = pl.program_id(0)
    kv = jnp.concatenate([kvp_ref[...], kvc_ref[...]], axis=0)
    k = kv[:, :KV_WIDTH].astype(BF16)
    vt = kv[:, KV_WIDTH:].T.astype(BF16)
    qt = q_ref[...].astype(F32).T.astype(BF16)
    n_cols = GROUP * PAIR
    n_keys = WINDOW + PAIR
    for p in range(TQ_ATTN // PAIR):
        w0 = p * PAIR
        row = lax.broadcasted_iota(jnp.int32, (n_keys, n_cols), 0)
        col = lax.broadcasted_iota(jnp.int32, (n_keys, n_cols), 1)
        first = jnp.where((col & (PAIR - 1)) >= CHUNK, CHUNK, 0)
        pos = row + (i * TQ_ATTN + w0 - WINDOW)
        ok = (row >= first) & (row < first + WINDOW + CHUNK) & (pos >= 0)
        bias = jnp.where(ok, 0.0, NEG)
        for h in range(N_KV_HEADS):
            kh = k[w0:w0 + n_keys, h * HEAD_DIM:(h + 1) * HEAD_DIM]
            rhs = jnp.concatenate(
                [qt[(h * GROUP + g) * HEAD_DIM:(h * GROUP + g + 1) * HEAD_DIM, w0:w0 + PAIR]
                 for g in range(GROUP)], axis=1)
            st = jnp.dot(kh, rhs, preferred_element_type=F32) + bias
            gcol = lax.shift_right_logical(lax.broadcasted_iota(jnp.int32, (1, n_cols), 1),
                                           PAIR.bit_length() - 1)
            sink = jnp.full((1, n_cols), sink_ref[h * GROUP], F32)
            for g in range(1, GROUP):
                sink = jnp.where(gcol == g, sink_ref[h * GROUP + g], sink)
            m = jnp.maximum(jnp.max(st, axis=0, keepdims=True), sink)
            pt = jnp.exp(st - m)
            denom = jnp.sum(pt, axis=0, keepdims=True) + jnp.exp(sink - m)
            ot = jnp.dot(vt[h * HEAD_DIM:(h + 1) * HEAD_DIM, w0:w0 + n_keys], pt.astype(BF16),
                         preferred_element_type=F32) / denom
            for g in range(GROUP):
                r0 = (h * GROUP + g) * HEAD_DIM
                at_sc[r0:r0 + HEAD_DIM, w0:w0 + PAIR] = ot[:, g * PAIR:(g + 1) * PAIR]
    a_ref[...] = at_sc[...].T.astype(BF16)


def _attn_prompt(sinks, q, kv):
    r = TQ_ATTN // WINDOW
    return pl.pallas_call(
        _attn_prompt_body,
        name="attn_prompt",
        grid=(T_P // TQ_ATTN,),
        in_specs=[
            pl.BlockSpec(memory_space=pltpu.SMEM),
            pl.BlockSpec((TQ_ATTN, ATTN_WIDTH), lambda i: (i, 0)),
            pl.BlockSpec((WINDOW, 2 * KV_WIDTH), lambda i: (jnp.maximum(i * r - 1, 0), 0)),
            pl.BlockSpec((TQ_ATTN, 2 * KV_WIDTH), lambda i: (i, 0)),
        ],
        out_specs=pl.BlockSpec((TQ_ATTN, ATTN_WIDTH), lambda i: (i, 0)),
        out_shape=jax.ShapeDtypeStruct((T_P, ATTN_WIDTH), BF16),
        scratch_shapes=[pltpu.VMEM((ATTN_WIDTH, TQ_ATTN), F32)],
        compiler_params=_params(1),
    )(sinks, q, kv, kv)


def _attn_sample_body(sink_ref, q_ref, kvc_ref, ck_ref, cv_ref, a_ref):
    kvc = kvc_ref[...]
    kw = jnp.concatenate([ck_ref[0], kvc[:, :KV_WIDTH]], axis=0).astype(BF16)
    vw = jnp.concatenate([cv_ref[0], kvc[:, KV_WIDTH:]], axis=0).astype(BF16)
    qc = q_ref[...]
    for h in range(N_KV_HEADS):
        kh = kw[:, h * HEAD_DIM:(h + 1) * HEAD_DIM]
        vh = vw[:, h * HEAD_DIM:(h + 1) * HEAD_DIM]
        qg = jnp.concatenate(
            [qc[:, (h * GROUP + g) * HEAD_DIM:(h * GROUP + g + 1) * HEAD_DIM] for g in range(GROUP)],
            axis=0)
        s = lax.dot_general(qg, kh, (((1,), (1,)), ((), ())), preferred_element_type=F32)
        sink = jnp.concatenate(
            [jnp.full((L_S, 1), sink_ref[h * GROUP + g], F32) for g in range(GROUP)], axis=0)
        m = jnp.maximum(jnp.max(s, axis=1, keepdims=True), sink)
        p = jnp.exp(s - m)
        denom = jnp.sum(p, axis=1, keepdims=True) + jnp.exp(sink - m)
        o = jnp.dot(p.astype(BF16), vh, preferred_element_type=F32) / denom
        for g in range(GROUP):
            c0 = (h * GROUP + g) * HEAD_DIM
            a_ref[:, c0:c0 + HEAD_DIM] = o[g * L_S:(g + 1) * L_S].astype(BF16)


def _attn_sample(sinks, q, kv, cache_k, cache_v):
    return pl.pallas_call(
        _attn_sample_body,
        name="attn_sample",
        grid=(N_STREAMS,),
        in_specs=[
            pl.BlockSpec(memory_space=pltpu.SMEM),
            pl.BlockSpec((L_S, ATTN_WIDTH), lambda b: (b, 0)),
            pl.BlockSpec((L_S, 2 * KV_WIDTH), lambda b: (b, 0)),
            pl.BlockSpec((1, WINDOW, KV_WIDTH), lambda b: (b, 0, 0)),
            pl.BlockSpec((1, WINDOW, KV_WIDTH), lambda b: (b, 0, 0)),
        ],
        out_specs=pl.BlockSpec((L_S, ATTN_WIDTH), lambda b: (b, 0)),
        out_shape=jax.ShapeDtypeStruct((T_S, ATTN_WIDTH), BF16),
        compiler_params=_params(1),
    )(sinks, q, kv, cache_k, cache_v)


def _conv_shifts(prev, cur_ref, ext_sc, sh_sc, rows):
    ext_sc[0:HIST_PAD, :] = prev
    ext_sc[HIST_PAD:HIST_PAD + rows, :] = cur_ref[...]
    ext_sc[HIST_PAD + rows:, :] = jnp.zeros((SUBLANES, CONV_CH), F32)
    lead = HIST_PAD - HIST
    span = rows + HIST_PAD - SUBLANES
    for s in range(SUBLANES):
        sh_sc[s, 0:span, :] = ext_sc[lead + s:lead + s + span, :]


def _conv_rows(r0, w_ref, b_ref, lg_ref, lb_ref, o_ref, sh_sc):
    acc = jnp.zeros((SUB_CONV, CONV_CH), F32) + b_ref[...]
    for k in range(CONV_WIDTH):
        a, s = divmod(k, SUBLANES)
        r = r0 + a * SUBLANES
        acc = acc + w_ref[k:k + 1, :] * sh_sc[s, r:r + SUB_CONV, :]
    mu = jnp.mean(acc, axis=-1, keepdims=True)
    d = acc - mu
    var = jnp.mean(d * d, axis=-1, keepdims=True)
    yn = d * lax.rsqrt(var + EPS) * lg_ref[...] + lb_ref[...]
    o_ref[r0:r0 + SUB_CONV, :] = (yn * _sigmoid(yn)).astype(BF16)


def _gates_conv_body(xn_ref, w_ref, b_ref, prev_ref, cur_ref, cw_ref, cb_ref, lg_ref, lb_ref,
                     gate_ref, c_ref, ext_sc, sh_sc, *, rows, fresh):
    if fresh:
        step = pl.program_id(0) * N_GATE_TILES + pl.program_id(1)
        prev = jnp.where(step > 0, prev_ref[...], 0.0)
    else:
        prev = prev_ref[0]
    _conv_shifts(prev, cur_ref, ext_sc, sh_sc, rows)
    n_chunks = TN_GATES // MXU_N
    w = [w_ref[:, c * MXU_N:(c + 1) * MXU_N].astype(BF16) for c in range(n_chunks)]
    for i in range(rows // SUB_CONV):
        _conv_rows(i * SUB_CONV, cw_ref, cb_ref, lg_ref, lb_ref, c_ref, sh_sc)
        r = slice(i * ROW_CHUNK, (i + 1) * ROW_CHUNK)
        xr = xn_ref[r, :]
        for c in range(n_chunks):
            cols = slice(c * MXU_N, (c + 1) * MXU_N)
            acc = jnp.dot(xr, w[c], preferred_element_type=F32) + b_ref[:, cols]
            gate_ref[r, cols] = _sigmoid(acc).astype(BF16)


def _gates_conv(xn, w_in, b_in, hist, u, conv_consts, tm, rows, fresh):
    t = xn.shape[0]
    n0 = OFF_GATES // TN_GATES
    tile = lambda m, n: m * N_GATE_TILES + n
    if fresh:
        r = rows // HIST_PAD
        prev_spec = pl.BlockSpec((HIST_PAD, CONV_CH), lambda m, n: (jnp.maximum(tile(m, n) * r - 1, 0), 0))
    else:
        prev_spec = pl.BlockSpec((1, HIST_PAD, CONV_CH), lambda m, n: (tile(m, n), 0, 0))
    const_map = lambda m, n: (0, 0)
    return pl.pallas_call(
        functools.partial(_gates_conv_body, rows=rows, fresh=fresh),
        name="gates_conv",
        grid=(t // tm, N_GATE_TILES),
        in_specs=[
            pl.BlockSpec((tm, D_MODEL), lambda m, n: (m, 0)),
            pl.BlockSpec((D_MODEL, TN_GATES), lambda m, n: (0, n0 + n)),
            pl.BlockSpec((1, TN_GATES), lambda m, n: (0, n0 + n)),
            prev_spec,
            pl.BlockSpec((rows, CONV_CH), lambda m, n: (tile(m, n), 0)),
            pl.BlockSpec((CONV_WIDTH, CONV_CH), const_map),
            pl.BlockSpec((1, CONV_CH), const_map),
            pl.BlockSpec((1, CONV_CH), const_map),
            pl.BlockSpec((1, CONV_CH), const_map),
        ],
        out_specs=[
            pl.BlockSpec((tm, TN_GATES), lambda m, n: (m, n)),
            pl.BlockSpec((rows, CONV_CH), lambda m, n: (tile(m, n), 0)),
        ],
        out_shape=[
            jax.ShapeDtypeStruct((t, 2 * D_MODEL), BF16),
            jax.ShapeDtypeStruct((t, CONV_CH), BF16),
        ],
        scratch_shapes=[
            pltpu.VMEM((HIST_PAD + rows + SUBLANES, CONV_CH), F32),
            pltpu.VMEM((SUBLANES, HIST_PAD + rows, CONV_CH), F32),
        ],
        compiler_params=_params(2),
    )(xn, w_in, b_in, hist, u, *conv_consts)


def _in_proj(x, hist, norm_g, w_in, b_in, conv_consts, tm, fresh):
    t = x.shape[0]
    xn, q, kv = _qkv(x, norm_g, w_in, b_in, tm)
    tw = min(TM_WIDE, t)
    u = _glu(xn, w_in, b_in, tw)
    gates, c = _gates_conv(xn, w_in, b_in, u if fresh else hist, u, conv_consts, tw,
                           tw // N_GATE_TILES, fresh)
    return q, kv, u, gates, c


def _mix_body(ap_ref, as_ref, cp_ref, cs_ref, gp_ref, gs_ref, xp_ref, xs_ref, wao_ref, wco_ref,
              wout_ref, g2_ref, wr_ref, br_ref, x1_ref, xn_ref, lg_ref):
    m = pl.program_id(0)
    is_sample = m >= T_P // TM_MIX
    a = jnp.where(is_sample, as_ref[...], ap_ref[...])
    c = jnp.where(is_sample, cs_ref[...], cp_ref[...])
    gate = jnp.where(is_sample, gs_ref[...], gp_ref[...])
    x = jnp.where(is_sample, xs_ref[...], xp_ref[...])
    pa = jnp.dot(a, wao_ref[...], preferred_element_type=F32)
    pc = jnp.dot(c, wco_ref[...], preferred_element_type=F32)
    y = (gate[:, :D_MODEL].astype(F32) * pa + gate[:, D_MODEL:].astype(F32) * pc).astype(BF16)
    x1 = x + jnp.dot(y, wout_ref[...], preferred_element_type=F32)
    x1_ref[...] = x1
    ms = jnp.mean(x1 * x1, axis=-1, keepdims=True)
    xn = x1 * lax.rsqrt(ms + EPS) * g2_ref[...]
    _store_packed_rows(xn_ref, xn)
    xn_hi = xn.astype(BF16)
    xn_lo = (xn - xn_hi.astype(F32)).astype(BF16)
    r_hi = jnp.dot(xn_hi, wr_ref[...], preferred_element_type=F32)
    r_lo = jnp.dot(xn_lo, wr_ref[...], preferred_element_type=F32)
    lg_ref[...] = r_hi + pltpu.roll(r_hi, ROUTER_PAD - N_ROUTER, axis=1) + r_lo + br_ref[...]


def _mix(a_p, a_s, c_p, c_s, g_p, g_s, x_p, x_s, wao, wco, wout, g2, wr, br):
    n_p = T_P // TM_MIX
    n_s = T_S // TM_MIX
    prompt_map = lambda m: (jnp.minimum(m, n_p - 1), 0)
    sample_map = lambda m: (jnp.clip(m - n_p, 0, n_s - 1), 0)
    row_map = lambda m: (m, 0)
    const_map = lambda m: (0, 0)
    once = pl.Buffered(1)

    def pair(width):
        return [pl.BlockSpec((TM_MIX, width), prompt_map), pl.BlockSpec((TM_MIX, width), sample_map)]

    return pl.pallas_call(
        _mix_body,
        name="mix",
        grid=(T // TM_MIX,),
        in_specs=pair(ATTN_WIDTH) + pair(CONV_CH) + pair(2 * D_MODEL) + pair(D_MODEL) + [
            pl.BlockSpec((ATTN_WIDTH, D_MODEL), const_map, pipeline_mode=once),
            pl.BlockSpec((CONV_CH, D_MODEL), const_map, pipeline_mode=once),
            pl.BlockSpec((D_MODEL, D_MODEL), const_map, pipeline_mode=once),
            pl.BlockSpec((1, D_MODEL), const_map),
            pl.BlockSpec((D_MODEL, ROUTER_PAD), const_map),
            pl.BlockSpec((1, ROUTER_PAD), const_map),
        ],
        out_specs=[
            pl.BlockSpec((TM_MIX, D_MODEL), row_map),
            pl.BlockSpec((TM_MIX * ROW_TILE, LANES), row_map),
            pl.BlockSpec((TM_MIX, ROUTER_PAD), row_map),
        ],
        out_shape=[
            jax.ShapeDtypeStruct((T, D_MODEL), F32),
            jax.ShapeDtypeStruct((T * ROW_TILE, LANES), U32),
            jax.ShapeDtypeStruct((T, ROUTER_PAD), F32),
        ],
        compiler_params=_params(1),
    )(a_p, a_s, c_p, c_s, g_p, g_s, x_p, x_s, wao, wco, wout, g2, wr, br)


def _route(logits):
    gl = logits[:, :N_GROUPS]
    gp = jax.nn.softmax(gl, axis=-1)
    grp = jnp.argmax(gl, axis=-1)
    p_grp = jnp.take_along_axis(gp, grp[:, None], axis=1)[:, 0]
    el = logits[:, N_GROUPS:N_ROUTER].reshape(T, N_GROUPS, EXPERTS_PER_GROUP)
    el = jnp.take_along_axis(el, grp[:, None, None], axis=1)[:, 0]
    top_p, top_i = lax.top_k(jax.nn.softmax(el, axis=-1), TOP_K)
    wts = p_grp[:, None] * top_p / jnp.sum(top_p, axis=-1, keepdims=True)
    eid = (grp[:, None] * EXPERTS_PER_GROUP + top_i).astype(jnp.int32).reshape(-1)

    onehot = (eid[:, None] == jnp.arange(N_EXPERTS, dtype=jnp.int32)[None, :]).astype(jnp.int32)
    csum = jnp.cumsum(onehot, axis=0)
    counts = csum[-1]
    rank = jnp.sum(csum * onehot, axis=1) - 1
    nblk = (counts + BM_MOE - 1) // BM_MOE
    blk_end = jnp.cumsum(nblk)
    blk_start = blk_end - nblk
    row_start = jnp.cumsum(counts) - counts
    pos = ((blk_start[eid] * BM_MOE + rank) * ROW_TILE).astype(jnp.int32)

    order = jnp.argsort(eid, stable=True)
    tok_sorted = jnp.concatenate([((order // TOP_K) * ROW_TILE).astype(jnp.int32),
                                  jnp.zeros((BM_MOE,), jnp.int32)])
    n_used = blk_end[-1]
    j = jnp.minimum(jnp.arange(N_BLK_MOE, dtype=jnp.int32), n_used - 1)
    blk_e = jnp.minimum(jnp.searchsorted(blk_end, j, side='right'), N_EXPERTS - 1).astype(jnp.int32)
    blk_src = (row_start[blk_e] + (j - blk_start[blk_e]) * BM_MOE).astype(jnp.int32)
    e_ids = jnp.arange(N_EXPERTS, dtype=jnp.int32)
    used = nblk > 0
    e_slot = ((jnp.cumsum(used.astype(jnp.int32)) - 1) % 2).astype(jnp.int32)
    nxt = lax.cummin(jnp.where(used, e_ids, N_EXPERTS), reverse=True)
    nxt = jnp.concatenate([nxt[1:], jnp.full((1,), N_EXPERTS, jnp.int32)])
    e_next = jnp.where(nxt < N_EXPERTS, nxt, e_ids).astype(jnp.int32)
    blk_first = (j == blk_start[blk_e]).astype(jnp.int32)
    plan = jnp.stack([blk_e, blk_src, blk_first, e_slot[blk_e], e_next[blk_e]])
    return wts, pos, tok_sorted, plan, n_used.astype(jnp.int32).reshape(1)


PLAN_E, PLAN_SRC, PLAN_FIRST, PLAN_WSLOT, PLAN_NEXT_E = range(5)
WEIGHT_DMA_PRIORITY = 1


def _moe_body(plan, n_used, tok, x_hbm, wg_hbm, wu_hbm, wd_hbm, o_ref,
              xb, wg_st, wu_st, wd_st, wg_bf, wu_bf, wd_bf, sem_x, sem_w):
    j = pl.program_id(0)
    nu = n_used[0]

    def row_copy(src, slot, r):
        t8 = pl.multiple_of(tok[src + r], ROW_TILE)
        return pltpu.make_async_copy(x_hbm.at[pl.ds(t8, ROW_TILE)], xb.at[slot, _row_tile(r)],
                                     sem_x.at[slot])

    def block_wait(slot):
        pltpu.make_async_copy(x_hbm.at[pl.ds(0, BM_MOE * ROW_TILE)], xb.at[slot],
                              sem_x.at[slot]).wait()

    def weight_copies(e, ws):
        return (pltpu.make_async_copy(wg_hbm.at[e], wg_st.at[ws], sem_w.at[ws, 0]),
                pltpu.make_async_copy(wu_hbm.at[e], wu_st.at[ws], sem_w.at[ws, 1]),
                pltpu.make_async_copy(wd_hbm.at[e], wd_st.at[ws], sem_w.at[ws, 2]))

    @pl.when(j == 0)
    def _():
        for cp in weight_copies(plan[PLAN_E, 0], 0):
            cp.start(priority=WEIGHT_DMA_PRIORITY)
        src = plan[PLAN_SRC, 0]

        def body(r, carry):
            row_copy(src, 0, r).start()
            return carry
        lax.fori_loop(0, BM_MOE, body, 0, unroll=8)

    @pl.when((j < nu) & (plan[PLAN_FIRST, jnp.minimum(j, N_BLK_MOE - 1)] == 1))
    def _():
        e = plan[PLAN_E, j]
        ws = plan[PLAN_WSLOT, j]
        e_next = plan[PLAN_NEXT_E, j]

        @pl.when(e_next != e)
        def _():
            for cp in weight_copies(e_next, 1 - ws):
                cp.start(priority=WEIGHT_DMA_PRIORITY)

        for cp in weight_copies(e, ws):
            cp.wait()
        wg_bf[...] = wg_st[ws].astype(BF16)
        wu_bf[...] = wu_st[ws].astype(BF16)
        wd_bf[...] = wd_st[ws].astype(BF16)

    @pl.when(j < nu)
    def _():
        slot = j % 2
        block_wait(slot)
        nsrc = plan[PLAN_SRC, jnp.minimum(j + 1, N_BLK_MOE - 1)]
        halves = [_load_packed_rows(xb.at[slot], BM_MOE, s) for s in range(ROW_TILE)]
        x = jnp.concatenate([lo for lo, _ in halves] + [hi for _, hi in halves], axis=1).astype(BF16)
        n_chunks = D_EXPERT // MXU_N
        rows_per_chunk = BM_MOE // n_chunks
        acc = None
        for c in range(n_chunks):
            for r in range(c * rows_per_chunk, (c + 1) * rows_per_chunk):
                row_copy(nsrc, 1 - slot, r).start()
            cols = slice(c * MXU_N, (c + 1) * MXU_N)
            g = jnp.dot(x, wg_bf[:, cols], preferred_element_type=F32)
            u = jnp.dot(x, wu_bf[:, cols], preferred_element_type=F32)
            h = (g * _sigmoid(g) * u).astype(BF16)
            part = jnp.dot(h, wd_bf[cols, :], preferred_element_type=F32)
            acc = part if acc is None else acc + part
        _store_packed_rows(o_ref, acc)

    @pl.when(j == nu)
    def _():
        block_wait(j % 2)

    @pl.when(j >= nu)
    def _():
        o_ref[...] = jnp.zeros_like(o_ref)


def _moe(plan, n_used, tok_sorted, xn_packed, w_g, w_u, w_d):
    grid_spec = pltpu.PrefetchScalarGridSpec(
        num_scalar_prefetch=3,
        grid=(N_BLK_MOE + 1,),
        in_specs=[pl.BlockSpec(memory_space=pl.ANY)] * 4,
        out_specs=pl.BlockSpec((BM_MOE * ROW_TILE, LANES), lambda j, pn, nu, tk: (j, 0)),
        scratch_shapes=[
            pltpu.VMEM((2, BM_MOE * ROW_TILE, LANES), U32),
            pltpu.VMEM((2, D_MODEL, D_EXPERT), F32),
            pltpu.VMEM((2, D_MODEL, D_EXPERT), F32),
            pltpu.VMEM((2, D_EXPERT, D_MODEL), F32),
            pltpu.VMEM((D_MODEL, D_EXPERT), BF16),
            pltpu.VMEM((D_MODEL, D_EXPERT), BF16),
            pltpu.VMEM((D_EXPERT, D_MODEL), BF16),
            pltpu.SemaphoreType.DMA((2,)),
            pltpu.SemaphoreType.DMA((2, 3)),
        ],
    )
    return pl.pallas_call(
        _moe_body,
        name="experts",
        grid_spec=grid_spec,
        out_shape=jax.ShapeDtypeStruct(((N_BLK_MOE + 1) * BM_MOE * ROW_TILE, LANES), U32),
        compiler_params=_params(1),
    )(plan, n_used, tok_sorted, xn_packed, w_g, w_u, w_d)


def _combine_body(pos, x1_ref, w_ref, gf_ref, o_hbm, y_ref, rb, sem, *, first_tile):
    m = pl.program_id(0)
    last = pl.num_programs(0) - 1

    def row_copy(base, slot, r, k):
        p8 = pl.multiple_of(pos[base + r * TOP_K + k], ROW_TILE)
        return pltpu.make_async_copy(o_hbm.at[pl.ds(p8, ROW_TILE)], rb.at[slot, k, _row_tile(r)],
                                     sem.at[slot])

    def tile_wait(slot):
        for k in range(TOP_K):
            pltpu.make_async_copy(o_hbm.at[pl.ds(0, TM_OUT * ROW_TILE)], rb.at[slot, k],
                                  sem.at[slot]).wait()

    def tile_base(mm):
        return (first_tile + mm) * (TM_OUT * TOP_K)

    @pl.when(m == 0)
    def _():
        base = tile_base(0)

        def body(r, carry):
            for k in range(TOP_K):
                row_copy(base, 0, r, k).start(priority=k)
            return carry
        lax.fori_loop(0, TM_OUT, body, 0, unroll=4)

    slot = m % 2
    tile_wait(slot)
    nbase = tile_base(jnp.minimum(m + 1, last))
    rows = TM_OUT // ROW_TILE
    w0 = w_ref[:, 0:1]
    w1 = w_ref[:, 1:2]
    ss = jnp.zeros((TM_OUT, 1), F32)
    for s in range(ROW_TILE):
        for r in range(s * rows, (s + 1) * rows):
            for k in range(TOP_K):
                row_copy(nbase, 1 - slot, r, k).start(priority=k)
        lo0, hi0 = _load_packed_rows(rb.at[slot, 0], TM_OUT, s)
        lo1, hi1 = _load_packed_rows(rb.at[slot, 1], TM_OUT, s)
        for off, r0, r1 in ((s * LANES, lo0, lo1), (HALF_D + s * LANES, hi0, hi1)):
            y = x1_ref[:, off:off + LANES] + w0 * r0 + w1 * r1
            ss = ss + jnp.sum(y * y, axis=-1, keepdims=True)
            y_ref[:, off:off + LANES] = y
    scale = lax.rsqrt(ss * (1.0 / D_MODEL) + EPS)
    y_ref[...] = y_ref[...] * scale * gf_ref[...]

    @pl.when(m == last)
    def _():
        tile_wait(1 - slot)


def _combine(pos, x1, wts, gf, out_sorted, first_tile, n_tiles):
    grid_spec = pltpu.PrefetchScalarGridSpec(
        num_scalar_prefetch=1,
        grid=(n_tiles,),
        in_specs=[
            pl.BlockSpec((TM_OUT, D_MODEL), lambda m, p: (first_tile + m, 0)),
            pl.BlockSpec((TM_OUT, TOP_K), lambda m, p: (first_tile + m, 0)),
            pl.BlockSpec((1, D_MODEL), lambda m, p: (0, 0)),
            pl.BlockSpec(memory_space=pl.ANY),
        ],
        out_specs=pl.BlockSpec((TM_OUT, D_MODEL), lambda m, p: (m, 0)),
        scratch_shapes=[
            pltpu.VMEM((2, TOP_K, TM_OUT * ROW_TILE, LANES), U32),
            pltpu.SemaphoreType.DMA((2,)),
        ],
    )
    return pl.pallas_call(
        functools.partial(_combine_body, first_tile=first_tile),
        name="combine",
        grid_spec=grid_spec,
        out_shape=jax.ShapeDtypeStruct((n_tiles * TM_OUT, D_MODEL), F32),
        compiler_params=_params(1),
    )(pos, x1, wts, gf, out_sorted)


def kernel(x_prompt, x_sample, cache_k, cache_v, state_conv, norm1_g, w_in, b_in, attn_sinks,
           w_attn_o, conv_dw, conv_dw_b, conv_ln_g, conv_ln_b, w_conv_o, w_out, norm2_g,
           w_router_group, b_router_group, w_router_expert, b_router_expert, w_e_gate, w_e_up,
           w_e_down, final_norm_g):
    x_p = x_prompt.reshape(T_P, D_MODEL)
    x_s = x_sample.reshape(T_S, D_MODEL)
    g1 = norm1_g[0][None, :]
    b1 = b_in[0][None, :]
    conv_consts = (conv_dw[0], conv_dw_b[0][None, :], conv_ln_g[0][None, :], conv_ln_b[0][None, :])
    hist_pad = jnp.pad(state_conv[0], ((0, 0), (HIST_PAD - HIST, 0), (0, 0)))
    q_p, kv_p, u_p, gates_p, c_p = _in_proj(x_p, None, g1, w_in[0], b1, conv_consts, TM_IN, True)
    q_s, kv_s, u_s, gates_s, c_s = _in_proj(x_s, hist_pad, g1, w_in[0], b1, conv_consts, T_S, False)

    sinks = attn_sinks[0]
    a_p = _attn_prompt(sinks, q_p, kv_p)
    ck = cache_k[0].reshape(N_STREAMS, WINDOW, KV_WIDTH)
    cv = cache_v[0].reshape(N_STREAMS, WINDOW, KV_WIDTH)
    a_s = _attn_sample(sinks, q_s, kv_s, ck, cv)

    w_r = jnp.concatenate([w_router_group[0], w_router_expert[0]], axis=1)
    w_r_hi = w_r.astype(BF16)
    w_r_lo = (w_r - w_r_hi.astype(F32)).astype(BF16)
    w_r_cat = jnp.concatenate(
        [w_r_hi, w_r_lo, jnp.zeros((D_MODEL, ROUTER_PAD - 2 * N_ROUTER), BF16)], axis=1)
    b_r = jnp.concatenate([b_router_group[0], b_router_expert[0],
                           jnp.zeros((ROUTER_PAD - N_ROUTER,), F32)])[None, :]

    x1, xn2, logits = _mix(a_p, a_s, c_p, c_s, gates_p, gates_s, x_p, x_s,
                           w_attn_o[0].astype(BF16), w_conv_o[0].astype(BF16), w_out[0].astype(BF16),
                           norm2_g[0][None, :], w_r_cat, b_r)

    wts, pos, tok_sorted, plan, n_used = _route(logits)
    out_sorted = _moe(plan, n_used, tok_sorted, xn2, w_e_gate[0], w_e_up[0], w_e_down[0])
    gf = final_norm_g[None, :]
    y_p = _combine(pos, x1, wts, gf, out_sorted, 0, T_P // TM_OUT)
    y_s = _combine(pos, x1, wts, gf, out_sorted, T_P // TM_OUT, T_S // TM_OUT)

    kv_shape = (1, -1, WINDOW, N_KV_HEADS, HEAD_DIM)
    new_k_prompt = kv_p[T_P - WINDOW:, :KV_WIDTH].reshape(kv_shape)
    new_v_prompt = kv_p[T_P - WINDOW:, KV_WIDTH:].reshape(kv_shape)
    new_conv_prompt = u_p[T_P - HIST:].reshape(1, 1, HIST, CONV_CH)
    k_s = kv_s[:, :KV_WIDTH].reshape(N_STREAMS, L_S, KV_WIDTH)
    v_s = kv_s[:, KV_WIDTH:].reshape(N_STREAMS, L_S, KV_WIDTH)
    new_k_sample = jnp.concatenate([ck[:, L_S:], k_s], axis=1).reshape(kv_shape)
    new_v_sample = jnp.concatenate([cv[:, L_S:], v_s], axis=1).reshape(kv_shape)
    new_conv_sample = u_s.reshape(N_STREAMS, L_S, CONV_CH)[:, L_S - HIST:].reshape(
        1, N_STREAMS, HIST, CONV_CH)

    return (y_p.reshape(1, T_P, D_MODEL), y_s.reshape(N_STREAMS, L_S, D_MODEL),
            new_k_prompt, new_v_prompt, new_conv_prompt,
            new_k_sample, new_v_sample, new_conv_sample)
```

```python
import functools

import numpy as np
import jax
import jax.numpy as jnp
from jax import lax
from jax.experimental import pallas as pl
from jax.experimental.pallas import tpu as pltpu

F32 = jnp.float32
BF16 = jnp.bfloat16
U32 = jnp.uint32

D_MODEL = 2048
T_P = 8192
N_STREAMS = 8
L_S = 64
T_S = N_STREAMS * L_S
T = T_P + T_S
CHUNK = 64
WINDOW = 128
HEAD_DIM = 64
N_Q_HEADS = 16
N_KV_HEADS = 4
GROUP = N_Q_HEADS // N_KV_HEADS
ATTN_WIDTH = N_Q_HEADS * HEAD_DIM
KV_WIDTH = N_KV_HEADS * HEAD_DIM
CONV_CH = 1024
CONV_WIDTH = 31
HIST = CONV_WIDTH - 1
HIST_PAD = 32
SUBLANES = 8
LANES = 128
ROW_TILE = SUBLANES
HALF_D = D_MODEL // 2
HI_MASK = np.uint32(0xFFFF0000)
OFF_KV = ATTN_WIDTH
OFF_GLU = OFF_KV + 2 * KV_WIDTH
OFF_GATES = OFF_GLU + 2 * CONV_CH
IN_WIDTH = OFF_GATES + 2 * D_MODEL
N_GROUPS = 8
EXPERTS_PER_GROUP = 4
N_EXPERTS = N_GROUPS * EXPERTS_PER_GROUP
TOP_K = 2
D_EXPERT = 512
N_ROUTER = N_GROUPS + N_EXPERTS
ROUTER_PAD = 128
EPS = 1e-6
NEG = -1e30

VMEM_LIMIT = 56 * 1024 * 1024
MXU_N = 256

TM_IN = 1024
TM_WIDE = 2048
TN_IN = 512
TN_GATES = TN_IN
N_GATE_TILES = 2 * D_MODEL // TN_GATES
ROW_CHUNK = 256
W_STAGE_ROWS = 256
TQ_ATTN = 256
PAIR = 2 * CHUNK
SUB_CONV = 32
TM_MIX = 256
BM_MOE = 256
R_ROUTE = 256
TM_OUT = 256
N_SLOTS = T * TOP_K
N_BLK_MOE = -(-(N_SLOTS + N_EXPERTS * (BM_MOE - 1)) // BM_MOE)


def _sigmoid(x):
    return 1.0 / (1.0 + jnp.exp(-x))


def _params(n_axes):
    return pltpu.CompilerParams(dimension_semantics=("arbitrary",) * n_axes,
                                vmem_limit_bytes=VMEM_LIMIT)


def _store_packed_rows(ref, y):
    rows = y.shape[0]
    for s in range(ROW_TILE):
        lo = y[:, s * LANES:(s + 1) * LANES].astype(BF16).astype(F32)
        hi = y[:, HALF_D + s * LANES:HALF_D + (s + 1) * LANES].astype(BF16).astype(F32)
        word = (lax.bitcast_convert_type(hi, U32) & HI_MASK) | (lax.bitcast_convert_type(lo, U32) >> 16)
        ref[pl.ds(s, rows, stride=ROW_TILE), :] = word


def _row_tile(r):
    start = r * ROW_TILE
    return pl.ds(start if isinstance(r, int) else pl.multiple_of(start, ROW_TILE), ROW_TILE)


def _load_packed_rows(ref, rows, s):
    word = ref[pl.ds(s, rows, stride=ROW_TILE), :]
    lo = lax.bitcast_convert_type(word << 16, F32)
    hi = lax.bitcast_convert_type(word & HI_MASK, F32)
    return lo, hi


def _chunk_dot(xn, w_ref, b_ref, c):
    w = w_ref[:, c * MXU_N:(c + 1) * MXU_N].astype(BF16)
    return jnp.dot(xn, w, preferred_element_type=F32) + b_ref[:, c * MXU_N:(c + 1) * MXU_N]


def _qkv_body(x_ref, g_ref, w_hbm, b_ref, xn_ref, q_ref, kv_ref, w_bf, stage, sem, *, tm):
    n_stage = D_MODEL // W_STAGE_ROWS

    @pl.when(pl.program_id(0) == 0)
    def _():
        def chunk_copy(i):
            return pltpu.make_async_copy(
                w_hbm.at[pl.ds(i * W_STAGE_ROWS, W_STAGE_ROWS), pl.ds(0, OFF_GLU)],
                stage.at[i % 2], sem.at[i % 2])
        chunk_copy(0).start()
        for i in range(n_stage):
            if i + 1 < n_stage:
                chunk_copy(i + 1).start()
            chunk_copy(i).wait()
            w_bf[i * W_STAGE_ROWS:(i + 1) * W_STAGE_ROWS, :] = stage[i % 2].astype(BF16)

    for r0 in range(0, tm, ROW_CHUNK):
        rows = slice(r0, r0 + ROW_CHUNK)
        x = x_ref[rows, :]
        ms = jnp.mean(x * x, axis=-1, keepdims=True)
        xn = (x * lax.rsqrt(ms + EPS) * g_ref[...]).astype(BF16)
        xn_ref[rows, :] = xn
        for c in range(OFF_GLU // MXU_N):
            cols = slice(c * MXU_N, (c + 1) * MXU_N)
            acc = jnp.dot(xn, w_bf[:, cols], preferred_element_type=F32) + b_ref[:, cols]
            if c < ATTN_WIDTH // MXU_N:
                q_ref[rows, cols] = (acc * (HEAD_DIM ** -0.5)).astype(BF16)
            else:
                kv_ref[rows, c * MXU_N - ATTN_WIDTH:(c + 1) * MXU_N - ATTN_WIDTH] = acc


def _qkv(x, norm_g, w_in, b_in, tm):
    t = x.shape[0]
    return pl.pallas_call(
        functools.partial(_qkv_body, tm=tm),
        name="qkv",
        grid=(t // tm,),
        in_specs=[
            pl.BlockSpec((tm, D_MODEL), lambda m: (m, 0)),
            pl.BlockSpec((1, D_MODEL), lambda m: (0, 0)),
            pl.BlockSpec(memory_space=pl.ANY),
            pl.BlockSpec((1, OFF_GLU), lambda m: (0, 0)),
        ],
        out_specs=[
            pl.BlockSpec((tm, D_MODEL), lambda m: (m, 0)),
            pl.BlockSpec((tm, ATTN_WIDTH), lambda m: (m, 0)),
            pl.BlockSpec((tm, 2 * KV_WIDTH), lambda m: (m, 0)),
        ],
        out_shape=[
            jax.ShapeDtypeStruct((t, D_MODEL), BF16),
            jax.ShapeDtypeStruct((t, ATTN_WIDTH), BF16),
            jax.ShapeDtypeStruct((t, 2 * KV_WIDTH), F32),
        ],
        scratch_shapes=[
            pltpu.VMEM((D_MODEL, OFF_GLU), BF16),
            pltpu.VMEM((2, W_STAGE_ROWS, OFF_GLU), F32),
            pltpu.SemaphoreType.DMA((2,)),
        ],
        compiler_params=_params(1),
    )(x, norm_g, w_in, b_in)


def _glu_body(xn_ref, wa_ref, ba_ref, wb_ref, bb_ref, u_ref):
    xn = xn_ref[...]
    for c in range(TN_IN // MXU_N):
        a = _chunk_dot(xn, wa_ref, ba_ref, c)
        b = _chunk_dot(xn, wb_ref, bb_ref, c)
        u_ref[:, c * MXU_N:(c + 1) * MXU_N] = a * _sigmoid(b)


def _glu(xn, w_in, b_in, tm):
    t = xn.shape[0]
    a0 = OFF_GLU // TN_IN
    b0 = (OFF_GLU + CONV_CH) // TN_IN
    return pl.pallas_call(
        _glu_body,
        name="glu",
        grid=(t // tm, CONV_CH // TN_IN),
        in_specs=[
            pl.BlockSpec((tm, D_MODEL), lambda m, n: (m, 0)),
            pl.BlockSpec((D_MODEL, TN_IN), lambda m, n: (0, a0 + n)),
            pl.BlockSpec((1, TN_IN), lambda m, n: (0, a0 + n)),
            pl.BlockSpec((D_MODEL, TN_IN), lambda m, n: (0, b0 + n)),
            pl.BlockSpec((1, TN_IN), lambda m, n: (0, b0 + n)),
        ],
        out_specs=pl.BlockSpec((tm, TN_IN), lambda m, n: (m, n)),
        out_shape=jax.ShapeDtypeStruct((t, CONV_CH), F32),
        compiler_params=_params(2),
    )(xn, w_in, b_in, w_in, b_in)


def _attn_prompt_body(sink_ref, q_ref, kvp_ref, kvc_ref, a_ref, at_sc):
    i = pl.program_id(0)
    kv = jnp.concatenate([kvp_ref[...], kvc_ref[...]], axis=0)
    k = kv[:, :KV_WIDTH].astype(BF16)
    vt = kv[:, KV_WIDTH:].T.astype(BF16)
    qt = q_ref[...].astype(F32).T.astype(BF16)
    n_cols = GROUP * PAIR
    n_keys = WINDOW + PAIR
    for p in range(TQ_ATTN // PAIR):
        w0 = p * PAIR
        row = lax.broadcasted_iota(jnp.int32, (n_keys, n_cols), 0)
        col = lax.broadcasted_iota(jnp.int32, (n_keys, n_cols), 1)
        first = jnp.where((col & (PAIR - 1)) >= CHUNK, CHUNK, 0)
        pos = row + (i * TQ_ATTN + w0 - WINDOW)
        ok = (row >= first) & (row < first + WINDOW + CHUNK) & (pos >= 0)
        bias = jnp.where(ok, 0.0, NEG)
        for h in range(N_KV_HEADS):
            kh = k[w0:w0 + n_keys, h * HEAD_DIM:(h + 1) * HEAD_DIM]
            rhs = jnp.concatenate(
                [qt[(h * GROUP + g) * HEAD_DIM:(h * GROUP + g + 1) * HEAD_DIM, w0:w0 + PAIR]
                 for g in range(GROUP)], axis=1)
            st = jnp.dot(kh, rhs, preferred_element_type=F32) + bias
            gcol = lax.shift_right_logical(lax.broadcasted_iota(jnp.int32, (1, n_cols), 1),
                                           PAIR.bit_length() - 1)
            sink = jnp.full((1, n_cols), sink_ref[h * GROUP], F32)
            for g in range(1, GROUP):
                sink = jnp.where(gcol == g, sink_ref[h * GROUP + g], sink)
            m = jnp.maximum(jnp.max(st, axis=0, keepdims=True), sink)
            pt = jnp.exp(st - m)
            denom = jnp.sum(pt, axis=0, keepdims=True) + jnp.exp(sink - m)
            ot = jnp.dot(vt[h * HEAD_DIM:(h + 1) * HEAD_DIM, w0:w0 + n_keys], pt.astype(BF16),
                         preferred_element_type=F32) / denom
            for g in range(GROUP):
                r0 = (h * GROUP + g) * HEAD_DIM
                at_sc[r0:r0 + HEAD_DIM, w0:w0 + PAIR] = ot[:, g * PAIR:(g + 1) * PAIR]
    a_ref[...] = at_sc[...].T.astype(BF16)


def _attn_prompt(sinks, q, kv):
    r = TQ_ATTN // WINDOW
    return pl.pallas_call(
        _attn_prompt_body,
        name="attn_prompt",
        grid=(T_P // TQ_ATTN,),
        in_specs=[
            pl.BlockSpec(memory_space=pltpu.SMEM),
            pl.BlockSpec((TQ_ATTN, ATTN_WIDTH), lambda i: (i, 0)),
            pl.BlockSpec((WINDOW, 2 * KV_WIDTH), lambda i: (jnp.maximum(i * r - 1, 0), 0)),
            pl.BlockSpec((TQ_ATTN, 2 * KV_WIDTH), lambda i: (i, 0)),
        ],
        out_specs=pl.BlockSpec((TQ_ATTN, ATTN_WIDTH), lambda i: (i, 0)),
        out_shape=jax.ShapeDtypeStruct((T_P, ATTN_WIDTH), BF16),
        scratch_shapes=[pltpu.VMEM((ATTN_WIDTH, TQ_ATTN), F32)],
        compiler_params=_params(1),
    )(sinks, q, kv, kv)


def _attn_sample_body(sink_ref, q_ref, kvc_ref, ck_ref, cv_ref, a_ref):
    kvc = kvc_ref[...]
    kw = jnp.concatenate([ck_ref[0], kvc[:, :KV_WIDTH]], axis=0).astype(BF16)
    vw = jnp.concatenate([cv_ref[0], kvc[:, KV_WIDTH:]], axis=0).astype(BF16)
    qc = q_ref[...]
    for h in range(N_KV_HEADS):
        kh = kw[:, h * HEAD_DIM:(h + 1) * HEAD_DIM]
        vh = vw[:, h * HEAD_DIM:(h + 1) * HEAD_DIM]
        qg = jnp.concatenate(
            [qc[:, (h * GROUP + g) * HEAD_DIM:(h * GROUP + g + 1) * HEAD_DIM] for g in range(GROUP)],
            axis=0)
        s = lax.dot_general(qg, kh, (((1,), (1,)), ((), ())), preferred_element_type=F32)
        sink = jnp.concatenate(
            [jnp.full((L_S, 1), sink_ref[h * GROUP + g], F32) for g in range(GROUP)], axis=0)
        m = jnp.maximum(jnp.max(s, axis=1, keepdims=True), sink)
        p = jnp.exp(s - m)
        denom = jnp.sum(p, axis=1, keepdims=True) + jnp.exp(sink - m)
        o = jnp.dot(p.astype(BF16), vh, preferred_element_type=F32) / denom
        for g in range(GROUP):
            c0 = (h * GROUP + g) * HEAD_DIM
            a_ref[:, c0:c0 + HEAD_DIM] = o[g * L_S:(g + 1) * L_S].astype(BF16)


def _attn_sample(sinks, q, kv, cache_k, cache_v):
    return pl.pallas_call(
        _attn_sample_body,
        name="attn_sample",
        grid=(N_STREAMS,),
        in_specs=[
            pl.BlockSpec(memory_space=pltpu.SMEM),
            pl.BlockSpec((L_S, ATTN_WIDTH), lambda b: (b, 0)),
            pl.BlockSpec((L_S, 2 * KV_WIDTH), lambda b: (b, 0)),
            pl.BlockSpec((1, WINDOW, KV_WIDTH), lambda b: (b, 0, 0)),
            pl.BlockSpec((1, WINDOW, KV_WIDTH), lambda b: (b, 0, 0)),
        ],
        out_specs=pl.BlockSpec((L_S, ATTN_WIDTH), lambda b: (b, 0)),
        out_shape=jax.ShapeDtypeStruct((T_S, ATTN_WIDTH), BF16),
        compiler_params=_params(1),
    )(sinks, q, kv, cache_k, cache_v)


def _conv_shifts(prev, cur_ref, ext_sc, sh_sc, rows):
    ext_sc[0:HIST_PAD, :] = prev
    ext_sc[HIST_PAD:HIST_PAD + rows, :] = cur_ref[...]
    ext_sc[HIST_PAD + rows:, :] = jnp.zeros((SUBLANES, CONV_CH), F32)
    lead = HIST_PAD - HIST
    span = rows + HIST_PAD - SUBLANES
    for s in range(SUBLANES):
        sh_sc[s, 0:span, :] = ext_sc[lead + s:lead + s + span, :]


def _conv_rows(r0, w_ref, b_ref, lg_ref, lb_ref, o_ref, sh_sc):
    acc = jnp.zeros((SUB_CONV, CONV_CH), F32) + b_ref[...]
    for k in range(CONV_WIDTH):
        a, s = divmod(k, SUBLANES)
        r = r0 + a * SUBLANES
        acc = acc + w_ref[k:k + 1, :] * sh_sc[s, r:r + SUB_CONV, :]
    mu = jnp.mean(acc, axis=-1, keepdims=True)
    d = acc - mu
    var = jnp.mean(d * d, axis=-1, keepdims=True)
    yn = d * lax.rsqrt(var + EPS) * lg_ref[...] + lb_ref[...]
    o_ref[r0:r0 + SUB_CONV, :] = (yn * _sigmoid(yn)).astype(BF16)


def _gates_conv_body(xn_ref, w_ref, b_ref, prev_ref, cur_ref, cw_ref, cb_ref, lg_ref, lb_ref,
                     gate_ref, c_ref, ext_sc, sh_sc, *, rows, fresh):
    if fresh:
        step = pl.program_id(0) * N_GATE_TILES + pl.program_id(1)
        prev = jnp.where(step > 0, prev_ref[...], 0.0)
    else:
        prev = prev_ref[0]
    _conv_shifts(prev, cur_ref, ext_sc, sh_sc, rows)
    n_chunks = TN_GATES // MXU_N
    w = [w_ref[:, c * MXU_N:(c + 1) * MXU_N].astype(BF16) for c in range(n_chunks)]
    for i in range(rows // SUB_CONV):
        _conv_rows(i * SUB_CONV, cw_ref, cb_ref, lg_ref, lb_ref, c_ref, sh_sc)
        r = slice(i * ROW_CHUNK, (i + 1) * ROW_CHUNK)
        xr = xn_ref[r, :]
        for c in range(n_chunks):
            cols = slice(c * MXU_N, (c + 1) * MXU_N)
            acc = jnp.dot(xr, w[c], preferred_element_type=F32) + b_ref[:, cols]
            gate_ref[r, cols] = _sigmoid(acc).astype(BF16)


def _gates_conv(xn, w_in, b_in, hist, u, conv_consts, tm, rows, fresh):
    t = xn.shape[0]
    n0 = OFF_GATES // TN_GATES
    tile = lambda m, n: m * N_GATE_TILES + n
    if fresh:
        r = rows // HIST_PAD
        prev_spec = pl.BlockSpec((HIST_PAD, CONV_CH), lambda m, n: (jnp.maximum(tile(m, n) * r - 1, 0), 0))
    else:
        prev_spec = pl.BlockSpec((1, HIST_PAD, CONV_CH), lambda m, n: (tile(m, n), 0, 0))
    const_map = lambda m, n: (0, 0)
    return pl.pallas_call(
        functools.partial(_gates_conv_body, rows=rows, fresh=fresh),
        name="gates_conv",
        grid=(t // tm, N_GATE_TILES),
        in_specs=[
            pl.BlockSpec((tm, D_MODEL), lambda m, n: (m, 0)),
            pl.BlockSpec((D_MODEL, TN_GATES), lambda m, n: (0, n0 + n)),
            pl.BlockSpec((1, TN_GATES), lambda m, n: (0, n0 + n)),
            prev_spec,
            pl.BlockSpec((rows, CONV_CH), lambda m, n: (tile(m, n), 0)),
            pl.BlockSpec((CONV_WIDTH, CONV_CH), const_map),
            pl.BlockSpec((1, CONV_CH), const_map),
            pl.BlockSpec((1, CONV_CH), const_map),
            pl.BlockSpec((1, CONV_CH), const_map),
        ],
        out_specs=[
            pl.BlockSpec((tm, TN_GATES), lambda m, n: (m, n)),
            pl.BlockSpec((rows, CONV_CH), lambda m, n: (tile(m, n), 0)),
        ],
        out_shape=[
            jax.ShapeDtypeStruct((t, 2 * D_MODEL), BF16),
            jax.ShapeDtypeStruct((t, CONV_CH), BF16),
        ],
        scratch_shapes=[
            pltpu.VMEM((HIST_PAD + rows + SUBLANES, CONV_CH), F32),
            pltpu.VMEM((SUBLANES, HIST_PAD + rows, CONV_CH), F32),
        ],
        compiler_params=_params(2),
    )(xn, w_in, b_in, hist, u, *conv_consts)


def _in_proj(x, hist, norm_g, w_in, b_in, conv_consts, tm, fresh):
    t = x.shape[0]
    xn, q, kv = _qkv(x, norm_g, w_in, b_in, tm)
    tw = min(TM_WIDE, t)
    u = _glu(xn, w_in, b_in, tw)
    gates, c = _gates_conv(xn, w_in, b_in, u if fresh else hist, u, conv_consts, tw,
                           tw // N_GATE_TILES, fresh)
    return q, kv, u, gates, c


def _mix_body(ap_ref, as_ref, cp_ref, cs_ref, gp_ref, gs_ref, xp_ref, xs_ref, wao_ref, wco_ref,
              wout_ref, g2_ref, wr_ref, br_ref, x1_ref, xn_ref, lg_ref):
    m = pl.program_id(0)
    is_sample = m >= T_P // TM_MIX
    a = jnp.where(is_sample, as_ref[...], ap_ref[...])
    c = jnp.where(is_sample, cs_ref[...], cp_ref[...])
    gate = jnp.where(is_sample, gs_ref[...], gp_ref[...])
    x = jnp.where(is_sample, xs_ref[...], xp_ref[...])
    pa = jnp.dot(a, wao_ref[...], preferred_element_type=F32)
    pc = jnp.dot(c, wco_ref[...], preferred_element_type=F32)
    y = (gate[:, :D_MODEL].astype(F32) * pa + gate[:, D_MODEL:].astype(F32) * pc).astype(BF16)
    x1 = x + jnp.dot(y, wout_ref[...], preferred_element_type=F32)
    x1_ref[...] = x1
    ms = jnp.mean(x1 * x1, axis=-1, keepdims=True)
    xn = x1 * lax.rsqrt(ms + EPS) * g2_ref[...]
    _store_packed_rows(xn_ref, xn)
    xn_hi = xn.astype(BF16)
    xn_lo = (xn - xn_hi.astype(F32)).astype(BF16)
    r_hi = jnp.dot(xn_hi, wr_ref[...], preferred_element_type=F32)
    r_lo = jnp.dot(xn_lo, wr_ref[...], preferred_element_type=F32)
    lg_ref[...] = r_hi + pltpu.roll(r_hi, ROUTER_PAD - N_ROUTER, axis=1) + r_lo + br_ref[...]


def _mix(a_p, a_s, c_p, c_s, g_p, g_s, x_p, x_s, wao, wco, wout, g2, wr, br):
    n_p = T_P // TM_MIX
    n_s = T_S // TM_MIX
    prompt_map = lambda m: (jnp.minimum(m, n_p - 1), 0)
    sample_map = lambda m: (jnp.clip(m - n_p, 0, n_s - 1), 0)
    row_map = lambda m: (m, 0)
    const_map = lambda m: (0, 0)
    once = pl.Buffered(1)

    def pair(width):
        return [pl.BlockSpec((TM_MIX, width), prompt_map), pl.BlockSpec((TM_MIX, width), sample_map)]

    return pl.pallas_call(
        _mix_body,
        name="mix",
        grid=(T // TM_MIX,),
        in_specs=pair(ATTN_WIDTH) + pair(CONV_CH) + pair(2 * D_MODEL) + pair(D_MODEL) + [
            pl.BlockSpec((ATTN_WIDTH, D_MODEL), const_map, pipeline_mode=once),
            pl.BlockSpec((CONV_CH, D_MODEL), const_map, pipeline_mode=once),
            pl.BlockSpec((D_MODEL, D_MODEL), const_map, pipeline_mode=once),
            pl.BlockSpec((1, D_MODEL), const_map),
            pl.BlockSpec((D_MODEL, ROUTER_PAD), const_map),
            pl.BlockSpec((1, ROUTER_PAD), const_map),
        ],
        out_specs=[
            pl.BlockSpec((TM_MIX, D_MODEL), row_map),
            pl.BlockSpec((TM_MIX * ROW_TILE, LANES), row_map),
            pl.BlockSpec((TM_MIX, ROUTER_PAD), row_map),
        ],
        out_shape=[
            jax.ShapeDtypeStruct((T, D_MODEL), F32),
            jax.ShapeDtypeStruct((T * ROW_TILE, LANES), U32),
            jax.ShapeDtypeStruct((T, ROUTER_PAD), F32),
        ],
        compiler_params=_params(1),
    )(a_p, a_s, c_p, c_s, g_p, g_s, x_p, x_s, wao, wco, wout, g2, wr, br)


REC_W, REC_E, REC_POS = 0, 2, 4


def _router_body(lg_ref, rec_ref, idx_ref, cnt_ref, carry_sc):
    lane = lax.broadcasted_iota(jnp.int32, (R_ROUTE, ROUTER_PAD), 1).astype(F32)
    ltri = (lax.broadcasted_iota(jnp.int32, (R_ROUTE, R_ROUTE), 1)
            < lax.broadcasted_iota(jnp.int32, (R_ROUTE, R_ROUTE), 0)).astype(BF16)

    def pick(vals, idx):
        return jnp.sum(jnp.where(lane == idx, vals, 0.0), axis=1, keepdims=True)

    def first_lane(mask):
        return jnp.min(jnp.where(mask, lane, float(ROUTER_PAD)), axis=1, keepdims=True)

    def masked_max(mask, v):
        return jnp.max(jnp.where(mask, v, NEG), axis=1, keepdims=True)

    carry_sc[...] = jnp.zeros_like(carry_sc)

    def count_pass(i, carry_unused):
        rows = pl.ds(pl.multiple_of(i * R_ROUTE, R_ROUTE), R_ROUTE)
        lg = lg_ref[rows, :]
        gmask = lane < N_GROUPS
        gmax = masked_max(gmask, lg)
        p_grp = 1.0 / jnp.sum(jnp.where(gmask, jnp.exp(lg - gmax), 0.0), axis=1, keepdims=True)
        grp = first_lane(gmask & (lg == gmax))
        lo = N_GROUPS + EXPERTS_PER_GROUP * grp
        emask = (lane >= lo) & (lane < lo + EXPERTS_PER_GROUP)
        e1 = masked_max(emask, lg)
        i1 = first_lane(emask & (lg == e1))
        emask2 = emask & (lane != i1)
        e2 = masked_max(emask2, lg)
        i2 = first_lane(emask2 & (lg == e2))
        t = jnp.exp(e2 - e1)
        w1 = p_grp / (1.0 + t)
        w2 = p_grp * t / (1.0 + t)
        x1 = i1 - N_GROUPS
        x2 = i2 - N_GROUPS
        onehot = ((lane == x1) | (lane == x2 + N_EXPERTS)).astype(BF16)
        carry = carry_sc[0:1, :]
        before = jnp.dot(ltri, onehot, preferred_element_type=F32) + carry
        rank1 = pick(before, x1)
        rank2 = pick(before, x2 + N_EXPERTS)
        carry_sc[0:1, :] = carry + jnp.sum(onehot.astype(F32), axis=0, keepdims=True)
        rec = jnp.where(lane == REC_W, w1, 0.0)
        for ln, v in ((REC_W + 1, w2), (REC_E, x1), (REC_E + 1, x2),
                      (REC_POS, rank1), (REC_POS + 1, rank2)):
            rec = jnp.where(lane == ln, v, rec)
        rec_ref[rows, :] = rec
        return carry_unused

    lax.fori_loop(0, T // R_ROUTE, count_pass, 0)

    tot = carry_sc[0:1, :]
    counts = tot + pltpu.roll(tot, ROUTER_PAD - N_EXPERTS, axis=1)
    lane1 = lax.broadcasted_iota(jnp.int32, (1, ROUTER_PAD), 1)
    counts = jnp.where(lane1 < N_EXPERTS, counts, 0.0)
    cnt_ref[...] = jnp.broadcast_to(counts, cnt_ref.shape).astype(jnp.int32)
    nblk = ((counts.astype(jnp.int32) + (BM_MOE - 1)) >> (BM_MOE.bit_length() - 1)).astype(F32)
    upper = (lax.broadcasted_iota(jnp.int32, (ROUTER_PAD, ROUTER_PAD), 0)
             < lax.broadcasted_iota(jnp.int32, (ROUTER_PAD, ROUTER_PAD), 1)).astype(BF16)
    blk_start = jnp.dot(jnp.broadcast_to(nblk, (SUBLANES, ROUTER_PAD)).astype(BF16), upper,
                        preferred_element_type=F32)[0:1, :]
    row0 = blk_start * BM_MOE

    def place_pass(i, carry_unused):
        rows = pl.ds(pl.multiple_of(i * R_ROUTE, R_ROUTE), R_ROUTE)
        rec = rec_ref[rows, :]
        x1 = pick(rec, REC_E)
        x2 = pick(rec, REC_E + 1)
        p1 = pick(row0, x1) + pick(rec, REC_POS)
        p2 = pick(row0, x2) + pick(tot, x2) + pick(rec, REC_POS + 1)
        rec = jnp.where(lane == REC_POS, p1 * ROW_TILE, rec)
        rec = jnp.where(lane == REC_POS + 1, p2 * ROW_TILE, rec)
        rec_ref[rows, :] = rec
        idx_ref[i] = rec.T[0:SUBLANES, :].astype(jnp.int32)
        return carry_unused

    lax.fori_loop(0, T // R_ROUTE, place_pass, 0)


def _router(logits):
    return pl.pallas_call(
        _router_body,
        name="router",
        grid=(1,),
        in_specs=[pl.BlockSpec((T, ROUTER_PAD), lambda i: (0, 0))],
        out_specs=[
            pl.BlockSpec((T, ROUTER_PAD), lambda i: (0, 0)),
            pl.BlockSpec((T // R_ROUTE, SUBLANES, R_ROUTE), lambda i: (0, 0, 0)),
            pl.BlockSpec((SUBLANES, ROUTER_PAD), lambda i: (0, 0)),
        ],
        out_shape=[
            jax.ShapeDtypeStruct((T, ROUTER_PAD), F32),
            jax.ShapeDtypeStruct((T // R_ROUTE, SUBLANES, R_ROUTE), jnp.int32),
            jax.ShapeDtypeStruct((SUBLANES, ROUTER_PAD), jnp.int32),
        ],
        scratch_shapes=[pltpu.VMEM((SUBLANES, ROUTER_PAD), F32)],
        compiler_params=_params(1),
    )(logits)


def _route(logits):
    rec, idx, cnt = _router(logits)
    idx = idx.transpose(1, 0, 2).reshape(SUBLANES, T)
    eid = idx[REC_E:REC_E + TOP_K].reshape(-1)
    pos = idx[REC_POS:REC_POS + TOP_K].reshape(-1)
    counts = cnt[0, :N_EXPERTS]
    nblk = (counts + BM_MOE - 1) // BM_MOE
    blk_end = jnp.cumsum(nblk)
    blk_start = blk_end - nblk
    row_start = jnp.cumsum(counts) - counts

    order = jnp.argsort(eid, stable=True)
    tok_sorted = jnp.concatenate([((order % T) * ROW_TILE).astype(jnp.int32),
                                  jnp.zeros((BM_MOE,), jnp.int32)])
    n_used = blk_end[-1]
    j = jnp.minimum(jnp.arange(N_BLK_MOE, dtype=jnp.int32), n_used - 1)
    blk_e = jnp.minimum(jnp.searchsorted(blk_end, j, side='right'), N_EXPERTS - 1).astype(jnp.int32)
    blk_src = (row_start[blk_e] + (j - blk_start[blk_e]) * BM_MOE).astype(jnp.int32)
    e_ids = jnp.arange(N_EXPERTS, dtype=jnp.int32)
    used = nblk > 0
    e_slot = ((jnp.cumsum(used.astype(jnp.int32)) - 1) % 2).astype(jnp.int32)
    nxt = lax.cummin(jnp.where(used, e_ids, N_EXPERTS), reverse=True)
    nxt = jnp.concatenate([nxt[1:], jnp.full((1,), N_EXPERTS, jnp.int32)])
    e_next = jnp.where(nxt < N_EXPERTS, nxt, e_ids).astype(jnp.int32)
    blk_first = (j == blk_start[blk_e]).astype(jnp.int32)
    plan = jnp.stack([blk_e, blk_src, blk_first, e_slot[blk_e], e_next[blk_e],
                      jnp.zeros_like(blk_e)])
    return rec, pos, tok_sorted, plan, n_used.astype(jnp.int32).reshape(1)


PLAN_E, PLAN_SRC, PLAN_FIRST, PLAN_WSLOT, PLAN_NEXT_E, PLAN_ZERO = range(6)
WEIGHT_DMA_PRIORITY = 1
GATHER_DMA_PRIORITY = 0


def _moe_body(plan, n_used, tok, x_hbm, wg_hbm, wu_hbm, wd_hbm, o_ref,
              xb, wg_st, wu_st, wd_st, wg_bf, wu_bf, wd_bf, sem_x, sem_w):
    j = pl.program_id(0)
    nu = n_used[0]

    def row_copy(src, slot, r):
        t8 = pl.multiple_of(tok[src + r], ROW_TILE)
        return pltpu.make_async_copy(x_hbm.at[pl.ds(t8, ROW_TILE)], xb.at[slot, _row_tile(r)],
                                     sem_x.at[slot])

    def block_wait(slot):
        pltpu.make_async_copy(x_hbm.at[pl.ds(0, BM_MOE * ROW_TILE)], xb.at[slot],
                              sem_x.at[slot]).wait()

    def weight_copies(e, ws):
        return (pltpu.make_async_copy(wg_hbm.at[e], wg_st.at[ws], sem_w.at[ws, 0]),
                pltpu.make_async_copy(wu_hbm.at[e], wu_st.at[ws], sem_w.at[ws, 1]),
                pltpu.make_async_copy(wd_hbm.at[e], wd_st.at[ws], sem_w.at[ws, 2]))

    @pl.when(j == 0)
    def _():
        for cp in weight_copies(plan[PLAN_E, 0], 0):
            cp.start(priority=WEIGHT_DMA_PRIORITY)
        src = plan[PLAN_SRC, 0]

        def body(r, carry):
            row_copy(src, 0, r).start(priority=GATHER_DMA_PRIORITY)
            return carry
        lax.fori_loop(0, BM_MOE, body, 0, unroll=8)

    @pl.when((j < nu) & (plan[PLAN_FIRST, jnp.minimum(j, N_BLK_MOE - 1)] == 1))
    def _():
        e = plan[PLAN_E, j]
        ws = plan[PLAN_WSLOT, j]
        e_next = plan[PLAN_NEXT_E, j]

        @pl.when(e_next != e)
        def _():
            for cp in weight_copies(e_next, 1 - ws):
                cp.start(priority=WEIGHT_DMA_PRIORITY)

        for cp in weight_copies(e, ws):
            cp.wait()
        wg_bf[...] = wg_st[ws].astype(BF16)
        wu_bf[...] = wu_st[ws].astype(BF16)
        wd_bf[...] = wd_st[ws].astype(BF16)

    @pl.when(j < nu)
    def _():
        slot = j % 2
        block_wait(slot)
        nsrc = plan[PLAN_SRC, jnp.minimum(j + 1, N_BLK_MOE - 1)]
        halves = [_load_packed_rows(xb.at[slot], BM_MOE, s) for s in range(ROW_TILE)]
        x = jnp.concatenate([lo for lo, _ in halves] + [hi for _, hi in halves], axis=1).astype(BF16)
        zero = plan[PLAN_ZERO, 0]
        group = BM_MOE // 8
        issued = [0]

        def gather_after(v):
            base = nsrc
            if v is not None:
                base = base + (lax.bitcast_convert_type(v[0:1, 0:1], jnp.int32)[0, 0] & zero)
            for r in range(issued[0], issued[0] + group):
                row_copy(base, 1 - slot, r).start(priority=GATHER_DMA_PRIORITY)
            issued[0] += group

        gather_after(None)
        gather_after(halves[ROW_TILE - 1][1])
        acc = None
        for c in range(D_EXPERT // MXU_N):
            cols = slice(c * MXU_N, (c + 1) * MXU_N)
            g = jnp.dot(x, wg_bf[:, cols], preferred_element_type=F32)
            gather_after(g)
            u = jnp.dot(x, wu_bf[:, cols], preferred_element_type=F32)
            gather_after(u)
            h = (g * _sigmoid(g) * u).astype(BF16)
            part = jnp.dot(h, wd_bf[cols, :], preferred_element_type=F32)
            if c == 0:
                gather_after(part)
                gather_after(part[:, D_MODEL - LANES:])
            acc = part if acc is None else acc + part
        assert issued[0] == BM_MOE
        _store_packed_rows(o_ref, acc)

    @pl.when(j == nu)
    def _():
        block_wait(j % 2)

    @pl.when(j >= nu)
    def _():
        o_ref[...] = jnp.zeros_like(o_ref)


def _moe(plan, n_used, tok_sorted, xn_packed, w_g, w_u, w_d):
    grid_spec = pltpu.PrefetchScalarGridSpec(
        num_scalar_prefetch=3,
        grid=(N_BLK_MOE + 1,),
        in_specs=[pl.BlockSpec(memory_space=pl.ANY)] * 4,
        out_specs=pl.BlockSpec((BM_MOE * ROW_TILE, LANES), lambda j, pn, nu, tk: (j, 0)),
        scratch_shapes=[
            pltpu.VMEM((2, BM_MOE * ROW_TILE, LANES), U32),
            pltpu.VMEM((2, D_MODEL, D_EXPERT), F32),
            pltpu.VMEM((2, D_MODEL, D_EXPERT), F32),
            pltpu.VMEM((2, D_EXPERT, D_MODEL), F32),
            pltpu.VMEM((D_MODEL, D_EXPERT), BF16),
            pltpu.VMEM((D_MODEL, D_EXPERT), BF16),
            pltpu.VMEM((D_EXPERT, D_MODEL), BF16),
            pltpu.SemaphoreType.DMA((2,)),
            pltpu.SemaphoreType.DMA((2, 3)),
        ],
    )
    return pl.pallas_call(
        _moe_body,
        name="experts",
        grid_spec=grid_spec,
        out_shape=jax.ShapeDtypeStruct(((N_BLK_MOE + 1) * BM_MOE * ROW_TILE, LANES), U32),
        compiler_params=_params(1),
    )(plan, n_used, tok_sorted, xn_packed, w_g, w_u, w_d)


def _combine_body(pos, x1_ref, w_ref, gf_ref, o_hbm, y_ref, rb, sem, *, first_tile):
    m = pl.program_id(0)
    last = pl.num_programs(0) - 1

    def row_copy(base, slot, r, k):
        p8 = pl.multiple_of(pos[k * T + base + r], ROW_TILE)
        return pltpu.make_async_copy(o_hbm.at[pl.ds(p8, ROW_TILE)], rb.at[slot, k, _row_tile(r)],
                                     sem.at[slot])

    def tile_wait(slot):
        for k in range(TOP_K):
            pltpu.make_async_copy(o_hbm.at[pl.ds(0, TM_OUT * ROW_TILE)], rb.at[slot, k],
                                  sem.at[slot]).wait()

    def tile_base(mm):
        return (first_tile + mm) * TM_OUT

    @pl.when(m == 0)
    def _():
        base = tile_base(0)

        def body(r, carry):
            for k in range(TOP_K):
                row_copy(base, 0, r, k).start(priority=k)
            return carry
        lax.fori_loop(0, TM_OUT, body, 0, unroll=4)

    slot = m % 2
    tile_wait(slot)
    nbase = tile_base(jnp.minimum(m + 1, last))
    rows = TM_OUT // ROW_TILE
    w0 = w_ref[:, 0:1]
    w1 = w_ref[:, 1:2]
    ss = jnp.zeros((TM_OUT, 1), F32)
    for s in range(ROW_TILE):
        for r in range(s * rows, (s + 1) * rows):
            for k in range(TOP_K):
                row_copy(nbase, 1 - slot, r, k).start(priority=k)
        lo0, hi0 = _load_packed_rows(rb.at[slot, 0], TM_OUT, s)
        lo1, hi1 = _load_packed_rows(rb.at[slot, 1], TM_OUT, s)
        for off, r0, r1 in ((s * LANES, lo0, lo1), (HALF_D + s * LANES, hi0, hi1)):
            y = x1_ref[:, off:off + LANES] + w0 * r0 + w1 * r1
            ss = ss + jnp.sum(y * y, axis=-1, keepdims=True)
            y_ref[:, off:off + LANES] = y
    scale = lax.rsqrt(ss * (1.0 / D_MODEL) + EPS)
    y_ref[...] = y_ref[...] * scale * gf_ref[...]

    @pl.when(m == last)
    def _():
        tile_wait(1 - slot)


def _combine(pos, x1, wts, gf, out_sorted, first_tile, n_tiles):
    grid_spec = pltpu.PrefetchScalarGridSpec(
        num_scalar_prefetch=1,
        grid=(n_tiles,),
        in_specs=[
            pl.BlockSpec((TM_OUT, D_MODEL), lambda m, p: (first_tile + m, 0)),
            pl.BlockSpec((TM_OUT, ROUTER_PAD), lambda m, p: (first_tile + m, 0)),
            pl.BlockSpec((1, D_MODEL), lambda m, p: (0, 0)),
            pl.BlockSpec(memory_space=pl.ANY),
        ],
        out_specs=pl.BlockSpec((TM_OUT, D_MODEL), lambda m, p: (m, 0)),
        scratch_shapes=[
            pltpu.VMEM((2, TOP_K, TM_OUT * ROW_TILE, LANES), U32),
            pltpu.SemaphoreType.DMA((2,)),
        ],
    )
    return pl.pallas_call(
        functools.partial(_combine_body, first_tile=first_tile),
        name="combine",
        grid_spec=grid_spec,
        out_shape=jax.ShapeDtypeStruct((n_tiles * TM_OUT, D_MODEL), F32),
        compiler_params=_params(1),
    )(pos, x1, wts, gf, out_sorted)


def kernel(x_prompt, x_sample, cache_k, cache_v, state_conv, norm1_g, w_in, b_in, attn_sinks,
           w_attn_o, conv_dw, conv_dw_b, conv_ln_g, conv_ln_b, w_conv_o, w_out, norm2_g,
           w_router_group, b_router_group, w_router_expert, b_router_expert, w_e_gate, w_e_up,
           w_e_down, final_norm_g):
    x_p = x_prompt.reshape(T_P, D_MODEL)
    x_s = x_sample.reshape(T_S, D_MODEL)
    g1 = norm1_g[0][None, :]
    b1 = b_in[0][None, :]
    conv_consts = (conv_dw[0], conv_dw_b[0][None, :], conv_ln_g[0][None, :], conv_ln_b[0][None, :])
    hist_pad = jnp.pad(state_conv[0], ((0, 0), (HIST_PAD - HIST, 0), (0, 0)))
    q_p, kv_p, u_p, gates_p, c_p = _in_proj(x_p, None, g1, w_in[0], b1, conv_consts, TM_IN, True)
    q_s, kv_s, u_s, gates_s, c_s = _in_proj(x_s, hist_pad, g1, w_in[0], b1, conv_consts, T_S, False)

    sinks = attn_sinks[0]
    a_p = _attn_prompt(sinks, q_p, kv_p)
    ck = cache_k[0].reshape(N_STREAMS, WINDOW, KV_WIDTH)
    cv = cache_v[0].reshape(N_STREAMS, WINDOW, KV_WIDTH)
    a_s = _attn_sample(sinks, q_s, kv_s, ck, cv)

    w_r = jnp.concatenate([w_router_group[0], w_router_expert[0]], axis=1)
    w_r_hi = w_r.astype(BF16)
    w_r_lo = (w_r - w_r_hi.astype(F32)).astype(BF16)
    w_r_cat = jnp.concatenate(
        [w_r_hi, w_r_lo, jnp.zeros((D_MODEL, ROUTER_PAD - 2 * N_ROUTER), BF16)], axis=1)
    b_r = jnp.concatenate([b_router_group[0], b_router_expert[0],
                           jnp.zeros((ROUTER_PAD - N_ROUTER,), F32)])[None, :]

    x1, xn2, logits = _mix(a_p, a_s, c_p, c_s, gates_p, gates_s, x_p, x_s,
                           w_attn_o[0].astype(BF16), w_conv_o[0].astype(BF16), w_out[0].astype(BF16),
                           norm2_g[0][None, :], w_r_cat, b_r)

    wts, pos, tok_sorted, plan, n_used = _route(logits)
    out_sorted = _moe(plan, n_used, tok_sorted, xn2, w_e_gate[0], w_e_up[0], w_e_down[0])
    gf = final_norm_g[None, :]
    y_p = _combine(pos, x1, wts, gf, out_sorted, 0, T_P // TM_OUT)
    y_s = _combine(pos, x1, wts, gf, out_sorted, T_P // TM_OUT, T_S // TM_OUT)

    kv_shape = (1, -1, WINDOW, N_KV_HEADS, HEAD_DIM)
    new_k_prompt = kv_p[T_P - WINDOW:, :KV_WIDTH].reshape(kv_shape)
    new_v_prompt = kv_p[T_P - WINDOW:, KV_WIDTH:].reshape(kv_shape)
    new_conv_prompt = u_p[T_P - HIST:].reshape(1, 1, HIST, CONV_CH)
    k_s = kv_s[:, :KV_WIDTH].reshape(N_STREAMS, L_S, KV_WIDTH)
    v_s = kv_s[:, KV_WIDTH:].reshape(N_STREAMS, L_S, KV_WIDTH)
    new_k_sample = jnp.concatenate([ck[:, L_S:], k_s], axis=1).reshape(kv_shape)
    new_v_sample = jnp.concatenate([cv[:, L_S:], v_s], axis=1).reshape(kv_shape)
    new_conv_sample = u_s.reshape(N_STREAMS, L_S, CONV_CH)[:, L_S - HIST:].reshape(
        1, N_STREAMS, HIST, CONV_CH)

    return (y_p.reshape(1, T_P, D_MODEL), y_s.reshape(N_STREAMS, L_S, D_MODEL),
            new_k_prompt, new_v_prompt, new_conv_prompt,
            new_k_sample, new_v_sample, new_conv_sample)
```

```python
import functools

import numpy as np
import jax
import jax.numpy as jnp
from jax import lax
from jax.experimental import pallas as pl
from jax.experimental.pallas import tpu as pltpu

F32 = jnp.float32
BF16 = jnp.bfloat16
U32 = jnp.uint32

D_MODEL = 2048
T_P = 8192
N_STREAMS = 8
L_S = 64
T_S = N_STREAMS * L_S
T = T_P + T_S
CHUNK = 64
WINDOW = 128
HEAD_DIM = 64
N_Q_HEADS = 16
N_KV_HEADS = 4
GROUP = N_Q_HEADS // N_KV_HEADS
ATTN_WIDTH = N_Q_HEADS * HEAD_DIM
KV_WIDTH = N_KV_HEADS * HEAD_DIM
CONV_CH = 1024
CONV_WIDTH = 31
HIST = CONV_WIDTH - 1
HIST_PAD = 32
SUBLANES = 8
LANES = 128
ROW_TILE = SUBLANES
HALF_D = D_MODEL // 2
HI_MASK = np.uint32(0xFFFF0000)
OFF_KV = ATTN_WIDTH
OFF_GLU = OFF_KV + 2 * KV_WIDTH
OFF_GATES = OFF_GLU + 2 * CONV_CH
IN_WIDTH = OFF_GATES + 2 * D_MODEL
N_GROUPS = 8
EXPERTS_PER_GROUP = 4
N_EXPERTS = N_GROUPS * EXPERTS_PER_GROUP
TOP_K = 2
D_EXPERT = 512
N_ROUTER = N_GROUPS + N_EXPERTS
ROUTER_PAD = 128
EPS = 1e-6
NEG = -1e30

VMEM_LIMIT = 56 * 1024 * 1024
MXU_N = 256

TM_IN = 1024
TM_WIDE = 2048
TN_IN = 512
TN_GATES = TN_IN
N_GATE_TILES = 2 * D_MODEL // TN_GATES
ROW_CHUNK = 256
W_STAGE_ROWS = 256
TQ_ATTN = 256
PAIR = 2 * CHUNK
BF16_ROWS = 16
SUB_CONV = 32
PIECE_CONV = 32
TM_MIX = 256
BM_MOE = 256
R_ROUTE = 256
TM_OUT = 256
N_SLOTS = T * TOP_K
N_BLK_MOE = -(-(N_SLOTS + N_EXPERTS * (BM_MOE - 1)) // BM_MOE)


def _sigmoid(x):
    return 1.0 / (1.0 + jnp.exp(-x))


def _params(n_axes):
    return pltpu.CompilerParams(dimension_semantics=("arbitrary",) * n_axes,
                                vmem_limit_bytes=VMEM_LIMIT)


def _store_packed_rows(ref, y):
    rows = y.shape[0]
    for s in range(ROW_TILE):
        lo = y[:, s * LANES:(s + 1) * LANES].astype(BF16).astype(F32)
        hi = y[:, HALF_D + s * LANES:HALF_D + (s + 1) * LANES].astype(BF16).astype(F32)
        word = (lax.bitcast_convert_type(hi, U32) & HI_MASK) | (lax.bitcast_convert_type(lo, U32) >> 16)
        ref[pl.ds(s, rows, stride=ROW_TILE), :] = word


def _row_tile(r):
    start = r * ROW_TILE
    return pl.ds(start if isinstance(r, int) else pl.multiple_of(start, ROW_TILE), ROW_TILE)


def _load_packed_rows(ref, rows, s):
    word = ref[pl.ds(s, rows, stride=ROW_TILE), :]
    lo = lax.bitcast_convert_type(word << 16, F32)
    hi = lax.bitcast_convert_type(word & HI_MASK, F32)
    return lo, hi


def _chunk_dot(xn, w_ref, b_ref, c):
    w = w_ref[:, c * MXU_N:(c + 1) * MXU_N].astype(BF16)
    return jnp.dot(xn, w, preferred_element_type=F32) + b_ref[:, c * MXU_N:(c + 1) * MXU_N]


def _qkv_body(x_ref, g_ref, w_hbm, b_ref, xn_ref, q_ref, kv_ref, w_bf, stage, sem, *, tm):
    n_stage = D_MODEL // W_STAGE_ROWS

    @pl.when(pl.program_id(0) == 0)
    def _():
        def chunk_copy(i):
            return pltpu.make_async_copy(
                w_hbm.at[pl.ds(i * W_STAGE_ROWS, W_STAGE_ROWS), pl.ds(0, OFF_GLU)],
                stage.at[i % 2], sem.at[i % 2])
        chunk_copy(0).start()
        for i in range(n_stage):
            if i + 1 < n_stage:
                chunk_copy(i + 1).start()
            chunk_copy(i).wait()
            w_bf[i * W_STAGE_ROWS:(i + 1) * W_STAGE_ROWS, :] = stage[i % 2].astype(BF16)

    for r0 in range(0, tm, ROW_CHUNK):
        rows = slice(r0, r0 + ROW_CHUNK)
        x = x_ref[rows, :]
        ms = jnp.mean(x * x, axis=-1, keepdims=True)
        xn = (x * lax.rsqrt(ms + EPS) * g_ref[...]).astype(BF16)
        xn_ref[rows, :] = xn
        for c in range(OFF_GLU // MXU_N):
            cols = slice(c * MXU_N, (c + 1) * MXU_N)
            acc = jnp.dot(xn, w_bf[:, cols], preferred_element_type=F32) + b_ref[:, cols]
            if c < ATTN_WIDTH // MXU_N:
                q_ref[rows, cols] = (acc * (HEAD_DIM ** -0.5)).astype(BF16)
            else:
                kv_ref[rows, c * MXU_N - ATTN_WIDTH:(c + 1) * MXU_N - ATTN_WIDTH] = acc


def _qkv(x, norm_g, w_in, b_in, tm):
    t = x.shape[0]
    return pl.pallas_call(
        functools.partial(_qkv_body, tm=tm),
        name="qkv",
        grid=(t // tm,),
        in_specs=[
            pl.BlockSpec((tm, D_MODEL), lambda m: (m, 0)),
            pl.BlockSpec((1, D_MODEL), lambda m: (0, 0)),
            pl.BlockSpec(memory_space=pl.ANY),
            pl.BlockSpec((1, OFF_GLU), lambda m: (0, 0)),
        ],
        out_specs=[
            pl.BlockSpec((tm, D_MODEL), lambda m: (m, 0)),
            pl.BlockSpec((tm, ATTN_WIDTH), lambda m: (m, 0)),
            pl.BlockSpec((tm, 2 * KV_WIDTH), lambda m: (m, 0)),
        ],
        out_shape=[
            jax.ShapeDtypeStruct((t, D_MODEL), BF16),
            jax.ShapeDtypeStruct((t, ATTN_WIDTH), BF16),
            jax.ShapeDtypeStruct((t, 2 * KV_WIDTH), F32),
        ],
        scratch_shapes=[
            pltpu.VMEM((D_MODEL, OFF_GLU), BF16),
            pltpu.VMEM((2, W_STAGE_ROWS, OFF_GLU), F32),
            pltpu.SemaphoreType.DMA((2,)),
        ],
        compiler_params=_params(1),
    )(x, norm_g, w_in, b_in)


def _glu_body(xn_ref, wa_ref, ba_ref, wb_ref, bb_ref, u_ref):
    xn = xn_ref[...]
    for c in range(TN_IN // MXU_N):
        a = _chunk_dot(xn, wa_ref, ba_ref, c)
        b = _chunk_dot(xn, wb_ref, bb_ref, c)
        u_ref[:, c * MXU_N:(c + 1) * MXU_N] = a * _sigmoid(b)


def _glu(xn, w_in, b_in, tm):
    t = xn.shape[0]
    a0 = OFF_GLU // TN_IN
    b0 = (OFF_GLU + CONV_CH) // TN_IN
    return pl.pallas_call(
        _glu_body,
        name="glu",
        grid=(t // tm, CONV_CH // TN_IN),
        in_specs=[
            pl.BlockSpec((tm, D_MODEL), lambda m, n: (m, 0)),
            pl.BlockSpec((D_MODEL, TN_IN), lambda m, n: (0, a0 + n)),
            pl.BlockSpec((1, TN_IN), lambda m, n: (0, a0 + n)),
            pl.BlockSpec((D_MODEL, TN_IN), lambda m, n: (0, b0 + n)),
            pl.BlockSpec((1, TN_IN), lambda m, n: (0, b0 + n)),
        ],
        out_specs=pl.BlockSpec((tm, TN_IN), lambda m, n: (m, n)),
        out_shape=jax.ShapeDtypeStruct((t, CONV_CH), F32),
        compiler_params=_params(2),
    )(xn, w_in, b_in, w_in, b_in)


def _attn_prompt_body(sink_ref, q_ref, kvp_ref, kvc_ref, a_ref, at_sc):
    i = pl.program_id(0)
    kv = jnp.concatenate([kvp_ref[...], kvc_ref[...]], axis=0)
    k = kv[:, :KV_WIDTH].astype(BF16)
    vt = kv[:, KV_WIDTH:].T.astype(BF16)
    qt = q_ref[...].astype(F32).T.astype(BF16)
    n_cols = GROUP * PAIR
    n_keys = WINDOW + PAIR
    row = lax.broadcasted_iota(jnp.int32, (n_keys, n_cols), 0)
    col = lax.broadcasted_iota(jnp.int32, (n_keys, n_cols), 1)
    first = jnp.where((col & (PAIR - 1)) >= CHUNK, CHUNK, 0)
    gcol = lax.shift_right_logical(lax.broadcasted_iota(jnp.int32, (1, n_cols), 1),
                                   PAIR.bit_length() - 1)
    biases = []
    for p in range(TQ_ATTN // PAIR):
        pos = row + (i * TQ_ATTN + p * PAIR - WINDOW)
        ok = (row >= first) & (row < first + WINDOW + CHUNK) & (pos >= 0)
        biases.append(jnp.where(ok, 0.0, NEG))

    def scores(p, h):
        w0 = p * PAIR
        kh = k[w0:w0 + n_keys, h * HEAD_DIM:(h + 1) * HEAD_DIM]
        rhs = jnp.concatenate(
            [qt[(h * GROUP + g) * HEAD_DIM:(h * GROUP + g + 1) * HEAD_DIM, w0:w0 + PAIR]
             for g in range(GROUP)], axis=1)
        return jnp.dot(kh, rhs, preferred_element_type=F32) + biases[p]

    def finish(p, h, st):
        w0 = p * PAIR
        sink = jnp.full((1, n_cols), sink_ref[h * GROUP], F32)
        for g in range(1, GROUP):
            sink = jnp.where(gcol == g, sink_ref[h * GROUP + g], sink)
        m = jnp.maximum(jnp.max(st, axis=0, keepdims=True), sink)
        pt = jnp.exp(st - m)
        denom = jnp.sum(pt, axis=0, keepdims=True) + jnp.exp(sink - m)
        ot = jnp.dot(vt[h * HEAD_DIM:(h + 1) * HEAD_DIM, w0:w0 + n_keys], pt.astype(BF16),
                     preferred_element_type=F32) / denom
        for g in range(GROUP):
            r0 = (h * GROUP + g) * HEAD_DIM
            at_sc[r0:r0 + HEAD_DIM, w0:w0 + PAIR] = ot[:, g * PAIR:(g + 1) * PAIR]

    items = [(p, h) for p in range(TQ_ATTN // PAIR) for h in range(N_KV_HEADS)]
    st_next = scores(*items[0])
    for n, (p, h) in enumerate(items):
        st = st_next
        if n + 1 < len(items):
            st_next = scores(*items[n + 1])
        finish(p, h, st)
    a_ref[...] = at_sc[...].T.astype(BF16)


def _attn_prompt(sinks, q, kv):
    r = TQ_ATTN // WINDOW
    return pl.pallas_call(
        _attn_prompt_body,
        name="attn_prompt",
        grid=(T_P // TQ_ATTN,),
        in_specs=[
            pl.BlockSpec(memory_space=pltpu.SMEM),
            pl.BlockSpec((TQ_ATTN, ATTN_WIDTH), lambda i: (i, 0)),
            pl.BlockSpec((WINDOW, 2 * KV_WIDTH), lambda i: (jnp.maximum(i * r - 1, 0), 0)),
            pl.BlockSpec((TQ_ATTN, 2 * KV_WIDTH), lambda i: (i, 0)),
        ],
        out_specs=pl.BlockSpec((TQ_ATTN, ATTN_WIDTH), lambda i: (i, 0)),
        out_shape=jax.ShapeDtypeStruct((T_P, ATTN_WIDTH), BF16),
        scratch_shapes=[pltpu.VMEM((ATTN_WIDTH, TQ_ATTN), F32)],
        compiler_params=_params(1),
    )(sinks, q, kv, kv)


def _attn_sample_body(sink_ref, q_ref, kvc_ref, ck_ref, cv_ref, a_ref):
    kvc = kvc_ref[...]
    kw = jnp.concatenate([ck_ref[0], kvc[:, :KV_WIDTH]], axis=0).astype(BF16)
    vw = jnp.concatenate([cv_ref[0], kvc[:, KV_WIDTH:]], axis=0).astype(BF16)
    qc = q_ref[...]
    for h in range(N_KV_HEADS):
        kh = kw[:, h * HEAD_DIM:(h + 1) * HEAD_DIM]
        vh = vw[:, h * HEAD_DIM:(h + 1) * HEAD_DIM]
        qg = jnp.concatenate(
            [qc[:, (h * GROUP + g) * HEAD_DIM:(h * GROUP + g + 1) * HEAD_DIM] for g in range(GROUP)],
            axis=0)
        s = lax.dot_general(qg, kh, (((1,), (1,)), ((), ())), preferred_element_type=F32)
        sink = jnp.concatenate(
            [jnp.full((L_S, 1), sink_ref[h * GROUP + g], F32) for g in range(GROUP)], axis=0)
        m = jnp.maximum(jnp.max(s, axis=1, keepdims=True), sink)
        p = jnp.exp(s - m)
        denom = jnp.sum(p, axis=1, keepdims=True) + jnp.exp(sink - m)
        o = jnp.dot(p.astype(BF16), vh, preferred_element_type=F32) / denom
        for g in range(GROUP):
            c0 = (h * GROUP + g) * HEAD_DIM
            a_ref[:, c0:c0 + HEAD_DIM] = o[g * L_S:(g + 1) * L_S].astype(BF16)


def _attn_sample(sinks, q, kv, cache_k, cache_v):
    return pl.pallas_call(
        _attn_sample_body,
        name="attn_sample",
        grid=(N_STREAMS,),
        in_specs=[
            pl.BlockSpec(memory_space=pltpu.SMEM),
            pl.BlockSpec((L_S, ATTN_WIDTH), lambda b: (b, 0)),
            pl.BlockSpec((L_S, 2 * KV_WIDTH), lambda b: (b, 0)),
            pl.BlockSpec((1, WINDOW, KV_WIDTH), lambda b: (b, 0, 0)),
            pl.BlockSpec((1, WINDOW, KV_WIDTH), lambda b: (b, 0, 0)),
        ],
        out_specs=pl.BlockSpec((L_S, ATTN_WIDTH), lambda b: (b, 0)),
        out_shape=jax.ShapeDtypeStruct((T_S, ATTN_WIDTH), BF16),
        compiler_params=_params(1),
    )(sinks, q, kv, cache_k, cache_v)


def _conv_shifts(prev, cur_ref, ext_sc, sh_sc, rows):
    ext_sc[0:HIST_PAD, :] = prev
    ext_sc[HIST_PAD:HIST_PAD + rows, :] = cur_ref[...]
    ext_sc[HIST_PAD + rows:, :] = jnp.zeros((SUBLANES, CONV_CH), F32)
    lead = HIST_PAD - HIST
    span = rows + HIST_PAD - SUBLANES
    for s in range(SUBLANES):
        sh_sc[s, 0:span, :] = ext_sc[lead + s:lead + s + span, :]


def _conv_rows(r0, w_ref, b_ref, lg_ref, lb_ref, o_ref, sh_sc):
    acc = jnp.zeros((SUB_CONV, CONV_CH), F32) + b_ref[...]
    for k in range(CONV_WIDTH):
        a, s = divmod(k, SUBLANES)
        r = r0 + a * SUBLANES
        acc = acc + w_ref[k:k + 1, :] * sh_sc[s, r:r + SUB_CONV, :]
    mu = jnp.mean(acc, axis=-1, keepdims=True)
    d = acc - mu
    var = jnp.mean(d * d, axis=-1, keepdims=True)
    yn = d * lax.rsqrt(var + EPS) * lg_ref[...] + lb_ref[...]
    out = yn * _sigmoid(yn)
    o_ref[r0:r0 + SUB_CONV, :] = out.astype(BF16)
    return out


def _gates_conv_body(xn_ref, w_ref, b_ref, prev_ref, cur_ref, cw_ref, cb_ref, lg_ref, lb_ref, zero_ref,
                     gate_ref, c_ref, ext_sc, sh_sc, *, rows, fresh):
    if fresh:
        step = pl.program_id(0) * N_GATE_TILES + pl.program_id(1)
        prev = jnp.where(step > 0, prev_ref[...], 0.0)
    else:
        prev = prev_ref[0]
    _conv_shifts(prev, cur_ref, ext_sc, sh_sc, rows)
    n_chunks = TN_GATES // MXU_N
    w = [w_ref[:, c * MXU_N:(c + 1) * MXU_N].astype(BF16) for c in range(n_chunks)]
    zero16 = None
    for i in range(rows // PIECE_CONV):
        r = slice(i * ROW_CHUNK, (i + 1) * ROW_CHUNK)
        xr = xn_ref[r, :]
        if zero16 is not None:
            top = jnp.concatenate([xr[0:BF16_ROWS, 0:LANES] + zero16, xr[0:BF16_ROWS, LANES:]], axis=1)
            xr = jnp.concatenate([top, xr[BF16_ROWS:, :]], axis=0)
        accs = [jnp.dot(xr, w[c], preferred_element_type=F32) for c in range(n_chunks)]
        outs = [_conv_rows(r0, cw_ref, cb_ref, lg_ref, lb_ref, c_ref, sh_sc)
                for r0 in range(i * PIECE_CONV, (i + 1) * PIECE_CONV, SUB_CONV)]
        bits = [lax.bitcast_convert_type(o[g * SUBLANES:(g + 1) * SUBLANES, 0:LANES], jnp.int32)
                for o in outs for g in range(SUB_CONV // SUBLANES)]
        zero8 = lax.bitcast_convert_type(functools.reduce(jnp.bitwise_or, bits) & zero_ref[...], F32)
        zero16 = jnp.concatenate([zero8, zero8], axis=0).astype(BF16)
        for c in range(n_chunks):
            cols = slice(c * MXU_N, (c + 1) * MXU_N)
            gate_ref[r, cols] = _sigmoid(accs[c] + b_ref[:, cols]).astype(BF16)


def _gates_conv(xn, w_in, b_in, hist, u, conv_consts, tm, rows, fresh):
    t = xn.shape[0]
    n0 = OFF_GATES // TN_GATES
    tile = lambda m, n: m * N_GATE_TILES + n
    if fresh:
        r = rows // HIST_PAD
        prev_spec = pl.BlockSpec((HIST_PAD, CONV_CH), lambda m, n: (jnp.maximum(tile(m, n) * r - 1, 0), 0))
    else:
        prev_spec = pl.BlockSpec((1, HIST_PAD, CONV_CH), lambda m, n: (tile(m, n), 0, 0))
    const_map = lambda m, n: (0, 0)
    return pl.pallas_call(
        functools.partial(_gates_conv_body, rows=rows, fresh=fresh),
        name="gates_conv",
        grid=(t // tm, N_GATE_TILES),
        in_specs=[
            pl.BlockSpec((tm, D_MODEL), lambda m, n: (m, 0)),
            pl.BlockSpec((D_MODEL, TN_GATES), lambda m, n: (0, n0 + n)),
            pl.BlockSpec((1, TN_GATES), lambda m, n: (0, n0 + n)),
            prev_spec,
            pl.BlockSpec((rows, CONV_CH), lambda m, n: (tile(m, n), 0)),
            pl.BlockSpec((CONV_WIDTH, CONV_CH), const_map),
            pl.BlockSpec((1, CONV_CH), const_map),
            pl.BlockSpec((1, CONV_CH), const_map),
            pl.BlockSpec((1, CONV_CH), const_map),
            pl.BlockSpec((SUBLANES, LANES), const_map),
        ],
        out_specs=[
            pl.BlockSpec((tm, TN_GATES), lambda m, n: (m, n)),
            pl.BlockSpec((rows, CONV_CH), lambda m, n: (tile(m, n), 0)),
        ],
        out_shape=[
            jax.ShapeDtypeStruct((t, 2 * D_MODEL), BF16),
            jax.ShapeDtypeStruct((t, CONV_CH), BF16),
        ],
        scratch_shapes=[
            pltpu.VMEM((HIST_PAD + rows + SUBLANES, CONV_CH), F32),
            pltpu.VMEM((SUBLANES, HIST_PAD + rows, CONV_CH), F32),
        ],
        compiler_params=_params(2),
    )(xn, w_in, b_in, hist, u, *conv_consts, jnp.zeros((SUBLANES, LANES), jnp.int32))


def _in_proj(x, hist, norm_g, w_in, b_in, conv_consts, tm, fresh):
    t = x.shape[0]
    xn, q, kv = _qkv(x, norm_g, w_in, b_in, tm)
    tw = min(TM_WIDE, t)
    u = _glu(xn, w_in, b_in, tw)
    gates, c = _gates_conv(xn, w_in, b_in, u if fresh else hist, u, conv_consts, tw,
                           tw // N_GATE_TILES, fresh)
    return q, kv, u, gates, c


def _mix_body(ap_ref, as_ref, cp_ref, cs_ref, gp_ref, gs_ref, xp_ref, xs_ref, wao_ref, wco_ref,
              wout_ref, g2_ref, wr_ref, br_ref, x1_ref, xn_ref, lg_ref):
    m = pl.program_id(0)
    is_sample = m >= T_P // TM_MIX
    a = jnp.where(is_sample, as_ref[...], ap_ref[...])
    c = jnp.where(is_sample, cs_ref[...], cp_ref[...])
    gate = jnp.where(is_sample, gs_ref[...], gp_ref[...])
    x = jnp.where(is_sample, xs_ref[...], xp_ref[...])
    pa = jnp.dot(a, wao_ref[...], preferred_element_type=F32)
    pc = jnp.dot(c, wco_ref[...], preferred_element_type=F32)
    y = (gate[:, :D_MODEL].astype(F32) * pa + gate[:, D_MODEL:].astype(F32) * pc).astype(BF16)
    x1 = x + jnp.dot(y, wout_ref[...], preferred_element_type=F32)
    x1_ref[...] = x1
    ms = jnp.mean(x1 * x1, axis=-1, keepdims=True)
    xn = x1 * lax.rsqrt(ms + EPS) * g2_ref[...]
    _store_packed_rows(xn_ref, xn)
    xn_hi = xn.astype(BF16)
    xn_lo = (xn - xn_hi.astype(F32)).astype(BF16)
    r_hi = jnp.dot(xn_hi, wr_ref[...], preferred_element_type=F32)
    r_lo = jnp.dot(xn_lo, wr_ref[...], preferred_element_type=F32)
    lg_ref[...] = r_hi + pltpu.roll(r_hi, ROUTER_PAD - N_ROUTER, axis=1) + r_lo + br_ref[...]


def _mix(a_p, a_s, c_p, c_s, g_p, g_s, x_p, x_s, wao, wco, wout, g2, wr, br):
    n_p = T_P // TM_MIX
    n_s = T_S // TM_MIX
    prompt_map = lambda m: (jnp.minimum(m, n_p - 1), 0)
    sample_map = lambda m: (jnp.clip(m - n_p, 0, n_s - 1), 0)
    row_map = lambda m: (m, 0)
    const_map = lambda m: (0, 0)
    once = pl.Buffered(1)

    def pair(width):
        return [pl.BlockSpec((TM_MIX, width), prompt_map), pl.BlockSpec((TM_MIX, width), sample_map)]

    return pl.pallas_call(
        _mix_body,
        name="mix",
        grid=(T // TM_MIX,),
        in_specs=pair(ATTN_WIDTH) + pair(CONV_CH) + pair(2 * D_MODEL) + pair(D_MODEL) + [
            pl.BlockSpec((ATTN_WIDTH, D_MODEL), const_map, pipeline_mode=once),
            pl.BlockSpec((CONV_CH, D_MODEL), const_map, pipeline_mode=once),
            pl.BlockSpec((D_MODEL, D_MODEL), const_map, pipeline_mode=once),
            pl.BlockSpec((1, D_MODEL), const_map),
            pl.BlockSpec((D_MODEL, ROUTER_PAD), const_map),
            pl.BlockSpec((1, ROUTER_PAD), const_map),
        ],
        out_specs=[
            pl.BlockSpec((TM_MIX, D_MODEL), row_map),
            pl.BlockSpec((TM_MIX * ROW_TILE, LANES), row_map),
            pl.BlockSpec((TM_MIX, ROUTER_PAD), row_map),
        ],
        out_shape=[
            jax.ShapeDtypeStruct((T, D_MODEL), F32),
            jax.ShapeDtypeStruct((T * ROW_TILE, LANES), U32),
            jax.ShapeDtypeStruct((T, ROUTER_PAD), F32),
        ],
        compiler_params=_params(1),
    )(a_p, a_s, c_p, c_s, g_p, g_s, x_p, x_s, wao, wco, wout, g2, wr, br)


REC_W, REC_E, REC_POS = 0, 2, 4


def _router_body(lg_ref, rec_ref, idx_ref, cnt_ref, carry_sc):
    lane = lax.broadcasted_iota(jnp.int32, (R_ROUTE, ROUTER_PAD), 1).astype(F32)
    ltri = (lax.broadcasted_iota(jnp.int32, (R_ROUTE, R_ROUTE), 1)
            < lax.broadcasted_iota(jnp.int32, (R_ROUTE, R_ROUTE), 0)).astype(BF16)

    def pick(vals, idx):
        return jnp.sum(jnp.where(lane == idx, vals, 0.0), axis=1, keepdims=True)

    def first_lane(mask):
        return jnp.min(jnp.where(mask, lane, float(ROUTER_PAD)), axis=1, keepdims=True)

    def masked_max(mask, v):
        return jnp.max(jnp.where(mask, v, NEG), axis=1, keepdims=True)

    carry_sc[...] = jnp.zeros_like(carry_sc)

    def count_pass(i, carry_unused):
        rows = pl.ds(pl.multiple_of(i * R_ROUTE, R_ROUTE), R_ROUTE)
        lg = lg_ref[rows, :]
        gmask = lane < N_GROUPS
        gmax = masked_max(gmask, lg)
        p_grp = 1.0 / jnp.sum(jnp.where(gmask, jnp.exp(lg - gmax), 0.0), axis=1, keepdims=True)
        grp = first_lane(gmask & (lg == gmax))
        lo = N_GROUPS + EXPERTS_PER_GROUP * grp
        emask = (lane >= lo) & (lane < lo + EXPERTS_PER_GROUP)
        e1 = masked_max(emask, lg)
        i1 = first_lane(emask & (lg == e1))
        emask2 = emask & (lane != i1)
        e2 = masked_max(emask2, lg)
        i2 = first_lane(emask2 & (lg == e2))
        t = jnp.exp(e2 - e1)
        w1 = p_grp / (1.0 + t)
        w2 = p_grp * t / (1.0 + t)
        x1 = i1 - N_GROUPS
        x2 = i2 - N_GROUPS
        onehot = ((lane == x1) | (lane == x2 + N_EXPERTS)).astype(BF16)
        carry = carry_sc[0:1, :]
        before = jnp.dot(ltri, onehot, preferred_element_type=F32) + carry
        rank1 = pick(before, x1)
        rank2 = pick(before, x2 + N_EXPERTS)
        carry_sc[0:1, :] = carry + jnp.sum(onehot.astype(F32), axis=0, keepdims=True)
        rec = jnp.where(lane == REC_W, w1, 0.0)
        for ln, v in ((REC_W + 1, w2), (REC_E, x1), (REC_E + 1, x2),
                      (REC_POS, rank1), (REC_POS + 1, rank2)):
            rec = jnp.where(lane == ln, v, rec)
        rec_ref[rows, :] = rec
        return carry_unused

    lax.fori_loop(0, T // R_ROUTE, count_pass, 0)

    tot = carry_sc[0:1, :]
    counts = tot + pltpu.roll(tot, ROUTER_PAD - N_EXPERTS, axis=1)
    lane1 = lax.broadcasted_iota(jnp.int32, (1, ROUTER_PAD), 1)
    counts = jnp.where(lane1 < N_EXPERTS, counts, 0.0)
    cnt_ref[...] = jnp.broadcast_to(counts, cnt_ref.shape).astype(jnp.int32)
    nblk = ((counts.astype(jnp.int32) + (BM_MOE - 1)) >> (BM_MOE.bit_length() - 1)).astype(F32)
    upper = (lax.broadcasted_iota(jnp.int32, (ROUTER_PAD, ROUTER_PAD), 0)
             < lax.broadcasted_iota(jnp.int32, (ROUTER_PAD, ROUTER_PAD), 1)).astype(BF16)
    blk_start = jnp.dot(jnp.broadcast_to(nblk, (SUBLANES, ROUTER_PAD)).astype(BF16), upper,
                        preferred_element_type=F32)[0:1, :]
    row0 = blk_start * BM_MOE

    def place_pass(i, carry_unused):
        rows = pl.ds(pl.multiple_of(i * R_ROUTE, R_ROUTE), R_ROUTE)
        rec = rec_ref[rows, :]
        x1 = pick(rec, REC_E)
        x2 = pick(rec, REC_E + 1)
        p1 = pick(row0, x1) + pick(rec, REC_POS)
        p2 = pick(row0, x2) + pick(tot, x2) + pick(rec, REC_POS + 1)
        rec = jnp.where(lane == REC_POS, p1 * ROW_TILE, rec)
        rec = jnp.where(lane == REC_POS + 1, p2 * ROW_TILE, rec)
        rec_ref[rows, :] = rec
        idx_ref[i] = rec.T[0:SUBLANES, :].astype(jnp.int32)
        return carry_unused

    lax.fori_loop(0, T // R_ROUTE, place_pass, 0)


def _router(logits):
    return pl.pallas_call(
        _router_body,
        name="router",
        grid=(1,),
        in_specs=[pl.BlockSpec((T, ROUTER_PAD), lambda i: (0, 0))],
        out_specs=[
            pl.BlockSpec((T, ROUTER_PAD), lambda i: (0, 0)),
            pl.BlockSpec((T // R_ROUTE, SUBLANES, R_ROUTE), lambda i: (0, 0, 0)),
            pl.BlockSpec((SUBLANES, ROUTER_PAD), lambda i: (0, 0)),
        ],
        out_shape=[
            jax.ShapeDtypeStruct((T, ROUTER_PAD), F32),
            jax.ShapeDtypeStruct((T // R_ROUTE, SUBLANES, R_ROUTE), jnp.int32),
            jax.ShapeDtypeStruct((SUBLANES, ROUTER_PAD), jnp.int32),
        ],
        scratch_shapes=[pltpu.VMEM((SUBLANES, ROUTER_PAD), F32)],
        compiler_params=_params(1),
    )(logits)


def _route(logits):
    rec, idx, cnt = _router(logits)
    idx = idx.transpose(1, 0, 2).reshape(SUBLANES, T)
    eid = idx[REC_E:REC_E + TOP_K].reshape(-1)
    pos = idx[REC_POS:REC_POS + TOP_K].reshape(-1)
    counts = cnt[0, :N_EXPERTS]
    nblk = (counts + BM_MOE - 1) // BM_MOE
    blk_end = jnp.cumsum(nblk)
    blk_start = blk_end - nblk
    row_start = jnp.cumsum(counts) - counts

    order = jnp.argsort(eid, stable=True)
    tok_sorted = jnp.concatenate([((order % T) * ROW_TILE).astype(jnp.int32),
                                  jnp.zeros((BM_MOE,), jnp.int32)])
    n_used = blk_end[-1]
    j = jnp.minimum(jnp.arange(N_BLK_MOE, dtype=jnp.int32), n_used - 1)
    blk_e = jnp.minimum(jnp.searchsorted(blk_end, j, side='right'), N_EXPERTS - 1).astype(jnp.int32)
    blk_src = (row_start[blk_e] + (j - blk_start[blk_e]) * BM_MOE).astype(jnp.int32)
    e_ids = jnp.arange(N_EXPERTS, dtype=jnp.int32)
    used = nblk > 0
    e_slot = ((jnp.cumsum(used.astype(jnp.int32)) - 1) % 2).astype(jnp.int32)
    nxt = lax.cummin(jnp.where(used, e_ids, N_EXPERTS), reverse=True)
    nxt = jnp.concatenate([nxt[1:], jnp.full((1,), N_EXPERTS, jnp.int32)])
    e_next = jnp.where(nxt < N_EXPERTS, nxt, e_ids).astype(jnp.int32)
    blk_first = (j == blk_start[blk_e]).astype(jnp.int32)
    plan = jnp.stack([blk_e, blk_src, blk_first, e_slot[blk_e], e_next[blk_e],
                      jnp.zeros_like(blk_e)])
    return rec, pos, tok_sorted, plan, n_used.astype(jnp.int32).reshape(1)


PLAN_E, PLAN_SRC, PLAN_FIRST, PLAN_WSLOT, PLAN_NEXT_E, PLAN_ZERO = range(6)
WEIGHT_DMA_PRIORITY = 1
GATHER_DMA_PRIORITY = 0


def _moe_body(plan, n_used, tok, x_hbm, wg_hbm, wu_hbm, wd_hbm, o_ref,
              xb, wg_st, wu_st, wd_st, wg_bf, wu_bf, wd_bf, sem_x, sem_w):
    j = pl.program_id(0)
    nu = n_used[0]

    def row_copy(src, slot, r):
        t8 = pl.multiple_of(tok[src + r], ROW_TILE)
        return pltpu.make_async_copy(x_hbm.at[pl.ds(t8, ROW_TILE)], xb.at[slot, _row_tile(r)],
                                     sem_x.at[slot])

    def block_wait(slot):
        pltpu.make_async_copy(x_hbm.at[pl.ds(0, BM_MOE * ROW_TILE)], xb.at[slot],
                              sem_x.at[slot]).wait()

    def weight_copies(e, ws):
        return (pltpu.make_async_copy(wg_hbm.at[e], wg_st.at[ws], sem_w.at[ws, 0]),
                pltpu.make_async_copy(wu_hbm.at[e], wu_st.at[ws], sem_w.at[ws, 1]),
                pltpu.make_async_copy(wd_hbm.at[e], wd_st.at[ws], sem_w.at[ws, 2]))

    @pl.when(j == 0)
    def _():
        for cp in weight_copies(plan[PLAN_E, 0], 0):
            cp.start(priority=WEIGHT_DMA_PRIORITY)
        src = plan[PLAN_SRC, 0]

        def body(r, carry):
            row_copy(src, 0, r).start(priority=GATHER_DMA_PRIORITY)
            return carry
        lax.fori_loop(0, BM_MOE, body, 0, unroll=8)

    @pl.when((j < nu) & (plan[PLAN_FIRST, jnp.minimum(j, N_BLK_MOE - 1)] == 1))
    def _():
        e = plan[PLAN_E, j]
        ws = plan[PLAN_WSLOT, j]
        e_next = plan[PLAN_NEXT_E, j]

        @pl.when(e_next != e)
        def _():
            for cp in weight_copies(e_next, 1 - ws):
                cp.start(priority=WEIGHT_DMA_PRIORITY)

        for cp in weight_copies(e, ws):
            cp.wait()
        wg_bf[...] = wg_st[ws].astype(BF16)
        wu_bf[...] = wu_st[ws].astype(BF16)
        wd_bf[...] = wd_st[ws].astype(BF16)

    @pl.when(j < nu)
    def _():
        slot = j % 2
        block_wait(slot)
        nsrc = plan[PLAN_SRC, jnp.minimum(j + 1, N_BLK_MOE - 1)]
        halves = [_load_packed_rows(xb.at[slot], BM_MOE, s) for s in range(ROW_TILE)]
        x = jnp.concatenate([lo for lo, _ in halves] + [hi for _, hi in halves], axis=1).astype(BF16)
        zero = plan[PLAN_ZERO, 0]
        group = BM_MOE // 8
        issued = [0]

        def gather_after(v):
            base = nsrc
            if v is not None:
                base = base + (lax.bitcast_convert_type(v[0:1, 0:1], jnp.int32)[0, 0] & zero)
            for r in range(issued[0], issued[0] + group):
                row_copy(base, 1 - slot, r).start(priority=GATHER_DMA_PRIORITY)
            issued[0] += group

        gather_after(None)
        gather_after(halves[ROW_TILE - 1][1])
        acc = None
        for c in range(D_EXPERT // MXU_N):
            cols = slice(c * MXU_N, (c + 1) * MXU_N)
            g = jnp.dot(x, wg_bf[:, cols], preferred_element_type=F32)
            gather_after(g)
            u = jnp.dot(x, wu_bf[:, cols], preferred_element_type=F32)
            gather_after(u)
            h = (g * _sigmoid(g) * u).astype(BF16)
            part = jnp.dot(h, wd_bf[cols, :], preferred_element_type=F32)
            if c == 0:
                gather_after(part)
                gather_after(part[:, D_MODEL - LANES:])
            acc = part if acc is None else acc + part
        assert issued[0] == BM_MOE
        _store_packed_rows(o_ref, acc)

    @pl.when(j == nu)
    def _():
        block_wait(j % 2)

    @pl.when(j >= nu)
    def _():
        o_ref[...] = jnp.zeros_like(o_ref)


def _moe(plan, n_used, tok_sorted, xn_packed, w_g, w_u, w_d):
    grid_spec = pltpu.PrefetchScalarGridSpec(
        num_scalar_prefetch=3,
        grid=(N_BLK_MOE + 1,),
        in_specs=[pl.BlockSpec(memory_space=pl.ANY)] * 4,
        out_specs=pl.BlockSpec((BM_MOE * ROW_TILE, LANES), lambda j, pn, nu, tk: (j, 0)),
        scratch_shapes=[
            pltpu.VMEM((2, BM_MOE * ROW_TILE, LANES), U32),
            pltpu.VMEM((2, D_MODEL, D_EXPERT), F32),
            pltpu.VMEM((2, D_MODEL, D_EXPERT), F32),
            pltpu.VMEM((2, D_EXPERT, D_MODEL), F32),
            pltpu.VMEM((D_MODEL, D_EXPERT), BF16),
            pltpu.VMEM((D_MODEL, D_EXPERT), BF16),
            pltpu.VMEM((D_EXPERT, D_MODEL), BF16),
            pltpu.SemaphoreType.DMA((2,)),
            pltpu.SemaphoreType.DMA((2, 3)),
        ],
    )
    return pl.pallas_call(
        _moe_body,
        name="experts",
        grid_spec=grid_spec,
        out_shape=jax.ShapeDtypeStruct(((N_BLK_MOE + 1) * BM_MOE * ROW_TILE, LANES), U32),
        compiler_params=_params(1),
    )(plan, n_used, tok_sorted, xn_packed, w_g, w_u, w_d)


def _combine_body(pos, x1_ref, w_ref, gf_ref, o_hbm, y_ref, rb, sem, *, first_tile):
    m = pl.program_id(0)
    last = pl.num_programs(0) - 1

    def row_copy(base, slot, r, k):
        p8 = pl.multiple_of(pos[k * T + base + r], ROW_TILE)
        return pltpu.make_async_copy(o_hbm.at[pl.ds(p8, ROW_TILE)], rb.at[slot, k, _row_tile(r)],
                                     sem.at[slot])

    def tile_wait(slot):
        for k in range(TOP_K):
            pltpu.make_async_copy(o_hbm.at[pl.ds(0, TM_OUT * ROW_TILE)], rb.at[slot, k],
                                  sem.at[slot]).wait()

    def tile_base(mm):
        return (first_tile + mm) * TM_OUT

    @pl.when(m == 0)
    def _():
        base = tile_base(0)

        def body(r, carry):
            for k in range(TOP_K):
                row_copy(base, 0, r, k).start(priority=k)
            return carry
        lax.fori_loop(0, TM_OUT, body, 0, unroll=4)

    slot = m % 2
    tile_wait(slot)
    nbase = tile_base(jnp.minimum(m + 1, last))
    rows = TM_OUT // ROW_TILE
    w0 = w_ref[:, 0:1]
    w1 = w_ref[:, 1:2]
    ss = jnp.zeros((TM_OUT, 1), F32)
    for s in range(ROW_TILE):
        for r in range(s * rows, (s + 1) * rows):
            for k in range(TOP_K):
                row_copy(nbase, 1 - slot, r, k).start(priority=k)
        lo0, hi0 = _load_packed_rows(rb.at[slot, 0], TM_OUT, s)
        lo1, hi1 = _load_packed_rows(rb.at[slot, 1], TM_OUT, s)
        for off, r0, r1 in ((s * LANES, lo0, lo1), (HALF_D + s * LANES, hi0, hi1)):
            y = x1_ref[:, off:off + LANES] + w0 * r0 + w1 * r1
            ss = ss + jnp.sum(y * y, axis=-1, keepdims=True)
            y_ref[:, off:off + LANES] = y
    scale = lax.rsqrt(ss * (1.0 / D_MODEL) + EPS)
    y_ref[...] = y_ref[...] * scale * gf_ref[...]

    @pl.when(m == last)
    def _():
        tile_wait(1 - slot)


def _combine(pos, x1, wts, gf, out_sorted, first_tile, n_tiles):
    grid_spec = pltpu.PrefetchScalarGridSpec(
        num_scalar_prefetch=1,
        grid=(n_tiles,),
        in_specs=[
            pl.BlockSpec((TM_OUT, D_MODEL), lambda m, p: (first_tile + m, 0)),
            pl.BlockSpec((TM_OUT, ROUTER_PAD), lambda m, p: (first_tile + m, 0)),
            pl.BlockSpec((1, D_MODEL), lambda m, p: (0, 0)),
            pl.BlockSpec(memory_space=pl.ANY),
        ],
        out_specs=pl.BlockSpec((TM_OUT, D_MODEL), lambda m, p: (m, 0)),
        scratch_shapes=[
            pltpu.VMEM((2, TOP_K, TM_OUT * ROW_TILE, LANES), U32),
            pltpu.SemaphoreType.DMA((2,)),
        ],
    )
    return pl.pallas_call(
        functools.partial(_combine_body, first_tile=first_tile),
        name="combine",
        grid_spec=grid_spec,
        out_shape=jax.ShapeDtypeStruct((n_tiles * TM_OUT, D_MODEL), F32),
        compiler_params=_params(1),
    )(pos, x1, wts, gf, out_sorted)


def kernel(x_prompt, x_sample, cache_k, cache_v, state_conv, norm1_g, w_in, b_in, attn_sinks,
           w_attn_o, conv_dw, conv_dw_b, conv_ln_g, conv_ln_b, w_conv_o, w_out, norm2_g,
           w_router_group, b_router_group, w_router_expert, b_router_expert, w_e_gate, w_e_up,
           w_e_down, final_norm_g):
    x_p = x_prompt.reshape(T_P, D_MODEL)
    x_s = x_sample.reshape(T_S, D_MODEL)
    g1 = norm1_g[0][None, :]
    b1 = b_in[0][None, :]
    conv_consts = (conv_dw[0], conv_dw_b[0][None, :], conv_ln_g[0][None, :], conv_ln_b[0][None, :])
    hist_pad = jnp.pad(state_conv[0], ((0, 0), (HIST_PAD - HIST, 0), (0, 0)))
    q_p, kv_p, u_p, gates_p, c_p = _in_proj(x_p, None, g1, w_in[0], b1, conv_consts, TM_IN, True)
    q_s, kv_s, u_s, gates_s, c_s = _in_proj(x_s, hist_pad, g1, w_in[0], b1, conv_consts, T_S, False)

    sinks = attn_sinks[0]
    a_p = _attn_prompt(sinks, q_p, kv_p)
    ck = cache_k[0].reshape(N_STREAMS, WINDOW, KV_WIDTH)
    cv = cache_v[0].reshape(N_STREAMS, WINDOW, KV_WIDTH)
    a_s = _attn_sample(sinks, q_s, kv_s, ck, cv)

    w_r = jnp.concatenate([w_router_group[0], w_router_expert[0]], axis=1)
    w_r_hi = w_r.astype(BF16)
    w_r_lo = (w_r - w_r_hi.astype(F32)).astype(BF16)
    w_r_cat = jnp.concatenate(
        [w_r_hi, w_r_lo, jnp.zeros((D_MODEL, ROUTER_PAD - 2 * N_ROUTER), BF16)], axis=1)
    b_r = jnp.concatenate([b_router_group[0], b_router_expert[0],
                           jnp.zeros((ROUTER_PAD - N_ROUTER,), F32)])[None, :]

    x1, xn2, logits = _mix(a_p, a_s, c_p, c_s, gates_p, gates_s, x_p, x_s,
                           w_attn_o[0].astype(BF16), w_conv_o[0].astype(BF16), w_out[0].astype(BF16),
                           norm2_g[0][None, :], w_r_cat, b_r)

    wts, pos, tok_sorted, plan, n_used = _route(logits)
    out_sorted = _moe(plan, n_used, tok_sorted, xn2, w_e_gate[0], w_e_up[0], w_e_down[0])
    gf = final_norm_g[None, :]
    y_p = _combine(pos, x1, wts, gf, out_sorted, 0, T_P // TM_OUT)
    y_s = _combine(pos, x1, wts, gf, out_sorted, T_P // TM_OUT, T_S // TM_OUT)

    kv_shape = (1, -1, WINDOW, N_KV_HEADS, HEAD_DIM)
    new_k_prompt = kv_p[T_P - WINDOW:, :KV_WIDTH].reshape(kv_shape)
    new_v_prompt = kv_p[T_P - WINDOW:, KV_WIDTH:].reshape(kv_shape)
    new_conv_prompt = u_p[T_P - HIST:].reshape(1, 1, HIST, CONV_CH)
    k_s = kv_s[:, :KV_WIDTH].reshape(N_STREAMS, L_S, KV_WIDTH)
    v_s = kv_s[:, KV_WIDTH:].reshape(N_STREAMS, L_S, KV_WIDTH)
    new_k_sample = jnp.concatenate([ck[:, L_S:], k_s], axis=1).reshape(kv_shape)
    new_v_sample = jnp.concatenate([cv[:, L_S:], v_s], axis=1).reshape(kv_shape)
    new_conv_sample = u_s.reshape(N_STREAMS, L_S, CONV_CH)[:, L_S - HIST:].reshape(
        1, N_STREAMS, HIST, CONV_CH)

    return (y_p.reshape(1, T_P, D_MODEL), y_s.reshape(N_STREAMS, L_S, D_MODEL),
            new_k_prompt, new_v_prompt, new_conv_prompt,
            new_k_sample, new_v_sample, new_conv_sample)
```

```python
import functools

import numpy as np
import jax
import jax.numpy as jnp
from jax import lax
from jax.experimental import pallas as pl
from jax.experimental.pallas import tpu as pltpu

F32 = jnp.float32
BF16 = jnp.bfloat16
U32 = jnp.uint32

D_MODEL = 2048
T_P = 8192
N_STREAMS = 8
L_S = 64
T_S = N_STREAMS * L_S
T = T_P + T_S
CHUNK = 64
WINDOW = 128
HEAD_DIM = 64
N_Q_HEADS = 16
N_KV_HEADS = 4
GROUP = N_Q_HEADS // N_KV_HEADS
ATTN_WIDTH = N_Q_HEADS * HEAD_DIM
KV_WIDTH = N_KV_HEADS * HEAD_DIM
CONV_CH = 1024
CONV_WIDTH = 31
HIST = CONV_WIDTH - 1
HIST_PAD = 32
SUBLANES = 8
LANES = 128
ROW_TILE = SUBLANES
HALF_D = D_MODEL // 2
HI_MASK = np.uint32(0xFFFF0000)
OFF_KV = ATTN_WIDTH
OFF_GLU = OFF_KV + 2 * KV_WIDTH
OFF_GATES = OFF_GLU + 2 * CONV_CH
IN_WIDTH = OFF_GATES + 2 * D_MODEL
N_GROUPS = 8
EXPERTS_PER_GROUP = 4
N_EXPERTS = N_GROUPS * EXPERTS_PER_GROUP
TOP_K = 2
D_EXPERT = 512
N_ROUTER = N_GROUPS + N_EXPERTS
ROUTER_PAD = 128
EPS = 1e-6
NEG = -1e30

VMEM_LIMIT = 56 * 1024 * 1024
MXU_N = 256

TM_IN = 1024
TM_WIDE = 2048
TN_IN = 512
TN_GATES = TN_IN
N_GATE_TILES = 2 * D_MODEL // TN_GATES
ROW_CHUNK = 256
W_STAGE_ROWS = 256
TQ_ATTN = 256
PAIR = 2 * CHUNK
BF16_ROWS = 16
SUB_CONV = 32
PIECE_CONV = 32
TM_MIX = 256
BM_MOE = 256
R_ROUTE = 256
TM_OUT = 256
N_SLOTS = T * TOP_K
N_BLK_MOE = -(-(N_SLOTS + N_EXPERTS * (BM_MOE - 1)) // BM_MOE)


def _sigmoid(x):
    return 1.0 / (1.0 + jnp.exp(-x))


def _params(n_axes):
    return pltpu.CompilerParams(dimension_semantics=("arbitrary",) * n_axes,
                                vmem_limit_bytes=VMEM_LIMIT)


def _store_packed_rows(ref, y):
    rows = y.shape[0]
    for s in range(ROW_TILE):
        lo = y[:, s * LANES:(s + 1) * LANES].astype(BF16).astype(F32)
        hi = y[:, HALF_D + s * LANES:HALF_D + (s + 1) * LANES].astype(BF16).astype(F32)
        word = (lax.bitcast_convert_type(hi, U32) & HI_MASK) | (lax.bitcast_convert_type(lo, U32) >> 16)
        ref[pl.ds(s, rows, stride=ROW_TILE), :] = word


def _row_tile(r):
    start = r * ROW_TILE
    return pl.ds(start if isinstance(r, int) else pl.multiple_of(start, ROW_TILE), ROW_TILE)


def _load_packed_rows(ref, rows, s):
    word = ref[pl.ds(s, rows, stride=ROW_TILE), :]
    lo = lax.bitcast_convert_type(word << 16, F32)
    hi = lax.bitcast_convert_type(word & HI_MASK, F32)
    return lo, hi


def _chunk_dot(xn, w_ref, b_ref, c):
    w = w_ref[:, c * MXU_N:(c + 1) * MXU_N].astype(BF16)
    return jnp.dot(xn, w, preferred_element_type=F32) + b_ref[:, c * MXU_N:(c + 1) * MXU_N]


def _qkv_body(x_ref, g_ref, w_hbm, b_ref, xn_ref, q_ref, kv_ref, w_bf, stage, sem, *, tm):
    n_stage = D_MODEL // W_STAGE_ROWS

    @pl.when(pl.program_id(0) == 0)
    def _():
        def chunk_copy(i):
            return pltpu.make_async_copy(
                w_hbm.at[pl.ds(i * W_STAGE_ROWS, W_STAGE_ROWS), pl.ds(0, OFF_GLU)],
                stage.at[i % 2], sem.at[i % 2])
        chunk_copy(0).start()
        for i in range(n_stage):
            if i + 1 < n_stage:
                chunk_copy(i + 1).start()
            chunk_copy(i).wait()
            w_bf[i * W_STAGE_ROWS:(i + 1) * W_STAGE_ROWS, :] = stage[i % 2].astype(BF16)

    for r0 in range(0, tm, ROW_CHUNK):
        rows = slice(r0, r0 + ROW_CHUNK)
        x = x_ref[rows, :]
        ms = jnp.mean(x * x, axis=-1, keepdims=True)
        xn = (x * lax.rsqrt(ms + EPS) * g_ref[...]).astype(BF16)
        xn_ref[rows, :] = xn
        for c in range(OFF_GLU // MXU_N):
            cols = slice(c * MXU_N, (c + 1) * MXU_N)
            acc = jnp.dot(xn, w_bf[:, cols], preferred_element_type=F32) + b_ref[:, cols]
            if c < ATTN_WIDTH // MXU_N:
                q_ref[rows, cols] = (acc * (HEAD_DIM ** -0.5)).astype(BF16)
            else:
                kv_ref[rows, c * MXU_N - ATTN_WIDTH:(c + 1) * MXU_N - ATTN_WIDTH] = acc


def _qkv(x, norm_g, w_in, b_in, tm):
    t = x.shape[0]
    return pl.pallas_call(
        functools.partial(_qkv_body, tm=tm),
        name="qkv",
        grid=(t // tm,),
        in_specs=[
            pl.BlockSpec((tm, D_MODEL), lambda m: (m, 0)),
            pl.BlockSpec((1, D_MODEL), lambda m: (0, 0)),
            pl.BlockSpec(memory_space=pl.ANY),
            pl.BlockSpec((1, OFF_GLU), lambda m: (0, 0)),
        ],
        out_specs=[
            pl.BlockSpec((tm, D_MODEL), lambda m: (m, 0)),
            pl.BlockSpec((tm, ATTN_WIDTH), lambda m: (m, 0)),
            pl.BlockSpec((tm, 2 * KV_WIDTH), lambda m: (m, 0)),
        ],
        out_shape=[
            jax.ShapeDtypeStruct((t, D_MODEL), BF16),
            jax.ShapeDtypeStruct((t, ATTN_WIDTH), BF16),
            jax.ShapeDtypeStruct((t, 2 * KV_WIDTH), F32),
        ],
        scratch_shapes=[
            pltpu.VMEM((D_MODEL, OFF_GLU), BF16),
            pltpu.VMEM((2, W_STAGE_ROWS, OFF_GLU), F32),
            pltpu.SemaphoreType.DMA((2,)),
        ],
        compiler_params=_params(1),
    )(x, norm_g, w_in, b_in)


def _glu_body(xn_ref, wa_ref, ba_ref, wb_ref, bb_ref, u_ref):
    xn = xn_ref[...]
    for c in range(TN_IN // MXU_N):
        a = _chunk_dot(xn, wa_ref, ba_ref, c)
        b = _chunk_dot(xn, wb_ref, bb_ref, c)
        u_ref[:, c * MXU_N:(c + 1) * MXU_N] = a * _sigmoid(b)


def _glu(xn, w_in, b_in, tm):
    t = xn.shape[0]
    a0 = OFF_GLU // TN_IN
    b0 = (OFF_GLU + CONV_CH) // TN_IN
    return pl.pallas_call(
        _glu_body,
        name="glu",
        grid=(t // tm, CONV_CH // TN_IN),
        in_specs=[
            pl.BlockSpec((tm, D_MODEL), lambda m, n: (m, 0)),
            pl.BlockSpec((D_MODEL, TN_IN), lambda m, n: (0, a0 + n)),
            pl.BlockSpec((1, TN_IN), lambda m, n: (0, a0 + n)),
            pl.BlockSpec((D_MODEL, TN_IN), lambda m, n: (0, b0 + n)),
            pl.BlockSpec((1, TN_IN), lambda m, n: (0, b0 + n)),
        ],
        out_specs=pl.BlockSpec((tm, TN_IN), lambda m, n: (m, n)),
        out_shape=jax.ShapeDtypeStruct((t, CONV_CH), F32),
        compiler_params=_params(2),
    )(xn, w_in, b_in, w_in, b_in)


def _attn_prompt_body(sink_ref, q_ref, kvp_ref, kvc_ref, a_ref, at_sc):
    i = pl.program_id(0)
    kv = jnp.concatenate([kvp_ref[...], kvc_ref[...]], axis=0)
    k = kv[:, :KV_WIDTH].astype(BF16)
    vt = kv[:, KV_WIDTH:].T.astype(BF16)
    qt = q_ref[...].astype(F32).T.astype(BF16)
    n_cols = GROUP * PAIR
    n_keys = WINDOW + PAIR
    row = lax.broadcasted_iota(jnp.int32, (n_keys, n_cols), 0)
    col = lax.broadcasted_iota(jnp.int32, (n_keys, n_cols), 1)
    first = jnp.where((col & (PAIR - 1)) >= CHUNK, CHUNK, 0)
    gcol = lax.shift_right_logical(lax.broadcasted_iota(jnp.int32, (1, n_cols), 1),
                                   PAIR.bit_length() - 1)
    biases = []
    for p in range(TQ_ATTN // PAIR):
        pos = row + (i * TQ_ATTN + p * PAIR - WINDOW)
        ok = (row >= first) & (row < first + WINDOW + CHUNK) & (pos >= 0)
        biases.append(jnp.where(ok, 0.0, NEG))

    def scores(p, h):
        w0 = p * PAIR
        kh = k[w0:w0 + n_keys, h * HEAD_DIM:(h + 1) * HEAD_DIM]
        rhs = jnp.concatenate(
            [qt[(h * GROUP + g) * HEAD_DIM:(h * GROUP + g + 1) * HEAD_DIM, w0:w0 + PAIR]
             for g in range(GROUP)], axis=1)
        return jnp.dot(kh, rhs, preferred_element_type=F32) + biases[p]

    def finish(p, h, st):
        w0 = p * PAIR
        sink = jnp.full((1, n_cols), sink_ref[h * GROUP], F32)
        for g in range(1, GROUP):
            sink = jnp.where(gcol == g, sink_ref[h * GROUP + g], sink)
        m = jnp.maximum(jnp.max(st, axis=0, keepdims=True), sink)
        pt = jnp.exp(st - m)
        denom = jnp.sum(pt, axis=0, keepdims=True) + jnp.exp(sink - m)
        ot = jnp.dot(vt[h * HEAD_DIM:(h + 1) * HEAD_DIM, w0:w0 + n_keys], pt.astype(BF16),
                     preferred_element_type=F32) / denom
        for g in range(GROUP):
            r0 = (h * GROUP + g) * HEAD_DIM
            at_sc[r0:r0 + HEAD_DIM, w0:w0 + PAIR] = ot[:, g * PAIR:(g + 1) * PAIR]

    items = [(p, h) for p in range(TQ_ATTN // PAIR) for h in range(N_KV_HEADS)]
    st_next = scores(*items[0])
    for n, (p, h) in enumerate(items):
        st = st_next
        if n + 1 < len(items):
            st_next = scores(*items[n + 1])
        finish(p, h, st)
    a_ref[...] = at_sc[...].T.astype(BF16)


def _attn_prompt(sinks, q, kv):
    r = TQ_ATTN // WINDOW
    return pl.pallas_call(
        _attn_prompt_body,
        name="attn_prompt",
        grid=(T_P // TQ_ATTN,),
        in_specs=[
            pl.BlockSpec(memory_space=pltpu.SMEM),
            pl.BlockSpec((TQ_ATTN, ATTN_WIDTH), lambda i: (i, 0)),
            pl.BlockSpec((WINDOW, 2 * KV_WIDTH), lambda i: (jnp.maximum(i * r - 1, 0), 0)),
            pl.BlockSpec((TQ_ATTN, 2 * KV_WIDTH), lambda i: (i, 0)),
        ],
        out_specs=pl.BlockSpec((TQ_ATTN, ATTN_WIDTH), lambda i: (i, 0)),
        out_shape=jax.ShapeDtypeStruct((T_P, ATTN_WIDTH), BF16),
        scratch_shapes=[pltpu.VMEM((ATTN_WIDTH, TQ_ATTN), F32)],
        compiler_params=_params(1),
    )(sinks, q, kv, kv)


def _attn_sample_body(sink_ref, q_ref, kvc_ref, ck_ref, cv_ref, a_ref):
    kvc = kvc_ref[...]
    kw = jnp.concatenate([ck_ref[0], kvc[:, :KV_WIDTH]], axis=0).astype(BF16)
    vw = jnp.concatenate([cv_ref[0], kvc[:, KV_WIDTH:]], axis=0).astype(BF16)
    qc = q_ref[...]
    for h in range(N_KV_HEADS):
        kh = kw[:, h * HEAD_DIM:(h + 1) * HEAD_DIM]
        vh = vw[:, h * HEAD_DIM:(h + 1) * HEAD_DIM]
        qg = jnp.concatenate(
            [qc[:, (h * GROUP + g) * HEAD_DIM:(h * GROUP + g + 1) * HEAD_DIM] for g in range(GROUP)],
            axis=0)
        s = lax.dot_general(qg, kh, (((1,), (1,)), ((), ())), preferred_element_type=F32)
        sink = jnp.concatenate(
            [jnp.full((L_S, 1), sink_ref[h * GROUP + g], F32) for g in range(GROUP)], axis=0)
        m = jnp.maximum(jnp.max(s, axis=1, keepdims=True), sink)
        p = jnp.exp(s - m)
        denom = jnp.sum(p, axis=1, keepdims=True) + jnp.exp(sink - m)
        o = jnp.dot(p.astype(BF16), vh, preferred_element_type=F32) / denom
        for g in range(GROUP):
            c0 = (h * GROUP + g) * HEAD_DIM
            a_ref[:, c0:c0 + HEAD_DIM] = o[g * L_S:(g + 1) * L_S].astype(BF16)


def _attn_sample(sinks, q, kv, cache_k, cache_v):
    return pl.pallas_call(
        _attn_sample_body,
        name="attn_sample",
        grid=(N_STREAMS,),
        in_specs=[
            pl.BlockSpec(memory_space=pltpu.SMEM),
            pl.BlockSpec((L_S, ATTN_WIDTH), lambda b: (b, 0)),
            pl.BlockSpec((L_S, 2 * KV_WIDTH), lambda b: (b, 0)),
            pl.BlockSpec((1, WINDOW, KV_WIDTH), lambda b: (b, 0, 0)),
            pl.BlockSpec((1, WINDOW, KV_WIDTH), lambda b: (b, 0, 0)),
        ],
        out_specs=pl.BlockSpec((L_S, ATTN_WIDTH), lambda b: (b, 0)),
        out_shape=jax.ShapeDtypeStruct((T_S, ATTN_WIDTH), BF16),
        compiler_params=_params(1),
    )(sinks, q, kv, cache_k, cache_v)


def _conv_shifts(prev, cur_ref, ext_sc, sh_sc, rows):
    ext_sc[0:HIST_PAD, :] = prev
    ext_sc[HIST_PAD:HIST_PAD + rows, :] = cur_ref[...]
    ext_sc[HIST_PAD + rows:, :] = jnp.zeros((SUBLANES, CONV_CH), F32)
    lead = HIST_PAD - HIST
    span = rows + HIST_PAD - SUBLANES
    for s in range(SUBLANES):
        sh_sc[s, 0:span, :] = ext_sc[lead + s:lead + s + span, :]


def _conv_rows(r0, w_ref, b_ref, lg_ref, lb_ref, o_ref, sh_sc):
    acc = jnp.zeros((SUB_CONV, CONV_CH), F32) + b_ref[...]
    for k in range(CONV_WIDTH):
        a, s = divmod(k, SUBLANES)
        r = r0 + a * SUBLANES
        acc = acc + w_ref[k:k + 1, :] * sh_sc[s, r:r + SUB_CONV, :]
    mu = jnp.mean(acc, axis=-1, keepdims=True)
    d = acc - mu
    var = jnp.mean(d * d, axis=-1, keepdims=True)
    yn = d * lax.rsqrt(var + EPS) * lg_ref[...] + lb_ref[...]
    out = yn * _sigmoid(yn)
    o_ref[r0:r0 + SUB_CONV, :] = out.astype(BF16)
    return out


def _gates_conv_body(xn_ref, w_ref, b_ref, prev_ref, cur_ref, cw_ref, cb_ref, lg_ref, lb_ref, zero_ref,
                     gate_ref, c_ref, ext_sc, sh_sc, *, rows, fresh):
    if fresh:
        step = pl.program_id(0) * N_GATE_TILES + pl.program_id(1)
        prev = jnp.where(step > 0, prev_ref[...], 0.0)
    else:
        prev = prev_ref[0]
    _conv_shifts(prev, cur_ref, ext_sc, sh_sc, rows)
    n_chunks = TN_GATES // MXU_N
    w = [w_ref[:, c * MXU_N:(c + 1) * MXU_N].astype(BF16) for c in range(n_chunks)]
    zero16 = None
    for i in range(rows // PIECE_CONV):
        r = slice(i * ROW_CHUNK, (i + 1) * ROW_CHUNK)
        xr = xn_ref[r, :]
        if zero16 is not None:
            top = jnp.concatenate([xr[0:BF16_ROWS, 0:LANES] + zero16, xr[0:BF16_ROWS, LANES:]], axis=1)
            xr = jnp.concatenate([top, xr[BF16_ROWS:, :]], axis=0)
        accs = [jnp.dot(xr, w[c], preferred_element_type=F32) for c in range(n_chunks)]
        outs = [_conv_rows(r0, cw_ref, cb_ref, lg_ref, lb_ref, c_ref, sh_sc)
                for r0 in range(i * PIECE_CONV, (i + 1) * PIECE_CONV, SUB_CONV)]
        bits = [lax.bitcast_convert_type(o[g * SUBLANES:(g + 1) * SUBLANES, 0:LANES], jnp.int32)
                for o in outs for g in range(SUB_CONV // SUBLANES)]
        zero8 = lax.bitcast_convert_type(functools.reduce(jnp.bitwise_or, bits) & zero_ref[...], F32)
        zero16 = jnp.concatenate([zero8, zero8], axis=0).astype(BF16)
        for c in range(n_chunks):
            cols = slice(c * MXU_N, (c + 1) * MXU_N)
            gate_ref[r, cols] = _sigmoid(accs[c] + b_ref[:, cols]).astype(BF16)


def _gates_conv(xn, w_in, b_in, hist, u, conv_consts, tm, rows, fresh):
    t = xn.shape[0]
    n0 = OFF_GATES // TN_GATES
    tile = lambda m, n: m * N_GATE_TILES + n
    if fresh:
        r = rows // HIST_PAD
        prev_spec = pl.BlockSpec((HIST_PAD, CONV_CH), lambda m, n: (jnp.maximum(tile(m, n) * r - 1, 0), 0))
    else:
        prev_spec = pl.BlockSpec((1, HIST_PAD, CONV_CH), lambda m, n: (tile(m, n), 0, 0))
    const_map = lambda m, n: (0, 0)
    return pl.pallas_call(
        functools.partial(_gates_conv_body, rows=rows, fresh=fresh),
        name="gates_conv",
        grid=(t // tm, N_GATE_TILES),
        in_specs=[
            pl.BlockSpec((tm, D_MODEL), lambda m, n: (m, 0)),
            pl.BlockSpec((D_MODEL, TN_GATES), lambda m, n: (0, n0 + n)),
            pl.BlockSpec((1, TN_GATES), lambda m, n: (0, n0 + n)),
            prev_spec,
            pl.BlockSpec((rows, CONV_CH), lambda m, n: (tile(m, n), 0)),
            pl.BlockSpec((CONV_WIDTH, CONV_CH), const_map),
            pl.BlockSpec((1, CONV_CH), const_map),
            pl.BlockSpec((1, CONV_CH), const_map),
            pl.BlockSpec((1, CONV_CH), const_map),
            pl.BlockSpec((SUBLANES, LANES), const_map),
        ],
        out_specs=[
            pl.BlockSpec((tm, TN_GATES), lambda m, n: (m, n)),
            pl.BlockSpec((rows, CONV_CH), lambda m, n: (tile(m, n), 0)),
        ],
        out_shape=[
            jax.ShapeDtypeStruct((t, 2 * D_MODEL), BF16),
            jax.ShapeDtypeStruct((t, CONV_CH), BF16),
        ],
        scratch_shapes=[
            pltpu.VMEM((HIST_PAD + rows + SUBLANES, CONV_CH), F32),
            pltpu.VMEM((SUBLANES, HIST_PAD + rows, CONV_CH), F32),
        ],
        compiler_params=_params(2),
    )(xn, w_in, b_in, hist, u, *conv_consts, jnp.zeros((SUBLANES, LANES), jnp.int32))


def _in_proj(x, hist, norm_g, w_in, b_in, conv_consts, tm, fresh):
    t = x.shape[0]
    xn, q, kv = _qkv(x, norm_g, w_in, b_in, tm)
    tw = min(TM_WIDE, t)
    u = _glu(xn, w_in, b_in, tw)
    gates, c = _gates_conv(xn, w_in, b_in, u if fresh else hist, u, conv_consts, tw,
                           tw // N_GATE_TILES, fresh)
    return q, kv, u, gates, c


def _mix_body(ap_ref, as_ref, cp_ref, cs_ref, gp_ref, gs_ref, xp_ref, xs_ref, wao_ref, wco_ref,
              wout_ref, g2_ref, wr_ref, br_ref, x1_ref, xn_ref, lg_ref):
    m = pl.program_id(0)
    is_sample = m >= T_P // TM_MIX
    a = jnp.where(is_sample, as_ref[...], ap_ref[...])
    c = jnp.where(is_sample, cs_ref[...], cp_ref[...])
    gate = jnp.where(is_sample, gs_ref[...], gp_ref[...])
    x = jnp.where(is_sample, xs_ref[...], xp_ref[...])
    pa = jnp.dot(a, wao_ref[...], preferred_element_type=F32)
    pc = jnp.dot(c, wco_ref[...], preferred_element_type=F32)
    y = (gate[:, :D_MODEL].astype(F32) * pa + gate[:, D_MODEL:].astype(F32) * pc).astype(BF16)
    x1 = x + jnp.dot(y, wout_ref[...], preferred_element_type=F32)
    x1_ref[...] = x1
    ms = jnp.mean(x1 * x1, axis=-1, keepdims=True)
    xn = x1 * lax.rsqrt(ms + EPS) * g2_ref[...]
    _store_packed_rows(xn_ref, xn)
    xn_hi = xn.astype(BF16)
    xn_lo = (xn - xn_hi.astype(F32)).astype(BF16)
    r_hi = jnp.dot(xn_hi, wr_ref[...], preferred_element_type=F32)
    r_lo = jnp.dot(xn_lo, wr_ref[...], preferred_element_type=F32)
    lg_ref[...] = r_hi + pltpu.roll(r_hi, ROUTER_PAD - N_ROUTER, axis=1) + r_lo + br_ref[...]


def _mix(a_p, a_s, c_p, c_s, g_p, g_s, x_p, x_s, wao, wco, wout, g2, wr, br):
    n_p = T_P // TM_MIX
    n_s = T_S // TM_MIX
    prompt_map = lambda m: (jnp.minimum(m, n_p - 1), 0)
    sample_map = lambda m: (jnp.clip(m - n_p, 0, n_s - 1), 0)
    row_map = lambda m: (m, 0)
    const_map = lambda m: (0, 0)
    once = pl.Buffered(1)

    def pair(width):
        return [pl.BlockSpec((TM_MIX, width), prompt_map), pl.BlockSpec((TM_MIX, width), sample_map)]

    return pl.pallas_call(
        _mix_body,
        name="mix",
        grid=(T // TM_MIX,),
        in_specs=pair(ATTN_WIDTH) + pair(CONV_CH) + pair(2 * D_MODEL) + pair(D_MODEL) + [
            pl.BlockSpec((ATTN_WIDTH, D_MODEL), const_map, pipeline_mode=once),
            pl.BlockSpec((CONV_CH, D_MODEL), const_map, pipeline_mode=once),
            pl.BlockSpec((D_MODEL, D_MODEL), const_map, pipeline_mode=once),
            pl.BlockSpec((1, D_MODEL), const_map),
            pl.BlockSpec((D_MODEL, ROUTER_PAD), const_map),
            pl.BlockSpec((1, ROUTER_PAD), const_map),
        ],
        out_specs=[
            pl.BlockSpec((TM_MIX, D_MODEL), row_map),
            pl.BlockSpec((TM_MIX * ROW_TILE, LANES), row_map),
            pl.BlockSpec((TM_MIX, ROUTER_PAD), row_map),
        ],
        out_shape=[
            jax.ShapeDtypeStruct((T, D_MODEL), F32),
            jax.ShapeDtypeStruct((T * ROW_TILE, LANES), U32),
            jax.ShapeDtypeStruct((T, ROUTER_PAD), F32),
        ],
        compiler_params=_params(1),
    )(a_p, a_s, c_p, c_s, g_p, g_s, x_p, x_s, wao, wco, wout, g2, wr, br)


REC_W, REC_E, REC_POS = 0, 2, 4


def _router_body(lg_ref, rec_ref, idx_ref, cnt_ref, carry_sc):
    lane = lax.broadcasted_iota(jnp.int32, (R_ROUTE, ROUTER_PAD), 1).astype(F32)
    ltri = (lax.broadcasted_iota(jnp.int32, (R_ROUTE, R_ROUTE), 1)
            < lax.broadcasted_iota(jnp.int32, (R_ROUTE, R_ROUTE), 0)).astype(BF16)

    def pick(vals, idx):
        return jnp.sum(jnp.where(lane == idx, vals, 0.0), axis=1, keepdims=True)

    def first_lane(mask):
        return jnp.min(jnp.where(mask, lane, float(ROUTER_PAD)), axis=1, keepdims=True)

    def masked_max(mask, v):
        return jnp.max(jnp.where(mask, v, NEG), axis=1, keepdims=True)

    carry_sc[...] = jnp.zeros_like(carry_sc)

    def count_pass(i, carry_unused):
        rows = pl.ds(pl.multiple_of(i * R_ROUTE, R_ROUTE), R_ROUTE)
        lg = lg_ref[rows, :]
        gmask = lane < N_GROUPS
        gmax = masked_max(gmask, lg)
        p_grp = 1.0 / jnp.sum(jnp.where(gmask, jnp.exp(lg - gmax), 0.0), axis=1, keepdims=True)
        grp = first_lane(gmask & (lg == gmax))
        lo = N_GROUPS + EXPERTS_PER_GROUP * grp
        emask = (lane >= lo) & (lane < lo + EXPERTS_PER_GROUP)
        e1 = masked_max(emask, lg)
        i1 = first_lane(emask & (lg == e1))
        emask2 = emask & (lane != i1)
        e2 = masked_max(emask2, lg)
        i2 = first_lane(emask2 & (lg == e2))
        t = jnp.exp(e2 - e1)
        w1 = p_grp / (1.0 + t)
        w2 = p_grp * t / (1.0 + t)
        x1 = i1 - N_GROUPS
        x2 = i2 - N_GROUPS
        onehot = ((lane == x1) | (lane == x2 + N_EXPERTS)).astype(BF16)
        carry = carry_sc[0:1, :]
        before = jnp.dot(ltri, onehot, preferred_element_type=F32) + carry
        rank1 = pick(before, x1)
        rank2 = pick(before, x2 + N_EXPERTS)
        carry_sc[0:1, :] = carry + jnp.sum(onehot.astype(F32), axis=0, keepdims=True)
        rec = jnp.where(lane == REC_W, w1, 0.0)
        for ln, v in ((REC_W + 1, w2), (REC_E, x1), (REC_E + 1, x2),
                      (REC_POS, rank1), (REC_POS + 1, rank2)):
            rec = jnp.where(lane == ln, v, rec)
        rec_ref[rows, :] = rec
        return carry_unused

    lax.fori_loop(0, T // R_ROUTE, count_pass, 0)

    tot = carry_sc[0:1, :]
    counts = tot + pltpu.roll(tot, ROUTER_PAD - N_EXPERTS, axis=1)
    lane1 = lax.broadcasted_iota(jnp.int32, (1, ROUTER_PAD), 1)
    counts = jnp.where(lane1 < N_EXPERTS, counts, 0.0)
    cnt_ref[...] = jnp.broadcast_to(counts, cnt_ref.shape).astype(jnp.int32)
    nblk = ((counts.astype(jnp.int32) + (BM_MOE - 1)) >> (BM_MOE.bit_length() - 1)).astype(F32)
    upper = (lax.broadcasted_iota(jnp.int32, (ROUTER_PAD, ROUTER_PAD), 0)
             < lax.broadcasted_iota(jnp.int32, (ROUTER_PAD, ROUTER_PAD), 1)).astype(BF16)
    blk_start = jnp.dot(jnp.broadcast_to(nblk, (SUBLANES, ROUTER_PAD)).astype(BF16), upper,
                        preferred_element_type=F32)[0:1, :]
    row0 = blk_start * BM_MOE

    def place_pass(i, carry_unused):
        rows = pl.ds(pl.multiple_of(i * R_ROUTE, R_ROUTE), R_ROUTE)
        rec = rec_ref[rows, :]
        x1 = pick(rec, REC_E)
        x2 = pick(rec, REC_E + 1)
        p1 = pick(row0, x1) + pick(rec, REC_POS)
        p2 = pick(row0, x2) + pick(tot, x2) + pick(rec, REC_POS + 1)
        rec = jnp.where(lane == REC_POS, p1 * ROW_TILE, rec)
        rec = jnp.where(lane == REC_POS + 1, p2 * ROW_TILE, rec)
        rec_ref[rows, :] = rec
        idx_ref[i] = rec.T[0:SUBLANES, :].astype(jnp.int32)
        return carry_unused

    lax.fori_loop(0, T // R_ROUTE, place_pass, 0)


def _router(logits):
    return pl.pallas_call(
        _router_body,
        name="router",
        grid=(1,),
        in_specs=[pl.BlockSpec((T, ROUTER_PAD), lambda i: (0, 0))],
        out_specs=[
            pl.BlockSpec((T, ROUTER_PAD), lambda i: (0, 0)),
            pl.BlockSpec((T // R_ROUTE, SUBLANES, R_ROUTE), lambda i: (0, 0, 0)),
            pl.BlockSpec((SUBLANES, ROUTER_PAD), lambda i: (0, 0)),
        ],
        out_shape=[
            jax.ShapeDtypeStruct((T, ROUTER_PAD), F32),
            jax.ShapeDtypeStruct((T // R_ROUTE, SUBLANES, R_ROUTE), jnp.int32),
            jax.ShapeDtypeStruct((SUBLANES, ROUTER_PAD), jnp.int32),
        ],
        scratch_shapes=[pltpu.VMEM((SUBLANES, ROUTER_PAD), F32)],
        compiler_params=_params(1),
    )(logits)


def _route(logits):
    rec, idx, cnt = _router(logits)
    idx = idx.transpose(1, 0, 2).reshape(SUBLANES, T)
    eid = idx[REC_E:REC_E + TOP_K].reshape(-1)
    pos = idx[REC_POS:REC_POS + TOP_K].reshape(-1)
    counts = cnt[0, :N_EXPERTS]
    nblk = (counts + BM_MOE - 1) // BM_MOE
    blk_end = jnp.cumsum(nblk)
    blk_start = blk_end - nblk
    row_start = jnp.cumsum(counts) - counts

    order = jnp.argsort(eid, stable=True)
    tok_sorted = jnp.concatenate([((order % T) * ROW_TILE).astype(jnp.int32),
                                  jnp.zeros((BM_MOE,), jnp.int32)])
    n_used = blk_end[-1]
    j = jnp.minimum(jnp.arange(N_BLK_MOE, dtype=jnp.int32), n_used - 1)
    blk_e = jnp.minimum(jnp.searchsorted(blk_end, j, side='right'), N_EXPERTS - 1).astype(jnp.int32)
    blk_src = (row_start[blk_e] + (j - blk_start[blk_e]) * BM_MOE).astype(jnp.int32)
    e_ids = jnp.arange(N_EXPERTS, dtype=jnp.int32)
    used = nblk > 0
    e_slot = ((jnp.cumsum(used.astype(jnp.int32)) - 1) % 2).astype(jnp.int32)
    nxt = lax.cummin(jnp.where(used, e_ids, N_EXPERTS), reverse=True)
    nxt = jnp.concatenate([nxt[1:], jnp.full((1,), N_EXPERTS, jnp.int32)])
    e_next = jnp.where(nxt < N_EXPERTS, nxt, e_ids).astype(jnp.int32)
    blk_first = (j == blk_start[blk_e]).astype(jnp.int32)
    plan = jnp.stack([blk_e, blk_src, blk_first, e_slot[blk_e], e_next[blk_e],
                      jnp.zeros_like(blk_e)])
    return rec, pos, tok_sorted, plan, n_used.astype(jnp.int32).reshape(1)


PLAN_E, PLAN_SRC, PLAN_FIRST, PLAN_WSLOT, PLAN_NEXT_E, PLAN_ZERO = range(6)
GATHER_AHEAD = 2
X_SLOTS = GATHER_AHEAD + 1
WEIGHT_DMA_PRIORITY = 1
GATHER_DMA_PRIORITY = 0


def _moe_body(plan, n_used, tok, x_hbm, wg_hbm, wu_hbm, wd_hbm, o_ref,
              xb, wg_st, wu_st, wd_st, wg_bf, wu_bf, wd_bf, sem_x, sem_w):
    j = pl.program_id(0)
    nu = n_used[0]

    def row_copy(src, slot, r):
        t8 = pl.multiple_of(tok[src + r], ROW_TILE)
        return pltpu.make_async_copy(x_hbm.at[pl.ds(t8, ROW_TILE)], xb.at[slot, _row_tile(r)],
                                     sem_x.at[slot])

    def block_wait(slot):
        pltpu.make_async_copy(x_hbm.at[pl.ds(0, BM_MOE * ROW_TILE)], xb.at[slot],
                              sem_x.at[slot]).wait()

    def weight_copies(e, ws):
        return (pltpu.make_async_copy(wg_hbm.at[e], wg_st.at[ws], sem_w.at[ws, 0]),
                pltpu.make_async_copy(wu_hbm.at[e], wu_st.at[ws], sem_w.at[ws, 1]),
                pltpu.make_async_copy(wd_hbm.at[e], wd_st.at[ws], sem_w.at[ws, 2]))

    @pl.when(j == 0)
    def _():
        for cp in weight_copies(plan[PLAN_E, 0], 0):
            cp.start(priority=WEIGHT_DMA_PRIORITY)
        for b in range(GATHER_AHEAD):
            src = plan[PLAN_SRC, b]

            def body(r, carry, src=src, b=b):
                row_copy(src, b, r).start(priority=GATHER_DMA_PRIORITY)
                return carry
            lax.fori_loop(0, BM_MOE, body, 0, unroll=8)

    @pl.when((j < nu) & (plan[PLAN_FIRST, jnp.minimum(j, N_BLK_MOE - 1)] == 1))
    def _():
        e = plan[PLAN_E, j]
        ws = plan[PLAN_WSLOT, j]
        e_next = plan[PLAN_NEXT_E, j]

        @pl.when(e_next != e)
        def _():
            for cp in weight_copies(e_next, 1 - ws):
                cp.start(priority=WEIGHT_DMA_PRIORITY)

        for cp in weight_copies(e, ws):
            cp.wait()
        wg_bf[...] = wg_st[ws].astype(BF16)
        wu_bf[...] = wu_st[ws].astype(BF16)
        wd_bf[...] = wd_st[ws].astype(BF16)

    @pl.when(j < nu)
    def _():
        slot = j % X_SLOTS
        next_slot = (j + GATHER_AHEAD) % X_SLOTS
        block_wait(slot)
        nsrc = plan[PLAN_SRC, jnp.minimum(j + GATHER_AHEAD, N_BLK_MOE - 1)]
        halves = [_load_packed_rows(xb.at[slot], BM_MOE, s) for s in range(ROW_TILE)]
        x = jnp.concatenate([lo for lo, _ in halves] + [hi for _, hi in halves], axis=1).astype(BF16)
        zero = plan[PLAN_ZERO, 0]
        group = BM_MOE // 8
        issued = [0]

        def gather_after(v):
            base = nsrc
            if v is not None:
                base = base + (lax.bitcast_convert_type(v[0:1, 0:1], jnp.int32)[0, 0] & zero)
            for r in range(issued[0], issued[0] + group):
                row_copy(base, next_slot, r).start(priority=GATHER_DMA_PRIORITY)
            issued[0] += group

        gather_after(None)
        gather_after(halves[ROW_TILE - 1][1])
        gu = []
        for c in range(D_EXPERT // MXU_N):
            cols = slice(c * MXU_N, (c + 1) * MXU_N)
            g = jnp.dot(x, wg_bf[:, cols], preferred_element_type=F32)
            gather_after(g)
            u = jnp.dot(x, wu_bf[:, cols], preferred_element_type=F32)
            gather_after(u)
            gu.append((g, u))
        acc = None
        for c, (g, u) in enumerate(gu):
            cols = slice(c * MXU_N, (c + 1) * MXU_N)
            h = (g * _sigmoid(g) * u).astype(BF16)
            part = jnp.dot(h, wd_bf[cols, :], preferred_element_type=F32)
            if c == 0:
                gather_after(part)
                gather_after(part[:, D_MODEL - LANES:])
            acc = part if acc is None else acc + part
        assert issued[0] == BM_MOE
        _store_packed_rows(o_ref, acc)

    @pl.when(j == nu)
    def _():
        for b in range(GATHER_AHEAD):
            block_wait((j + b) % X_SLOTS)

    @pl.when(j >= nu)
    def _():
        o_ref[...] = jnp.zeros_like(o_ref)


def _moe(plan, n_used, tok_sorted, xn_packed, w_g, w_u, w_d):
    grid_spec = pltpu.PrefetchScalarGridSpec(
        num_scalar_prefetch=3,
        grid=(N_BLK_MOE + 1,),
        in_specs=[pl.BlockSpec(memory_space=pl.ANY)] * 4,
        out_specs=pl.BlockSpec((BM_MOE * ROW_TILE, LANES), lambda j, pn, nu, tk: (j, 0)),
        scratch_shapes=[
            pltpu.VMEM((X_SLOTS, BM_MOE * ROW_TILE, LANES), U32),
            pltpu.VMEM((2, D_MODEL, D_EXPERT), F32),
            pltpu.VMEM((2, D_MODEL, D_EXPERT), F32),
            pltpu.VMEM((2, D_EXPERT, D_MODEL), F32),
            pltpu.VMEM((D_MODEL, D_EXPERT), BF16),
            pltpu.VMEM((D_MODEL, D_EXPERT), BF16),
            pltpu.VMEM((D_EXPERT, D_MODEL), BF16),
            pltpu.SemaphoreType.DMA((X_SLOTS,)),
            pltpu.SemaphoreType.DMA((2, 3)),
        ],
    )
    return pl.pallas_call(
        _moe_body,
        name="experts",
        grid_spec=grid_spec,
        out_shape=jax.ShapeDtypeStruct(((N_BLK_MOE + 1) * BM_MOE * ROW_TILE, LANES), U32),
        compiler_params=_params(1),
    )(plan, n_used, tok_sorted, xn_packed, w_g, w_u, w_d)


def _combine_body(pos, x1_ref, w_ref, gf_ref, o_hbm, y_ref, rb, sem, *, first_tile):
    m = pl.program_id(0)
    last = pl.num_programs(0) - 1

    def row_copy(base, slot, r, k):
        p8 = pl.multiple_of(pos[k * T + base + r], ROW_TILE)
        return pltpu.make_async_copy(o_hbm.at[pl.ds(p8, ROW_TILE)], rb.at[slot, k, _row_tile(r)],
                                     sem.at[slot])

    def tile_wait(slot):
        for k in range(TOP_K):
            pltpu.make_async_copy(o_hbm.at[pl.ds(0, TM_OUT * ROW_TILE)], rb.at[slot, k],
                                  sem.at[slot]).wait()

    def tile_base(mm):
        return (first_tile + mm) * TM_OUT

    @pl.when(m == 0)
    def _():
        for b in range(GATHER_AHEAD):
            base = tile_base(jnp.minimum(b, last))

            def body(r, carry, base=base, b=b):
                for k in range(TOP_K):
                    row_copy(base, b, r, k).start(priority=k)
                return carry
            lax.fori_loop(0, TM_OUT, body, 0, unroll=4)

    slot = m % X_SLOTS
    next_slot = (m + GATHER_AHEAD) % X_SLOTS
    tile_wait(slot)
    nbase = tile_base(jnp.minimum(m + GATHER_AHEAD, last))
    rows = TM_OUT // ROW_TILE
    w0 = w_ref[:, 0:1]
    w1 = w_ref[:, 1:2]
    ss = jnp.zeros((TM_OUT, 1), F32)
    for s in range(ROW_TILE):
        for r in range(s * rows, (s + 1) * rows):
            for k in range(TOP_K):
                row_copy(nbase, next_slot, r, k).start(priority=k)
        lo0, hi0 = _load_packed_rows(rb.at[slot, 0], TM_OUT, s)
        lo1, hi1 = _load_packed_rows(rb.at[slot, 1], TM_OUT, s)
        for off, r0, r1 in ((s * LANES, lo0, lo1), (HALF_D + s * LANES, hi0, hi1)):
            y = x1_ref[:, off:off + LANES] + w0 * r0 + w1 * r1
            ss = ss + jnp.sum(y * y, axis=-1, keepdims=True)
            y_ref[:, off:off + LANES] = y
    scale = lax.rsqrt(ss * (1.0 / D_MODEL) + EPS)
    y_ref[...] = y_ref[...] * scale * gf_ref[...]

    @pl.when(m == last)
    def _():
        for b in range(1, GATHER_AHEAD + 1):
            tile_wait((m + b) % X_SLOTS)


def _combine(pos, x1, wts, gf, out_sorted, first_tile, n_tiles):
    grid_spec = pltpu.PrefetchScalarGridSpec(
        num_scalar_prefetch=1,
        grid=(n_tiles,),
        in_specs=[
            pl.BlockSpec((TM_OUT, D_MODEL), lambda m, p: (first_tile + m, 0)),
            pl.BlockSpec((TM_OUT, ROUTER_PAD), lambda m, p: (first_tile + m, 0)),
            pl.BlockSpec((1, D_MODEL), lambda m, p: (0, 0)),
            pl.BlockSpec(memory_space=pl.ANY),
        ],
        out_specs=pl.BlockSpec((TM_OUT, D_MODEL), lambda m, p: (m, 0)),
        scratch_shapes=[
            pltpu.VMEM((X_SLOTS, TOP_K, TM_OUT * ROW_TILE, LANES), U32),
            pltpu.SemaphoreType.DMA((X_SLOTS,)),
        ],
    )
    return pl.pallas_call(
        functools.partial(_combine_body, first_tile=first_tile),
        name="combine",
        grid_spec=grid_spec,
        out_shape=jax.ShapeDtypeStruct((n_tiles * TM_OUT, D_MODEL), F32),
        compiler_params=_params(1),
    )(pos, x1, wts, gf, out_sorted)


def kernel(x_prompt, x_sample, cache_k, cache_v, state_conv, norm1_g, w_in, b_in, attn_sinks,
           w_attn_o, conv_dw, conv_dw_b, conv_ln_g, conv_ln_b, w_conv_o, w_out, norm2_g,
           w_router_group, b_router_group, w_router_expert, b_router_expert, w_e_gate, w_e_up,
           w_e_down, final_norm_g):
    x_p = x_prompt.reshape(T_P, D_MODEL)
    x_s = x_sample.reshape(T_S, D_MODEL)
    g1 = norm1_g[0][None, :]
    b1 = b_in[0][None, :]
    conv_consts = (conv_dw[0], conv_dw_b[0][None, :], conv_ln_g[0][None, :], conv_ln_b[0][None, :])
    hist_pad = jnp.pad(state_conv[0], ((0, 0), (HIST_PAD - HIST, 0), (0, 0)))
    q_p, kv_p, u_p, gates_p, c_p = _in_proj(x_p, None, g1, w_in[0], b1, conv_consts, TM_IN, True)
    q_s, kv_s, u_s, gates_s, c_s = _in_proj(x_s, hist_pad, g1, w_in[0], b1, conv_consts, T_S, False)

    sinks = attn_sinks[0]
    a_p = _attn_prompt(sinks, q_p, kv_p)
    ck = cache_k[0].reshape(N_STREAMS, WINDOW, KV_WIDTH)
    cv = cache_v[0].reshape(N_STREAMS, WINDOW, KV_WIDTH)
    a_s = _attn_sample(sinks, q_s, kv_s, ck, cv)

    w_r = jnp.concatenate([w_router_group[0], w_router_expert[0]], axis=1)
    w_r_hi = w_r.astype(BF16)
    w_r_lo = (w_r - w_r_hi.astype(F32)).astype(BF16)
    w_r_cat = jnp.concatenate(
        [w_r_hi, w_r_lo, jnp.zeros((D_MODEL, ROUTER_PAD - 2 * N_ROUTER), BF16)], axis=1)
    b_r = jnp.concatenate([b_router_group[0], b_router_expert[0],
                           jnp.zeros((ROUTER_PAD - N_ROUTER,), F32)])[None, :]

    x1, xn2, logits = _mix(a_p, a_s, c_p, c_s, gates_p, gates_s, x_p, x_s,
                           w_attn_o[0].astype(BF16), w_conv_o[0].astype(BF16), w_out[0].astype(BF16),
                           norm2_g[0][None, :], w_r_cat, b_r)

    wts, pos, tok_sorted, plan, n_used = _route(logits)
    out_sorted = _moe(plan, n_used, tok_sorted, xn2, w_e_gate[0], w_e_up[0], w_e_down[0])
    gf = final_norm_g[None, :]
    y_p = _combine(pos, x1, wts, gf, out_sorted, 0, T_P // TM_OUT)
    y_s = _combine(pos, x1, wts, gf, out_sorted, T_P // TM_OUT, T_S // TM_OUT)

    kv_shape = (1, -1, WINDOW, N_KV_HEADS, HEAD_DIM)
    new_k_prompt = kv_p[T_P - WINDOW:, :KV_WIDTH].reshape(kv_shape)
    new_v_prompt = kv_p[T_P - WINDOW:, KV_WIDTH:].reshape(kv_shape)
    new_conv_prompt = u_p[T_P - HIST:].reshape(1, 1, HIST, CONV_CH)
    k_s = kv_s[:, :KV_WIDTH].reshape(N_STREAMS, L_S, KV_WIDTH)
    v_s = kv_s[:, KV_WIDTH:].reshape(N_STREAMS, L_S, KV_WIDTH)
    new_k_sample = jnp.concatenate([ck[:, L_S:], k_s], axis=1).reshape(kv_shape)
    new_v_sample = jnp.concatenate([cv[:, L_S:], v_s], axis=1).reshape(kv_shape)
    new_conv_sample = u_s.reshape(N_STREAMS, L_S, CONV_CH)[:, L_S - HIST:].reshape(
        1, N_STREAMS, HIST, CONV_CH)

    return (y_p.reshape(1, T_P, D_MODEL), y_s.reshape(N_STREAMS, L_S, D_MODEL),
            new_k_prompt, new_v_prompt, new_conv_prompt,
            new_k_sample, new_v_sample, new_conv_sample)
```

```python
import functools

import numpy as np
import jax
import jax.numpy as jnp
from jax import lax
from jax.experimental import pallas as pl
from jax.experimental.pallas import tpu as pltpu

F32 = jnp.float32
BF16 = jnp.bfloat16
U32 = jnp.uint32

D_MODEL = 2048
T_P = 8192
N_STREAMS = 8
L_S = 64
T_S = N_STREAMS * L_S
T = T_P + T_S
CHUNK = 64
WINDOW = 128
HEAD_DIM = 64
N_Q_HEADS = 16
N_KV_HEADS = 4
GROUP = N_Q_HEADS // N_KV_HEADS
ATTN_WIDTH = N_Q_HEADS * HEAD_DIM
KV_WIDTH = N_KV_HEADS * HEAD_DIM
CONV_CH = 1024
CONV_WIDTH = 31
HIST = CONV_WIDTH - 1
HIST_PAD = 32
SUBLANES = 8
LANES = 128
ROW_TILE = SUBLANES
HALF_D = D_MODEL // 2
HI_MASK = np.uint32(0xFFFF0000)
OFF_KV = ATTN_WIDTH
OFF_GLU = OFF_KV + 2 * KV_WIDTH
OFF_GATES = OFF_GLU + 2 * CONV_CH
IN_WIDTH = OFF_GATES + 2 * D_MODEL
N_GROUPS = 8
EXPERTS_PER_GROUP = 4
N_EXPERTS = N_GROUPS * EXPERTS_PER_GROUP
TOP_K = 2
D_EXPERT = 512
N_ROUTER = N_GROUPS + N_EXPERTS
ROUTER_PAD = 128
EPS = 1e-6
NEG = -1e30

VMEM_LIMIT = 56 * 1024 * 1024
MXU_N = 256

TM_IN = 1024
TM_WIDE = 2048
TN_IN = 512
TN_GATES = TN_IN
N_GATE_TILES = 2 * D_MODEL // TN_GATES
ROW_CHUNK = 256
W_STAGE_ROWS = 256
TQ_ATTN = 256
PAIR = 2 * CHUNK
BF16_ROWS = 16
SUB_CONV = 32
PIECE_CONV = 32
CONV_LAG = 1
TM_MIX = 256
BM_MOE = 256
R_ROUTE = 256
TM_OUT = 256
N_SLOTS = T * TOP_K
N_BLK_MOE = -(-(N_SLOTS + N_EXPERTS * (BM_MOE - 1)) // BM_MOE)


def _sigmoid(x):
    return 1.0 / (1.0 + jnp.exp(-x))


def _params(n_axes):
    return pltpu.CompilerParams(dimension_semantics=("arbitrary",) * n_axes,
                                vmem_limit_bytes=VMEM_LIMIT)


def _store_packed_rows(ref, y):
    rows = y.shape[0]
    for s in range(ROW_TILE):
        lo = y[:, s * LANES:(s + 1) * LANES].astype(BF16).astype(F32)
        hi = y[:, HALF_D + s * LANES:HALF_D + (s + 1) * LANES].astype(BF16).astype(F32)
        word = (lax.bitcast_convert_type(hi, U32) & HI_MASK) | (lax.bitcast_convert_type(lo, U32) >> 16)
        ref[pl.ds(s, rows, stride=ROW_TILE), :] = word


def _row_tile(r):
    start = r * ROW_TILE
    return pl.ds(start if isinstance(r, int) else pl.multiple_of(start, ROW_TILE), ROW_TILE)


def _load_packed_rows(ref, rows, s):
    word = ref[pl.ds(s, rows, stride=ROW_TILE), :]
    lo = lax.bitcast_convert_type(word << 16, F32)
    hi = lax.bitcast_convert_type(word & HI_MASK, F32)
    return lo, hi


def _chunk_dot(xn, w_ref, b_ref, c):
    w = w_ref[:, c * MXU_N:(c + 1) * MXU_N].astype(BF16)
    return jnp.dot(xn, w, preferred_element_type=F32) + b_ref[:, c * MXU_N:(c + 1) * MXU_N]


def _stage_weights(jobs, stage, sem):
    width = stage.shape[-1]
    chunks = [(w_hbm, w_bf, r0) for w_hbm, w_bf in jobs
              for r0 in range(0, w_bf.shape[0], W_STAGE_ROWS)]

    def chunk_copy(i):
        w_hbm, _, r0 = chunks[i]
        return pltpu.make_async_copy(w_hbm.at[pl.ds(r0, W_STAGE_ROWS), pl.ds(0, width)],
                                     stage.at[i % 2], sem.at[i % 2])
    chunk_copy(0).start()
    for i, (_, w_bf, r0) in enumerate(chunks):
        if i + 1 < len(chunks):
            chunk_copy(i + 1).start()
        chunk_copy(i).wait()
        w_bf[r0:r0 + W_STAGE_ROWS, :] = stage[i % 2].astype(BF16)


def _qkv_body(x_ref, g_ref, w_hbm, b_ref, xn_ref, q_ref, kv_ref, w_bf, stage, sem, *, tm):
    @pl.when(pl.program_id(0) == 0)
    def _():
        _stage_weights([(w_hbm, w_bf)], stage, sem)

    for r0 in range(0, tm, ROW_CHUNK):
        rows = slice(r0, r0 + ROW_CHUNK)
        x = x_ref[rows, :]
        ms = jnp.mean(x * x, axis=-1, keepdims=True)
        xn = (x * lax.rsqrt(ms + EPS) * g_ref[...]).astype(BF16)
        xn_ref[rows, :] = xn
        for c in range(OFF_GLU // MXU_N):
            cols = slice(c * MXU_N, (c + 1) * MXU_N)
            acc = jnp.dot(xn, w_bf[:, cols], preferred_element_type=F32) + b_ref[:, cols]
            if c < ATTN_WIDTH // MXU_N:
                q_ref[rows, cols] = (acc * (HEAD_DIM ** -0.5)).astype(BF16)
            else:
                kv_ref[rows, c * MXU_N - ATTN_WIDTH:(c + 1) * MXU_N - ATTN_WIDTH] = acc


def _qkv(x, norm_g, w_in, b_in, tm):
    t = x.shape[0]
    return pl.pallas_call(
        functools.partial(_qkv_body, tm=tm),
        name="qkv",
        grid=(t // tm,),
        in_specs=[
            pl.BlockSpec((tm, D_MODEL), lambda m: (m, 0)),
            pl.BlockSpec((1, D_MODEL), lambda m: (0, 0)),
            pl.BlockSpec(memory_space=pl.ANY),
            pl.BlockSpec((1, OFF_GLU), lambda m: (0, 0)),
        ],
        out_specs=[
            pl.BlockSpec((tm, D_MODEL), lambda m: (m, 0)),
            pl.BlockSpec((tm, ATTN_WIDTH), lambda m: (m, 0)),
            pl.BlockSpec((tm, 2 * KV_WIDTH), lambda m: (m, 0)),
        ],
        out_shape=[
            jax.ShapeDtypeStruct((t, D_MODEL), BF16),
            jax.ShapeDtypeStruct((t, ATTN_WIDTH), BF16),
            jax.ShapeDtypeStruct((t, 2 * KV_WIDTH), F32),
        ],
        scratch_shapes=[
            pltpu.VMEM((D_MODEL, OFF_GLU), BF16),
            pltpu.VMEM((2, W_STAGE_ROWS, OFF_GLU), F32),
            pltpu.SemaphoreType.DMA((2,)),
        ],
        compiler_params=_params(1),
    )(x, norm_g, w_in, b_in)


def _glu_body(xn_ref, wa_ref, ba_ref, wb_ref, bb_ref, u_ref):
    xn = xn_ref[...]
    for c in range(TN_IN // MXU_N):
        a = _chunk_dot(xn, wa_ref, ba_ref, c)
        b = _chunk_dot(xn, wb_ref, bb_ref, c)
        u_ref[:, c * MXU_N:(c + 1) * MXU_N] = a * _sigmoid(b)


def _glu(xn, w_in, b_in, tm):
    t = xn.shape[0]
    a0 = OFF_GLU // TN_IN
    b0 = (OFF_GLU + CONV_CH) // TN_IN
    return pl.pallas_call(
        _glu_body,
        name="glu",
        grid=(t // tm, CONV_CH // TN_IN),
        in_specs=[
            pl.BlockSpec((tm, D_MODEL), lambda m, n: (m, 0)),
            pl.BlockSpec((D_MODEL, TN_IN), lambda m, n: (0, a0 + n)),
            pl.BlockSpec((1, TN_IN), lambda m, n: (0, a0 + n)),
            pl.BlockSpec((D_MODEL, TN_IN), lambda m, n: (0, b0 + n)),
            pl.BlockSpec((1, TN_IN), lambda m, n: (0, b0 + n)),
        ],
        out_specs=pl.BlockSpec((tm, TN_IN), lambda m, n: (m, n)),
        out_shape=jax.ShapeDtypeStruct((t, CONV_CH), F32),
        compiler_params=_params(2),
    )(xn, w_in, b_in, w_in, b_in)


def _attn_prompt_body(sink_ref, q_ref, kvp_ref, kvc_ref, a_ref, at_sc):
    i = pl.program_id(0)
    kv = jnp.concatenate([kvp_ref[...], kvc_ref[...]], axis=0)
    k = kv[:, :KV_WIDTH].astype(BF16)
    vt = kv[:, KV_WIDTH:].T.astype(BF16)
    qt = q_ref[...].astype(F32).T.astype(BF16)
    n_cols = GROUP * PAIR
    n_keys = WINDOW + PAIR
    row = lax.broadcasted_iota(jnp.int32, (n_keys, n_cols), 0)
    col = lax.broadcasted_iota(jnp.int32, (n_keys, n_cols), 1)
    first = jnp.where((col & (PAIR - 1)) >= CHUNK, CHUNK, 0)
    gcol = lax.shift_right_logical(lax.broadcasted_iota(jnp.int32, (1, n_cols), 1),
                                   PAIR.bit_length() - 1)
    biases = []
    for p in range(TQ_ATTN // PAIR):
        pos = row + (i * TQ_ATTN + p * PAIR - WINDOW)
        ok = (row >= first) & (row < first + WINDOW + CHUNK) & (pos >= 0)
        biases.append(jnp.where(ok, 0.0, NEG))

    def scores(p, h):
        w0 = p * PAIR
        kh = k[w0:w0 + n_keys, h * HEAD_DIM:(h + 1) * HEAD_DIM]
        rhs = jnp.concatenate(
            [qt[(h * GROUP + g) * HEAD_DIM:(h * GROUP + g + 1) * HEAD_DIM, w0:w0 + PAIR]
             for g in range(GROUP)], axis=1)
        return jnp.dot(kh, rhs, preferred_element_type=F32) + biases[p]

    def finish(p, h, st):
        w0 = p * PAIR
        sink = jnp.full((1, n_cols), sink_ref[h * GROUP], F32)
        for g in range(1, GROUP):
            sink = jnp.where(gcol == g, sink_ref[h * GROUP + g], sink)
        m = jnp.maximum(jnp.max(st, axis=0, keepdims=True), sink)
        pt = jnp.exp(st - m)
        denom = jnp.sum(pt, axis=0, keepdims=True) + jnp.exp(sink - m)
        ot = jnp.dot(vt[h * HEAD_DIM:(h + 1) * HEAD_DIM, w0:w0 + n_keys], pt.astype(BF16),
                     preferred_element_type=F32) / denom
        for g in range(GROUP):
            r0 = (h * GROUP + g) * HEAD_DIM
            at_sc[r0:r0 + HEAD_DIM, w0:w0 + PAIR] = ot[:, g * PAIR:(g + 1) * PAIR]

    items = [(p, h) for p in range(TQ_ATTN // PAIR) for h in range(N_KV_HEADS)]
    st_next = scores(*items[0])
    for n, (p, h) in enumerate(items):
        st = st_next
        if n + 1 < len(items):
            st_next = scores(*items[n + 1])
        finish(p, h, st)
    a_ref[...] = at_sc[...].T.astype(BF16)


def _attn_prompt(sinks, q, kv):
    r = TQ_ATTN // WINDOW
    return pl.pallas_call(
        _attn_prompt_body,
        name="attn_prompt",
        grid=(T_P // TQ_ATTN,),
        in_specs=[
            pl.BlockSpec(memory_space=pltpu.SMEM),
            pl.BlockSpec((TQ_ATTN, ATTN_WIDTH), lambda i: (i, 0)),
            pl.BlockSpec((WINDOW, 2 * KV_WIDTH), lambda i: (jnp.maximum(i * r - 1, 0), 0)),
            pl.BlockSpec((TQ_ATTN, 2 * KV_WIDTH), lambda i: (i, 0)),
        ],
        out_specs=pl.BlockSpec((TQ_ATTN, ATTN_WIDTH), lambda i: (i, 0)),
        out_shape=jax.ShapeDtypeStruct((T_P, ATTN_WIDTH), BF16),
        scratch_shapes=[pltpu.VMEM((ATTN_WIDTH, TQ_ATTN), F32)],
        compiler_params=_params(1),
    )(sinks, q, kv, kv)


def _attn_sample_body(sink_ref, q_ref, kvc_ref, ck_ref, cv_ref, a_ref):
    kvc = kvc_ref[...]
    kw = jnp.concatenate([ck_ref[0], kvc[:, :KV_WIDTH]], axis=0).astype(BF16)
    vw = jnp.concatenate([cv_ref[0], kvc[:, KV_WIDTH:]], axis=0).astype(BF16)
    qc = q_ref[...]
    for h in range(N_KV_HEADS):
        kh = kw[:, h * HEAD_DIM:(h + 1) * HEAD_DIM]
        vh = vw[:, h * HEAD_DIM:(h + 1) * HEAD_DIM]
        qg = jnp.concatenate(
            [qc[:, (h * GROUP + g) * HEAD_DIM:(h * GROUP + g + 1) * HEAD_DIM] for g in range(GROUP)],
            axis=0)
        s = lax.dot_general(qg, kh, (((1,), (1,)), ((), ())), preferred_element_type=F32)
        sink = jnp.concatenate(
            [jnp.full((L_S, 1), sink_ref[h * GROUP + g], F32) for g in range(GROUP)], axis=0)
        m = jnp.maximum(jnp.max(s, axis=1, keepdims=True), sink)
        p = jnp.exp(s - m)
        denom = jnp.sum(p, axis=1, keepdims=True) + jnp.exp(sink - m)
        o = jnp.dot(p.astype(BF16), vh, preferred_element_type=F32) / denom
        for g in range(GROUP):
            c0 = (h * GROUP + g) * HEAD_DIM
            a_ref[:, c0:c0 + HEAD_DIM] = o[g * L_S:(g + 1) * L_S].astype(BF16)


def _attn_sample(sinks, q, kv, cache_k, cache_v):
    return pl.pallas_call(
        _attn_sample_body,
        name="attn_sample",
        grid=(N_STREAMS,),
        in_specs=[
            pl.BlockSpec(memory_space=pltpu.SMEM),
            pl.BlockSpec((L_S, ATTN_WIDTH), lambda b: (b, 0)),
            pl.BlockSpec((L_S, 2 * KV_WIDTH), lambda b: (b, 0)),
            pl.BlockSpec((1, WINDOW, KV_WIDTH), lambda b: (b, 0, 0)),
            pl.BlockSpec((1, WINDOW, KV_WIDTH), lambda b: (b, 0, 0)),
        ],
        out_specs=pl.BlockSpec((L_S, ATTN_WIDTH), lambda b: (b, 0)),
        out_shape=jax.ShapeDtypeStruct((T_S, ATTN_WIDTH), BF16),
        compiler_params=_params(1),
    )(sinks, q, kv, cache_k, cache_v)


def _conv_shifts(prev, cur_ref, ext_sc, sh_sc, rows):
    ext_sc[0:HIST_PAD, :] = prev
    ext_sc[HIST_PAD:HIST_PAD + rows, :] = cur_ref[...]
    ext_sc[HIST_PAD + rows:, :] = jnp.zeros((SUBLANES, CONV_CH), F32)
    lead = HIST_PAD - HIST
    span = rows + HIST_PAD - SUBLANES
    for s in range(SUBLANES):
        sh_sc[s, 0:span, :] = ext_sc[lead + s:lead + s + span, :]


def _conv_rows(r0, w_ref, b_ref, lg_ref, lb_ref, o_ref, sh_sc):
    acc = jnp.zeros((SUB_CONV, CONV_CH), F32) + b_ref[...]
    for k in range(CONV_WIDTH):
        a, s = divmod(k, SUBLANES)
        r = r0 + a * SUBLANES
        acc = acc + w_ref[k:k + 1, :] * sh_sc[s, r:r + SUB_CONV, :]
    mu = jnp.mean(acc, axis=-1, keepdims=True)
    d = acc - mu
    var = jnp.mean(d * d, axis=-1, keepdims=True)
    yn = d * lax.rsqrt(var + EPS) * lg_ref[...] + lb_ref[...]
    out = yn * _sigmoid(yn)
    o_ref[r0:r0 + SUB_CONV, :] = out.astype(BF16)
    return out


def _gates_conv_body(xn_ref, w_ref, b_ref, prev_ref, cur_ref, cw_ref, cb_ref, lg_ref, lb_ref, zero_ref,
                     gate_ref, c_ref, ext_sc, sh_sc, *, rows, fresh):
    if fresh:
        step = pl.program_id(0) * N_GATE_TILES + pl.program_id(1)
        prev = jnp.where(step > 0, prev_ref[...], 0.0)
    else:
        prev = prev_ref[0]
    _conv_shifts(prev, cur_ref, ext_sc, sh_sc, rows)
    n_chunks = TN_GATES // MXU_N
    w = [w_ref[:, c * MXU_N:(c + 1) * MXU_N].astype(BF16) for c in range(n_chunks)]
    zeros = []
    for i in range(rows // PIECE_CONV):
        r = slice(i * ROW_CHUNK, (i + 1) * ROW_CHUNK)
        xr = xn_ref[r, :]
        if i >= CONV_LAG:
            zero16 = zeros[i - CONV_LAG]
            top = jnp.concatenate([xr[0:BF16_ROWS, 0:LANES] + zero16, xr[0:BF16_ROWS, LANES:]], axis=1)
            xr = jnp.concatenate([top, xr[BF16_ROWS:, :]], axis=0)
        accs = [jnp.dot(xr, w[c], preferred_element_type=F32) for c in range(n_chunks)]
        outs = [_conv_rows(r0, cw_ref, cb_ref, lg_ref, lb_ref, c_ref, sh_sc)
                for r0 in range(i * PIECE_CONV, (i + 1) * PIECE_CONV, SUB_CONV)]
        bits = [lax.bitcast_convert_type(o[g * SUBLANES:(g + 1) * SUBLANES, 0:LANES], jnp.int32)
                for o in outs for g in range(SUB_CONV // SUBLANES)]
        zero8 = lax.bitcast_convert_type(functools.reduce(jnp.bitwise_or, bits) & zero_ref[...], F32)
        zeros.append(jnp.concatenate([zero8, zero8], axis=0).astype(BF16))
        for c in range(n_chunks):
            cols = slice(c * MXU_N, (c + 1) * MXU_N)
            gate_ref[r, cols] = _sigmoid(accs[c] + b_ref[:, cols]).astype(BF16)


def _gates_conv(xn, w_in, b_in, hist, u, conv_consts, tm, rows, fresh):
    t = xn.shape[0]
    n0 = OFF_GATES // TN_GATES
    tile = lambda m, n: m * N_GATE_TILES + n
    if fresh:
        r = rows // HIST_PAD
        prev_spec = pl.BlockSpec((HIST_PAD, CONV_CH), lambda m, n: (jnp.maximum(tile(m, n) * r - 1, 0), 0))
    else:
        prev_spec = pl.BlockSpec((1, HIST_PAD, CONV_CH), lambda m, n: (tile(m, n), 0, 0))
    const_map = lambda m, n: (0, 0)
    return pl.pallas_call(
        functools.partial(_gates_conv_body, rows=rows, fresh=fresh),
        name="gates_conv",
        grid=(t // tm, N_GATE_TILES),
        in_specs=[
            pl.BlockSpec((tm, D_MODEL), lambda m, n: (m, 0)),
            pl.BlockSpec((D_MODEL, TN_GATES), lambda m, n: (0, n0 + n)),
            pl.BlockSpec((1, TN_GATES), lambda m, n: (0, n0 + n)),
            prev_spec,
            pl.BlockSpec((rows, CONV_CH), lambda m, n: (tile(m, n), 0)),
            pl.BlockSpec((CONV_WIDTH, CONV_CH), const_map),
            pl.BlockSpec((1, CONV_CH), const_map),
            pl.BlockSpec((1, CONV_CH), const_map),
            pl.BlockSpec((1, CONV_CH), const_map),
            pl.BlockSpec((SUBLANES, LANES), const_map),
        ],
        out_specs=[
            pl.BlockSpec((tm, TN_GATES), lambda m, n: (m, n)),
            pl.BlockSpec((rows, CONV_CH), lambda m, n: (tile(m, n), 0)),
        ],
        out_shape=[
            jax.ShapeDtypeStruct((t, 2 * D_MODEL), BF16),
            jax.ShapeDtypeStruct((t, CONV_CH), BF16),
        ],
        scratch_shapes=[
            pltpu.VMEM((HIST_PAD + rows + SUBLANES, CONV_CH), F32),
            pltpu.VMEM((SUBLANES, HIST_PAD + rows, CONV_CH), F32),
        ],
        compiler_params=_params(2),
    )(xn, w_in, b_in, hist, u, *conv_consts, jnp.zeros((SUBLANES, LANES), jnp.int32))


def _in_proj(x, hist, norm_g, w_in, b_in, conv_consts, tm, fresh):
    t = x.shape[0]
    xn, q, kv = _qkv(x, norm_g, w_in, b_in, tm)
    tw = min(TM_WIDE, t)
    u = _glu(xn, w_in, b_in, tw)
    gates, c = _gates_conv(xn, w_in, b_in, u if fresh else hist, u, conv_consts, tw,
                           tw // N_GATE_TILES, fresh)
    return q, kv, u, gates, c


def _mix_body(ap_ref, as_ref, cp_ref, cs_ref, gp_ref, gs_ref, xp_ref, xs_ref, wao_hbm, wco_hbm,
              wout_hbm, g2_ref, wr_ref, br_ref, x1_ref, xn_ref, lg_ref,
              wao_ref, wco_ref, wout_ref, stage, sem):
    m = pl.program_id(0)

    @pl.when(m == 0)
    def _():
        _stage_weights([(wao_hbm, wao_ref), (wco_hbm, wco_ref), (wout_hbm, wout_ref)], stage, sem)

    is_sample = m >= T_P // TM_MIX
    a = jnp.where(is_sample, as_ref[...], ap_ref[...])
    c = jnp.where(is_sample, cs_ref[...], cp_ref[...])
    gate = jnp.where(is_sample, gs_ref[...], gp_ref[...])
    x = jnp.where(is_sample, xs_ref[...], xp_ref[...])
    pa = jnp.dot(a, wao_ref[...], preferred_element_type=F32)
    pc = jnp.dot(c, wco_ref[...], preferred_element_type=F32)
    y = (gate[:, :D_MODEL].astype(F32) * pa + gate[:, D_MODEL:].astype(F32) * pc).astype(BF16)
    x1 = x + jnp.dot(y, wout_ref[...], preferred_element_type=F32)
    x1_ref[...] = x1
    ms = jnp.mean(x1 * x1, axis=-1, keepdims=True)
    xn = x1 * lax.rsqrt(ms + EPS) * g2_ref[...]
    _store_packed_rows(xn_ref, xn)
    xn_hi = xn.astype(BF16)
    xn_lo = (xn - xn_hi.astype(F32)).astype(BF16)
    r_hi = jnp.dot(xn_hi, wr_ref[...], preferred_element_type=F32)
    r_lo = jnp.dot(xn_lo, wr_ref[...], preferred_element_type=F32)
    lg_ref[...] = r_hi + pltpu.roll(r_hi, ROUTER_PAD - N_ROUTER, axis=1) + r_lo + br_ref[...]


def _mix(a_p, a_s, c_p, c_s, g_p, g_s, x_p, x_s, wao, wco, wout, g2, wr, br):
    n_p = T_P // TM_MIX
    n_s = T_S // TM_MIX
    prompt_map = lambda m: (jnp.minimum(m, n_p - 1), 0)
    sample_map = lambda m: (jnp.clip(m - n_p, 0, n_s - 1), 0)
    row_map = lambda m: (m, 0)
    const_map = lambda m: (0, 0)

    def pair(width):
        return [pl.BlockSpec((TM_MIX, width), prompt_map), pl.BlockSpec((TM_MIX, width), sample_map)]

    return pl.pallas_call(
        _mix_body,
        name="mix",
        grid=(T // TM_MIX,),
        in_specs=pair(ATTN_WIDTH) + pair(CONV_CH) + pair(2 * D_MODEL) + pair(D_MODEL) + [
            pl.BlockSpec(memory_space=pl.ANY),
            pl.BlockSpec(memory_space=pl.ANY),
            pl.BlockSpec(memory_space=pl.ANY),
            pl.BlockSpec((1, D_MODEL), const_map),
            pl.BlockSpec((D_MODEL, ROUTER_PAD), const_map),
            pl.BlockSpec((1, ROUTER_PAD), const_map),
        ],
        out_specs=[
            pl.BlockSpec((TM_MIX, D_MODEL), row_map),
            pl.BlockSpec((TM_MIX * ROW_TILE, LANES), row_map),
            pl.BlockSpec((TM_MIX, ROUTER_PAD), row_map),
        ],
        out_shape=[
            jax.ShapeDtypeStruct((T, D_MODEL), F32),
            jax.ShapeDtypeStruct((T * ROW_TILE, LANES), U32),
            jax.ShapeDtypeStruct((T, ROUTER_PAD), F32),
        ],
        scratch_shapes=[
            pltpu.VMEM((ATTN_WIDTH, D_MODEL), BF16),
            pltpu.VMEM((CONV_CH, D_MODEL), BF16),
            pltpu.VMEM((D_MODEL, D_MODEL), BF16),
            pltpu.VMEM((2, W_STAGE_ROWS, D_MODEL), F32),
            pltpu.SemaphoreType.DMA((2,)),
        ],
        compiler_params=_params(1),
    )(a_p, a_s, c_p, c_s, g_p, g_s, x_p, x_s, wao, wco, wout, g2, wr, br)


REC_W, REC_E, REC_POS = 0, 2, 4


PLAN_E, PLAN_SRC, PLAN_FIRST, PLAN_WSLOT, PLAN_NEXT_E, PLAN_ZERO = range(6)


def _write_plan(counts, plan_ref, nu_ref):
    shift = BM_MOE.bit_length() - 1
    cnt = [counts[0, e] for e in range(N_EXPERTS)]
    nblk = [(c + (BM_MOE - 1)) >> shift for c in cnt]
    nxt = [None] * N_EXPERTS
    later = jnp.int32(-1)
    for e in reversed(range(N_EXPERTS)):
        nxt[e] = jnp.where(later >= 0, later, e)
        later = jnp.where(nblk[e] > 0, e, later)

    def clear(j, carry):
        for row in range(PLAN_ZERO + 1):
            plan_ref[row, j] = 0
        return carry
    lax.fori_loop(0, N_BLK_MOE, clear, 0)

    j0 = jnp.int32(0)
    row0 = jnp.int32(0)
    n_before = jnp.int32(0)
    for e in range(N_EXPERTS):
        def fill(b, carry, e=e, j0=j0, row0=row0, n_before=n_before):
            plan_ref[PLAN_E, j0 + b] = e
            plan_ref[PLAN_SRC, j0 + b] = row0 + b * BM_MOE
            plan_ref[PLAN_FIRST, j0 + b] = jnp.where(b == 0, 1, 0)
            plan_ref[PLAN_WSLOT, j0 + b] = n_before & 1
            plan_ref[PLAN_NEXT_E, j0 + b] = nxt[e]
            return carry
        lax.fori_loop(0, nblk[e], fill, 0)
        j0 = j0 + nblk[e]
        row0 = row0 + cnt[e]
        n_before = n_before + jnp.where(nblk[e] > 0, 1, 0)
    nu_ref[0] = j0


def _router_body(lg_ref, rec_ref, idx_ref, plan_ref, nu_ref, carry_sc):
    lane = lax.broadcasted_iota(jnp.int32, (R_ROUTE, ROUTER_PAD), 1).astype(F32)
    ltri = (lax.broadcasted_iota(jnp.int32, (R_ROUTE, R_ROUTE), 1)
            < lax.broadcasted_iota(jnp.int32, (R_ROUTE, R_ROUTE), 0)).astype(BF16)

    def pick(vals, idx):
        return jnp.sum(jnp.where(lane == idx, vals, 0.0), axis=1, keepdims=True)

    def first_lane(mask):
        return jnp.min(jnp.where(mask, lane, float(ROUTER_PAD)), axis=1, keepdims=True)

    def masked_max(mask, v):
        return jnp.max(jnp.where(mask, v, NEG), axis=1, keepdims=True)

    carry_sc[...] = jnp.zeros_like(carry_sc)

    def count_pass(i, carry_unused):
        rows = pl.ds(pl.multiple_of(i * R_ROUTE, R_ROUTE), R_ROUTE)
        lg = lg_ref[rows, :]
        gmask = lane < N_GROUPS
        gmax = masked_max(gmask, lg)
        p_grp = 1.0 / jnp.sum(jnp.where(gmask, jnp.exp(lg - gmax), 0.0), axis=1, keepdims=True)
        grp = first_lane(gmask & (lg == gmax))
        lo = N_GROUPS + EXPERTS_PER_GROUP * grp
        emask = (lane >= lo) & (lane < lo + EXPERTS_PER_GROUP)
        e1 = masked_max(emask, lg)
        i1 = first_lane(emask & (lg == e1))
        emask2 = emask & (lane != i1)
        e2 = masked_max(emask2, lg)
        i2 = first_lane(emask2 & (lg == e2))
        t = jnp.exp(e2 - e1)
        w1 = p_grp / (1.0 + t)
        w2 = p_grp * t / (1.0 + t)
        x1 = i1 - N_GROUPS
        x2 = i2 - N_GROUPS
        onehot = ((lane == x1) | (lane == x2 + N_EXPERTS)).astype(BF16)
        carry = carry_sc[0:1, :]
        before = jnp.dot(ltri, onehot, preferred_element_type=F32) + carry
        rank1 = pick(before, x1)
        rank2 = pick(before, x2 + N_EXPERTS)
        carry_sc[0:1, :] = carry + jnp.sum(onehot.astype(F32), axis=0, keepdims=True)
        rec = jnp.where(lane == REC_W, w1, 0.0)
        for ln, v in ((REC_W + 1, w2), (REC_E, x1), (REC_E + 1, x2),
                      (REC_POS, rank1), (REC_POS + 1, rank2)):
            rec = jnp.where(lane == ln, v, rec)
        rec_ref[rows, :] = rec
        return carry_unused

    lax.fori_loop(0, T // R_ROUTE, count_pass, 0)

    tot = carry_sc[0:1, :]
    counts = tot + pltpu.roll(tot, ROUTER_PAD - N_EXPERTS, axis=1)
    lane1 = lax.broadcasted_iota(jnp.int32, (1, ROUTER_PAD), 1)
    counts = jnp.where(lane1 < N_EXPERTS, counts, 0.0)
    _write_plan(counts.astype(jnp.int32), plan_ref, nu_ref)
    nblk = ((counts.astype(jnp.int32) + (BM_MOE - 1)) >> (BM_MOE.bit_length() - 1)).astype(F32)
    upper = (lax.broadcasted_iota(jnp.int32, (ROUTER_PAD, ROUTER_PAD), 0)
             < lax.broadcasted_iota(jnp.int32, (ROUTER_PAD, ROUTER_PAD), 1)).astype(BF16)
    blk_start = jnp.dot(jnp.broadcast_to(nblk, (SUBLANES, ROUTER_PAD)).astype(BF16), upper,
                        preferred_element_type=F32)[0:1, :]
    row0 = blk_start * BM_MOE

    def place_pass(i, carry_unused):
        rows = pl.ds(pl.multiple_of(i * R_ROUTE, R_ROUTE), R_ROUTE)
        rec = rec_ref[rows, :]
        x1 = pick(rec, REC_E)
        x2 = pick(rec, REC_E + 1)
        p1 = pick(row0, x1) + pick(rec, REC_POS)
        p2 = pick(row0, x2) + pick(tot, x2) + pick(rec, REC_POS + 1)
        rec = jnp.where(lane == REC_POS, p1 * ROW_TILE, rec)
        rec = jnp.where(lane == REC_POS + 1, p2 * ROW_TILE, rec)
        rec_ref[rows, :] = rec
        idx_ref[i] = rec.T[0:SUBLANES, :].astype(jnp.int32)
        return carry_unused

    lax.fori_loop(0, T // R_ROUTE, place_pass, 0)


def _router(logits):
    return pl.pallas_call(
        _router_body,
        name="router",
        grid=(1,),
        in_specs=[pl.BlockSpec((T, ROUTER_PAD), lambda i: (0, 0))],
        out_specs=[
            pl.BlockSpec((T, ROUTER_PAD), lambda i: (0, 0)),
            pl.BlockSpec((T // R_ROUTE, SUBLANES, R_ROUTE), lambda i: (0, 0, 0)),
            pl.BlockSpec(memory_space=pltpu.SMEM),
            pl.BlockSpec(memory_space=pltpu.SMEM),
        ],
        out_shape=[
            jax.ShapeDtypeStruct((T, ROUTER_PAD), F32),
            jax.ShapeDtypeStruct((T // R_ROUTE, SUBLANES, R_ROUTE), jnp.int32),
            jax.ShapeDtypeStruct((PLAN_ZERO + 1, N_BLK_MOE), jnp.int32),
            jax.ShapeDtypeStruct((1,), jnp.int32),
        ],
        scratch_shapes=[pltpu.VMEM((SUBLANES, ROUTER_PAD), F32)],
        compiler_params=_params(1),
    )(logits)


def _route(logits):
    rec, idx, plan, n_used = _router(logits)
    idx = idx.transpose(1, 0, 2).reshape(SUBLANES, T)
    eid = idx[REC_E:REC_E + TOP_K].reshape(-1)
    pos = idx[REC_POS:REC_POS + TOP_K].reshape(-1)
    order = jnp.argsort(eid, stable=True)
    tok_sorted = jnp.concatenate([((order % T) * ROW_TILE).astype(jnp.int32),
                                  jnp.zeros((BM_MOE,), jnp.int32)])
    return rec, pos, tok_sorted, plan, n_used


GATHER_AHEAD = 2
X_SLOTS = GATHER_AHEAD + 1
WEIGHT_DMA_PRIORITY = 1
GATHER_DMA_PRIORITY = 0


def _moe_body(plan, n_used, tok, x_hbm, wg_hbm, wu_hbm, wd_hbm, o_ref,
              xb, wg_st, wu_st, wd_st, wg_bf, wu_bf, wd_bf, sem_x, sem_w):
    j = pl.program_id(0)
    nu = n_used[0]

    def row_copy(src, slot, r):
        t8 = pl.multiple_of(tok[src + r], ROW_TILE)
        return pltpu.make_async_copy(x_hbm.at[pl.ds(t8, ROW_TILE)], xb.at[slot, _row_tile(r)],
                                     sem_x.at[slot])

    def block_wait(slot):
        pltpu.make_async_copy(x_hbm.at[pl.ds(0, BM_MOE * ROW_TILE)], xb.at[slot],
                              sem_x.at[slot]).wait()

    def weight_copies(e, ws):
        return (pltpu.make_async_copy(wg_hbm.at[e], wg_st.at[ws], sem_w.at[ws, 0]),
                pltpu.make_async_copy(wu_hbm.at[e], wu_st.at[ws], sem_w.at[ws, 1]),
                pltpu.make_async_copy(wd_hbm.at[e], wd_st.at[ws], sem_w.at[ws, 2]))

    @pl.when(j == 0)
    def _():
        for cp in weight_copies(plan[PLAN_E, 0], 0):
            cp.start(priority=WEIGHT_DMA_PRIORITY)
        for b in range(GATHER_AHEAD):
            src = plan[PLAN_SRC, b]

            def body(r, carry, src=src, b=b):
                row_copy(src, b, r).start(priority=GATHER_DMA_PRIORITY)
                return carry
            lax.fori_loop(0, BM_MOE, body, 0, unroll=8)

    @pl.when((j < nu) & (plan[PLAN_FIRST, jnp.minimum(j, N_BLK_MOE - 1)] == 1))
    def _():
        e = plan[PLAN_E, j]
        ws = plan[PLAN_WSLOT, j]
        e_next = plan[PLAN_NEXT_E, j]

        @pl.when(e_next != e)
        def _():
            for cp in weight_copies(e_next, 1 - ws):
                cp.start(priority=WEIGHT_DMA_PRIORITY)

        for cp in weight_copies(e, ws):
            cp.wait()
        wg_bf[...] = wg_st[ws].astype(BF16)
        wu_bf[...] = wu_st[ws].astype(BF16)
        wd_bf[...] = wd_st[ws].astype(BF16)

    @pl.when(j < nu)
    def _():
        slot = j % X_SLOTS
        next_slot = (j + GATHER_AHEAD) % X_SLOTS
        block_wait(slot)
        nsrc = plan[PLAN_SRC, jnp.minimum(j + GATHER_AHEAD, N_BLK_MOE - 1)]
        halves = [_load_packed_rows(xb.at[slot], BM_MOE, s) for s in range(ROW_TILE)]
        x = jnp.concatenate([lo for lo, _ in halves] + [hi for _, hi in halves], axis=1).astype(BF16)
        zero = plan[PLAN_ZERO, 0]
        group = BM_MOE // 8
        issued = [0]

        def gather_after(v):
            base = nsrc
            if v is not None:
                base = base + (lax.bitcast_convert_type(v[0:1, 0:1], jnp.int32)[0, 0] & zero)
            for r in range(issued[0], issued[0] + group):
                row_copy(base, next_slot, r).start(priority=GATHER_DMA_PRIORITY)
            issued[0] += group

        gather_after(None)
        gather_after(halves[ROW_TILE - 1][1])
        gu = []
        for c in range(D_EXPERT // MXU_N):
            cols = slice(c * MXU_N, (c + 1) * MXU_N)
            g = jnp.dot(x, wg_bf[:, cols], preferred_element_type=F32)
            gather_after(g)
            u = jnp.dot(x, wu_bf[:, cols], preferred_element_type=F32)
            gather_after(u)
            gu.append((g, u))
        acc = None
        for c, (g, u) in enumerate(gu):
            cols = slice(c * MXU_N, (c + 1) * MXU_N)
            h = (g * _sigmoid(g) * u).astype(BF16)
            part = jnp.dot(h, wd_bf[cols, :], preferred_element_type=F32)
            if c == 0:
                gather_after(part)
                gather_after(part[:, D_MODEL - LANES:])
            acc = part if acc is None else acc + part
        assert issued[0] == BM_MOE
        _store_packed_rows(o_ref, acc)

    @pl.when(j == nu)
    def _():
        for b in range(GATHER_AHEAD):
            block_wait((j + b) % X_SLOTS)

    @pl.when(j >= nu)
    def _():
        o_ref[...] = jnp.zeros_like(o_ref)


def _moe(plan, n_used, tok_sorted, xn_packed, w_g, w_u, w_d):
    grid_spec = pltpu.PrefetchScalarGridSpec(
        num_scalar_prefetch=3,
        grid=(N_BLK_MOE + 1,),
        in_specs=[pl.BlockSpec(memory_space=pl.ANY)] * 4,
        out_specs=pl.BlockSpec((BM_MOE * ROW_TILE, LANES), lambda j, pn, nu, tk: (j, 0)),
        scratch_shapes=[
            pltpu.VMEM((X_SLOTS, BM_MOE * ROW_TILE, LANES), U32),
            pltpu.VMEM((2, D_MODEL, D_EXPERT), F32),
            pltpu.VMEM((2, D_MODEL, D_EXPERT), F32),
            pltpu.VMEM((2, D_EXPERT, D_MODEL), F32),
            pltpu.VMEM((D_MODEL, D_EXPERT), BF16),
            pltpu.VMEM((D_MODEL, D_EXPERT), BF16),
            pltpu.VMEM((D_EXPERT, D_MODEL), BF16),
            pltpu.SemaphoreType.DMA((X_SLOTS,)),
            pltpu.SemaphoreType.DMA((2, 3)),
        ],
    )
    return pl.pallas_call(
        _moe_body,
        name="experts",
        grid_spec=grid_spec,
        out_shape=jax.ShapeDtypeStruct(((N_BLK_MOE + 1) * BM_MOE * ROW_TILE, LANES), U32),
        compiler_params=_params(1),
    )(plan, n_used, tok_sorted, xn_packed, w_g, w_u, w_d)


def _combine_body(pos, x1_ref, w_ref, gf_ref, o_hbm, y_ref, rb, sem, *, first_tile):
    m = pl.program_id(0)
    last = pl.num_programs(0) - 1

    def row_copy(base, slot, r, k):
        p8 = pl.multiple_of(pos[k * T + base + r], ROW_TILE)
        return pltpu.make_async_copy(o_hbm.at[pl.ds(p8, ROW_TILE)], rb.at[slot, k, _row_tile(r)],
                                     sem.at[slot])

    def tile_wait(slot):
        for k in range(TOP_K):
            pltpu.make_async_copy(o_hbm.at[pl.ds(0, TM_OUT * ROW_TILE)], rb.at[slot, k],
                                  sem.at[slot]).wait()

    def tile_base(mm):
        return (first_tile + mm) * TM_OUT

    @pl.when(m == 0)
    def _():
        for b in range(GATHER_AHEAD):
            base = tile_base(jnp.minimum(b, last))

            def body(r, carry, base=base, b=b):
                for k in range(TOP_K):
                    row_copy(base, b, r, k).start(priority=k)
                return carry
            lax.fori_loop(0, TM_OUT, body, 0, unroll=4)

    slot = m % X_SLOTS
    next_slot = (m + GATHER_AHEAD) % X_SLOTS
    tile_wait(slot)
    nbase = tile_base(jnp.minimum(m + GATHER_AHEAD, last))
    rows = TM_OUT // ROW_TILE
    w0 = w_ref[:, 0:1]
    w1 = w_ref[:, 1:2]
    ss = jnp.zeros((TM_OUT, 1), F32)
    for s in range(ROW_TILE):
        for r in range(s * rows, (s + 1) * rows):
            for k in range(TOP_K):
                row_copy(nbase, next_slot, r, k).start(priority=k)
        lo0, hi0 = _load_packed_rows(rb.at[slot, 0], TM_OUT, s)
        lo1, hi1 = _load_packed_rows(rb.at[slot, 1], TM_OUT, s)
        for off, r0, r1 in ((s * LANES, lo0, lo1), (HALF_D + s * LANES, hi0, hi1)):
            y = x1_ref[:, off:off + LANES] + w0 * r0 + w1 * r1
            ss = ss + jnp.sum(y * y, axis=-1, keepdims=True)
            y_ref[:, off:off + LANES] = y
    scale = lax.rsqrt(ss * (1.0 / D_MODEL) + EPS)
    y_ref[...] = y_ref[...] * scale * gf_ref[...]

    @pl.when(m == last)
    def _():
        for b in range(1, GATHER_AHEAD + 1):
            tile_wait((m + b) % X_SLOTS)


def _combine(pos, x1, wts, gf, out_sorted, first_tile, n_tiles):
    grid_spec = pltpu.PrefetchScalarGridSpec(
        num_scalar_prefetch=1,
        grid=(n_tiles,),
        in_specs=[
            pl.BlockSpec((TM_OUT, D_MODEL), lambda m, p: (first_tile + m, 0)),
            pl.BlockSpec((TM_OUT, ROUTER_PAD), lambda m, p: (first_tile + m, 0)),
            pl.BlockSpec((1, D_MODEL), lambda m, p: (0, 0)),
            pl.BlockSpec(memory_space=pl.ANY),
        ],
        out_specs=pl.BlockSpec((TM_OUT, D_MODEL), lambda m, p: (m, 0)),
        scratch_shapes=[
            pltpu.VMEM((X_SLOTS, TOP_K, TM_OUT * ROW_TILE, LANES), U32),
            pltpu.SemaphoreType.DMA((X_SLOTS,)),
        ],
    )
    return pl.pallas_call(
        functools.partial(_combine_body, first_tile=first_tile),
        name="combine",
        grid_spec=grid_spec,
        out_shape=jax.ShapeDtypeStruct((n_tiles * TM_OUT, D_MODEL), F32),
        compiler_params=_params(1),
    )(pos, x1, wts, gf, out_sorted)


def kernel(x_prompt, x_sample, cache_k, cache_v, state_conv, norm1_g, w_in, b_in, attn_sinks,
           w_attn_o, conv_dw, conv_dw_b, conv_ln_g, conv_ln_b, w_conv_o, w_out, norm2_g,
           w_router_group, b_router_group, w_router_expert, b_router_expert, w_e_gate, w_e_up,
           w_e_down, final_norm_g):
    x_p = x_prompt.reshape(T_P, D_MODEL)
    x_s = x_sample.reshape(T_S, D_MODEL)
    g1 = norm1_g[0][None, :]
    b1 = b_in[0][None, :]
    conv_consts = (conv_dw[0], conv_dw_b[0][None, :], conv_ln_g[0][None, :], conv_ln_b[0][None, :])
    hist_pad = jnp.pad(state_conv[0], ((0, 0), (HIST_PAD - HIST, 0), (0, 0)))
    q_p, kv_p, u_p, gates_p, c_p = _in_proj(x_p, None, g1, w_in[0], b1, conv_consts, TM_IN, True)
    q_s, kv_s, u_s, gates_s, c_s = _in_proj(x_s, hist_pad, g1, w_in[0], b1, conv_consts, T_S, False)

    sinks = attn_sinks[0]
    a_p = _attn_prompt(sinks, q_p, kv_p)
    ck = cache_k[0].reshape(N_STREAMS, WINDOW, KV_WIDTH)
    cv = cache_v[0].reshape(N_STREAMS, WINDOW, KV_WIDTH)
    a_s = _attn_sample(sinks, q_s, kv_s, ck, cv)

    w_r = jnp.concatenate([w_router_group[0], w_router_expert[0]], axis=1)
    w_r_hi = w_r.astype(BF16)
    w_r_lo = (w_r - w_r_hi.astype(F32)).astype(BF16)
    w_r_cat = jnp.concatenate(
        [w_r_hi, w_r_lo, jnp.zeros((D_MODEL, ROUTER_PAD - 2 * N_ROUTER), BF16)], axis=1)
    b_r = jnp.concatenate([b_router_group[0], b_router_expert[0],
                           jnp.zeros((ROUTER_PAD - N_ROUTER,), F32)])[None, :]

    x1, xn2, logits = _mix(a_p, a_s, c_p, c_s, gates_p, gates_s, x_p, x_s,
                           w_attn_o[0], w_conv_o[0], w_out[0],
                           norm2_g[0][None, :], w_r_cat, b_r)

    wts, pos, tok_sorted, plan, n_used = _route(logits)
    out_sorted = _moe(plan, n_used, tok_sorted, xn2, w_e_gate[0], w_e_up[0], w_e_down[0])
    gf = final_norm_g[None, :]
    y_p = _combine(pos, x1, wts, gf, out_sorted, 0, T_P // TM_OUT)
    y_s = _combine(pos, x1, wts, gf, out_sorted, T_P // TM_OUT, T_S // TM_OUT)

    kv_shape = (1, -1, WINDOW, N_KV_HEADS, HEAD_DIM)
    new_k_prompt = kv_p[T_P - WINDOW:, :KV_WIDTH].reshape(kv_shape)
    new_v_prompt = kv_p[T_P - WINDOW:, KV_WIDTH:].reshape(kv_shape)
    new_conv_prompt = u_p[T_P - HIST:].reshape(1, 1, HIST, CONV_CH)
    k_s = kv_s[:, :KV_WIDTH].reshape(N_STREAMS, L_S, KV_WIDTH)
    v_s = kv_s[:, KV_WIDTH:].reshape(N_STREAMS, L_S, KV_WIDTH)
    new_k_sample = jnp.concatenate([ck[:, L_S:], k_s], axis=1).reshape(kv_shape)
    new_v_sample = jnp.concatenate([cv[:, L_S:], v_s], axis=1).reshape(kv_shape)
    new_conv_sample = u_s.reshape(N_STREAMS, L_S, CONV_CH)[:, L_S - HIST:].reshape(
        1, N_STREAMS, HIST, CONV_CH)

    return (y_p.reshape(1, T_P, D_MODEL), y_s.reshape(N_STREAMS, L_S, D_MODEL),
            new_k_prompt, new_v_prompt, new_conv_prompt,
            new_k_sample, new_v_sample, new_conv_sample)
```

```python
import functools

import numpy as np
import jax
import jax.numpy as jnp
from jax import lax
from jax.experimental import pallas as pl
from jax.experimental.pallas import tpu as pltpu

F32 = jnp.float32
BF16 = jnp.bfloat16
U32 = jnp.uint32

D_MODEL = 2048
T_P = 8192
N_STREAMS = 8
L_S = 64
T_S = N_STREAMS * L_S
T = T_P + T_S
CHUNK = 64
WINDOW = 128
HEAD_DIM = 64
N_Q_HEADS = 16
N_KV_HEADS = 4
GROUP = N_Q_HEADS // N_KV_HEADS
ATTN_WIDTH = N_Q_HEADS * HEAD_DIM
KV_WIDTH = N_KV_HEADS * HEAD_DIM
CONV_CH = 1024
CONV_WIDTH = 31
HIST = CONV_WIDTH - 1
HIST_PAD = 32
SUBLANES = 8
LANES = 128
ROW_TILE = SUBLANES
HALF_D = D_MODEL // 2
HI_MASK = np.uint32(0xFFFF0000)
OFF_KV = ATTN_WIDTH
OFF_GLU = OFF_KV + 2 * KV_WIDTH
OFF_GATES = OFF_GLU + 2 * CONV_CH
IN_WIDTH = OFF_GATES + 2 * D_MODEL
N_GROUPS = 8
EXPERTS_PER_GROUP = 4
N_EXPERTS = N_GROUPS * EXPERTS_PER_GROUP
TOP_K = 2
D_EXPERT = 512
N_ROUTER = N_GROUPS + N_EXPERTS
ROUTER_PAD = 128
EPS = 1e-6
NEG = -1e30

VMEM_LIMIT = 56 * 1024 * 1024
MXU_N = 256

TM_IN = 1024
TM_WIDE = 2048
TN_IN = 512
TN_GATES = TN_IN
N_GATE_TILES = 2 * D_MODEL // TN_GATES
ROW_CHUNK = 256
W_STAGE_ROWS = 256
TQ_ATTN = 256
PAIR = 2 * CHUNK
BF16_ROWS = 16
SUB_CONV = 32
PIECE_CONV = 32
CONV_LAG = 1
TM_MIX = 256
BM_MOE = 256
R_ROUTE = 256
TM_OUT = 256
N_SLOTS = T * TOP_K
N_BLK_MOE = -(-(N_SLOTS + N_EXPERTS * (BM_MOE - 1)) // BM_MOE)


def _sigmoid(x):
    return 1.0 / (1.0 + jnp.exp(-x))


def _params(n_axes):
    return pltpu.CompilerParams(dimension_semantics=("arbitrary",) * n_axes,
                                vmem_limit_bytes=VMEM_LIMIT)


def _store_packed_rows(ref, y):
    rows = y.shape[0]
    for s in range(ROW_TILE):
        lo = y[:, s * LANES:(s + 1) * LANES].astype(BF16).astype(F32)
        hi = y[:, HALF_D + s * LANES:HALF_D + (s + 1) * LANES].astype(BF16).astype(F32)
        word = (lax.bitcast_convert_type(hi, U32) & HI_MASK) | (lax.bitcast_convert_type(lo, U32) >> 16)
        ref[pl.ds(s, rows, stride=ROW_TILE), :] = word


def _row_tile(r):
    start = r * ROW_TILE
    return pl.ds(start if isinstance(r, int) else pl.multiple_of(start, ROW_TILE), ROW_TILE)


def _load_packed_rows(ref, rows, s):
    word = ref[pl.ds(s, rows, stride=ROW_TILE), :]
    lo = lax.bitcast_convert_type(word << 16, F32)
    hi = lax.bitcast_convert_type(word & HI_MASK, F32)
    return lo, hi


def _chunk_dot(xn, w_ref, b_ref, c):
    w = w_ref[:, c * MXU_N:(c + 1) * MXU_N].astype(BF16)
    return jnp.dot(xn, w, preferred_element_type=F32) + b_ref[:, c * MXU_N:(c + 1) * MXU_N]


def _stage_weights(jobs, stage, sem):
    width = stage.shape[-1]
    chunks = [(w_hbm, w_bf, r0) for w_hbm, w_bf in jobs
              for r0 in range(0, w_bf.shape[0], W_STAGE_ROWS)]

    def chunk_copy(i):
        w_hbm, _, r0 = chunks[i]
        return pltpu.make_async_copy(w_hbm.at[pl.ds(r0, W_STAGE_ROWS), pl.ds(0, width)],
                                     stage.at[i % 2], sem.at[i % 2])
    chunk_copy(0).start()
    for i, (_, w_bf, r0) in enumerate(chunks):
        if i + 1 < len(chunks):
            chunk_copy(i + 1).start()
        chunk_copy(i).wait()
        w_bf[r0:r0 + W_STAGE_ROWS, :] = stage[i % 2].astype(BF16)


def _qkv_body(x_ref, g_ref, w_hbm, b_ref, xn_ref, q_ref, kv_ref, w_bf, stage, sem, *, tm):
    @pl.when(pl.program_id(0) == 0)
    def _():
        _stage_weights([(w_hbm, w_bf)], stage, sem)

    for r0 in range(0, tm, ROW_CHUNK):
        rows = slice(r0, r0 + ROW_CHUNK)
        x = x_ref[rows, :]
        ms = jnp.mean(x * x, axis=-1, keepdims=True)
        xn = (x * lax.rsqrt(ms + EPS) * g_ref[...]).astype(BF16)
        xn_ref[rows, :] = xn
        for c in range(OFF_GLU // MXU_N):
            cols = slice(c * MXU_N, (c + 1) * MXU_N)
            acc = jnp.dot(xn, w_bf[:, cols], preferred_element_type=F32) + b_ref[:, cols]
            if c < ATTN_WIDTH // MXU_N:
                q_ref[rows, cols] = (acc * (HEAD_DIM ** -0.5)).astype(BF16)
            else:
                kv_ref[rows, c * MXU_N - ATTN_WIDTH:(c + 1) * MXU_N - ATTN_WIDTH] = acc


def _qkv(x, norm_g, w_in, b_in, tm):
    t = x.shape[0]
    return pl.pallas_call(
        functools.partial(_qkv_body, tm=tm),
        name="qkv",
        grid=(t // tm,),
        in_specs=[
            pl.BlockSpec((tm, D_MODEL), lambda m: (m, 0)),
            pl.BlockSpec((1, D_MODEL), lambda m: (0, 0)),
            pl.BlockSpec(memory_space=pl.ANY),
            pl.BlockSpec((1, OFF_GLU), lambda m: (0, 0)),
        ],
        out_specs=[
            pl.BlockSpec((tm, D_MODEL), lambda m: (m, 0)),
            pl.BlockSpec((tm, ATTN_WIDTH), lambda m: (m, 0)),
            pl.BlockSpec((tm, 2 * KV_WIDTH), lambda m: (m, 0)),
        ],
        out_shape=[
            jax.ShapeDtypeStruct((t, D_MODEL), BF16),
            jax.ShapeDtypeStruct((t, ATTN_WIDTH), BF16),
            jax.ShapeDtypeStruct((t, 2 * KV_WIDTH), F32),
        ],
        scratch_shapes=[
            pltpu.VMEM((D_MODEL, OFF_GLU), BF16),
            pltpu.VMEM((2, W_STAGE_ROWS, OFF_GLU), F32),
            pltpu.SemaphoreType.DMA((2,)),
        ],
        compiler_params=_params(1),
    )(x, norm_g, w_in, b_in)


def _glu_body(xn_ref, wa_ref, ba_ref, wb_ref, bb_ref, u_ref):
    xn = xn_ref[...]
    for c in range(TN_IN // MXU_N):
        a = _chunk_dot(xn, wa_ref, ba_ref, c)
        b = _chunk_dot(xn, wb_ref, bb_ref, c)
        u_ref[:, c * MXU_N:(c + 1) * MXU_N] = a * _sigmoid(b)


def _glu(xn, w_in, b_in, tm):
    t = xn.shape[0]
    a0 = OFF_GLU // TN_IN
    b0 = (OFF_GLU + CONV_CH) // TN_IN
    return pl.pallas_call(
        _glu_body,
        name="glu",
        grid=(t // tm, CONV_CH // TN_IN),
        in_specs=[
            pl.BlockSpec((tm, D_MODEL), lambda m, n: (m, 0)),
            pl.BlockSpec((D_MODEL, TN_IN), lambda m, n: (0, a0 + n)),
            pl.BlockSpec((1, TN_IN), lambda m, n: (0, a0 + n)),
            pl.BlockSpec((D_MODEL, TN_IN), lambda m, n: (0, b0 + n)),
            pl.BlockSpec((1, TN_IN), lambda m, n: (0, b0 + n)),
        ],
        out_specs=pl.BlockSpec((tm, TN_IN), lambda m, n: (m, n)),
        out_shape=jax.ShapeDtypeStruct((t, CONV_CH), F32),
        compiler_params=_params(2),
    )(xn, w_in, b_in, w_in, b_in)


def _attn_prompt_body(sink_ref, q_ref, kvp_ref, kvc_ref, a_ref, at_sc):
    i = pl.program_id(0)
    kv = jnp.concatenate([kvp_ref[...], kvc_ref[...]], axis=0)
    k = kv[:, :KV_WIDTH].astype(BF16)
    vt = kv[:, KV_WIDTH:].T.astype(BF16)
    qt = q_ref[...].astype(F32).T.astype(BF16)
    n_cols = GROUP * PAIR
    n_keys = WINDOW + PAIR
    row = lax.broadcasted_iota(jnp.int32, (n_keys, n_cols), 0)
    col = lax.broadcasted_iota(jnp.int32, (n_keys, n_cols), 1)
    first = jnp.where((col & (PAIR - 1)) >= CHUNK, CHUNK, 0)
    gcol = lax.shift_right_logical(lax.broadcasted_iota(jnp.int32, (1, n_cols), 1),
                                   PAIR.bit_length() - 1)
    biases = []
    for p in range(TQ_ATTN // PAIR):
        pos = row + (i * TQ_ATTN + p * PAIR - WINDOW)
        ok = (row >= first) & (row < first + WINDOW + CHUNK) & (pos >= 0)
        biases.append(jnp.where(ok, 0.0, NEG))

    def scores(p, h):
        w0 = p * PAIR
        kh = k[w0:w0 + n_keys, h * HEAD_DIM:(h + 1) * HEAD_DIM]
        rhs = jnp.concatenate(
            [qt[(h * GROUP + g) * HEAD_DIM:(h * GROUP + g + 1) * HEAD_DIM, w0:w0 + PAIR]
             for g in range(GROUP)], axis=1)
        return jnp.dot(kh, rhs, preferred_element_type=F32) + biases[p]

    def finish(p, h, st):
        w0 = p * PAIR
        sink = jnp.full((1, n_cols), sink_ref[h * GROUP], F32)
        for g in range(1, GROUP):
            sink = jnp.where(gcol == g, sink_ref[h * GROUP + g], sink)
        m = jnp.maximum(jnp.max(st, axis=0, keepdims=True), sink)
        pt = jnp.exp(st - m)
        denom = jnp.sum(pt, axis=0, keepdims=True) + jnp.exp(sink - m)
        ot = jnp.dot(vt[h * HEAD_DIM:(h + 1) * HEAD_DIM, w0:w0 + n_keys], pt.astype(BF16),
                     preferred_element_type=F32) / denom
        for g in range(GROUP):
            r0 = (h * GROUP + g) * HEAD_DIM
            at_sc[r0:r0 + HEAD_DIM, w0:w0 + PAIR] = ot[:, g * PAIR:(g + 1) * PAIR]

    items = [(p, h) for p in range(TQ_ATTN // PAIR) for h in range(N_KV_HEADS)]
    st_next = scores(*items[0])
    for n, (p, h) in enumerate(items):
        st = st_next
        if n + 1 < len(items):
            st_next = scores(*items[n + 1])
        finish(p, h, st)
    a_ref[...] = at_sc[...].T.astype(BF16)


def _attn_prompt(sinks, q, kv):
    r = TQ_ATTN // WINDOW
    return pl.pallas_call(
        _attn_prompt_body,
        name="attn_prompt",
        grid=(T_P // TQ_ATTN,),
        in_specs=[
            pl.BlockSpec(memory_space=pltpu.SMEM),
            pl.BlockSpec((TQ_ATTN, ATTN_WIDTH), lambda i: (i, 0)),
            pl.BlockSpec((WINDOW, 2 * KV_WIDTH), lambda i: (jnp.maximum(i * r - 1, 0), 0)),
            pl.BlockSpec((TQ_ATTN, 2 * KV_WIDTH), lambda i: (i, 0)),
        ],
        out_specs=pl.BlockSpec((TQ_ATTN, ATTN_WIDTH), lambda i: (i, 0)),
        out_shape=jax.ShapeDtypeStruct((T_P, ATTN_WIDTH), BF16),
        scratch_shapes=[pltpu.VMEM((ATTN_WIDTH, TQ_ATTN), F32)],
        compiler_params=_params(1),
    )(sinks, q, kv, kv)


def _attn_sample_body(sink_ref, q_ref, kvc_ref, ck_ref, cv_ref, a_ref):
    kvc = kvc_ref[...]
    kw = jnp.concatenate([ck_ref[0], kvc[:, :KV_WIDTH]], axis=0).astype(BF16)
    vw = jnp.concatenate([cv_ref[0], kvc[:, KV_WIDTH:]], axis=0).astype(BF16)
    qc = q_ref[...]
    for h in range(N_KV_HEADS):
        kh = kw[:, h * HEAD_DIM:(h + 1) * HEAD_DIM]
        vh = vw[:, h * HEAD_DIM:(h + 1) * HEAD_DIM]
        qg = jnp.concatenate(
            [qc[:, (h * GROUP + g) * HEAD_DIM:(h * GROUP + g + 1) * HEAD_DIM] for g in range(GROUP)],
            axis=0)
        s = lax.dot_general(qg, kh, (((1,), (1,)), ((), ())), preferred_element_type=F32)
        sink = jnp.concatenate(
            [jnp.full((L_S, 1), sink_ref[h * GROUP + g], F32) for g in range(GROUP)], axis=0)
        m = jnp.maximum(jnp.max(s, axis=1, keepdims=True), sink)
        p = jnp.exp(s - m)
        denom = jnp.sum(p, axis=1, keepdims=True) + jnp.exp(sink - m)
        o = jnp.dot(p.astype(BF16), vh, preferred_element_type=F32) / denom
        for g in range(GROUP):
            c0 = (h * GROUP + g) * HEAD_DIM
            a_ref[:, c0:c0 + HEAD_DIM] = o[g * L_S:(g + 1) * L_S].astype(BF16)


def _attn_sample(sinks, q, kv, cache_k, cache_v):
    return pl.pallas_call(
        _attn_sample_body,
        name="attn_sample",
        grid=(N_STREAMS,),
        in_specs=[
            pl.BlockSpec(memory_space=pltpu.SMEM),
            pl.BlockSpec((L_S, ATTN_WIDTH), lambda b: (b, 0)),
            pl.BlockSpec((L_S, 2 * KV_WIDTH), lambda b: (b, 0)),
            pl.BlockSpec((1, WINDOW, KV_WIDTH), lambda b: (b, 0, 0)),
            pl.BlockSpec((1, WINDOW, KV_WIDTH), lambda b: (b, 0, 0)),
        ],
        out_specs=pl.BlockSpec((L_S, ATTN_WIDTH), lambda b: (b, 0)),
        out_shape=jax.ShapeDtypeStruct((T_S, ATTN_WIDTH), BF16),
        compiler_params=_params(1),
    )(sinks, q, kv, cache_k, cache_v)


def _conv_shifts(prev, cur_ref, ext_sc, sh_sc, rows):
    ext_sc[0:HIST_PAD, :] = prev
    ext_sc[HIST_PAD:HIST_PAD + rows, :] = cur_ref[...]
    ext_sc[HIST_PAD + rows:, :] = jnp.zeros((SUBLANES, CONV_CH), F32)
    lead = HIST_PAD - HIST
    span = rows + HIST_PAD - SUBLANES
    for s in range(SUBLANES):
        sh_sc[s, 0:span, :] = ext_sc[lead + s:lead + s + span, :]


def _conv_rows(r0, w_ref, b_ref, lg_ref, lb_ref, o_ref, sh_sc):
    acc = jnp.zeros((SUB_CONV, CONV_CH), F32) + b_ref[...]
    for k in range(CONV_WIDTH):
        a, s = divmod(k, SUBLANES)
        r = r0 + a * SUBLANES
        acc = acc + w_ref[k:k + 1, :] * sh_sc[s, r:r + SUB_CONV, :]
    mu = jnp.mean(acc, axis=-1, keepdims=True)
    d = acc - mu
    var = jnp.mean(d * d, axis=-1, keepdims=True)
    yn = d * lax.rsqrt(var + EPS) * lg_ref[...] + lb_ref[...]
    out = yn * _sigmoid(yn)
    o_ref[r0:r0 + SUB_CONV, :] = out.astype(BF16)
    return out


def _gates_conv_body(xn_ref, w_ref, b_ref, prev_ref, cur_ref, cw_ref, cb_ref, lg_ref, lb_ref, zero_ref,
                     gate_ref, c_ref, ext_sc, sh_sc, *, rows, fresh):
    if fresh:
        step = pl.program_id(0) * N_GATE_TILES + pl.program_id(1)
        prev = jnp.where(step > 0, prev_ref[...], 0.0)
    else:
        prev = prev_ref[0]
    _conv_shifts(prev, cur_ref, ext_sc, sh_sc, rows)
    n_chunks = TN_GATES // MXU_N
    w = [w_ref[:, c * MXU_N:(c + 1) * MXU_N].astype(BF16) for c in range(n_chunks)]
    zeros = []
    for i in range(rows // PIECE_CONV):
        r = slice(i * ROW_CHUNK, (i + 1) * ROW_CHUNK)
        xr = xn_ref[r, :]
        if i >= CONV_LAG:
            zero16 = zeros[i - CONV_LAG]
            top = jnp.concatenate([xr[0:BF16_ROWS, 0:LANES] + zero16, xr[0:BF16_ROWS, LANES:]], axis=1)
            xr = jnp.concatenate([top, xr[BF16_ROWS:, :]], axis=0)
        accs = [jnp.dot(xr, w[c], preferred_element_type=F32) for c in range(n_chunks)]
        outs = [_conv_rows(r0, cw_ref, cb_ref, lg_ref, lb_ref, c_ref, sh_sc)
                for r0 in range(i * PIECE_CONV, (i + 1) * PIECE_CONV, SUB_CONV)]
        bits = [lax.bitcast_convert_type(o[g * SUBLANES:(g + 1) * SUBLANES, 0:LANES], jnp.int32)
                for o in outs for g in range(SUB_CONV // SUBLANES)]
        zero8 = lax.bitcast_convert_type(functools.reduce(jnp.bitwise_or, bits) & zero_ref[...], F32)
        zeros.append(jnp.concatenate([zero8, zero8], axis=0).astype(BF16))
        for c in range(n_chunks):
            cols = slice(c * MXU_N, (c + 1) * MXU_N)
            gate_ref[r, cols] = _sigmoid(accs[c] + b_ref[:, cols]).astype(BF16)


def _gates_conv(xn, w_in, b_in, hist, u, conv_consts, tm, rows, fresh):
    t = xn.shape[0]
    n0 = OFF_GATES // TN_GATES
    tile = lambda m, n: m * N_GATE_TILES + n
    if fresh:
        r = rows // HIST_PAD
        prev_spec = pl.BlockSpec((HIST_PAD, CONV_CH), lambda m, n: (jnp.maximum(tile(m, n) * r - 1, 0), 0))
    else:
        prev_spec = pl.BlockSpec((1, HIST_PAD, CONV_CH), lambda m, n: (tile(m, n), 0, 0))
    const_map = lambda m, n: (0, 0)
    return pl.pallas_call(
        functools.partial(_gates_conv_body, rows=rows, fresh=fresh),
        name="gates_conv",
        grid=(t // tm, N_GATE_TILES),
        in_specs=[
            pl.BlockSpec((tm, D_MODEL), lambda m, n: (m, 0)),
            pl.BlockSpec((D_MODEL, TN_GATES), lambda m, n: (0, n0 + n)),
            pl.BlockSpec((1, TN_GATES), lambda m, n: (0, n0 + n)),
            prev_spec,
            pl.BlockSpec((rows, CONV_CH), lambda m, n: (tile(m, n), 0)),
            pl.BlockSpec((CONV_WIDTH, CONV_CH), const_map),
            pl.BlockSpec((1, CONV_CH), const_map),
            pl.BlockSpec((1, CONV_CH), const_map),
            pl.BlockSpec((1, CONV_CH), const_map),
            pl.BlockSpec((SUBLANES, LANES), const_map),
        ],
        out_specs=[
            pl.BlockSpec((tm, TN_GATES), lambda m, n: (m, n)),
            pl.BlockSpec((rows, CONV_CH), lambda m, n: (tile(m, n), 0)),
        ],
        out_shape=[
            jax.ShapeDtypeStruct((t, 2 * D_MODEL), BF16),
            jax.ShapeDtypeStruct((t, CONV_CH), BF16),
        ],
        scratch_shapes=[
            pltpu.VMEM((HIST_PAD + rows + SUBLANES, CONV_CH), F32),
            pltpu.VMEM((SUBLANES, HIST_PAD + rows, CONV_CH), F32),
        ],
        compiler_params=_params(2),
    )(xn, w_in, b_in, hist, u, *conv_consts, jnp.zeros((SUBLANES, LANES), jnp.int32))


def _in_proj(x, hist, norm_g, w_in, b_in, conv_consts, tm, fresh):
    t = x.shape[0]
    xn, q, kv = _qkv(x, norm_g, w_in, b_in, tm)
    tw = min(TM_WIDE, t)
    u = _glu(xn, w_in, b_in, tw)
    gates, c = _gates_conv(xn, w_in, b_in, u if fresh else hist, u, conv_consts, tw,
                           tw // N_GATE_TILES, fresh)
    return q, kv, u, gates, c


def _mix_body(ap_ref, as_ref, cp_ref, cs_ref, gp_ref, gs_ref, xp_ref, xs_ref, wao_hbm, wco_hbm,
              wout_hbm, g2_ref, wr_ref, br_ref, x1_ref, xn_ref, lg_ref,
              wao_ref, wco_ref, wout_ref, stage, sem):
    m = pl.program_id(0)

    @pl.when(m == 0)
    def _():
        _stage_weights([(wao_hbm, wao_ref), (wco_hbm, wco_ref), (wout_hbm, wout_ref)], stage, sem)

    is_sample = m >= T_P // TM_MIX
    a = jnp.where(is_sample, as_ref[...], ap_ref[...])
    c = jnp.where(is_sample, cs_ref[...], cp_ref[...])
    gate = jnp.where(is_sample, gs_ref[...], gp_ref[...])
    x = jnp.where(is_sample, xs_ref[...], xp_ref[...])
    pa = jnp.dot(a, wao_ref[...], preferred_element_type=F32)
    pc = jnp.dot(c, wco_ref[...], preferred_element_type=F32)
    y = (gate[:, :D_MODEL].astype(F32) * pa + gate[:, D_MODEL:].astype(F32) * pc).astype(BF16)
    x1 = x + jnp.dot(y, wout_ref[...], preferred_element_type=F32)
    x1_ref[...] = x1
    ms = jnp.mean(x1 * x1, axis=-1, keepdims=True)
    xn = x1 * lax.rsqrt(ms + EPS) * g2_ref[...]
    _store_packed_rows(xn_ref, xn)
    r = jnp.dot(xn.astype(BF16), wr_ref[...], preferred_element_type=F32)
    lg_ref[...] = r + pltpu.roll(r, ROUTER_PAD - N_ROUTER, axis=1) + br_ref[...]


def _mix(a_p, a_s, c_p, c_s, g_p, g_s, x_p, x_s, wao, wco, wout, g2, wr, br):
    n_p = T_P // TM_MIX
    n_s = T_S // TM_MIX
    prompt_map = lambda m: (jnp.minimum(m, n_p - 1), 0)
    sample_map = lambda m: (jnp.clip(m - n_p, 0, n_s - 1), 0)
    row_map = lambda m: (m, 0)
    const_map = lambda m: (0, 0)

    def pair(width):
        return [pl.BlockSpec((TM_MIX, width), prompt_map), pl.BlockSpec((TM_MIX, width), sample_map)]

    return pl.pallas_call(
        _mix_body,
        name="mix",
        grid=(T // TM_MIX,),
        in_specs=pair(ATTN_WIDTH) + pair(CONV_CH) + pair(2 * D_MODEL) + pair(D_MODEL) + [
            pl.BlockSpec(memory_space=pl.ANY),
            pl.BlockSpec(memory_space=pl.ANY),
            pl.BlockSpec(memory_space=pl.ANY),
            pl.BlockSpec((1, D_MODEL), const_map),
            pl.BlockSpec((D_MODEL, ROUTER_PAD), const_map),
            pl.BlockSpec((1, ROUTER_PAD), const_map),
        ],
        out_specs=[
            pl.BlockSpec((TM_MIX, D_MODEL), row_map),
            pl.BlockSpec((TM_MIX * ROW_TILE, LANES), row_map),
            pl.BlockSpec((TM_MIX, ROUTER_PAD), row_map),
        ],
        out_shape=[
            jax.ShapeDtypeStruct((T, D_MODEL), F32),
            jax.ShapeDtypeStruct((T * ROW_TILE, LANES), U32),
            jax.ShapeDtypeStruct((T, ROUTER_PAD), F32),
        ],
        scratch_shapes=[
            pltpu.VMEM((ATTN_WIDTH, D_MODEL), BF16),
            pltpu.VMEM((CONV_CH, D_MODEL), BF16),
            pltpu.VMEM((D_MODEL, D_MODEL), BF16),
            pltpu.VMEM((2, W_STAGE_ROWS, D_MODEL), F32),
            pltpu.SemaphoreType.DMA((2,)),
        ],
        compiler_params=_params(1),
    )(a_p, a_s, c_p, c_s, g_p, g_s, x_p, x_s, wao, wco, wout, g2, wr, br)


REC_W, REC_E, REC_POS = 0, 2, 4


PLAN_E, PLAN_SRC, PLAN_FIRST, PLAN_WSLOT, PLAN_NEXT_E, PLAN_ZERO = range(6)


def _write_plan(cnt, plan_ref, nu_ref):
    shift = BM_MOE.bit_length() - 1
    nblk = [(c + (BM_MOE - 1)) >> shift for c in cnt]
    nxt = [None] * N_EXPERTS
    later = jnp.int32(-1)
    for e in reversed(range(N_EXPERTS)):
        nxt[e] = jnp.where(later >= 0, later, e)
        later = jnp.where(nblk[e] > 0, e, later)

    def clear(j, carry):
        for row in range(PLAN_ZERO + 1):
            plan_ref[row, j] = 0
        return carry
    lax.fori_loop(0, N_BLK_MOE, clear, 0)

    first_blocks = []
    j0 = jnp.int32(0)
    row0 = jnp.int32(0)
    n_before = jnp.int32(0)
    for e in range(N_EXPERTS):
        def fill(b, carry, e=e, j0=j0, row0=row0, n_before=n_before):
            plan_ref[PLAN_E, j0 + b] = e
            plan_ref[PLAN_SRC, j0 + b] = row0 + b * BM_MOE
            plan_ref[PLAN_FIRST, j0 + b] = jnp.where(b == 0, 1, 0)
            plan_ref[PLAN_WSLOT, j0 + b] = n_before & 1
            plan_ref[PLAN_NEXT_E, j0 + b] = nxt[e]
            return carry
        lax.fori_loop(0, nblk[e], fill, 0)
        first_blocks.append(j0)
        j0 = j0 + nblk[e]
        row0 = row0 + cnt[e]
        n_before = n_before + jnp.where(nblk[e] > 0, 1, 0)
    nu_ref[0] = j0
    return first_blocks


def _router_body(lg_ref, rec_ref, idx_ref, plan_ref, nu_ref, carry_sc, rect_sc):
    n_chunks = T // R_ROUTE
    grow = lax.broadcasted_iota(jnp.int32, (N_GROUPS, R_ROUTE), 0).astype(F32)
    erow_i = lax.broadcasted_iota(jnp.int32, (N_EXPERTS, R_ROUTE), 0)
    erow = erow_i.astype(F32)
    egrp = (erow_i >> (EXPERTS_PER_GROUP.bit_length() - 1)).astype(F32)
    row8 = lax.broadcasted_iota(jnp.int32, (SUBLANES, R_ROUTE), 0)
    row128 = lax.broadcasted_iota(jnp.int32, (ROUTER_PAD, R_ROUTE), 0)
    upper = (lax.broadcasted_iota(jnp.int32, (R_ROUTE, R_ROUTE), 0)
             < lax.broadcasted_iota(jnp.int32, (R_ROUTE, R_ROUTE), 1)).astype(BF16)

    def rows_of(base, vals):
        out = jnp.zeros(base.shape, F32)
        for r, v in vals:
            out = jnp.where(base == r, v, out)
        return out

    carry_sc[...] = jnp.zeros_like(carry_sc)

    def count_pass(i, carry_unused):
        rows = pl.ds(pl.multiple_of(i * R_ROUTE, R_ROUTE), R_ROUTE)
        lgt = lg_ref[rows, :].T
        g = lgt[0:N_GROUPS]
        ex = lgt[N_GROUPS:N_ROUTER]
        gmax = jnp.max(g, axis=0, keepdims=True)
        p_grp = 1.0 / jnp.sum(jnp.exp(g - gmax), axis=0, keepdims=True)
        grp = jnp.min(jnp.where(g == gmax, grow, float(N_GROUPS)), axis=0, keepdims=True)
        emask = egrp == grp
        e1 = jnp.max(jnp.where(emask, ex, NEG), axis=0, keepdims=True)
        x1 = jnp.min(jnp.where(emask & (ex == e1), erow, float(N_EXPERTS)), axis=0, keepdims=True)
        emask2 = emask & (erow != x1)
        e2 = jnp.max(jnp.where(emask2, ex, NEG), axis=0, keepdims=True)
        x2 = jnp.min(jnp.where(emask2 & (ex == e2), erow, float(N_EXPERTS)), axis=0, keepdims=True)
        t = jnp.exp(e2 - e1)
        w1 = p_grp / (1.0 + t)
        w2 = p_grp * t / (1.0 + t)
        oh1 = erow == x1
        oh2 = erow == x2
        onehot = jnp.concatenate([oh1.astype(BF16), oh2.astype(BF16)], axis=0)
        carry = carry_sc[:, 0:1]
        before = jnp.dot(onehot, upper, preferred_element_type=F32) + carry
        rank1 = jnp.sum(jnp.where(oh1, before[0:N_EXPERTS], 0.0), axis=0, keepdims=True)
        rank2 = jnp.sum(jnp.where(oh2, before[N_EXPERTS:], 0.0), axis=0, keepdims=True)
        total = carry + jnp.sum(onehot.astype(F32), axis=1, keepdims=True)
        carry_sc[...] = jnp.broadcast_to(total, carry_sc.shape)
        rect_sc[i] = rows_of(row8, ((REC_E, x1), (REC_E + 1, x2), (REC_POS, rank1), (REC_POS + 1, rank2)))
        rec_ref[rows, :] = rows_of(row128, ((REC_W, w1), (REC_W + 1, w2))).T
        return carry_unused

    lax.fori_loop(0, n_chunks, count_pass, 0)

    tot = carry_sc[:, 0:1]
    tot1 = [tot[e, 0] for e in range(N_EXPERTS)]
    cnt = [(tot1[e] + tot[N_EXPERTS + e, 0]).astype(jnp.int32) for e in range(N_EXPERTS)]
    first_blocks = _write_plan(cnt, plan_ref, nu_ref)
    row0 = [(fb * BM_MOE).astype(F32) for fb in first_blocks]

    def place_pass(i, carry_unused):
        rec = rect_sc[i]
        x1 = rec[REC_E:REC_E + 1]
        x2 = rec[REC_E + 1:REC_E + 2]
        p1 = rec[REC_POS:REC_POS + 1]
        p2 = rec[REC_POS + 1:REC_POS + 2]
        for e in range(N_EXPERTS):
            p1 = p1 + jnp.where(x1 == e, row0[e], 0.0)
            p2 = p2 + jnp.where(x2 == e, row0[e] + tot1[e], 0.0)
        rec = jnp.where(row8 == REC_POS, p1 * ROW_TILE, rec)
        rec = jnp.where(row8 == REC_POS + 1, p2 * ROW_TILE, rec)
        idx_ref[i] = rec.astype(jnp.int32)
        return carry_unused

    lax.fori_loop(0, n_chunks, place_pass, 0)


def _router(logits):
    n_chunks = T // R_ROUTE
    return pl.pallas_call(
        _router_body,
        name="router",
        grid=(1,),
        in_specs=[pl.BlockSpec((T, ROUTER_PAD), lambda i: (0, 0))],
        out_specs=[
            pl.BlockSpec((T, ROUTER_PAD), lambda i: (0, 0)),
            pl.BlockSpec((n_chunks, SUBLANES, R_ROUTE), lambda i: (0, 0, 0)),
            pl.BlockSpec(memory_space=pltpu.SMEM),
            pl.BlockSpec(memory_space=pltpu.SMEM),
        ],
        out_shape=[
            jax.ShapeDtypeStruct((T, ROUTER_PAD), F32),
            jax.ShapeDtypeStruct((n_chunks, SUBLANES, R_ROUTE), jnp.int32),
            jax.ShapeDtypeStruct((PLAN_ZERO + 1, N_BLK_MOE), jnp.int32),
            jax.ShapeDtypeStruct((1,), jnp.int32),
        ],
        scratch_shapes=[
            pltpu.VMEM((2 * N_EXPERTS, LANES), F32),
            pltpu.VMEM((n_chunks, SUBLANES, R_ROUTE), F32),
        ],
        compiler_params=_params(1),
    )(logits)


def _route(logits):
    rec, idx, plan, n_used = _router(logits)
    idx = idx.transpose(1, 0, 2).reshape(SUBLANES, T)
    eid = idx[REC_E:REC_E + TOP_K].reshape(-1)
    pos = idx[REC_POS:REC_POS + TOP_K].reshape(-1)
    order = jnp.argsort(eid, stable=True)
    tok_sorted = jnp.concatenate([((order % T) * ROW_TILE).astype(jnp.int32),
                                  jnp.zeros((BM_MOE,), jnp.int32)])
    return rec, pos, tok_sorted, plan, n_used


GATHER_AHEAD = 2
X_SLOTS = GATHER_AHEAD + 1
WEIGHT_DMA_PRIORITY = 1
GATHER_DMA_PRIORITY = 0


def _moe_body(plan, n_used, tok, x_hbm, wg_hbm, wu_hbm, wd_hbm, o_ref,
              xb, wg_st, wu_st, wd_st, wg_bf, wu_bf, wd_bf, sem_x, sem_w):
    j = pl.program_id(0)
    nu = n_used[0]

    def row_copy(src, slot, r):
        t8 = pl.multiple_of(tok[src + r], ROW_TILE)
        return pltpu.make_async_copy(x_hbm.at[pl.ds(t8, ROW_TILE)], xb.at[slot, _row_tile(r)],
                                     sem_x.at[slot])

    def block_wait(slot):
        pltpu.make_async_copy(x_hbm.at[pl.ds(0, BM_MOE * ROW_TILE)], xb.at[slot],
                              sem_x.at[slot]).wait()

    def weight_copies(e, ws):
        return (pltpu.make_async_copy(wg_hbm.at[e], wg_st.at[ws], sem_w.at[ws, 0]),
                pltpu.make_async_copy(wu_hbm.at[e], wu_st.at[ws], sem_w.at[ws, 1]),
                pltpu.make_async_copy(wd_hbm.at[e], wd_st.at[ws], sem_w.at[ws, 2]))

    @pl.when(j == 0)
    def _():
        for cp in weight_copies(plan[PLAN_E, 0], 0):
            cp.start(priority=WEIGHT_DMA_PRIORITY)
        for b in range(GATHER_AHEAD):
            src = plan[PLAN_SRC, b]

            def body(r, carry, src=src, b=b):
                row_copy(src, b, r).start(priority=GATHER_DMA_PRIORITY)
                return carry
            lax.fori_loop(0, BM_MOE, body, 0, unroll=8)

    @pl.when((j < nu) & (plan[PLAN_FIRST, jnp.minimum(j, N_BLK_MOE - 1)] == 1))
    def _():
        e = plan[PLAN_E, j]
        ws = plan[PLAN_WSLOT, j]
        e_next = plan[PLAN_NEXT_E, j]

        @pl.when(e_next != e)
        def _():
            for cp in weight_copies(e_next, 1 - ws):
                cp.start(priority=WEIGHT_DMA_PRIORITY)

        for cp in weight_copies(e, ws):
            cp.wait()
        wg_bf[...] = wg_st[ws].astype(BF16)
        wu_bf[...] = wu_st[ws].astype(BF16)
        wd_bf[...] = wd_st[ws].astype(BF16)

    @pl.when(j < nu)
    def _():
        slot = j % X_SLOTS
        next_slot = (j + GATHER_AHEAD) % X_SLOTS
        block_wait(slot)
        nsrc = plan[PLAN_SRC, jnp.minimum(j + GATHER_AHEAD, N_BLK_MOE - 1)]
        halves = [_load_packed_rows(xb.at[slot], BM_MOE, s) for s in range(ROW_TILE)]
        x = jnp.concatenate([lo for lo, _ in halves] + [hi for _, hi in halves], axis=1).astype(BF16)
        zero = plan[PLAN_ZERO, 0]
        group = BM_MOE // 8
        issued = [0]

        def gather_after(v):
            base = nsrc
            if v is not None:
                base = base + (lax.bitcast_convert_type(v[0:1, 0:1], jnp.int32)[0, 0] & zero)
            for r in range(issued[0], issued[0] + group):
                row_copy(base, next_slot, r).start(priority=GATHER_DMA_PRIORITY)
            issued[0] += group

        gather_after(None)
        gather_after(halves[ROW_TILE - 1][1])
        gu = []
        for c in range(D_EXPERT // MXU_N):
            cols = slice(c * MXU_N, (c + 1) * MXU_N)
            g = jnp.dot(x, wg_bf[:, cols], preferred_element_type=F32)
            gather_after(g)
            u = jnp.dot(x, wu_bf[:, cols], preferred_element_type=F32)
            gather_after(u)
            gu.append((g, u))
        acc = None
        for c, (g, u) in enumerate(gu):
            cols = slice(c * MXU_N, (c + 1) * MXU_N)
            h = (g * _sigmoid(g) * u).astype(BF16)
            part = jnp.dot(h, wd_bf[cols, :], preferred_element_type=F32)
            if c == 0:
                gather_after(part)
                gather_after(part[:, D_MODEL - LANES:])
            acc = part if acc is None else acc + part
        assert issued[0] == BM_MOE
        _store_packed_rows(o_ref, acc)

    @pl.when(j == nu)
    def _():
        for b in range(GATHER_AHEAD):
            block_wait((j + b) % X_SLOTS)

    @pl.when(j >= nu)
    def _():
        o_ref[...] = jnp.zeros_like(o_ref)


def _moe(plan, n_used, tok_sorted, xn_packed, w_g, w_u, w_d):
    grid_spec = pltpu.PrefetchScalarGridSpec(
        num_scalar_prefetch=3,
        grid=(N_BLK_MOE + 1,),
        in_specs=[pl.BlockSpec(memory_space=pl.ANY)] * 4,
        out_specs=pl.BlockSpec((BM_MOE * ROW_TILE, LANES), lambda j, pn, nu, tk: (j, 0)),
        scratch_shapes=[
            pltpu.VMEM((X_SLOTS, BM_MOE * ROW_TILE, LANES), U32),
            pltpu.VMEM((2, D_MODEL, D_EXPERT), F32),
            pltpu.VMEM((2, D_MODEL, D_EXPERT), F32),
            pltpu.VMEM((2, D_EXPERT, D_MODEL), F32),
            pltpu.VMEM((D_MODEL, D_EXPERT), BF16),
            pltpu.VMEM((D_MODEL, D_EXPERT), BF16),
            pltpu.VMEM((D_EXPERT, D_MODEL), BF16),
            pltpu.SemaphoreType.DMA((X_SLOTS,)),
            pltpu.SemaphoreType.DMA((2, 3)),
        ],
    )
    return pl.pallas_call(
        _moe_body,
        name="experts",
        grid_spec=grid_spec,
        out_shape=jax.ShapeDtypeStruct(((N_BLK_MOE + 1) * BM_MOE * ROW_TILE, LANES), U32),
        compiler_params=_params(1),
    )(plan, n_used, tok_sorted, xn_packed, w_g, w_u, w_d)


def _combine_body(pos, x1_ref, w_ref, gf_ref, o_hbm, y_ref, rb, sem, *, first_tile):
    m = pl.program_id(0)
    last = pl.num_programs(0) - 1

    def row_copy(base, slot, r, k):
        p8 = pl.multiple_of(pos[k * T + base + r], ROW_TILE)
        return pltpu.make_async_copy(o_hbm.at[pl.ds(p8, ROW_TILE)], rb.at[slot, k, _row_tile(r)],
                                     sem.at[slot])

    def tile_wait(slot):
        for k in range(TOP_K):
            pltpu.make_async_copy(o_hbm.at[pl.ds(0, TM_OUT * ROW_TILE)], rb.at[slot, k],
                                  sem.at[slot]).wait()

    def tile_base(mm):
        return (first_tile + mm) * TM_OUT

    @pl.when(m == 0)
    def _():
        for b in range(GATHER_AHEAD):
            base = tile_base(jnp.minimum(b, last))

            def body(r, carry, base=base, b=b):
                for k in range(TOP_K):
                    row_copy(base, b, r, k).start(priority=k)
                return carry
            lax.fori_loop(0, TM_OUT, body, 0, unroll=4)

    slot = m % X_SLOTS
    next_slot = (m + GATHER_AHEAD) % X_SLOTS
    tile_wait(slot)
    nbase = tile_base(jnp.minimum(m + GATHER_AHEAD, last))
    rows = TM_OUT // ROW_TILE
    w0 = w_ref[:, 0:1]
    w1 = w_ref[:, 1:2]
    ss = jnp.zeros((TM_OUT, 1), F32)
    for s in range(ROW_TILE):
        for r in range(s * rows, (s + 1) * rows):
            for k in range(TOP_K):
                row_copy(nbase, next_slot, r, k).start(priority=k)
        lo0, hi0 = _load_packed_rows(rb.at[slot, 0], TM_OUT, s)
        lo1, hi1 = _load_packed_rows(rb.at[slot, 1], TM_OUT, s)
        for off, r0, r1 in ((s * LANES, lo0, lo1), (HALF_D + s * LANES, hi0, hi1)):
            y = x1_ref[:, off:off + LANES] + w0 * r0 + w1 * r1
            ss = ss + jnp.sum(y * y, axis=-1, keepdims=True)
            y_ref[:, off:off + LANES] = y
    scale = lax.rsqrt(ss * (1.0 / D_MODEL) + EPS)
    y_ref[...] = y_ref[...] * scale * gf_ref[...]

    @pl.when(m == last)
    def _():
        for b in range(1, GATHER_AHEAD + 1):
            tile_wait((m + b) % X_SLOTS)


def _combine(pos, x1, wts, gf, out_sorted, first_tile, n_tiles):
    grid_spec = pltpu.PrefetchScalarGridSpec(
        num_scalar_prefetch=1,
        grid=(n_tiles,),
        in_specs=[
            pl.BlockSpec((TM_OUT, D_MODEL), lambda m, p: (first_tile + m, 0)),
            pl.BlockSpec((TM_OUT, ROUTER_PAD), lambda m, p: (first_tile + m, 0)),
            pl.BlockSpec((1, D_MODEL), lambda m, p: (0, 0)),
            pl.BlockSpec(memory_space=pl.ANY),
        ],
        out_specs=pl.BlockSpec((TM_OUT, D_MODEL), lambda m, p: (m, 0)),
        scratch_shapes=[
            pltpu.VMEM((X_SLOTS, TOP_K, TM_OUT * ROW_TILE, LANES), U32),
            pltpu.SemaphoreType.DMA((X_SLOTS,)),
        ],
    )
    return pl.pallas_call(
        functools.partial(_combine_body, first_tile=first_tile),
        name="combine",
        grid_spec=grid_spec,
        out_shape=jax.ShapeDtypeStruct((n_tiles * TM_OUT, D_MODEL), F32),
        compiler_params=_params(1),
    )(pos, x1, wts, gf, out_sorted)


def kernel(x_prompt, x_sample, cache_k, cache_v, state_conv, norm1_g, w_in, b_in, attn_sinks,
           w_attn_o, conv_dw, conv_dw_b, conv_ln_g, conv_ln_b, w_conv_o, w_out, norm2_g,
           w_router_group, b_router_group, w_router_expert, b_router_expert, w_e_gate, w_e_up,
           w_e_down, final_norm_g):
    x_p = x_prompt.reshape(T_P, D_MODEL)
    x_s = x_sample.reshape(T_S, D_MODEL)
    g1 = norm1_g[0][None, :]
    b1 = b_in[0][None, :]
    conv_consts = (conv_dw[0], conv_dw_b[0][None, :], conv_ln_g[0][None, :], conv_ln_b[0][None, :])
    hist_pad = jnp.pad(state_conv[0], ((0, 0), (HIST_PAD - HIST, 0), (0, 0)))
    q_p, kv_p, u_p, gates_p, c_p = _in_proj(x_p, None, g1, w_in[0], b1, conv_consts, TM_IN, True)
    q_s, kv_s, u_s, gates_s, c_s = _in_proj(x_s, hist_pad, g1, w_in[0], b1, conv_consts, T_S, False)

    sinks = attn_sinks[0]
    a_p = _attn_prompt(sinks, q_p, kv_p)
    ck = cache_k[0].reshape(N_STREAMS, WINDOW, KV_WIDTH)
    cv = cache_v[0].reshape(N_STREAMS, WINDOW, KV_WIDTH)
    a_s = _attn_sample(sinks, q_s, kv_s, ck, cv)

    w_r = jnp.concatenate([w_router_group[0], w_router_expert[0]], axis=1)
    w_r_hi = w_r.astype(BF16)
    w_r_lo = (w_r - w_r_hi.astype(F32)).astype(BF16)
    w_r_cat = jnp.concatenate(
        [w_r_hi, w_r_lo, jnp.zeros((D_MODEL, ROUTER_PAD - 2 * N_ROUTER), BF16)], axis=1)
    b_r = jnp.concatenate([b_router_group[0], b_router_expert[0],
                           jnp.zeros((ROUTER_PAD - N_ROUTER,), F32)])[None, :]

    x1, xn2, logits = _mix(a_p, a_s, c_p, c_s, gates_p, gates_s, x_p, x_s,
                           w_attn_o[0], w_conv_o[0], w_out[0],
                           norm2_g[0][None, :], w_r_cat, b_r)

    wts, pos, tok_sorted, plan, n_used = _route(logits)
    out_sorted = _moe(plan, n_used, tok_sorted, xn2, w_e_gate[0], w_e_up[0], w_e_down[0])
    gf = final_norm_g[None, :]
    y_p = _combine(pos, x1, wts, gf, out_sorted, 0, T_P // TM_OUT)
    y_s = _combine(pos, x1, wts, gf, out_sorted, T_P // TM_OUT, T_S // TM_OUT)

    kv_shape = (1, -1, WINDOW, N_KV_HEADS, HEAD_DIM)
    new_k_prompt = kv_p[T_P - WINDOW:, :KV_WIDTH].reshape(kv_shape)
    new_v_prompt = kv_p[T_P - WINDOW:, KV_WIDTH:].reshape(kv_shape)
    new_conv_prompt = u_p[T_P - HIST:].reshape(1, 1, HIST, CONV_CH)
    k_s = kv_s[:, :KV_WIDTH].reshape(N_STREAMS, L_S, KV_WIDTH)
    v_s = kv_s[:, KV_WIDTH:].reshape(N_STREAMS, L_S, KV_WIDTH)
    new_k_sample = jnp.concatenate([ck[:, L_S:], k_s], axis=1).reshape(kv_shape)
    new_v_sample = jnp.concatenate([cv[:, L_S:], v_s], axis=1).reshape(kv_shape)
    new_conv_sample = u_s.reshape(N_STREAMS, L_S, CONV_CH)[:, L_S - HIST:].reshape(
        1, N_STREAMS, HIST, CONV_CH)

    return (y_p.reshape(1, T_P, D_MODEL), y_s.reshape(N_STREAMS, L_S, D_MODEL),
            new_k_prompt, new_v_prompt, new_conv_prompt,
            new_k_sample, new_v_sample, new_conv_sample)
```

```python
import functools

import numpy as np
import jax
import jax.numpy as jnp
from jax import lax
from jax.experimental import pallas as pl
from jax.experimental.pallas import tpu as pltpu

F32 = jnp.float32
BF16 = jnp.bfloat16
U32 = jnp.uint32

D_MODEL = 2048
T_P = 8192
N_STREAMS = 8
L_S = 64
T_S = N_STREAMS * L_S
T = T_P + T_S
CHUNK = 64
WINDOW = 128
HEAD_DIM = 64
N_Q_HEADS = 16
N_KV_HEADS = 4
GROUP = N_Q_HEADS // N_KV_HEADS
ATTN_WIDTH = N_Q_HEADS * HEAD_DIM
KV_WIDTH = N_KV_HEADS * HEAD_DIM
CONV_CH = 1024
CONV_WIDTH = 31
HIST = CONV_WIDTH - 1
HIST_PAD = 32
SUBLANES = 8
LANES = 128
ROW_TILE = SUBLANES
HALF_D = D_MODEL // 2
HI_MASK = np.uint32(0xFFFF0000)
OFF_KV = ATTN_WIDTH
OFF_GLU = OFF_KV + 2 * KV_WIDTH
OFF_GATES = OFF_GLU + 2 * CONV_CH
IN_WIDTH = OFF_GATES + 2 * D_MODEL
N_GROUPS = 8
EXPERTS_PER_GROUP = 4
N_EXPERTS = N_GROUPS * EXPERTS_PER_GROUP
TOP_K = 2
D_EXPERT = 512
N_ROUTER = N_GROUPS + N_EXPERTS
ROUTER_PAD = 128
EPS = 1e-6
NEG = -1e30

VMEM_LIMIT = 56 * 1024 * 1024
MXU_N = 256

TM_WIDE = 2048
TM_GLU = T // 4
TN_IN = 512
TN_GATES = TN_IN
N_GATE_TILES = 2 * D_MODEL // TN_GATES
ROW_CHUNK = 256
W_STAGE_ROWS = 256
TQ_ATTN = 256
PAIR = 2 * CHUNK
BF16_ROWS = 16
SUB_CONV = 32
PIECE_CONV = 32
CONV_LAG = 1
TM_MIX = 256
BM_MOE = 256
R_ROUTE = 256
TM_OUT = 256
N_SLOTS = T * TOP_K
N_BLK_MOE = -(-(N_SLOTS + N_EXPERTS * (BM_MOE - 1)) // BM_MOE)


def _sigmoid(x):
    return 1.0 / (1.0 + jnp.exp(-x))


def _params(n_axes):
    return pltpu.CompilerParams(dimension_semantics=("arbitrary",) * n_axes,
                                vmem_limit_bytes=VMEM_LIMIT)


def _store_packed_rows(ref, y):
    rows = y.shape[0]
    for s in range(ROW_TILE):
        lo = y[:, s * LANES:(s + 1) * LANES].astype(BF16).astype(F32)
        hi = y[:, HALF_D + s * LANES:HALF_D + (s + 1) * LANES].astype(BF16).astype(F32)
        word = (lax.bitcast_convert_type(hi, U32) & HI_MASK) | (lax.bitcast_convert_type(lo, U32) >> 16)
        ref[pl.ds(s, rows, stride=ROW_TILE), :] = word


def _row_tile(r):
    start = r * ROW_TILE
    return pl.ds(start if isinstance(r, int) else pl.multiple_of(start, ROW_TILE), ROW_TILE)


def _load_packed_rows(ref, rows, s):
    word = ref[pl.ds(s, rows, stride=ROW_TILE), :]
    lo = lax.bitcast_convert_type(word << 16, F32)
    hi = lax.bitcast_convert_type(word & HI_MASK, F32)
    return lo, hi


def _chunk_dot(xn, w_ref, b_ref, c):
    w = w_ref[:, c * MXU_N:(c + 1) * MXU_N].astype(BF16)
    return jnp.dot(xn, w, preferred_element_type=F32) + b_ref[:, c * MXU_N:(c + 1) * MXU_N]


def _stage_weights(jobs, stage, sem):
    width = stage.shape[-1]
    chunks = [(w_hbm, w_bf, r0) for w_hbm, w_bf in jobs
              for r0 in range(0, w_bf.shape[0], W_STAGE_ROWS)]

    def chunk_copy(i):
        w_hbm, _, r0 = chunks[i]
        return pltpu.make_async_copy(w_hbm.at[pl.ds(r0, W_STAGE_ROWS), pl.ds(0, width)],
                                     stage.at[i % 2], sem.at[i % 2])
    chunk_copy(0).start()
    for i, (_, w_bf, r0) in enumerate(chunks):
        if i + 1 < len(chunks):
            chunk_copy(i + 1).start()
        chunk_copy(i).wait()
        w_bf[r0:r0 + W_STAGE_ROWS, :] = stage[i % 2].astype(BF16)


def _qkv_body(xp_ref, xs_ref, g_ref, w_hbm, b_ref, xn_ref, q_ref, kv_ref, w_bf, stage, sem, *, tm):
    m = pl.program_id(0)

    @pl.when(m == 0)
    def _():
        _stage_weights([(w_hbm, w_bf)], stage, sem)

    is_sample = m >= T_P // tm
    for r0 in range(0, tm, ROW_CHUNK):
        rows = slice(r0, r0 + ROW_CHUNK)
        x = jnp.where(is_sample, xs_ref[rows, :], xp_ref[rows, :])
        ms = jnp.mean(x * x, axis=-1, keepdims=True)
        xn = (x * lax.rsqrt(ms + EPS) * g_ref[...]).astype(BF16)
        xn_ref[rows, :] = xn
        for c in range(OFF_GLU // MXU_N):
            cols = slice(c * MXU_N, (c + 1) * MXU_N)
            acc = jnp.dot(xn, w_bf[:, cols], preferred_element_type=F32) + b_ref[:, cols]
            if c < ATTN_WIDTH // MXU_N:
                q_ref[rows, cols] = (acc * (HEAD_DIM ** -0.5)).astype(BF16)
            else:
                kv_ref[rows, c * MXU_N - ATTN_WIDTH:(c + 1) * MXU_N - ATTN_WIDTH] = acc


def _qkv(x_p, x_s, norm_g, w_in, b_in):
    tm = T_S
    t = T
    n_p = T_P // tm
    return pl.pallas_call(
        functools.partial(_qkv_body, tm=tm),
        name="qkv",
        grid=(t // tm,),
        in_specs=[
            pl.BlockSpec((tm, D_MODEL), lambda m: (jnp.minimum(m, n_p - 1), 0)),
            pl.BlockSpec((tm, D_MODEL), lambda m: (0, 0)),
            pl.BlockSpec((1, D_MODEL), lambda m: (0, 0)),
            pl.BlockSpec(memory_space=pl.ANY),
            pl.BlockSpec((1, OFF_GLU), lambda m: (0, 0)),
        ],
        out_specs=[
            pl.BlockSpec((tm, D_MODEL), lambda m: (m, 0)),
            pl.BlockSpec((tm, ATTN_WIDTH), lambda m: (m, 0)),
            pl.BlockSpec((tm, 2 * KV_WIDTH), lambda m: (m, 0)),
        ],
        out_shape=[
            jax.ShapeDtypeStruct((t, D_MODEL), BF16),
            jax.ShapeDtypeStruct((t, ATTN_WIDTH), BF16),
            jax.ShapeDtypeStruct((t, 2 * KV_WIDTH), F32),
        ],
        scratch_shapes=[
            pltpu.VMEM((D_MODEL, OFF_GLU), BF16),
            pltpu.VMEM((2, W_STAGE_ROWS, OFF_GLU), F32),
            pltpu.SemaphoreType.DMA((2,)),
        ],
        compiler_params=_params(1),
    )(x_p, x_s, norm_g, w_in, b_in)


def _glu_body(xn_ref, wa_ref, ba_ref, wb_ref, bb_ref, u_ref):
    xn = xn_ref[...]
    for c in range(TN_IN // MXU_N):
        a = _chunk_dot(xn, wa_ref, ba_ref, c)
        b = _chunk_dot(xn, wb_ref, bb_ref, c)
        u_ref[:, c * MXU_N:(c + 1) * MXU_N] = a * _sigmoid(b)


def _glu(xn, w_in, b_in, tm):
    t = xn.shape[0]
    a0 = OFF_GLU // TN_IN
    b0 = (OFF_GLU + CONV_CH) // TN_IN
    return pl.pallas_call(
        _glu_body,
        name="glu",
        grid=(t // tm, CONV_CH // TN_IN),
        in_specs=[
            pl.BlockSpec((tm, D_MODEL), lambda m, n: (m, 0)),
            pl.BlockSpec((D_MODEL, TN_IN), lambda m, n: (0, a0 + n)),
            pl.BlockSpec((1, TN_IN), lambda m, n: (0, a0 + n)),
            pl.BlockSpec((D_MODEL, TN_IN), lambda m, n: (0, b0 + n)),
            pl.BlockSpec((1, TN_IN), lambda m, n: (0, b0 + n)),
        ],
        out_specs=pl.BlockSpec((tm, TN_IN), lambda m, n: (m, n)),
        out_shape=jax.ShapeDtypeStruct((t, CONV_CH), F32),
        compiler_params=_params(2),
    )(xn, w_in, b_in, w_in, b_in)


def _attn_prompt_body(sink_ref, q_ref, kvp_ref, kvc_ref, a_ref, at_sc):
    i = pl.program_id(0)
    kv = jnp.concatenate([kvp_ref[...], kvc_ref[...]], axis=0)
    k = kv[:, :KV_WIDTH].astype(BF16)
    vt = kv[:, KV_WIDTH:].T.astype(BF16)
    qt = q_ref[...].astype(F32).T.astype(BF16)
    n_cols = GROUP * PAIR
    n_keys = WINDOW + PAIR
    row = lax.broadcasted_iota(jnp.int32, (n_keys, n_cols), 0)
    col = lax.broadcasted_iota(jnp.int32, (n_keys, n_cols), 1)
    first = jnp.where((col & (PAIR - 1)) >= CHUNK, CHUNK, 0)
    gcol = lax.shift_right_logical(lax.broadcasted_iota(jnp.int32, (1, n_cols), 1),
                                   PAIR.bit_length() - 1)
    biases = []
    for p in range(TQ_ATTN // PAIR):
        pos = row + (i * TQ_ATTN + p * PAIR - WINDOW)
        ok = (row >= first) & (row < first + WINDOW + CHUNK) & (pos >= 0)
        biases.append(jnp.where(ok, 0.0, NEG))

    def scores(p, h):
        w0 = p * PAIR
        kh = k[w0:w0 + n_keys, h * HEAD_DIM:(h + 1) * HEAD_DIM]
        rhs = jnp.concatenate(
            [qt[(h * GROUP + g) * HEAD_DIM:(h * GROUP + g + 1) * HEAD_DIM, w0:w0 + PAIR]
             for g in range(GROUP)], axis=1)
        return jnp.dot(kh, rhs, preferred_element_type=F32) + biases[p]

    def finish(p, h, st):
        w0 = p * PAIR
        sink = jnp.full((1, n_cols), sink_ref[h * GROUP], F32)
        for g in range(1, GROUP):
            sink = jnp.where(gcol == g, sink_ref[h * GROUP + g], sink)
        m = jnp.maximum(jnp.max(st, axis=0, keepdims=True), sink)
        pt = jnp.exp(st - m)
        denom = jnp.sum(pt, axis=0, keepdims=True) + jnp.exp(sink - m)
        ot = jnp.dot(vt[h * HEAD_DIM:(h + 1) * HEAD_DIM, w0:w0 + n_keys], pt.astype(BF16),
                     preferred_element_type=F32) / denom
        for g in range(GROUP):
            r0 = (h * GROUP + g) * HEAD_DIM
            at_sc[r0:r0 + HEAD_DIM, w0:w0 + PAIR] = ot[:, g * PAIR:(g + 1) * PAIR]

    items = [(p, h) for p in range(TQ_ATTN // PAIR) for h in range(N_KV_HEADS)]
    st_next = scores(*items[0])
    for n, (p, h) in enumerate(items):
        st = st_next
        if n + 1 < len(items):
            st_next = scores(*items[n + 1])
        finish(p, h, st)
    a_ref[...] = at_sc[...].T.astype(BF16)


def _attn_prompt(sinks, q, kv):
    r = TQ_ATTN // WINDOW
    return pl.pallas_call(
        _attn_prompt_body,
        name="attn_prompt",
        grid=(T_P // TQ_ATTN,),
        in_specs=[
            pl.BlockSpec(memory_space=pltpu.SMEM),
            pl.BlockSpec((TQ_ATTN, ATTN_WIDTH), lambda i: (i, 0)),
            pl.BlockSpec((WINDOW, 2 * KV_WIDTH), lambda i: (jnp.maximum(i * r - 1, 0), 0)),
            pl.BlockSpec((TQ_ATTN, 2 * KV_WIDTH), lambda i: (i, 0)),
        ],
        out_specs=pl.BlockSpec((TQ_ATTN, ATTN_WIDTH), lambda i: (i, 0)),
        out_shape=jax.ShapeDtypeStruct((T_P, ATTN_WIDTH), BF16),
        scratch_shapes=[pltpu.VMEM((ATTN_WIDTH, TQ_ATTN), F32)],
        compiler_params=_params(1),
    )(sinks, q, kv, kv)


def _attn_sample_body(sink_ref, q_ref, kvc_ref, ck_ref, cv_ref, a_ref):
    kvc = kvc_ref[...]
    kw = jnp.concatenate([ck_ref[0], kvc[:, :KV_WIDTH]], axis=0).astype(BF16)
    vw = jnp.concatenate([cv_ref[0], kvc[:, KV_WIDTH:]], axis=0).astype(BF16)
    qc = q_ref[...]
    for h in range(N_KV_HEADS):
        kh = kw[:, h * HEAD_DIM:(h + 1) * HEAD_DIM]
        vh = vw[:, h * HEAD_DIM:(h + 1) * HEAD_DIM]
        qg = jnp.concatenate(
            [qc[:, (h * GROUP + g) * HEAD_DIM:(h * GROUP + g + 1) * HEAD_DIM] for g in range(GROUP)],
            axis=0)
        s = lax.dot_general(qg, kh, (((1,), (1,)), ((), ())), preferred_element_type=F32)
        sink = jnp.concatenate(
            [jnp.full((L_S, 1), sink_ref[h * GROUP + g], F32) for g in range(GROUP)], axis=0)
        m = jnp.maximum(jnp.max(s, axis=1, keepdims=True), sink)
        p = jnp.exp(s - m)
        denom = jnp.sum(p, axis=1, keepdims=True) + jnp.exp(sink - m)
        o = jnp.dot(p.astype(BF16), vh, preferred_element_type=F32) / denom
        for g in range(GROUP):
            c0 = (h * GROUP + g) * HEAD_DIM
            a_ref[:, c0:c0 + HEAD_DIM] = o[g * L_S:(g + 1) * L_S].astype(BF16)


def _attn_sample(sinks, q, kv, cache_k, cache_v):
    first = T_P // L_S
    return pl.pallas_call(
        _attn_sample_body,
        name="attn_sample",
        grid=(N_STREAMS,),
        in_specs=[
            pl.BlockSpec(memory_space=pltpu.SMEM),
            pl.BlockSpec((L_S, ATTN_WIDTH), lambda b: (first + b, 0)),
            pl.BlockSpec((L_S, 2 * KV_WIDTH), lambda b: (first + b, 0)),
            pl.BlockSpec((1, WINDOW, KV_WIDTH), lambda b: (b, 0, 0)),
            pl.BlockSpec((1, WINDOW, KV_WIDTH), lambda b: (b, 0, 0)),
        ],
        out_specs=pl.BlockSpec((L_S, ATTN_WIDTH), lambda b: (b, 0)),
        out_shape=jax.ShapeDtypeStruct((T_S, ATTN_WIDTH), BF16),
        compiler_params=_params(1),
    )(sinks, q, kv, cache_k, cache_v)


def _conv_shifts(prev, cur_ref, ext_sc, sh_sc, rows):
    ext_sc[0:HIST_PAD, :] = prev
    ext_sc[HIST_PAD:HIST_PAD + rows, :] = cur_ref[...]
    ext_sc[HIST_PAD + rows:, :] = jnp.zeros((SUBLANES, CONV_CH), F32)
    lead = HIST_PAD - HIST
    span = rows + HIST_PAD - SUBLANES
    for s in range(SUBLANES):
        sh_sc[s, 0:span, :] = ext_sc[lead + s:lead + s + span, :]


def _conv_rows(r0, w_ref, b_ref, lg_ref, lb_ref, o_ref, sh_sc):
    acc = jnp.zeros((SUB_CONV, CONV_CH), F32) + b_ref[...]
    for k in range(CONV_WIDTH):
        a, s = divmod(k, SUBLANES)
        r = r0 + a * SUBLANES
        acc = acc + w_ref[k:k + 1, :] * sh_sc[s, r:r + SUB_CONV, :]
    mu = jnp.mean(acc, axis=-1, keepdims=True)
    d = acc - mu
    var = jnp.mean(d * d, axis=-1, keepdims=True)
    yn = d * lax.rsqrt(var + EPS) * lg_ref[...] + lb_ref[...]
    out = yn * _sigmoid(yn)
    o_ref[r0:r0 + SUB_CONV, :] = out.astype(BF16)
    return out


def _gates_conv_body(xn_ref, w_ref, b_ref, prev_ref, cur_ref, cw_ref, cb_ref, lg_ref, lb_ref, zero_ref,
                     gate_ref, c_ref, ext_sc, sh_sc, *, rows, fresh):
    if fresh:
        step = pl.program_id(0) * N_GATE_TILES + pl.program_id(1)
        prev = jnp.where(step > 0, prev_ref[...], 0.0)
    else:
        prev = prev_ref[0]
    _conv_shifts(prev, cur_ref, ext_sc, sh_sc, rows)
    n_chunks = TN_GATES // MXU_N
    w = [w_ref[:, c * MXU_N:(c + 1) * MXU_N].astype(BF16) for c in range(n_chunks)]
    zeros = []
    for i in range(rows // PIECE_CONV):
        r = slice(i * ROW_CHUNK, (i + 1) * ROW_CHUNK)
        xr = xn_ref[r, :]
        if i >= CONV_LAG:
            zero16 = zeros[i - CONV_LAG]
            top = jnp.concatenate([xr[0:BF16_ROWS, 0:LANES] + zero16, xr[0:BF16_ROWS, LANES:]], axis=1)
            xr = jnp.concatenate([top, xr[BF16_ROWS:, :]], axis=0)
        accs = [jnp.dot(xr, w[c], preferred_element_type=F32) for c in range(n_chunks)]
        outs = [_conv_rows(r0, cw_ref, cb_ref, lg_ref, lb_ref, c_ref, sh_sc)
                for r0 in range(i * PIECE_CONV, (i + 1) * PIECE_CONV, SUB_CONV)]
        bits = [lax.bitcast_convert_type(o[g * SUBLANES:(g + 1) * SUBLANES, 0:LANES], jnp.int32)
                for o in outs for g in range(SUB_CONV // SUBLANES)]
        zero8 = lax.bitcast_convert_type(functools.reduce(jnp.bitwise_or, bits) & zero_ref[...], F32)
        zeros.append(jnp.concatenate([zero8, zero8], axis=0).astype(BF16))
        for c in range(n_chunks):
            cols = slice(c * MXU_N, (c + 1) * MXU_N)
            gate_ref[r, cols] = _sigmoid(accs[c] + b_ref[:, cols]).astype(BF16)


def _gates_conv(xn, w_in, b_in, hist, u, conv_consts, tm, rows, fresh):
    t = T_P if fresh else T_S
    xn_tile0 = 0 if fresh else T_P // tm
    u_tile0 = 0 if fresh else T_P // rows
    n0 = OFF_GATES // TN_GATES
    tile = lambda m, n: m * N_GATE_TILES + n
    if fresh:
        r = rows // HIST_PAD
        prev_spec = pl.BlockSpec((HIST_PAD, CONV_CH), lambda m, n: (jnp.maximum(tile(m, n) * r - 1, 0), 0))
    else:
        prev_spec = pl.BlockSpec((1, HIST_PAD, CONV_CH), lambda m, n: (tile(m, n), 0, 0))
    const_map = lambda m, n: (0, 0)
    return pl.pallas_call(
        functools.partial(_gates_conv_body, rows=rows, fresh=fresh),
        name="gates_conv",
        grid=(t // tm, N_GATE_TILES),
        in_specs=[
            pl.BlockSpec((tm, D_MODEL), lambda m, n: (xn_tile0 + m, 0)),
            pl.BlockSpec((D_MODEL, TN_GATES), lambda m, n: (0, n0 + n)),
            pl.BlockSpec((1, TN_GATES), lambda m, n: (0, n0 + n)),
            prev_spec,
            pl.BlockSpec((rows, CONV_CH), lambda m, n: (u_tile0 + tile(m, n), 0)),
            pl.BlockSpec((CONV_WIDTH, CONV_CH), const_map),
            pl.BlockSpec((1, CONV_CH), const_map),
            pl.BlockSpec((1, CONV_CH), const_map),
            pl.BlockSpec((1, CONV_CH), const_map),
            pl.BlockSpec((SUBLANES, LANES), const_map),
        ],
        out_specs=[
            pl.BlockSpec((tm, TN_GATES), lambda m, n: (m, n)),
            pl.BlockSpec((rows, CONV_CH), lambda m, n: (tile(m, n), 0)),
        ],
        out_shape=[
            jax.ShapeDtypeStruct((t, 2 * D_MODEL), BF16),
            jax.ShapeDtypeStruct((t, CONV_CH), BF16),
        ],
        scratch_shapes=[
            pltpu.VMEM((HIST_PAD + rows + SUBLANES, CONV_CH), F32),
            pltpu.VMEM((SUBLANES, HIST_PAD + rows, CONV_CH), F32),
        ],
        compiler_params=_params(2),
    )(xn, w_in, b_in, hist, u, *conv_consts, jnp.zeros((SUBLANES, LANES), jnp.int32))


def _in_proj(x_p, x_s, hist, norm_g, w_in, b_in, conv_consts):
    xn, q, kv = _qkv(x_p, x_s, norm_g, w_in, b_in)
    u = _glu(xn, w_in, b_in, TM_GLU)
    gates_p, c_p = _gates_conv(xn, w_in, b_in, u, u, conv_consts, TM_WIDE,
                               TM_WIDE // N_GATE_TILES, True)
    gates_s, c_s = _gates_conv(xn, w_in, b_in, hist, u, conv_consts, T_S,
                               T_S // N_GATE_TILES, False)
    return q, kv, u, gates_p, c_p, gates_s, c_s


def _mix_body(ap_ref, as_ref, cp_ref, cs_ref, gp_ref, gs_ref, xp_ref, xs_ref, wao_hbm, wco_hbm,
              wout_hbm, g2_ref, wr_ref, br_ref, x1_ref, xn_ref, lg_ref,
              wao_ref, wco_ref, wout_ref, stage, sem):
    m = pl.program_id(0)

    @pl.when(m == 0)
    def _():
        _stage_weights([(wao_hbm, wao_ref), (wco_hbm, wco_ref), (wout_hbm, wout_ref)], stage, sem)

    is_sample = m >= T_P // TM_MIX
    a = jnp.where(is_sample, as_ref[...], ap_ref[...])
    c = jnp.where(is_sample, cs_ref[...], cp_ref[...])
    gate = jnp.where(is_sample, gs_ref[...], gp_ref[...])
    x = jnp.where(is_sample, xs_ref[...], xp_ref[...])
    pa = jnp.dot(a, wao_ref[...], preferred_element_type=F32)
    pc = jnp.dot(c, wco_ref[...], preferred_element_type=F32)
    y = (gate[:, :D_MODEL].astype(F32) * pa + gate[:, D_MODEL:].astype(F32) * pc).astype(BF16)
    x1 = x + jnp.dot(y, wout_ref[...], preferred_element_type=F32)
    x1_ref[...] = x1
    ms = jnp.mean(x1 * x1, axis=-1, keepdims=True)
    xn = x1 * lax.rsqrt(ms + EPS) * g2_ref[...]
    _store_packed_rows(xn_ref, xn)
    r = jnp.dot(xn.astype(BF16), wr_ref[...], preferred_element_type=F32)
    lg_ref[...] = r + pltpu.roll(r, ROUTER_PAD - N_ROUTER, axis=1) + br_ref[...]


def _mix(a_p, a_s, c_p, c_s, g_p, g_s, x_p, x_s, wao, wco, wout, g2, wr, br):
    n_p = T_P // TM_MIX
    n_s = T_S // TM_MIX
    prompt_map = lambda m: (jnp.minimum(m, n_p - 1), 0)
    sample_map = lambda m: (jnp.clip(m - n_p, 0, n_s - 1), 0)
    row_map = lambda m: (m, 0)
    const_map = lambda m: (0, 0)

    def pair(width):
        return [pl.BlockSpec((TM_MIX, width), prompt_map), pl.BlockSpec((TM_MIX, width), sample_map)]

    return pl.pallas_call(
        _mix_body,
        name="mix",
        grid=(T // TM_MIX,),
        in_specs=pair(ATTN_WIDTH) + pair(CONV_CH) + pair(2 * D_MODEL) + pair(D_MODEL) + [
            pl.BlockSpec(memory_space=pl.ANY),
            pl.BlockSpec(memory_space=pl.ANY),
            pl.BlockSpec(memory_space=pl.ANY),
            pl.BlockSpec((1, D_MODEL), const_map),
            pl.BlockSpec((D_MODEL, ROUTER_PAD), const_map),
            pl.BlockSpec((1, ROUTER_PAD), const_map),
        ],
        out_specs=[
            pl.BlockSpec((TM_MIX, D_MODEL), row_map),
            pl.BlockSpec((TM_MIX * ROW_TILE, LANES), row_map),
            pl.BlockSpec((TM_MIX, ROUTER_PAD), row_map),
        ],
        out_shape=[
            jax.ShapeDtypeStruct((T, D_MODEL), F32),
            jax.ShapeDtypeStruct((T * ROW_TILE, LANES), U32),
            jax.ShapeDtypeStruct((T, ROUTER_PAD), F32),
        ],
        scratch_shapes=[
            pltpu.VMEM((ATTN_WIDTH, D_MODEL), BF16),
            pltpu.VMEM((CONV_CH, D_MODEL), BF16),
            pltpu.VMEM((D_MODEL, D_MODEL), BF16),
            pltpu.VMEM((2, W_STAGE_ROWS, D_MODEL), F32),
            pltpu.SemaphoreType.DMA((2,)),
        ],
        compiler_params=_params(1),
    )(a_p, a_s, c_p, c_s, g_p, g_s, x_p, x_s, wao, wco, wout, g2, wr, br)


REC_W, REC_E, REC_POS = 0, 2, 4


PLAN_E, PLAN_SRC, PLAN_FIRST, PLAN_WSLOT, PLAN_NEXT_E, PLAN_ZERO = range(6)


def _write_plan(cnt, plan_ref, nu_ref):
    shift = BM_MOE.bit_length() - 1
    nblk = [(c + (BM_MOE - 1)) >> shift for c in cnt]
    nxt = [None] * N_EXPERTS
    later = jnp.int32(-1)
    for e in reversed(range(N_EXPERTS)):
        nxt[e] = jnp.where(later >= 0, later, e)
        later = jnp.where(nblk[e] > 0, e, later)

    def clear(j, carry):
        for row in range(PLAN_ZERO + 1):
            plan_ref[row, j] = 0
        return carry
    lax.fori_loop(0, N_BLK_MOE, clear, 0)

    first_blocks = []
    j0 = jnp.int32(0)
    row0 = jnp.int32(0)
    n_before = jnp.int32(0)
    for e in range(N_EXPERTS):
        def fill(b, carry, e=e, j0=j0, row0=row0, n_before=n_before):
            plan_ref[PLAN_E, j0 + b] = e
            plan_ref[PLAN_SRC, j0 + b] = row0 + b * BM_MOE
            plan_ref[PLAN_FIRST, j0 + b] = jnp.where(b == 0, 1, 0)
            plan_ref[PLAN_WSLOT, j0 + b] = n_before & 1
            plan_ref[PLAN_NEXT_E, j0 + b] = nxt[e]
            return carry
        lax.fori_loop(0, nblk[e], fill, 0)
        first_blocks.append(j0)
        j0 = j0 + nblk[e]
        row0 = row0 + cnt[e]
        n_before = n_before + jnp.where(nblk[e] > 0, 1, 0)
    nu_ref[0] = j0
    return first_blocks


def _router_body(lg_ref, rec_ref, idx_ref, plan_ref, nu_ref, carry_sc, rect_sc):
    n_chunks = T // R_ROUTE
    grow = lax.broadcasted_iota(jnp.int32, (N_GROUPS, R_ROUTE), 0).astype(F32)
    erow_i = lax.broadcasted_iota(jnp.int32, (N_EXPERTS, R_ROUTE), 0)
    erow = erow_i.astype(F32)
    egrp = (erow_i >> (EXPERTS_PER_GROUP.bit_length() - 1)).astype(F32)
    row8 = lax.broadcasted_iota(jnp.int32, (SUBLANES, R_ROUTE), 0)
    row128 = lax.broadcasted_iota(jnp.int32, (ROUTER_PAD, R_ROUTE), 0)
    upper = (lax.broadcasted_iota(jnp.int32, (R_ROUTE, R_ROUTE), 0)
             < lax.broadcasted_iota(jnp.int32, (R_ROUTE, R_ROUTE), 1)).astype(BF16)

    def rows_of(base, vals):
        out = jnp.zeros(base.shape, F32)
        for r, v in vals:
            out = jnp.where(base == r, v, out)
        return out

    carry_sc[...] = jnp.zeros_like(carry_sc)

    def count_pass(i, carry_unused):
        rows = pl.ds(pl.multiple_of(i * R_ROUTE, R_ROUTE), R_ROUTE)
        lgt = lg_ref[rows, :].T
        g = lgt[0:N_GROUPS]
        ex = lgt[N_GROUPS:N_ROUTER]
        gmax = jnp.max(g, axis=0, keepdims=True)
        p_grp = 1.0 / jnp.sum(jnp.exp(g - gmax), axis=0, keepdims=True)
        grp = jnp.min(jnp.where(g == gmax, grow, float(N_GROUPS)), axis=0, keepdims=True)
        emask = egrp == grp
        e1 = jnp.max(jnp.where(emask, ex, NEG), axis=0, keepdims=True)
        x1 = jnp.min(jnp.where(emask & (ex == e1), erow, float(N_EXPERTS)), axis=0, keepdims=True)
        emask2 = emask & (erow != x1)
        e2 = jnp.max(jnp.where(emask2, ex, NEG), axis=0, keepdims=True)
        x2 = jnp.min(jnp.where(emask2 & (ex == e2), erow, float(N_EXPERTS)), axis=0, keepdims=True)
        t = jnp.exp(e2 - e1)
        w1 = p_grp / (1.0 + t)
        w2 = p_grp * t / (1.0 + t)
        oh1 = erow == x1
        oh2 = erow == x2
        onehot = jnp.concatenate([oh1.astype(BF16), oh2.astype(BF16)], axis=0)
        carry = carry_sc[:, 0:1]
        before = jnp.dot(onehot, upper, preferred_element_type=F32) + carry
        rank1 = jnp.sum(jnp.where(oh1, before[0:N_EXPERTS], 0.0), axis=0, keepdims=True)
        rank2 = jnp.sum(jnp.where(oh2, before[N_EXPERTS:], 0.0), axis=0, keepdims=True)
        total = carry + jnp.sum(onehot.astype(F32), axis=1, keepdims=True)
        carry_sc[...] = jnp.broadcast_to(total, carry_sc.shape)
        rect_sc[i] = rows_of(row8, ((REC_E, x1), (REC_E + 1, x2), (REC_POS, rank1), (REC_POS + 1, rank2)))
        rec_ref[rows, :] = rows_of(row128, ((REC_W, w1), (REC_W + 1, w2))).T
        return carry_unused

    lax.fori_loop(0, n_chunks, count_pass, 0)

    tot = carry_sc[:, 0:1]
    tot1 = [tot[e, 0] for e in range(N_EXPERTS)]
    cnt = [(tot1[e] + tot[N_EXPERTS + e, 0]).astype(jnp.int32) for e in range(N_EXPERTS)]
    first_blocks = _write_plan(cnt, plan_ref, nu_ref)
    row0 = [(fb * BM_MOE).astype(F32) for fb in first_blocks]

    def place_pass(i, carry_unused):
        rec = rect_sc[i]
        x1 = rec[REC_E:REC_E + 1]
        x2 = rec[REC_E + 1:REC_E + 2]
        p1 = rec[REC_POS:REC_POS + 1]
        p2 = rec[REC_POS + 1:REC_POS + 2]
        for e in range(N_EXPERTS):
            p1 = p1 + jnp.where(x1 == e, row0[e], 0.0)
            p2 = p2 + jnp.where(x2 == e, row0[e] + tot1[e], 0.0)
        rec = jnp.where(row8 == REC_POS, p1 * ROW_TILE, rec)
        rec = jnp.where(row8 == REC_POS + 1, p2 * ROW_TILE, rec)
        idx_ref[i] = rec.astype(jnp.int32)
        return carry_unused

    lax.fori_loop(0, n_chunks, place_pass, 0)


def _router(logits):
    n_chunks = T // R_ROUTE
    return pl.pallas_call(
        _router_body,
        name="router",
        grid=(1,),
        in_specs=[pl.BlockSpec((T, ROUTER_PAD), lambda i: (0, 0))],
        out_specs=[
            pl.BlockSpec((T, ROUTER_PAD), lambda i: (0, 0)),
            pl.BlockSpec((n_chunks, SUBLANES, R_ROUTE), lambda i: (0, 0, 0)),
            pl.BlockSpec(memory_space=pltpu.SMEM),
            pl.BlockSpec(memory_space=pltpu.SMEM),
        ],
        out_shape=[
            jax.ShapeDtypeStruct((T, ROUTER_PAD), F32),
            jax.ShapeDtypeStruct((n_chunks, SUBLANES, R_ROUTE), jnp.int32),
            jax.ShapeDtypeStruct((PLAN_ZERO + 1, N_BLK_MOE), jnp.int32),
            jax.ShapeDtypeStruct((1,), jnp.int32),
        ],
        scratch_shapes=[
            pltpu.VMEM((2 * N_EXPERTS, LANES), F32),
            pltpu.VMEM((n_chunks, SUBLANES, R_ROUTE), F32),
        ],
        compiler_params=_params(1),
    )(logits)


def _route(logits):
    rec, idx, plan, n_used = _router(logits)
    idx = idx.transpose(1, 0, 2).reshape(SUBLANES, T)
    eid = idx[REC_E:REC_E + TOP_K].reshape(-1)
    pos = idx[REC_POS:REC_POS + TOP_K].reshape(-1)
    order = jnp.argsort(eid, stable=True)
    tok_sorted = jnp.concatenate([((order % T) * ROW_TILE).astype(jnp.int32),
                                  jnp.zeros((BM_MOE,), jnp.int32)])
    return rec, pos, tok_sorted, plan, n_used


GATHER_AHEAD = 2
X_SLOTS = GATHER_AHEAD + 1
WEIGHT_DMA_PRIORITY = 1
GATHER_DMA_PRIORITY = 0


def _moe_body(plan, n_used, tok, x_hbm, wg_hbm, wu_hbm, wd_hbm, o_ref,
              xb, wg_st, wu_st, wd_st, wg_bf, wu_bf, wd_bf, sem_x, sem_w):
    j = pl.program_id(0)
    nu = n_used[0]

    def row_copy(src, slot, r):
        t8 = pl.multiple_of(tok[src + r], ROW_TILE)
        return pltpu.make_async_copy(x_hbm.at[pl.ds(t8, ROW_TILE)], xb.at[slot, _row_tile(r)],
                                     sem_x.at[slot])

    def block_wait(slot):
        pltpu.make_async_copy(x_hbm.at[pl.ds(0, BM_MOE * ROW_TILE)], xb.at[slot],
                              sem_x.at[slot]).wait()

    def weight_copies(e, ws):
        return (pltpu.make_async_copy(wg_hbm.at[e], wg_st.at[ws], sem_w.at[ws, 0]),
                pltpu.make_async_copy(wu_hbm.at[e], wu_st.at[ws], sem_w.at[ws, 1]),
                pltpu.make_async_copy(wd_hbm.at[e], wd_st.at[ws], sem_w.at[ws, 2]))

    @pl.when(j == 0)
    def _():
        for cp in weight_copies(plan[PLAN_E, 0], 0):
            cp.start(priority=WEIGHT_DMA_PRIORITY)
        for b in range(GATHER_AHEAD):
            src = plan[PLAN_SRC, b]

            def body(r, carry, src=src, b=b):
                row_copy(src, b, r).start(priority=GATHER_DMA_PRIORITY)
                return carry
            lax.fori_loop(0, BM_MOE, body, 0, unroll=8)

    @pl.when((j < nu) & (plan[PLAN_FIRST, jnp.minimum(j, N_BLK_MOE - 1)] == 1))
    def _():
        e = plan[PLAN_E, j]
        ws = plan[PLAN_WSLOT, j]
        e_next = plan[PLAN_NEXT_E, j]

        @pl.when(e_next != e)
        def _():
            for cp in weight_copies(e_next, 1 - ws):
                cp.start(priority=WEIGHT_DMA_PRIORITY)

        for cp in weight_copies(e, ws):
            cp.wait()
        wg_bf[...] = wg_st[ws].astype(BF16)
        wu_bf[...] = wu_st[ws].astype(BF16)
        wd_bf[...] = wd_st[ws].astype(BF16)

    @pl.when(j < nu)
    def _():
        slot = j % X_SLOTS
        next_slot = (j + GATHER_AHEAD) % X_SLOTS
        block_wait(slot)
        nsrc = plan[PLAN_SRC, jnp.minimum(j + GATHER_AHEAD, N_BLK_MOE - 1)]
        halves = [_load_packed_rows(xb.at[slot], BM_MOE, s) for s in range(ROW_TILE)]
        x = jnp.concatenate([lo for lo, _ in halves] + [hi for _, hi in halves], axis=1).astype(BF16)
        zero = plan[PLAN_ZERO, 0]
        group = BM_MOE // 8
        issued = [0]

        def gather_after(v):
            base = nsrc
            if v is not None:
                base = base + (lax.bitcast_convert_type(v[0:1, 0:1], jnp.int32)[0, 0] & zero)
            for r in range(issued[0], issued[0] + group):
                row_copy(base, next_slot, r).start(priority=GATHER_DMA_PRIORITY)
            issued[0] += group

        gather_after(None)
        gather_after(halves[ROW_TILE - 1][1])
        gu = []
        for c in range(D_EXPERT // MXU_N):
            cols = slice(c * MXU_N, (c + 1) * MXU_N)
            g = jnp.dot(x, wg_bf[:, cols], preferred_element_type=F32)
            gather_after(g)
            u = jnp.dot(x, wu_bf[:, cols], preferred_element_type=F32)
            gather_after(u)
            gu.append((g, u))
        acc = None
        for c, (g, u) in enumerate(gu):
            cols = slice(c * MXU_N, (c + 1) * MXU_N)
            h = (g * _sigmoid(g) * u).astype(BF16)
            part = jnp.dot(h, wd_bf[cols, :], preferred_element_type=F32)
            if c == 0:
                gather_after(part)
                gather_after(part[:, D_MODEL - LANES:])
            acc = part if acc is None else acc + part
        assert issued[0] == BM_MOE
        _store_packed_rows(o_ref, acc)

    @pl.when(j == nu)
    def _():
        for b in range(GATHER_AHEAD):
            block_wait((j + b) % X_SLOTS)

    @pl.when(j >= nu)
    def _():
        o_ref[...] = jnp.zeros_like(o_ref)


def _moe(plan, n_used, tok_sorted, xn_packed, w_g, w_u, w_d):
    grid_spec = pltpu.PrefetchScalarGridSpec(
        num_scalar_prefetch=3,
        grid=(N_BLK_MOE + 1,),
        in_specs=[pl.BlockSpec(memory_space=pl.ANY)] * 4,
        out_specs=pl.BlockSpec((BM_MOE * ROW_TILE, LANES), lambda j, pn, nu, tk: (j, 0)),
        scratch_shapes=[
            pltpu.VMEM((X_SLOTS, BM_MOE * ROW_TILE, LANES), U32),
            pltpu.VMEM((2, D_MODEL, D_EXPERT), F32),
            pltpu.VMEM((2, D_MODEL, D_EXPERT), F32),
            pltpu.VMEM((2, D_EXPERT, D_MODEL), F32),
            pltpu.VMEM((D_MODEL, D_EXPERT), BF16),
            pltpu.VMEM((D_MODEL, D_EXPERT), BF16),
            pltpu.VMEM((D_EXPERT, D_MODEL), BF16),
            pltpu.SemaphoreType.DMA((X_SLOTS,)),
            pltpu.SemaphoreType.DMA((2, 3)),
        ],
    )
    return pl.pallas_call(
        _moe_body,
        name="experts",
        grid_spec=grid_spec,
        out_shape=jax.ShapeDtypeStruct(((N_BLK_MOE + 1) * BM_MOE * ROW_TILE, LANES), U32),
        compiler_params=_params(1),
    )(plan, n_used, tok_sorted, xn_packed, w_g, w_u, w_d)


def _combine_body(pos, x1_ref, w_ref, gf_ref, o_hbm, y_ref, rb, sem, *, first_tile):
    m = pl.program_id(0)
    last = pl.num_programs(0) - 1

    def row_copy(base, slot, r, k):
        p8 = pl.multiple_of(pos[k * T + base + r], ROW_TILE)
        return pltpu.make_async_copy(o_hbm.at[pl.ds(p8, ROW_TILE)], rb.at[slot, k, _row_tile(r)],
                                     sem.at[slot])

    def tile_wait(slot):
        for k in range(TOP_K):
            pltpu.make_async_copy(o_hbm.at[pl.ds(0, TM_OUT * ROW_TILE)], rb.at[slot, k],
                                  sem.at[slot]).wait()

    def tile_base(mm):
        return (first_tile + mm) * TM_OUT

    @pl.when(m == 0)
    def _():
        for b in range(GATHER_AHEAD):
            base = tile_base(jnp.minimum(b, last))

            def body(r, carry, base=base, b=b):
                for k in range(TOP_K):
                    row_copy(base, b, r, k).start(priority=k)
                return carry
            lax.fori_loop(0, TM_OUT, body, 0, unroll=4)

    slot = m % X_SLOTS
    next_slot = (m + GATHER_AHEAD) % X_SLOTS
    tile_wait(slot)
    nbase = tile_base(jnp.minimum(m + GATHER_AHEAD, last))
    rows = TM_OUT // ROW_TILE
    w0 = w_ref[:, 0:1]
    w1 = w_ref[:, 1:2]
    ss = jnp.zeros((TM_OUT, 1), F32)
    for s in range(ROW_TILE):
        for r in range(s * rows, (s + 1) * rows):
            for k in range(TOP_K):
                row_copy(nbase, next_slot, r, k).start(priority=k)
        lo0, hi0 = _load_packed_rows(rb.at[slot, 0], TM_OUT, s)
        lo1, hi1 = _load_packed_rows(rb.at[slot, 1], TM_OUT, s)
        for off, r0, r1 in ((s * LANES, lo0, lo1), (HALF_D + s * LANES, hi0, hi1)):
            y = x1_ref[:, off:off + LANES] + w0 * r0 + w1 * r1
            ss = ss + jnp.sum(y * y, axis=-1, keepdims=True)
            y_ref[:, off:off + LANES] = y
    scale = lax.rsqrt(ss * (1.0 / D_MODEL) + EPS)
    y_ref[...] = y_ref[...] * scale * gf_ref[...]

    @pl.when(m == last)
    def _():
        for b in range(1, GATHER_AHEAD + 1):
            tile_wait((m + b) % X_SLOTS)


def _combine(pos, x1, wts, gf, out_sorted, first_tile, n_tiles):
    grid_spec = pltpu.PrefetchScalarGridSpec(
        num_scalar_prefetch=1,
        grid=(n_tiles,),
        in_specs=[
            pl.BlockSpec((TM_OUT, D_MODEL), lambda m, p: (first_tile + m, 0)),
            pl.BlockSpec((TM_OUT, ROUTER_PAD), lambda m, p: (first_tile + m, 0)),
            pl.BlockSpec((1, D_MODEL), lambda m, p: (0, 0)),
            pl.BlockSpec(memory_space=pl.ANY),
        ],
        out_specs=pl.BlockSpec((TM_OUT, D_MODEL), lambda m, p: (m, 0)),
        scratch_shapes=[
            pltpu.VMEM((X_SLOTS, TOP_K, TM_OUT * ROW_TILE, LANES), U32),
            pltpu.SemaphoreType.DMA((X_SLOTS,)),
        ],
    )
    return pl.pallas_call(
        functools.partial(_combine_body, first_tile=first_tile),
        name="combine",
        grid_spec=grid_spec,
        out_shape=jax.ShapeDtypeStruct((n_tiles * TM_OUT, D_MODEL), F32),
        compiler_params=_params(1),
    )(pos, x1, wts, gf, out_sorted)


def kernel(x_prompt, x_sample, cache_k, cache_v, state_conv, norm1_g, w_in, b_in, attn_sinks,
           w_attn_o, conv_dw, conv_dw_b, conv_ln_g, conv_ln_b, w_conv_o, w_out, norm2_g,
           w_router_group, b_router_group, w_router_expert, b_router_expert, w_e_gate, w_e_up,
           w_e_down, final_norm_g):
    x_p = x_prompt.reshape(T_P, D_MODEL)
    x_s = x_sample.reshape(T_S, D_MODEL)
    g1 = norm1_g[0][None, :]
    b1 = b_in[0][None, :]
    conv_consts = (conv_dw[0], conv_dw_b[0][None, :], conv_ln_g[0][None, :], conv_ln_b[0][None, :])
    hist_pad = jnp.pad(state_conv[0], ((0, 0), (HIST_PAD - HIST, 0), (0, 0)))
    q, kv, u, gates_p, c_p, gates_s, c_s = _in_proj(x_p, x_s, hist_pad, g1, w_in[0], b1, conv_consts)

    sinks = attn_sinks[0]
    a_p = _attn_prompt(sinks, q, kv)
    ck = cache_k[0].reshape(N_STREAMS, WINDOW, KV_WIDTH)
    cv = cache_v[0].reshape(N_STREAMS, WINDOW, KV_WIDTH)
    a_s = _attn_sample(sinks, q, kv, ck, cv)

    w_r = jnp.concatenate([w_router_group[0], w_router_expert[0]], axis=1)
    w_r_hi = w_r.astype(BF16)
    w_r_lo = (w_r - w_r_hi.astype(F32)).astype(BF16)
    w_r_cat = jnp.concatenate(
        [w_r_hi, w_r_lo, jnp.zeros((D_MODEL, ROUTER_PAD - 2 * N_ROUTER), BF16)], axis=1)
    b_r = jnp.concatenate([b_router_group[0], b_router_expert[0],
                           jnp.zeros((ROUTER_PAD - N_ROUTER,), F32)])[None, :]

    x1, xn2, logits = _mix(a_p, a_s, c_p, c_s, gates_p, gates_s, x_p, x_s,
                           w_attn_o[0], w_conv_o[0], w_out[0],
                           norm2_g[0][None, :], w_r_cat, b_r)

    wts, pos, tok_sorted, plan, n_used = _route(logits)
    out_sorted = _moe(plan, n_used, tok_sorted, xn2, w_e_gate[0], w_e_up[0], w_e_down[0])
    gf = final_norm_g[None, :]
    y_p = _combine(pos, x1, wts, gf, out_sorted, 0, T_P // TM_OUT)
    y_s = _combine(pos, x1, wts, gf, out_sorted, T_P // TM_OUT, T_S // TM_OUT)

    kv_shape = (1, -1, WINDOW, N_KV_HEADS, HEAD_DIM)
    new_k_prompt = kv[T_P - WINDOW:T_P, :KV_WIDTH].reshape(kv_shape)
    new_v_prompt = kv[T_P - WINDOW:T_P, KV_WIDTH:].reshape(kv_shape)
    new_conv_prompt = u[T_P - HIST:T_P].reshape(1, 1, HIST, CONV_CH)
    k_s = kv[T_P:, :KV_WIDTH].reshape(N_STREAMS, L_S, KV_WIDTH)
    v_s = kv[T_P:, KV_WIDTH:].reshape(N_STREAMS, L_S, KV_WIDTH)
    new_k_sample = jnp.concatenate([ck[:, L_S:], k_s], axis=1).reshape(kv_shape)
    new_v_sample = jnp.concatenate([cv[:, L_S:], v_s], axis=1).reshape(kv_shape)
    new_conv_sample = u[T_P:].reshape(N_STREAMS, L_S, CONV_CH)[:, L_S - HIST:].reshape(
        1, N_STREAMS, HIST, CONV_CH)

    return (y_p.reshape(1, T_P, D_MODEL), y_s.reshape(N_STREAMS, L_S, D_MODEL),
            new_k_prompt, new_v_prompt, new_conv_prompt,
            new_k_sample, new_v_sample, new_conv_sample)
```

```python
import functools

import numpy as np
import jax
import jax.numpy as jnp
from jax import lax
from jax.experimental import pallas as pl
from jax.experimental.pallas import tpu as pltpu

F32 = jnp.float32
BF16 = jnp.bfloat16
U32 = jnp.uint32

D_MODEL = 2048
T_P = 8192
N_STREAMS = 8
L_S = 64
T_S = N_STREAMS * L_S
T = T_P + T_S
CHUNK = 64
WINDOW = 128
HEAD_DIM = 64
N_Q_HEADS = 16
N_KV_HEADS = 4
GROUP = N_Q_HEADS // N_KV_HEADS
ATTN_WIDTH = N_Q_HEADS * HEAD_DIM
KV_WIDTH = N_KV_HEADS * HEAD_DIM
CONV_CH = 1024
CONV_WIDTH = 31
HIST = CONV_WIDTH - 1
HIST_PAD = 32
SUBLANES = 8
LANES = 128
ROW_TILE = SUBLANES
HALF_D = D_MODEL // 2
HI_MASK = np.uint32(0xFFFF0000)
OFF_KV = ATTN_WIDTH
OFF_GLU = OFF_KV + 2 * KV_WIDTH
OFF_GATES = OFF_GLU + 2 * CONV_CH
IN_WIDTH = OFF_GATES + 2 * D_MODEL
N_GROUPS = 8
EXPERTS_PER_GROUP = 4
N_EXPERTS = N_GROUPS * EXPERTS_PER_GROUP
TOP_K = 2
D_EXPERT = 512
N_ROUTER = N_GROUPS + N_EXPERTS
ROUTER_PAD = 128
EPS = 1e-6
NEG = -1e30

VMEM_LIMIT = 56 * 1024 * 1024
MXU_N = 256

TM_WIDE = 2048
TM_GLU = T // 4
TN_IN = 512
TN_GATES = TN_IN
N_GATE_TILES = 2 * D_MODEL // TN_GATES
ROW_CHUNK = 256
W_STAGE_ROWS = 256
TQ_ATTN = 512
PAIR = 2 * CHUNK
BF16_ROWS = 16
SUB_CONV = 32
PIECE_CONV = 32
CONV_LAG = 1
TM_MIX = 256
BM_MOE = 256
R_ROUTE = 256
TM_OUT = 512
N_SLOTS = T * TOP_K
N_BLK_MOE = -(-(N_SLOTS + N_EXPERTS * (BM_MOE - 1)) // BM_MOE)


def _sigmoid(x):
    return 1.0 / (1.0 + jnp.exp(-x))


def _params(n_axes):
    return pltpu.CompilerParams(dimension_semantics=("arbitrary",) * n_axes,
                                vmem_limit_bytes=VMEM_LIMIT)


def _store_packed_rows(ref, y):
    rows = y.shape[0]
    for s in range(ROW_TILE):
        lo = y[:, s * LANES:(s + 1) * LANES].astype(BF16).astype(F32)
        hi = y[:, HALF_D + s * LANES:HALF_D + (s + 1) * LANES].astype(BF16).astype(F32)
        word = (lax.bitcast_convert_type(hi, U32) & HI_MASK) | (lax.bitcast_convert_type(lo, U32) >> 16)
        ref[pl.ds(s, rows, stride=ROW_TILE), :] = word


def _row_tile(r):
    start = r * ROW_TILE
    return pl.ds(start if isinstance(r, int) else pl.multiple_of(start, ROW_TILE), ROW_TILE)


def _load_packed_rows(ref, rows, s):
    word = ref[pl.ds(s, rows, stride=ROW_TILE), :]
    lo = lax.bitcast_convert_type(word << 16, F32)
    hi = lax.bitcast_convert_type(word & HI_MASK, F32)
    return lo, hi


def _chunk_dot(xn, w_ref, b_ref, c):
    w = w_ref[:, c * MXU_N:(c + 1) * MXU_N].astype(BF16)
    return jnp.dot(xn, w, preferred_element_type=F32) + b_ref[:, c * MXU_N:(c + 1) * MXU_N]


def _stage_weights(jobs, stage, sem):
    width = stage.shape[-1]
    chunks = [(w_hbm, w_bf, r0) for w_hbm, w_bf in jobs
              for r0 in range(0, w_bf.shape[0], W_STAGE_ROWS)]

    def chunk_copy(i):
        w_hbm, _, r0 = chunks[i]
        return pltpu.make_async_copy(w_hbm.at[pl.ds(r0, W_STAGE_ROWS), pl.ds(0, width)],
                                     stage.at[i % 2], sem.at[i % 2])
    chunk_copy(0).start()
    for i, (_, w_bf, r0) in enumerate(chunks):
        if i + 1 < len(chunks):
            chunk_copy(i + 1).start()
        chunk_copy(i).wait()
        w_bf[r0:r0 + W_STAGE_ROWS, :] = stage[i % 2].astype(BF16)


def _qkv_body(xp_ref, xs_ref, g_ref, w_hbm, b_ref, xn_ref, q_ref, kv_ref, w_bf, stage, sem, *, tm):
    m = pl.program_id(0)

    @pl.when(m == 0)
    def _():
        _stage_weights([(w_hbm, w_bf)], stage, sem)

    is_sample = m >= T_P // tm
    for r0 in range(0, tm, ROW_CHUNK):
        rows = slice(r0, r0 + ROW_CHUNK)
        x = jnp.where(is_sample, xs_ref[rows, :], xp_ref[rows, :])
        ms = jnp.mean(x * x, axis=-1, keepdims=True)
        xn = (x * lax.rsqrt(ms + EPS) * g_ref[...]).astype(BF16)
        xn_ref[rows, :] = xn
        for c in range(OFF_GLU // MXU_N):
            cols = slice(c * MXU_N, (c + 1) * MXU_N)
            acc = jnp.dot(xn, w_bf[:, cols], preferred_element_type=F32) + b_ref[:, cols]
            if c < ATTN_WIDTH // MXU_N:
                q_ref[rows, cols] = (acc * (HEAD_DIM ** -0.5)).astype(BF16)
            else:
                kv_ref[rows, c * MXU_N - ATTN_WIDTH:(c + 1) * MXU_N - ATTN_WIDTH] = acc


def _qkv(x_p, x_s, norm_g, w_in, b_in):
    tm = T_S
    t = T
    n_p = T_P // tm
    return pl.pallas_call(
        functools.partial(_qkv_body, tm=tm),
        name="qkv",
        grid=(t // tm,),
        in_specs=[
            pl.BlockSpec((tm, D_MODEL), lambda m: (jnp.minimum(m, n_p - 1), 0)),
            pl.BlockSpec((tm, D_MODEL), lambda m: (0, 0)),
            pl.BlockSpec((1, D_MODEL), lambda m: (0, 0)),
            pl.BlockSpec(memory_space=pl.ANY),
            pl.BlockSpec((1, OFF_GLU), lambda m: (0, 0)),
        ],
        out_specs=[
            pl.BlockSpec((tm, D_MODEL), lambda m: (m, 0)),
            pl.BlockSpec((tm, ATTN_WIDTH), lambda m: (m, 0)),
            pl.BlockSpec((tm, 2 * KV_WIDTH), lambda m: (m, 0)),
        ],
        out_shape=[
            jax.ShapeDtypeStruct((t, D_MODEL), BF16),
            jax.ShapeDtypeStruct((t, ATTN_WIDTH), BF16),
            jax.ShapeDtypeStruct((t, 2 * KV_WIDTH), F32),
        ],
        scratch_shapes=[
            pltpu.VMEM((D_MODEL, OFF_GLU), BF16),
            pltpu.VMEM((2, W_STAGE_ROWS, OFF_GLU), F32),
            pltpu.SemaphoreType.DMA((2,)),
        ],
        compiler_params=_params(1),
    )(x_p, x_s, norm_g, w_in, b_in)


def _glu_body(xn_ref, wa_ref, ba_ref, wb_ref, bb_ref, u_ref):
    xn = xn_ref[...]
    for c in range(TN_IN // MXU_N):
        a = _chunk_dot(xn, wa_ref, ba_ref, c)
        b = _chunk_dot(xn, wb_ref, bb_ref, c)
        u_ref[:, c * MXU_N:(c + 1) * MXU_N] = a * _sigmoid(b)


def _glu(xn, w_in, b_in, tm):
    t = xn.shape[0]
    a0 = OFF_GLU // TN_IN
    b0 = (OFF_GLU + CONV_CH) // TN_IN
    return pl.pallas_call(
        _glu_body,
        name="glu",
        grid=(t // tm, CONV_CH // TN_IN),
        in_specs=[
            pl.BlockSpec((tm, D_MODEL), lambda m, n: (m, 0)),
            pl.BlockSpec((D_MODEL, TN_IN), lambda m, n: (0, a0 + n)),
            pl.BlockSpec((1, TN_IN), lambda m, n: (0, a0 + n)),
            pl.BlockSpec((D_MODEL, TN_IN), lambda m, n: (0, b0 + n)),
            pl.BlockSpec((1, TN_IN), lambda m, n: (0, b0 + n)),
        ],
        out_specs=pl.BlockSpec((tm, TN_IN), lambda m, n: (m, n)),
        out_shape=jax.ShapeDtypeStruct((t, CONV_CH), F32),
        compiler_params=_params(2),
    )(xn, w_in, b_in, w_in, b_in)


def _attn_prompt_body(sink_ref, q_ref, kvp_ref, kvc_ref, a_ref, at_sc):
    i = pl.program_id(0)
    kv = jnp.concatenate([kvp_ref[...], kvc_ref[...]], axis=0)
    k = kv[:, :KV_WIDTH].astype(BF16)
    vt = kv[:, KV_WIDTH:].T.astype(BF16)
    qt = q_ref[...].astype(F32).T.astype(BF16)
    n_cols = GROUP * PAIR
    n_keys = WINDOW + PAIR
    row = lax.broadcasted_iota(jnp.int32, (n_keys, n_cols), 0)
    col = lax.broadcasted_iota(jnp.int32, (n_keys, n_cols), 1)
    first = jnp.where((col & (PAIR - 1)) >= CHUNK, CHUNK, 0)
    gcol = lax.shift_right_logical(lax.broadcasted_iota(jnp.int32, (1, n_cols), 1),
                                   PAIR.bit_length() - 1)
    biases = []
    for p in range(TQ_ATTN // PAIR):
        pos = row + (i * TQ_ATTN + p * PAIR - WINDOW)
        ok = (row >= first) & (row < first + WINDOW + CHUNK) & (pos >= 0)
        biases.append(jnp.where(ok, 0.0, NEG))

    def scores(p, h):
        w0 = p * PAIR
        kh = k[w0:w0 + n_keys, h * HEAD_DIM:(h + 1) * HEAD_DIM]
        rhs = jnp.concatenate(
            [qt[(h * GROUP + g) * HEAD_DIM:(h * GROUP + g + 1) * HEAD_DIM, w0:w0 + PAIR]
             for g in range(GROUP)], axis=1)
        return jnp.dot(kh, rhs, preferred_element_type=F32) + biases[p]

    def finish(p, h, st):
        w0 = p * PAIR
        sink = jnp.full((1, n_cols), sink_ref[h * GROUP], F32)
        for g in range(1, GROUP):
            sink = jnp.where(gcol == g, sink_ref[h * GROUP + g], sink)
        m = jnp.maximum(jnp.max(st, axis=0, keepdims=True), sink)
        pt = jnp.exp(st - m)
        denom = jnp.sum(pt, axis=0, keepdims=True) + jnp.exp(sink - m)
        ot = jnp.dot(vt[h * HEAD_DIM:(h + 1) * HEAD_DIM, w0:w0 + n_keys], pt.astype(BF16),
                     preferred_element_type=F32) / denom
        for g in range(GROUP):
            r0 = (h * GROUP + g) * HEAD_DIM
            at_sc[r0:r0 + HEAD_DIM, w0:w0 + PAIR] = ot[:, g * PAIR:(g + 1) * PAIR]

    items = [(p, h) for p in range(TQ_ATTN // PAIR) for h in range(N_KV_HEADS)]
    st_next = scores(*items[0])
    for n, (p, h) in enumerate(items):
        st = st_next
        if n + 1 < len(items):
            st_next = scores(*items[n + 1])
        finish(p, h, st)
    a_ref[...] = at_sc[...].T.astype(BF16)


def _attn_prompt(sinks, q, kv):
    r = TQ_ATTN // WINDOW
    return pl.pallas_call(
        _attn_prompt_body,
        name="attn_prompt",
        grid=(T_P // TQ_ATTN,),
        in_specs=[
            pl.BlockSpec(memory_space=pltpu.SMEM),
            pl.BlockSpec((TQ_ATTN, ATTN_WIDTH), lambda i: (i, 0)),
            pl.BlockSpec((WINDOW, 2 * KV_WIDTH), lambda i: (jnp.maximum(i * r - 1, 0), 0)),
            pl.BlockSpec((TQ_ATTN, 2 * KV_WIDTH), lambda i: (i, 0)),
        ],
        out_specs=pl.BlockSpec((TQ_ATTN, ATTN_WIDTH), lambda i: (i, 0)),
        out_shape=jax.ShapeDtypeStruct((T_P, ATTN_WIDTH), BF16),
        scratch_shapes=[pltpu.VMEM((ATTN_WIDTH, TQ_ATTN), F32)],
        compiler_params=_params(1),
    )(sinks, q, kv, kv)


def _attn_sample_body(sink_ref, q_ref, kvc_ref, ck_ref, cv_ref, a_ref):
    kvc = kvc_ref[...]
    kw = jnp.concatenate([ck_ref[0], kvc[:, :KV_WIDTH]], axis=0).astype(BF16)
    vw = jnp.concatenate([cv_ref[0], kvc[:, KV_WIDTH:]], axis=0).astype(BF16)
    qc = q_ref[...]
    for h in range(N_KV_HEADS):
        kh = kw[:, h * HEAD_DIM:(h + 1) * HEAD_DIM]
        vh = vw[:, h * HEAD_DIM:(h + 1) * HEAD_DIM]
        qg = jnp.concatenate(
            [qc[:, (h * GROUP + g) * HEAD_DIM:(h * GROUP + g + 1) * HEAD_DIM] for g in range(GROUP)],
            axis=0)
        s = lax.dot_general(qg, kh, (((1,), (1,)), ((), ())), preferred_element_type=F32)
        sink = jnp.concatenate(
            [jnp.full((L_S, 1), sink_ref[h * GROUP + g], F32) for g in range(GROUP)], axis=0)
        m = jnp.maximum(jnp.max(s, axis=1, keepdims=True), sink)
        p = jnp.exp(s - m)
        denom = jnp.sum(p, axis=1, keepdims=True) + jnp.exp(sink - m)
        o = jnp.dot(p.astype(BF16), vh, preferred_element_type=F32) / denom
        for g in range(GROUP):
            c0 = (h * GROUP + g) * HEAD_DIM
            a_ref[:, c0:c0 + HEAD_DIM] = o[g * L_S:(g + 1) * L_S].astype(BF16)


def _attn_sample(sinks, q, kv, cache_k, cache_v):
    first = T_P // L_S
    return pl.pallas_call(
        _attn_sample_body,
        name="attn_sample",
        grid=(N_STREAMS,),
        in_specs=[
            pl.BlockSpec(memory_space=pltpu.SMEM),
            pl.BlockSpec((L_S, ATTN_WIDTH), lambda b: (first + b, 0)),
            pl.BlockSpec((L_S, 2 * KV_WIDTH), lambda b: (first + b, 0)),
            pl.BlockSpec((1, WINDOW, KV_WIDTH), lambda b: (b, 0, 0)),
            pl.BlockSpec((1, WINDOW, KV_WIDTH), lambda b: (b, 0, 0)),
        ],
        out_specs=pl.BlockSpec((L_S, ATTN_WIDTH), lambda b: (b, 0)),
        out_shape=jax.ShapeDtypeStruct((T_S, ATTN_WIDTH), BF16),
        compiler_params=_params(1),
    )(sinks, q, kv, cache_k, cache_v)


def _conv_shifts(prev, cur_ref, ext_sc, sh_sc, rows):
    ext_sc[0:HIST_PAD, :] = prev
    ext_sc[HIST_PAD:HIST_PAD + rows, :] = cur_ref[...]
    ext_sc[HIST_PAD + rows:, :] = jnp.zeros((SUBLANES, CONV_CH), F32)
    lead = HIST_PAD - HIST
    span = rows + HIST_PAD - SUBLANES
    for s in range(SUBLANES):
        sh_sc[s, 0:span, :] = ext_sc[lead + s:lead + s + span, :]


def _conv_rows(r0, w_ref, b_ref, lg_ref, lb_ref, o_ref, sh_sc):
    acc = jnp.zeros((SUB_CONV, CONV_CH), F32) + b_ref[...]
    for k in range(CONV_WIDTH):
        a, s = divmod(k, SUBLANES)
        r = r0 + a * SUBLANES
        acc = acc + w_ref[k:k + 1, :] * sh_sc[s, r:r + SUB_CONV, :]
    mu = jnp.mean(acc, axis=-1, keepdims=True)
    d = acc - mu
    var = jnp.mean(d * d, axis=-1, keepdims=True)
    yn = d * lax.rsqrt(var + EPS) * lg_ref[...] + lb_ref[...]
    out = yn * _sigmoid(yn)
    o_ref[r0:r0 + SUB_CONV, :] = out.astype(BF16)
    return out


def _gates_conv_body(xn_ref, w_ref, b_ref, prev_ref, cur_ref, cw_ref, cb_ref, lg_ref, lb_ref, zero_ref,
                     gate_ref, c_ref, ext_sc, sh_sc, *, rows, fresh):
    if fresh:
        step = pl.program_id(0) * N_GATE_TILES + pl.program_id(1)
        prev = jnp.where(step > 0, prev_ref[...], 0.0)
    else:
        prev = prev_ref[0]
    _conv_shifts(prev, cur_ref, ext_sc, sh_sc, rows)
    n_chunks = TN_GATES // MXU_N
    w = [w_ref[:, c * MXU_N:(c + 1) * MXU_N].astype(BF16) for c in range(n_chunks)]
    zeros = []
    for i in range(rows // PIECE_CONV):
        r = slice(i * ROW_CHUNK, (i + 1) * ROW_CHUNK)
        xr = xn_ref[r, :]
        if i >= CONV_LAG:
            zero16 = zeros[i - CONV_LAG]
            top = jnp.concatenate([xr[0:BF16_ROWS, 0:LANES] + zero16, xr[0:BF16_ROWS, LANES:]], axis=1)
            xr = jnp.concatenate([top, xr[BF16_ROWS:, :]], axis=0)
        accs = [jnp.dot(xr, w[c], preferred_element_type=F32) for c in range(n_chunks)]
        outs = [_conv_rows(r0, cw_ref, cb_ref, lg_ref, lb_ref, c_ref, sh_sc)
                for r0 in range(i * PIECE_CONV, (i + 1) * PIECE_CONV, SUB_CONV)]
        bits = [lax.bitcast_convert_type(o[g * SUBLANES:(g + 1) * SUBLANES, 0:LANES], jnp.int32)
                for o in outs for g in range(SUB_CONV // SUBLANES)]
        zero8 = lax.bitcast_convert_type(functools.reduce(jnp.bitwise_or, bits) & zero_ref[...], F32)
        zeros.append(jnp.concatenate([zero8, zero8], axis=0).astype(BF16))
        for c in range(n_chunks):
            cols = slice(c * MXU_N, (c + 1) * MXU_N)
            gate_ref[r, cols] = _sigmoid(accs[c] + b_ref[:, cols]).astype(BF16)


def _gates_conv(xn, w_in, b_in, hist, u, conv_consts, tm, rows, fresh):
    t = T_P if fresh else T_S
    xn_tile0 = 0 if fresh else T_P // tm
    u_tile0 = 0 if fresh else T_P // rows
    n0 = OFF_GATES // TN_GATES
    tile = lambda m, n: m * N_GATE_TILES + n
    if fresh:
        r = rows // HIST_PAD
        prev_spec = pl.BlockSpec((HIST_PAD, CONV_CH), lambda m, n: (jnp.maximum(tile(m, n) * r - 1, 0), 0))
    else:
        prev_spec = pl.BlockSpec((1, HIST_PAD, CONV_CH), lambda m, n: (tile(m, n), 0, 0))
    const_map = lambda m, n: (0, 0)
    return pl.pallas_call(
        functools.partial(_gates_conv_body, rows=rows, fresh=fresh),
        name="gates_conv",
        grid=(t // tm, N_GATE_TILES),
        in_specs=[
            pl.BlockSpec((tm, D_MODEL), lambda m, n: (xn_tile0 + m, 0)),
            pl.BlockSpec((D_MODEL, TN_GATES), lambda m, n: (0, n0 + n)),
            pl.BlockSpec((1, TN_GATES), lambda m, n: (0, n0 + n)),
            prev_spec,
            pl.BlockSpec((rows, CONV_CH), lambda m, n: (u_tile0 + tile(m, n), 0)),
            pl.BlockSpec((CONV_WIDTH, CONV_CH), const_map),
            pl.BlockSpec((1, CONV_CH), const_map),
            pl.BlockSpec((1, CONV_CH), const_map),
            pl.BlockSpec((1, CONV_CH), const_map),
            pl.BlockSpec((SUBLANES, LANES), const_map),
        ],
        out_specs=[
            pl.BlockSpec((tm, TN_GATES), lambda m, n: (m, n)),
            pl.BlockSpec((rows, CONV_CH), lambda m, n: (tile(m, n), 0)),
        ],
        out_shape=[
            jax.ShapeDtypeStruct((t, 2 * D_MODEL), BF16),
            jax.ShapeDtypeStruct((t, CONV_CH), BF16),
        ],
        scratch_shapes=[
            pltpu.VMEM((HIST_PAD + rows + SUBLANES, CONV_CH), F32),
            pltpu.VMEM((SUBLANES, HIST_PAD + rows, CONV_CH), F32),
        ],
        compiler_params=_params(2),
    )(xn, w_in, b_in, hist, u, *conv_consts, jnp.zeros((SUBLANES, LANES), jnp.int32))


def _in_proj(x_p, x_s, hist, norm_g, w_in, b_in, conv_consts):
    xn, q, kv = _qkv(x_p, x_s, norm_g, w_in, b_in)
    u = _glu(xn, w_in, b_in, TM_GLU)
    gates_p, c_p = _gates_conv(xn, w_in, b_in, u, u, conv_consts, TM_WIDE,
                               TM_WIDE // N_GATE_TILES, True)
    gates_s, c_s = _gates_conv(xn, w_in, b_in, hist, u, conv_consts, T_S,
                               T_S // N_GATE_TILES, False)
    return q, kv, u, gates_p, c_p, gates_s, c_s


def _mix_body(ap_ref, as_ref, cp_ref, cs_ref, gp_ref, gs_ref, xp_ref, xs_ref, wao_hbm, wco_hbm,
              wout_hbm, g2_ref, wr_ref, br_ref, x1_ref, xn_ref, lg_ref,
              wao_ref, wco_ref, wout_ref, stage, sem):
    m = pl.program_id(0)

    @pl.when(m == 0)
    def _():
        _stage_weights([(wao_hbm, wao_ref), (wco_hbm, wco_ref), (wout_hbm, wout_ref)], stage, sem)

    is_sample = m >= T_P // TM_MIX
    a = jnp.where(is_sample, as_ref[...], ap_ref[...])
    c = jnp.where(is_sample, cs_ref[...], cp_ref[...])
    gate = jnp.where(is_sample, gs_ref[...], gp_ref[...])
    x = jnp.where(is_sample, xs_ref[...], xp_ref[...])
    pa = jnp.dot(a, wao_ref[...], preferred_element_type=F32)
    pc = jnp.dot(c, wco_ref[...], preferred_element_type=F32)
    y = (gate[:, :D_MODEL].astype(F32) * pa + gate[:, D_MODEL:].astype(F32) * pc).astype(BF16)
    x1 = x + jnp.dot(y, wout_ref[...], preferred_element_type=F32)
    x1_ref[...] = x1
    ms = jnp.mean(x1 * x1, axis=-1, keepdims=True)
    xn = x1 * lax.rsqrt(ms + EPS) * g2_ref[...]
    _store_packed_rows(xn_ref, xn)
    r = jnp.dot(xn.astype(BF16), wr_ref[...], preferred_element_type=F32)
    lg_ref[...] = r + pltpu.roll(r, ROUTER_PAD - N_ROUTER, axis=1) + br_ref[...]


def _mix(a_p, a_s, c_p, c_s, g_p, g_s, x_p, x_s, wao, wco, wout, g2, wr, br):
    n_p = T_P // TM_MIX
    n_s = T_S // TM_MIX
    prompt_map = lambda m: (jnp.minimum(m, n_p - 1), 0)
    sample_map = lambda m: (jnp.clip(m - n_p, 0, n_s - 1), 0)
    row_map = lambda m: (m, 0)
    const_map = lambda m: (0, 0)

    def pair(width):
        return [pl.BlockSpec((TM_MIX, width), prompt_map), pl.BlockSpec((TM_MIX, width), sample_map)]

    return pl.pallas_call(
        _mix_body,
        name="mix",
        grid=(T // TM_MIX,),
        in_specs=pair(ATTN_WIDTH) + pair(CONV_CH) + pair(2 * D_MODEL) + pair(D_MODEL) + [
            pl.BlockSpec(memory_space=pl.ANY),
            pl.BlockSpec(memory_space=pl.ANY),
            pl.BlockSpec(memory_space=pl.ANY),
            pl.BlockSpec((1, D_MODEL), const_map),
            pl.BlockSpec((D_MODEL, ROUTER_PAD), const_map),
            pl.BlockSpec((1, ROUTER_PAD), const_map),
        ],
        out_specs=[
            pl.BlockSpec((TM_MIX, D_MODEL), row_map),
            pl.BlockSpec((TM_MIX * ROW_TILE, LANES), row_map),
            pl.BlockSpec((TM_MIX, ROUTER_PAD), row_map),
        ],
        out_shape=[
            jax.ShapeDtypeStruct((T, D_MODEL), F32),
            jax.ShapeDtypeStruct((T * ROW_TILE, LANES), U32),
            jax.ShapeDtypeStruct((T, ROUTER_PAD), F32),
        ],
        scratch_shapes=[
            pltpu.VMEM((ATTN_WIDTH, D_MODEL), BF16),
            pltpu.VMEM((CONV_CH, D_MODEL), BF16),
            pltpu.VMEM((D_MODEL, D_MODEL), BF16),
            pltpu.VMEM((2, W_STAGE_ROWS, D_MODEL), F32),
            pltpu.SemaphoreType.DMA((2,)),
        ],
        compiler_params=_params(1),
    )(a_p, a_s, c_p, c_s, g_p, g_s, x_p, x_s, wao, wco, wout, g2, wr, br)


REC_W, REC_E, REC_POS = 0, 2, 4


PLAN_E, PLAN_SRC, PLAN_FIRST, PLAN_WSLOT, PLAN_NEXT_E, PLAN_ZERO = range(6)


def _write_plan(cnt, plan_ref, nu_ref):
    shift = BM_MOE.bit_length() - 1
    nblk = [(c + (BM_MOE - 1)) >> shift for c in cnt]
    nxt = [None] * N_EXPERTS
    later = jnp.int32(-1)
    for e in reversed(range(N_EXPERTS)):
        nxt[e] = jnp.where(later >= 0, later, e)
        later = jnp.where(nblk[e] > 0, e, later)

    def clear(j, carry):
        for row in range(PLAN_ZERO + 1):
            plan_ref[row, j] = 0
        return carry
    lax.fori_loop(0, N_BLK_MOE, clear, 0)

    first_blocks = []
    j0 = jnp.int32(0)
    row0 = jnp.int32(0)
    n_before = jnp.int32(0)
    for e in range(N_EXPERTS):
        def fill(b, carry, e=e, j0=j0, row0=row0, n_before=n_before):
            plan_ref[PLAN_E, j0 + b] = e
            plan_ref[PLAN_SRC, j0 + b] = row0 + b * BM_MOE
            plan_ref[PLAN_FIRST, j0 + b] = jnp.where(b == 0, 1, 0)
            plan_ref[PLAN_WSLOT, j0 + b] = n_before & 1
            plan_ref[PLAN_NEXT_E, j0 + b] = nxt[e]
            return carry
        lax.fori_loop(0, nblk[e], fill, 0)
        first_blocks.append(j0)
        j0 = j0 + nblk[e]
        row0 = row0 + cnt[e]
        n_before = n_before + jnp.where(nblk[e] > 0, 1, 0)
    nu_ref[0] = j0
    return first_blocks


def _router_body(lg_ref, rec_ref, idx_ref, plan_ref, nu_ref, carry_sc, rect_sc):
    n_chunks = T // R_ROUTE
    grow = lax.broadcasted_iota(jnp.int32, (N_GROUPS, R_ROUTE), 0).astype(F32)
    erow_i = lax.broadcasted_iota(jnp.int32, (N_EXPERTS, R_ROUTE), 0)
    erow = erow_i.astype(F32)
    egrp = (erow_i >> (EXPERTS_PER_GROUP.bit_length() - 1)).astype(F32)
    row8 = lax.broadcasted_iota(jnp.int32, (SUBLANES, R_ROUTE), 0)
    row128 = lax.broadcasted_iota(jnp.int32, (ROUTER_PAD, R_ROUTE), 0)
    upper = (lax.broadcasted_iota(jnp.int32, (R_ROUTE, R_ROUTE), 0)
             < lax.broadcasted_iota(jnp.int32, (R_ROUTE, R_ROUTE), 1)).astype(BF16)

    def rows_of(base, vals):
        out = jnp.zeros(base.shape, F32)
        for r, v in vals:
            out = jnp.where(base == r, v, out)
        return out

    carry_sc[...] = jnp.zeros_like(carry_sc)

    def count_pass(i, carry_unused):
        rows = pl.ds(pl.multiple_of(i * R_ROUTE, R_ROUTE), R_ROUTE)
        lgt = lg_ref[rows, :].T
        g = lgt[0:N_GROUPS]
        ex = lgt[N_GROUPS:N_ROUTER]
        gmax = jnp.max(g, axis=0, keepdims=True)
        p_grp = 1.0 / jnp.sum(jnp.exp(g - gmax), axis=0, keepdims=True)
        grp = jnp.min(jnp.where(g == gmax, grow, float(N_GROUPS)), axis=0, keepdims=True)
        emask = egrp == grp
        e1 = jnp.max(jnp.where(emask, ex, NEG), axis=0, keepdims=True)
        x1 = jnp.min(jnp.where(emask & (ex == e1), erow, float(N_EXPERTS)), axis=0, keepdims=True)
        emask2 = emask & (erow != x1)
        e2 = jnp.max(jnp.where(emask2, ex, NEG), axis=0, keepdims=True)
        x2 = jnp.min(jnp.where(emask2 & (ex == e2), erow, float(N_EXPERTS)), axis=0, keepdims=True)
        t = jnp.exp(e2 - e1)
        w1 = p_grp / (1.0 + t)
        w2 = p_grp * t / (1.0 + t)
        oh1 = erow == x1
        oh2 = erow == x2
        onehot = jnp.concatenate([oh1.astype(BF16), oh2.astype(BF16)], axis=0)
        carry = carry_sc[:, 0:1]
        before = jnp.dot(onehot, upper, preferred_element_type=F32) + carry
        rank1 = jnp.sum(jnp.where(oh1, before[0:N_EXPERTS], 0.0), axis=0, keepdims=True)
        rank2 = jnp.sum(jnp.where(oh2, before[N_EXPERTS:], 0.0), axis=0, keepdims=True)
        total = carry + jnp.sum(onehot.astype(F32), axis=1, keepdims=True)
        carry_sc[...] = jnp.broadcast_to(total, carry_sc.shape)
        rect_sc[i] = rows_of(row8, ((REC_E, x1), (REC_E + 1, x2), (REC_POS, rank1), (REC_POS + 1, rank2)))
        rec_ref[rows, :] = rows_of(row128, ((REC_W, w1), (REC_W + 1, w2))).T
        return carry_unused

    lax.fori_loop(0, n_chunks, count_pass, 0)

    tot = carry_sc[:, 0:1]
    tot1 = [tot[e, 0] for e in range(N_EXPERTS)]
    cnt = [(tot1[e] + tot[N_EXPERTS + e, 0]).astype(jnp.int32) for e in range(N_EXPERTS)]
    first_blocks = _write_plan(cnt, plan_ref, nu_ref)
    row0 = [(fb * BM_MOE).astype(F32) for fb in first_blocks]

    def place_pass(i, carry_unused):
        rec = rect_sc[i]
        x1 = rec[REC_E:REC_E + 1]
        x2 = rec[REC_E + 1:REC_E + 2]
        p1 = rec[REC_POS:REC_POS + 1]
        p2 = rec[REC_POS + 1:REC_POS + 2]
        for e in range(N_EXPERTS):
            p1 = p1 + jnp.where(x1 == e, row0[e], 0.0)
            p2 = p2 + jnp.where(x2 == e, row0[e] + tot1[e], 0.0)
        rec = jnp.where(row8 == REC_POS, p1 * ROW_TILE, rec)
        rec = jnp.where(row8 == REC_POS + 1, p2 * ROW_TILE, rec)
        idx_ref[i] = rec.astype(jnp.int32)
        return carry_unused

    lax.fori_loop(0, n_chunks, place_pass, 0)


def _router(logits):
    n_chunks = T // R_ROUTE
    return pl.pallas_call(
        _router_body,
        name="router",
        grid=(1,),
        in_specs=[pl.BlockSpec((T, ROUTER_PAD), lambda i: (0, 0))],
        out_specs=[
            pl.BlockSpec((T, ROUTER_PAD), lambda i: (0, 0)),
            pl.BlockSpec((n_chunks, SUBLANES, R_ROUTE), lambda i: (0, 0, 0)),
            pl.BlockSpec(memory_space=pltpu.SMEM),
            pl.BlockSpec(memory_space=pltpu.SMEM),
        ],
        out_shape=[
            jax.ShapeDtypeStruct((T, ROUTER_PAD), F32),
            jax.ShapeDtypeStruct((n_chunks, SUBLANES, R_ROUTE), jnp.int32),
            jax.ShapeDtypeStruct((PLAN_ZERO + 1, N_BLK_MOE), jnp.int32),
            jax.ShapeDtypeStruct((1,), jnp.int32),
        ],
        scratch_shapes=[
            pltpu.VMEM((2 * N_EXPERTS, LANES), F32),
            pltpu.VMEM((n_chunks, SUBLANES, R_ROUTE), F32),
        ],
        compiler_params=_params(1),
    )(logits)


def _route(logits):
    rec, idx, plan, n_used = _router(logits)
    idx = idx.transpose(1, 0, 2).reshape(SUBLANES, T)
    eid = idx[REC_E:REC_E + TOP_K].reshape(-1)
    pos = idx[REC_POS:REC_POS + TOP_K].reshape(-1)
    order = jnp.argsort(eid, stable=True)
    tok_sorted = jnp.concatenate([((order % T) * ROW_TILE).astype(jnp.int32),
                                  jnp.zeros((BM_MOE,), jnp.int32)])
    return rec, pos, tok_sorted, plan, n_used


BLOCKS_PER_STEP = 2
N_STEPS_MOE = -(-(N_BLK_MOE + 1) // BLOCKS_PER_STEP)
GATHER_AHEAD = 2
X_SLOTS = GATHER_AHEAD + 1
WEIGHT_DMA_PRIORITY = 1
GATHER_DMA_PRIORITY = 0


def _moe_body(plan, n_used, tok, x_hbm, wg_hbm, wu_hbm, wd_hbm, o_ref,
              xb, wg_st, wu_st, wd_st, wg_bf, wu_bf, wd_bf, sem_x, sem_w):
    for h in range(BLOCKS_PER_STEP):
        rows = pl.ds(h * BM_MOE * ROW_TILE, BM_MOE * ROW_TILE)
        _moe_block(pl.program_id(0) * BLOCKS_PER_STEP + h, plan, n_used, tok, x_hbm, wg_hbm, wu_hbm,
                   wd_hbm, o_ref.at[rows], xb, wg_st, wu_st, wd_st, wg_bf, wu_bf, wd_bf, sem_x, sem_w)


def _moe_block(j, plan, n_used, tok, x_hbm, wg_hbm, wu_hbm, wd_hbm, o_ref,
               xb, wg_st, wu_st, wd_st, wg_bf, wu_bf, wd_bf, sem_x, sem_w):
    nu = n_used[0]

    def row_copy(src, slot, r):
        t8 = pl.multiple_of(tok[src + r], ROW_TILE)
        return pltpu.make_async_copy(x_hbm.at[pl.ds(t8, ROW_TILE)], xb.at[slot, _row_tile(r)],
                                     sem_x.at[slot])

    def block_wait(slot):
        pltpu.make_async_copy(x_hbm.at[pl.ds(0, BM_MOE * ROW_TILE)], xb.at[slot],
                              sem_x.at[slot]).wait()

    def weight_copies(e, ws):
        return (pltpu.make_async_copy(wg_hbm.at[e], wg_st.at[ws], sem_w.at[ws, 0]),
                pltpu.make_async_copy(wu_hbm.at[e], wu_st.at[ws], sem_w.at[ws, 1]),
                pltpu.make_async_copy(wd_hbm.at[e], wd_st.at[ws], sem_w.at[ws, 2]))

    @pl.when(j == 0)
    def _():
        for cp in weight_copies(plan[PLAN_E, 0], 0):
            cp.start(priority=WEIGHT_DMA_PRIORITY)
        for b in range(GATHER_AHEAD):
            src = plan[PLAN_SRC, b]

            def body(r, carry, src=src, b=b):
                row_copy(src, b, r).start(priority=GATHER_DMA_PRIORITY)
                return carry
            lax.fori_loop(0, BM_MOE, body, 0, unroll=8)

    @pl.when((j < nu) & (plan[PLAN_FIRST, jnp.minimum(j, N_BLK_MOE - 1)] == 1))
    def _():
        e = plan[PLAN_E, j]
        ws = plan[PLAN_WSLOT, j]
        e_next = plan[PLAN_NEXT_E, j]

        @pl.when(e_next != e)
        def _():
            for cp in weight_copies(e_next, 1 - ws):
                cp.start(priority=WEIGHT_DMA_PRIORITY)

        for cp in weight_copies(e, ws):
            cp.wait()
        wg_bf[...] = wg_st[ws].astype(BF16)
        wu_bf[...] = wu_st[ws].astype(BF16)
        wd_bf[...] = wd_st[ws].astype(BF16)

    @pl.when(j < nu)
    def _():
        slot = j % X_SLOTS
        next_slot = (j + GATHER_AHEAD) % X_SLOTS
        block_wait(slot)
        nsrc = plan[PLAN_SRC, jnp.minimum(j + GATHER_AHEAD, N_BLK_MOE - 1)]
        halves = [_load_packed_rows(xb.at[slot], BM_MOE, s) for s in range(ROW_TILE)]
        x = jnp.concatenate([lo for lo, _ in halves] + [hi for _, hi in halves], axis=1).astype(BF16)
        zero = plan[PLAN_ZERO, 0]
        group = BM_MOE // 8
        issued = [0]

        def gather_after(v):
            base = nsrc
            if v is not None:
                base = base + (lax.bitcast_convert_type(v[0:1, 0:1], jnp.int32)[0, 0] & zero)
            for r in range(issued[0], issued[0] + group):
                row_copy(base, next_slot, r).start(priority=GATHER_DMA_PRIORITY)
            issued[0] += group

        gather_after(None)
        gather_after(halves[ROW_TILE - 1][1])
        gu = []
        for c in range(D_EXPERT // MXU_N):
            cols = slice(c * MXU_N, (c + 1) * MXU_N)
            g = jnp.dot(x, wg_bf[:, cols], preferred_element_type=F32)
            gather_after(g)
            u = jnp.dot(x, wu_bf[:, cols], preferred_element_type=F32)
            gather_after(u)
            gu.append((g, u))
        acc = None
        for c, (g, u) in enumerate(gu):
            cols = slice(c * MXU_N, (c + 1) * MXU_N)
            h = (g * _sigmoid(g) * u).astype(BF16)
            part = jnp.dot(h, wd_bf[cols, :], preferred_element_type=F32)
            if c == 0:
                gather_after(part)
                gather_after(part[:, D_MODEL - LANES:])
            acc = part if acc is None else acc + part
        assert issued[0] == BM_MOE
        _store_packed_rows(o_ref, acc)

    @pl.when(j == nu)
    def _():
        for b in range(GATHER_AHEAD):
            block_wait((j + b) % X_SLOTS)

    @pl.when(j >= nu)
    def _():
        o_ref[...] = jnp.zeros_like(o_ref)


def _moe(plan, n_used, tok_sorted, xn_packed, w_g, w_u, w_d):
    grid_spec = pltpu.PrefetchScalarGridSpec(
        num_scalar_prefetch=3,
        grid=(N_STEPS_MOE,),
        in_specs=[pl.BlockSpec(memory_space=pl.ANY)] * 4,
        out_specs=pl.BlockSpec((BLOCKS_PER_STEP * BM_MOE * ROW_TILE, LANES),
                               lambda j, pn, nu, tk: (j, 0)),
        scratch_shapes=[
            pltpu.VMEM((X_SLOTS, BM_MOE * ROW_TILE, LANES), U32),
            pltpu.VMEM((2, D_MODEL, D_EXPERT), F32),
            pltpu.VMEM((2, D_MODEL, D_EXPERT), F32),
            pltpu.VMEM((2, D_EXPERT, D_MODEL), F32),
            pltpu.VMEM((D_MODEL, D_EXPERT), BF16),
            pltpu.VMEM((D_MODEL, D_EXPERT), BF16),
            pltpu.VMEM((D_EXPERT, D_MODEL), BF16),
            pltpu.SemaphoreType.DMA((X_SLOTS,)),
            pltpu.SemaphoreType.DMA((2, 3)),
        ],
    )
    return pl.pallas_call(
        _moe_body,
        name="experts",
        grid_spec=grid_spec,
        out_shape=jax.ShapeDtypeStruct(
            (N_STEPS_MOE * BLOCKS_PER_STEP * BM_MOE * ROW_TILE, LANES), U32),
        compiler_params=_params(1),
    )(plan, n_used, tok_sorted, xn_packed, w_g, w_u, w_d)


def _combine_body(pos, x1_ref, w_ref, gf_ref, o_hbm, y_ref, rb, sem, *, first_tile):
    m = pl.program_id(0)
    last = pl.num_programs(0) - 1

    def row_copy(base, slot, r, k):
        p8 = pl.multiple_of(pos[k * T + base + r], ROW_TILE)
        return pltpu.make_async_copy(o_hbm.at[pl.ds(p8, ROW_TILE)], rb.at[slot, k, _row_tile(r)],
                                     sem.at[slot])

    def tile_wait(slot):
        for k in range(TOP_K):
            pltpu.make_async_copy(o_hbm.at[pl.ds(0, TM_OUT * ROW_TILE)], rb.at[slot, k],
                                  sem.at[slot]).wait()

    def tile_base(mm):
        return (first_tile + mm) * TM_OUT

    @pl.when(m == 0)
    def _():
        for b in range(GATHER_AHEAD):
            base = tile_base(jnp.minimum(b, last))

            def body(r, carry, base=base, b=b):
                for k in range(TOP_K):
                    row_copy(base, b, r, k).start(priority=k)
                return carry
            lax.fori_loop(0, TM_OUT, body, 0, unroll=4)

    slot = m % X_SLOTS
    next_slot = (m + GATHER_AHEAD) % X_SLOTS
    tile_wait(slot)
    nbase = tile_base(jnp.minimum(m + GATHER_AHEAD, last))
    rows = TM_OUT // ROW_TILE
    w0 = w_ref[:, 0:1]
    w1 = w_ref[:, 1:2]
    ss = jnp.zeros((TM_OUT, 1), F32)
    for s in range(ROW_TILE):
        for r in range(s * rows, (s + 1) * rows):
            for k in range(TOP_K):
                row_copy(nbase, next_slot, r, k).start(priority=k)
        lo0, hi0 = _load_packed_rows(rb.at[slot, 0], TM_OUT, s)
        lo1, hi1 = _load_packed_rows(rb.at[slot, 1], TM_OUT, s)
        for off, r0, r1 in ((s * LANES, lo0, lo1), (HALF_D + s * LANES, hi0, hi1)):
            y = x1_ref[:, off:off + LANES] + w0 * r0 + w1 * r1
            ss = ss + jnp.sum(y * y, axis=-1, keepdims=True)
            y_ref[:, off:off + LANES] = y
    scale = lax.rsqrt(ss * (1.0 / D_MODEL) + EPS)
    y_ref[...] = y_ref[...] * scale * gf_ref[...]

    @pl.when(m == last)
    def _():
        for b in range(1, GATHER_AHEAD + 1):
            tile_wait((m + b) % X_SLOTS)


def _combine(pos, x1, wts, gf, out_sorted, first_tile, n_tiles):
    grid_spec = pltpu.PrefetchScalarGridSpec(
        num_scalar_prefetch=1,
        grid=(n_tiles,),
        in_specs=[
            pl.BlockSpec((TM_OUT, D_MODEL), lambda m, p: (first_tile + m, 0)),
            pl.BlockSpec((TM_OUT, ROUTER_PAD), lambda m, p: (first_tile + m, 0)),
            pl.BlockSpec((1, D_MODEL), lambda m, p: (0, 0)),
            pl.BlockSpec(memory_space=pl.ANY),
        ],
        out_specs=pl.BlockSpec((TM_OUT, D_MODEL), lambda m, p: (m, 0)),
        scratch_shapes=[
            pltpu.VMEM((X_SLOTS, TOP_K, TM_OUT * ROW_TILE, LANES), U32),
            pltpu.SemaphoreType.DMA((X_SLOTS,)),
        ],
    )
    return pl.pallas_call(
        functools.partial(_combine_body, first_tile=first_tile),
        name="combine",
        grid_spec=grid_spec,
        out_shape=jax.ShapeDtypeStruct((n_tiles * TM_OUT, D_MODEL), F32),
        compiler_params=_params(1),
    )(pos, x1, wts, gf, out_sorted)


def kernel(x_prompt, x_sample, cache_k, cache_v, state_conv, norm1_g, w_in, b_in, attn_sinks,
           w_attn_o, conv_dw, conv_dw_b, conv_ln_g, conv_ln_b, w_conv_o, w_out, norm2_g,
           w_router_group, b_router_group, w_router_expert, b_router_expert, w_e_gate, w_e_up,
           w_e_down, final_norm_g):
    x_p = x_prompt.reshape(T_P, D_MODEL)
    x_s = x_sample.reshape(T_S, D_MODEL)
    g1 = norm1_g[0][None, :]
    b1 = b_in[0][None, :]
    conv_consts = (conv_dw[0], conv_dw_b[0][None, :], conv_ln_g[0][None, :], conv_ln_b[0][None, :])
    hist_pad = jnp.pad(state_conv[0], ((0, 0), (HIST_PAD - HIST, 0), (0, 0)))
    q, kv, u, gates_p, c_p, gates_s, c_s = _in_proj(x_p, x_s, hist_pad, g1, w_in[0], b1, conv_consts)

    sinks = attn_sinks[0]
    a_p = _attn_prompt(sinks, q, kv)
    ck = cache_k[0].reshape(N_STREAMS, WINDOW, KV_WIDTH)
    cv = cache_v[0].reshape(N_STREAMS, WINDOW, KV_WIDTH)
    a_s = _attn_sample(sinks, q, kv, ck, cv)

    w_r = jnp.concatenate([w_router_group[0], w_router_expert[0]], axis=1)
    w_r_hi = w_r.astype(BF16)
    w_r_lo = (w_r - w_r_hi.astype(F32)).astype(BF16)
    w_r_cat = jnp.concatenate(
        [w_r_hi, w_r_lo, jnp.zeros((D_MODEL, ROUTER_PAD - 2 * N_ROUTER), BF16)], axis=1)
    b_r = jnp.concatenate([b_router_group[0], b_router_expert[0],
                           jnp.zeros((ROUTER_PAD - N_ROUTER,), F32)])[None, :]

    x1, xn2, logits = _mix(a_p, a_s, c_p, c_s, gates_p, gates_s, x_p, x_s,
                           w_attn_o[0], w_conv_o[0], w_out[0],
                           norm2_g[0][None, :], w_r_cat, b_r)

    wts, pos, tok_sorted, plan, n_used = _route(logits)
    out_sorted = _moe(plan, n_used, tok_sorted, xn2, w_e_gate[0], w_e_up[0], w_e_down[0])
    gf = final_norm_g[None, :]
    y_p = _combine(pos, x1, wts, gf, out_sorted, 0, T_P // TM_OUT)
    y_s = _combine(pos, x1, wts, gf, out_sorted, T_P // TM_OUT, T_S // TM_OUT)

    kv_shape = (1, -1, WINDOW, N_KV_HEADS, HEAD_DIM)
    new_k_prompt = kv[T_P - WINDOW:T_P, :KV_WIDTH].reshape(kv_shape)
    new_v_prompt = kv[T_P - WINDOW:T_P, KV_WIDTH:].reshape(kv_shape)
    new_conv_prompt = u[T_P - HIST:T_P].reshape(1, 1, HIST, CONV_CH)
    k_s = kv[T_P:, :KV_WIDTH].reshape(N_STREAMS, L_S, KV_WIDTH)
    v_s = kv[T_P:, KV_WIDTH:].reshape(N_STREAMS, L_S, KV_WIDTH)
    new_k_sample = jnp.concatenate([ck[:, L_S:], k_s], axis=1).reshape(kv_shape)
    new_v_sample = jnp.concatenate([cv[:, L_S:], v_s], axis=1).reshape(kv_shape)
    new_conv_sample = u[T_P:].reshape(N_STREAMS, L_S, CONV_CH)[:, L_S - HIST:].reshape(
        1, N_STREAMS, HIST, CONV_CH)

    return (y_p.reshape(1, T_P, D_MODEL), y_s.reshape(N_STREAMS, L_S, D_MODEL),
            new_k_prompt, new_v_prompt, new_conv_prompt,
            new_k_sample, new_v_sample, new_conv_sample)
```

```python
import functools

import numpy as np
import jax
import jax.numpy as jnp
from jax import lax
from jax.experimental import pallas as pl
from jax.experimental.pallas import tpu as pltpu

F32 = jnp.float32
BF16 = jnp.bfloat16
U32 = jnp.uint32

D_MODEL = 2048
T_P = 8192
N_STREAMS = 8
L_S = 64
T_S = N_STREAMS * L_S
T = T_P + T_S
CHUNK = 64
WINDOW = 128
HEAD_DIM = 64
N_Q_HEADS = 16
N_KV_HEADS = 4
GROUP = N_Q_HEADS // N_KV_HEADS
ATTN_WIDTH = N_Q_HEADS * HEAD_DIM
KV_WIDTH = N_KV_HEADS * HEAD_DIM
CONV_CH = 1024
CONV_WIDTH = 31
HIST = CONV_WIDTH - 1
HIST_PAD = 32
SUBLANES = 8
LANES = 128
ROW_TILE = SUBLANES
HALF_D = D_MODEL // 2
HI_MASK = np.uint32(0xFFFF0000)
OFF_KV = ATTN_WIDTH
OFF_GLU = OFF_KV + 2 * KV_WIDTH
OFF_GATES = OFF_GLU + 2 * CONV_CH
IN_WIDTH = OFF_GATES + 2 * D_MODEL
N_GROUPS = 8
EXPERTS_PER_GROUP = 4
N_EXPERTS = N_GROUPS * EXPERTS_PER_GROUP
TOP_K = 2
D_EXPERT = 512
N_ROUTER = N_GROUPS + N_EXPERTS
ROUTER_PAD = 128
EPS = 1e-6
NEG = -1e30

VMEM_LIMIT = 56 * 1024 * 1024
MXU_N = 256

TM_WIDE = 2048
TM_GLU = T // 4
TN_IN = 512
TN_GATES = TN_IN
N_GATE_TILES = 2 * D_MODEL // TN_GATES
ROW_CHUNK = 256
W_STAGE_ROWS = 256
TQ_ATTN = 512
PAIR = 2 * CHUNK
BF16_ROWS = 16
SUB_CONV = 32
PIECE_CONV = 32
CONV_LAG = 1
TM_MIX = 256
BM_MOE = 256
R_ROUTE = 256
TM_OUT = 256
N_SLOTS = T * TOP_K
N_BLK_MOE = -(-(N_SLOTS + N_EXPERTS * (BM_MOE - 1)) // BM_MOE)


def _sigmoid(x):
    return 1.0 / (1.0 + jnp.exp(-x))


def _params(n_axes):
    return pltpu.CompilerParams(dimension_semantics=("arbitrary",) * n_axes,
                                vmem_limit_bytes=VMEM_LIMIT)


def _store_packed_rows(ref, y):
    rows = y.shape[0]
    for s in range(ROW_TILE):
        lo = y[:, s * LANES:(s + 1) * LANES].astype(BF16).astype(F32)
        hi = y[:, HALF_D + s * LANES:HALF_D + (s + 1) * LANES].astype(BF16).astype(F32)
        word = (lax.bitcast_convert_type(hi, U32) & HI_MASK) | (lax.bitcast_convert_type(lo, U32) >> 16)
        ref[pl.ds(s, rows, stride=ROW_TILE), :] = word


def _row_tile(r):
    start = r * ROW_TILE
    return pl.ds(start if isinstance(r, int) else pl.multiple_of(start, ROW_TILE), ROW_TILE)


def _load_packed_rows(ref, rows, s):
    word = ref[pl.ds(s, rows, stride=ROW_TILE), :]
    lo = lax.bitcast_convert_type(word << 16, F32)
    hi = lax.bitcast_convert_type(word & HI_MASK, F32)
    return lo, hi


def _chunk_dot(xn, w_ref, b_ref, c):
    w = w_ref[:, c * MXU_N:(c + 1) * MXU_N].astype(BF16)
    return jnp.dot(xn, w, preferred_element_type=F32) + b_ref[:, c * MXU_N:(c + 1) * MXU_N]


def _stage_weights(jobs, stage, sem):
    width = stage.shape[-1]
    chunks = [(w_hbm, w_bf, r0) for w_hbm, w_bf in jobs
              for r0 in range(0, w_bf.shape[0], W_STAGE_ROWS)]

    def chunk_copy(i):
        w_hbm, _, r0 = chunks[i]
        return pltpu.make_async_copy(w_hbm.at[pl.ds(r0, W_STAGE_ROWS), pl.ds(0, width)],
                                     stage.at[i % 2], sem.at[i % 2])
    chunk_copy(0).start()
    for i, (_, w_bf, r0) in enumerate(chunks):
        if i + 1 < len(chunks):
            chunk_copy(i + 1).start()
        chunk_copy(i).wait()
        w_bf[r0:r0 + W_STAGE_ROWS, :] = stage[i % 2].astype(BF16)


def _qkv_body(xp_ref, xs_ref, g_ref, w_hbm, b_ref, xn_ref, q_ref, kv_ref, w_bf, stage, sem, *, tm):
    m = pl.program_id(0)

    @pl.when(m == 0)
    def _():
        _stage_weights([(w_hbm, w_bf)], stage, sem)

    is_sample = m >= T_P // tm
    for r0 in range(0, tm, ROW_CHUNK):
        rows = slice(r0, r0 + ROW_CHUNK)
        x = jnp.where(is_sample, xs_ref[rows, :], xp_ref[rows, :])
        ms = jnp.mean(x * x, axis=-1, keepdims=True)
        xn = (x * lax.rsqrt(ms + EPS) * g_ref[...]).astype(BF16)
        xn_ref[rows, :] = xn
        for c in range(OFF_GLU // MXU_N):
            cols = slice(c * MXU_N, (c + 1) * MXU_N)
            acc = jnp.dot(xn, w_bf[:, cols], preferred_element_type=F32) + b_ref[:, cols]
            if c < ATTN_WIDTH // MXU_N:
                q_ref[rows, cols] = (acc * (HEAD_DIM ** -0.5)).astype(BF16)
            else:
                kv_ref[rows, c * MXU_N - ATTN_WIDTH:(c + 1) * MXU_N - ATTN_WIDTH] = acc


def _qkv(x_p, x_s, norm_g, w_in, b_in):
    tm = T_S
    t = T
    n_p = T_P // tm
    return pl.pallas_call(
        functools.partial(_qkv_body, tm=tm),
        name="qkv",
        grid=(t // tm,),
        in_specs=[
            pl.BlockSpec((tm, D_MODEL), lambda m: (jnp.minimum(m, n_p - 1), 0)),
            pl.BlockSpec((tm, D_MODEL), lambda m: (0, 0)),
            pl.BlockSpec((1, D_MODEL), lambda m: (0, 0)),
            pl.BlockSpec(memory_space=pl.ANY),
            pl.BlockSpec((1, OFF_GLU), lambda m: (0, 0)),
        ],
        out_specs=[
            pl.BlockSpec((tm, D_MODEL), lambda m: (m, 0)),
            pl.BlockSpec((tm, ATTN_WIDTH), lambda m: (m, 0)),
            pl.BlockSpec((tm, 2 * KV_WIDTH), lambda m: (m, 0)),
        ],
        out_shape=[
            jax.ShapeDtypeStruct((t, D_MODEL), BF16),
            jax.ShapeDtypeStruct((t, ATTN_WIDTH), BF16),
            jax.ShapeDtypeStruct((t, 2 * KV_WIDTH), F32),
        ],
        scratch_shapes=[
            pltpu.VMEM((D_MODEL, OFF_GLU), BF16),
            pltpu.VMEM((2, W_STAGE_ROWS, OFF_GLU), F32),
            pltpu.SemaphoreType.DMA((2,)),
        ],
        compiler_params=_params(1),
    )(x_p, x_s, norm_g, w_in, b_in)


def _glu_body(xn_ref, wa_ref, ba_ref, wb_ref, bb_ref, u_ref):
    xn = xn_ref[...]
    for c in range(TN_IN // MXU_N):
        a = _chunk_dot(xn, wa_ref, ba_ref, c)
        b = _chunk_dot(xn, wb_ref, bb_ref, c)
        u_ref[:, c * MXU_N:(c + 1) * MXU_N] = a * _sigmoid(b)


def _glu(xn, w_in, b_in, tm):
    t = xn.shape[0]
    a0 = OFF_GLU // TN_IN
    b0 = (OFF_GLU + CONV_CH) // TN_IN
    return pl.pallas_call(
        _glu_body,
        name="glu",
        grid=(t // tm, CONV_CH // TN_IN),
        in_specs=[
            pl.BlockSpec((tm, D_MODEL), lambda m, n: (m, 0)),
            pl.BlockSpec((D_MODEL, TN_IN), lambda m, n: (0, a0 + n)),
            pl.BlockSpec((1, TN_IN), lambda m, n: (0, a0 + n)),
            pl.BlockSpec((D_MODEL, TN_IN), lambda m, n: (0, b0 + n)),
            pl.BlockSpec((1, TN_IN), lambda m, n: (0, b0 + n)),
        ],
        out_specs=pl.BlockSpec((tm, TN_IN), lambda m, n: (m, n)),
        out_shape=jax.ShapeDtypeStruct((t, CONV_CH), F32),
        compiler_params=_params(2),
    )(xn, w_in, b_in, w_in, b_in)


def _attn_prompt_body(sink_ref, q_ref, kvp_ref, kvc_ref, a_ref, at_sc):
    i = pl.program_id(0)
    kv = jnp.concatenate([kvp_ref[...], kvc_ref[...]], axis=0)
    k = kv[:, :KV_WIDTH].astype(BF16)
    vt = kv[:, KV_WIDTH:].T.astype(BF16)
    qt = q_ref[...].astype(F32).T.astype(BF16)
    n_cols = GROUP * PAIR
    n_keys = WINDOW + PAIR
    row = lax.broadcasted_iota(jnp.int32, (n_keys, n_cols), 0)
    col = lax.broadcasted_iota(jnp.int32, (n_keys, n_cols), 1)
    first = jnp.where((col & (PAIR - 1)) >= CHUNK, CHUNK, 0)
    gcol = lax.shift_right_logical(lax.broadcasted_iota(jnp.int32, (1, n_cols), 1),
                                   PAIR.bit_length() - 1)
    biases = []
    for p in range(TQ_ATTN // PAIR):
        pos = row + (i * TQ_ATTN + p * PAIR - WINDOW)
        ok = (row >= first) & (row < first + WINDOW + CHUNK) & (pos >= 0)
        biases.append(jnp.where(ok, 0.0, NEG))

    def scores(p, h):
        w0 = p * PAIR
        kh = k[w0:w0 + n_keys, h * HEAD_DIM:(h + 1) * HEAD_DIM]
        rhs = jnp.concatenate(
            [qt[(h * GROUP + g) * HEAD_DIM:(h * GROUP + g + 1) * HEAD_DIM, w0:w0 + PAIR]
             for g in range(GROUP)], axis=1)
        return jnp.dot(kh, rhs, preferred_element_type=F32) + biases[p]

    def finish(p, h, st):
        w0 = p * PAIR
        sink = jnp.full((1, n_cols), sink_ref[h * GROUP], F32)
        for g in range(1, GROUP):
            sink = jnp.where(gcol == g, sink_ref[h * GROUP + g], sink)
        m = jnp.maximum(jnp.max(st, axis=0, keepdims=True), sink)
        pt = jnp.exp(st - m)
        denom = jnp.sum(pt, axis=0, keepdims=True) + jnp.exp(sink - m)
        ot = jnp.dot(vt[h * HEAD_DIM:(h + 1) * HEAD_DIM, w0:w0 + n_keys], pt.astype(BF16),
                     preferred_element_type=F32) / denom
        for g in range(GROUP):
            r0 = (h * GROUP + g) * HEAD_DIM
            at_sc[r0:r0 + HEAD_DIM, w0:w0 + PAIR] = ot[:, g * PAIR:(g + 1) * PAIR]

    items = [(p, h) for p in range(TQ_ATTN // PAIR) for h in range(N_KV_HEADS)]
    st_next = scores(*items[0])
    for n, (p, h) in enumerate(items):
        st = st_next
        if n + 1 < len(items):
            st_next = scores(*items[n + 1])
        finish(p, h, st)
    a_ref[...] = at_sc[...].T.astype(BF16)


def _attn_prompt(sinks, q, kv):
    r = TQ_ATTN // WINDOW
    return pl.pallas_call(
        _attn_prompt_body,
        name="attn_prompt",
        grid=(T_P // TQ_ATTN,),
        in_specs=[
            pl.BlockSpec(memory_space=pltpu.SMEM),
            pl.BlockSpec((TQ_ATTN, ATTN_WIDTH), lambda i: (i, 0)),
            pl.BlockSpec((WINDOW, 2 * KV_WIDTH), lambda i: (jnp.maximum(i * r - 1, 0), 0)),
            pl.BlockSpec((TQ_ATTN, 2 * KV_WIDTH), lambda i: (i, 0)),
        ],
        out_specs=pl.BlockSpec((TQ_ATTN, ATTN_WIDTH), lambda i: (i, 0)),
        out_shape=jax.ShapeDtypeStruct((T_P, ATTN_WIDTH), BF16),
        scratch_shapes=[pltpu.VMEM((ATTN_WIDTH, TQ_ATTN), F32)],
        compiler_params=_params(1),
    )(sinks, q, kv, kv)


def _attn_sample_body(sink_ref, q_ref, kvc_ref, ck_ref, cv_ref, a_ref):
    kvc = kvc_ref[...]
    kw = jnp.concatenate([ck_ref[0], kvc[:, :KV_WIDTH]], axis=0).astype(BF16)
    vw = jnp.concatenate([cv_ref[0], kvc[:, KV_WIDTH:]], axis=0).astype(BF16)
    qc = q_ref[...]
    for h in range(N_KV_HEADS):
        kh = kw[:, h * HEAD_DIM:(h + 1) * HEAD_DIM]
        vh = vw[:, h * HEAD_DIM:(h + 1) * HEAD_DIM]
        qg = jnp.concatenate(
            [qc[:, (h * GROUP + g) * HEAD_DIM:(h * GROUP + g + 1) * HEAD_DIM] for g in range(GROUP)],
            axis=0)
        s = lax.dot_general(qg, kh, (((1,), (1,)), ((), ())), preferred_element_type=F32)
        sink = jnp.concatenate(
            [jnp.full((L_S, 1), sink_ref[h * GROUP + g], F32) for g in range(GROUP)], axis=0)
        m = jnp.maximum(jnp.max(s, axis=1, keepdims=True), sink)
        p = jnp.exp(s - m)
        denom = jnp.sum(p, axis=1, keepdims=True) + jnp.exp(sink - m)
        o = jnp.dot(p.astype(BF16), vh, preferred_element_type=F32) / denom
        for g in range(GROUP):
            c0 = (h * GROUP + g) * HEAD_DIM
            a_ref[:, c0:c0 + HEAD_DIM] = o[g * L_S:(g + 1) * L_S].astype(BF16)


def _attn_sample(sinks, q, kv, cache_k, cache_v):
    first = T_P // L_S
    return pl.pallas_call(
        _attn_sample_body,
        name="attn_sample",
        grid=(N_STREAMS,),
        in_specs=[
            pl.BlockSpec(memory_space=pltpu.SMEM),
            pl.BlockSpec((L_S, ATTN_WIDTH), lambda b: (first + b, 0)),
            pl.BlockSpec((L_S, 2 * KV_WIDTH), lambda b: (first + b, 0)),
            pl.BlockSpec((1, WINDOW, KV_WIDTH), lambda b: (b, 0, 0)),
            pl.BlockSpec((1, WINDOW, KV_WIDTH), lambda b: (b, 0, 0)),
        ],
        out_specs=pl.BlockSpec((L_S, ATTN_WIDTH), lambda b: (b, 0)),
        out_shape=jax.ShapeDtypeStruct((T_S, ATTN_WIDTH), BF16),
        compiler_params=_params(1),
    )(sinks, q, kv, cache_k, cache_v)


def _conv_shifts(prev, cur_ref, ext_sc, sh_sc, rows):
    ext_sc[0:HIST_PAD, :] = prev
    ext_sc[HIST_PAD:HIST_PAD + rows, :] = cur_ref[...]
    ext_sc[HIST_PAD + rows:, :] = jnp.zeros((SUBLANES, CONV_CH), F32)
    lead = HIST_PAD - HIST
    span = rows + HIST_PAD - SUBLANES
    for s in range(SUBLANES):
        sh_sc[s, 0:span, :] = ext_sc[lead + s:lead + s + span, :]


def _conv_rows(r0, w_ref, b_ref, lg_ref, lb_ref, o_ref, sh_sc):
    acc = jnp.zeros((SUB_CONV, CONV_CH), F32) + b_ref[...]
    for k in range(CONV_WIDTH):
        a, s = divmod(k, SUBLANES)
        r = r0 + a * SUBLANES
        acc = acc + w_ref[k:k + 1, :] * sh_sc[s, r:r + SUB_CONV, :]
    mu = jnp.mean(acc, axis=-1, keepdims=True)
    d = acc - mu
    var = jnp.mean(d * d, axis=-1, keepdims=True)
    yn = d * lax.rsqrt(var + EPS) * lg_ref[...] + lb_ref[...]
    out = yn * _sigmoid(yn)
    o_ref[r0:r0 + SUB_CONV, :] = out.astype(BF16)
    return out


def _gates_conv_body(xn_ref, w_ref, b_ref, prev_ref, cur_ref, cw_ref, cb_ref, lg_ref, lb_ref, zero_ref,
                     gate_ref, c_ref, ext_sc, sh_sc, *, rows, fresh):
    if fresh:
        step = pl.program_id(0) * N_GATE_TILES + pl.program_id(1)
        prev = jnp.where(step > 0, prev_ref[...], 0.0)
    else:
        prev = prev_ref[0]
    _conv_shifts(prev, cur_ref, ext_sc, sh_sc, rows)
    n_chunks = TN_GATES // MXU_N
    w = [w_ref[:, c * MXU_N:(c + 1) * MXU_N].astype(BF16) for c in range(n_chunks)]
    zeros = []
    for i in range(rows // PIECE_CONV):
        r = slice(i * ROW_CHUNK, (i + 1) * ROW_CHUNK)
        xr = xn_ref[r, :]
        if i >= CONV_LAG:
            zero16 = zeros[i - CONV_LAG]
            top = jnp.concatenate([xr[0:BF16_ROWS, 0:LANES] + zero16, xr[0:BF16_ROWS, LANES:]], axis=1)
            xr = jnp.concatenate([top, xr[BF16_ROWS:, :]], axis=0)
        accs = [jnp.dot(xr, w[c], preferred_element_type=F32) for c in range(n_chunks)]
        outs = [_conv_rows(r0, cw_ref, cb_ref, lg_ref, lb_ref, c_ref, sh_sc)
                for r0 in range(i * PIECE_CONV, (i + 1) * PIECE_CONV, SUB_CONV)]
        bits = [lax.bitcast_convert_type(o[g * SUBLANES:(g + 1) * SUBLANES, 0:LANES], jnp.int32)
                for o in outs for g in range(SUB_CONV // SUBLANES)]
        zero8 = lax.bitcast_convert_type(functools.reduce(jnp.bitwise_or, bits) & zero_ref[...], F32)
        zeros.append(jnp.concatenate([zero8, zero8], axis=0).astype(BF16))
        for c in range(n_chunks):
            cols = slice(c * MXU_N, (c + 1) * MXU_N)
            gate_ref[r, cols] = _sigmoid(accs[c] + b_ref[:, cols]).astype(BF16)


def _gates_conv(xn, w_in, b_in, hist, u, conv_consts, tm, rows, fresh):
    t = T_P if fresh else T_S
    xn_tile0 = 0 if fresh else T_P // tm
    u_tile0 = 0 if fresh else T_P // rows
    n0 = OFF_GATES // TN_GATES
    tile = lambda m, n: m * N_GATE_TILES + n
    if fresh:
        r = rows // HIST_PAD
        prev_spec = pl.BlockSpec((HIST_PAD, CONV_CH), lambda m, n: (jnp.maximum(tile(m, n) * r - 1, 0), 0))
    else:
        prev_spec = pl.BlockSpec((1, HIST_PAD, CONV_CH), lambda m, n: (tile(m, n), 0, 0))
    const_map = lambda m, n: (0, 0)
    return pl.pallas_call(
        functools.partial(_gates_conv_body, rows=rows, fresh=fresh),
        name="gates_conv",
        grid=(t // tm, N_GATE_TILES),
        in_specs=[
            pl.BlockSpec((tm, D_MODEL), lambda m, n: (xn_tile0 + m, 0)),
            pl.BlockSpec((D_MODEL, TN_GATES), lambda m, n: (0, n0 + n)),
            pl.BlockSpec((1, TN_GATES), lambda m, n: (0, n0 + n)),
            prev_spec,
            pl.BlockSpec((rows, CONV_CH), lambda m, n: (u_tile0 + tile(m, n), 0)),
            pl.BlockSpec((CONV_WIDTH, CONV_CH), const_map),
            pl.BlockSpec((1, CONV_CH), const_map),
            pl.BlockSpec((1, CONV_CH), const_map),
            pl.BlockSpec((1, CONV_CH), const_map),
            pl.BlockSpec((SUBLANES, LANES), const_map),
        ],
        out_specs=[
            pl.BlockSpec((tm, TN_GATES), lambda m, n: (m, n)),
            pl.BlockSpec((rows, CONV_CH), lambda m, n: (tile(m, n), 0)),
        ],
        out_shape=[
            jax.ShapeDtypeStruct((t, 2 * D_MODEL), BF16),
            jax.ShapeDtypeStruct((t, CONV_CH), BF16),
        ],
        scratch_shapes=[
            pltpu.VMEM((HIST_PAD + rows + SUBLANES, CONV_CH), F32),
            pltpu.VMEM((SUBLANES, HIST_PAD + rows, CONV_CH), F32),
        ],
        compiler_params=_params(2),
    )(xn, w_in, b_in, hist, u, *conv_consts, jnp.zeros((SUBLANES, LANES), jnp.int32))


def _in_proj(x_p, x_s, hist, norm_g, w_in, b_in, conv_consts):
    xn, q, kv = _qkv(x_p, x_s, norm_g, w_in, b_in)
    u = _glu(xn, w_in, b_in, TM_GLU)
    gates_p, c_p = _gates_conv(xn, w_in, b_in, u, u, conv_consts, TM_WIDE,
                               TM_WIDE // N_GATE_TILES, True)
    gates_s, c_s = _gates_conv(xn, w_in, b_in, hist, u, conv_consts, T_S,
                               T_S // N_GATE_TILES, False)
    return q, kv, u, gates_p, c_p, gates_s, c_s


def _mix_body(ap_ref, as_ref, cp_ref, cs_ref, gp_ref, gs_ref, xp_ref, xs_ref, wao_hbm, wco_hbm,
              wout_hbm, g2_ref, wr_ref, br_ref, x1_ref, xn_ref, lg_ref,
              wao_ref, wco_ref, wout_ref, stage, sem):
    m = pl.program_id(0)

    @pl.when(m == 0)
    def _():
        _stage_weights([(wao_hbm, wao_ref), (wco_hbm, wco_ref), (wout_hbm, wout_ref)], stage, sem)

    is_sample = m >= T_P // TM_MIX
    a = jnp.where(is_sample, as_ref[...], ap_ref[...])
    c = jnp.where(is_sample, cs_ref[...], cp_ref[...])
    gate = jnp.where(is_sample, gs_ref[...], gp_ref[...])
    x = jnp.where(is_sample, xs_ref[...], xp_ref[...])
    pa = jnp.dot(a, wao_ref[...], preferred_element_type=F32)
    pc = jnp.dot(c, wco_ref[...], preferred_element_type=F32)
    y = (gate[:, :D_MODEL].astype(F32) * pa + gate[:, D_MODEL:].astype(F32) * pc).astype(BF16)
    x1 = x + jnp.dot(y, wout_ref[...], preferred_element_type=F32)
    x1_ref[...] = x1
    ms = jnp.mean(x1 * x1, axis=-1, keepdims=True)
    xn = x1 * lax.rsqrt(ms + EPS) * g2_ref[...]
    _store_packed_rows(xn_ref, xn)
    r = jnp.dot(xn.astype(BF16), wr_ref[...], preferred_element_type=F32)
    lg_ref[...] = r + pltpu.roll(r, ROUTER_PAD - N_ROUTER, axis=1) + br_ref[...]


def _mix(a_p, a_s, c_p, c_s, g_p, g_s, x_p, x_s, wao, wco, wout, g2, wr, br):
    n_p = T_P // TM_MIX
    n_s = T_S // TM_MIX
    prompt_map = lambda m: (jnp.minimum(m, n_p - 1), 0)
    sample_map = lambda m: (jnp.clip(m - n_p, 0, n_s - 1), 0)
    row_map = lambda m: (m, 0)
    const_map = lambda m: (0, 0)

    def pair(width):
        return [pl.BlockSpec((TM_MIX, width), prompt_map), pl.BlockSpec((TM_MIX, width), sample_map)]

    return pl.pallas_call(
        _mix_body,
        name="mix",
        grid=(T // TM_MIX,),
        in_specs=pair(ATTN_WIDTH) + pair(CONV_CH) + pair(2 * D_MODEL) + pair(D_MODEL) + [
            pl.BlockSpec(memory_space=pl.ANY),
            pl.BlockSpec(memory_space=pl.ANY),
            pl.BlockSpec(memory_space=pl.ANY),
            pl.BlockSpec((1, D_MODEL), const_map),
            pl.BlockSpec((D_MODEL, ROUTER_PAD), const_map),
            pl.BlockSpec((1, ROUTER_PAD), const_map),
        ],
        out_specs=[
            pl.BlockSpec((TM_MIX, D_MODEL), row_map),
            pl.BlockSpec((TM_MIX * ROW_TILE, LANES), row_map),
            pl.BlockSpec((TM_MIX, ROUTER_PAD), row_map),
        ],
        out_shape=[
            jax.ShapeDtypeStruct((T, D_MODEL), F32),
            jax.ShapeDtypeStruct((T * ROW_TILE, LANES), U32),
            jax.ShapeDtypeStruct((T, ROUTER_PAD), F32),
        ],
        scratch_shapes=[
            pltpu.VMEM((ATTN_WIDTH, D_MODEL), BF16),
            pltpu.VMEM((CONV_CH, D_MODEL), BF16),
            pltpu.VMEM((D_MODEL, D_MODEL), BF16),
            pltpu.VMEM((2, W_STAGE_ROWS, D_MODEL), F32),
            pltpu.SemaphoreType.DMA((2,)),
        ],
        compiler_params=_params(1),
    )(a_p, a_s, c_p, c_s, g_p, g_s, x_p, x_s, wao, wco, wout, g2, wr, br)


REC_W, REC_E, REC_POS = 0, 2, 4


PLAN_E, PLAN_SRC, PLAN_FIRST, PLAN_WSLOT, PLAN_NEXT_E, PLAN_ZERO = range(6)


def _write_plan(cnt, plan_ref, nu_ref):
    shift = BM_MOE.bit_length() - 1
    nblk = [(c + (BM_MOE - 1)) >> shift for c in cnt]
    nxt = [None] * N_EXPERTS
    later = jnp.int32(-1)
    for e in reversed(range(N_EXPERTS)):
        nxt[e] = jnp.where(later >= 0, later, e)
        later = jnp.where(nblk[e] > 0, e, later)

    def clear(j, carry):
        for row in range(PLAN_ZERO + 1):
            plan_ref[row, j] = 0
        return carry
    lax.fori_loop(0, N_BLK_MOE, clear, 0)

    first_blocks = []
    j0 = jnp.int32(0)
    row0 = jnp.int32(0)
    n_before = jnp.int32(0)
    for e in range(N_EXPERTS):
        def fill(b, carry, e=e, j0=j0, row0=row0, n_before=n_before):
            plan_ref[PLAN_E, j0 + b] = e
            plan_ref[PLAN_SRC, j0 + b] = row0 + b * BM_MOE
            plan_ref[PLAN_FIRST, j0 + b] = jnp.where(b == 0, 1, 0)
            plan_ref[PLAN_WSLOT, j0 + b] = n_before & 1
            plan_ref[PLAN_NEXT_E, j0 + b] = nxt[e]
            return carry
        lax.fori_loop(0, nblk[e], fill, 0)
        first_blocks.append(j0)
        j0 = j0 + nblk[e]
        row0 = row0 + cnt[e]
        n_before = n_before + jnp.where(nblk[e] > 0, 1, 0)
    nu_ref[0] = j0
    return first_blocks


def _router_body(lg_ref, rec_ref, idx_ref, plan_ref, nu_ref, carry_sc, rect_sc):
    n_chunks = T // R_ROUTE
    grow = lax.broadcasted_iota(jnp.int32, (N_GROUPS, R_ROUTE), 0).astype(F32)
    erow_i = lax.broadcasted_iota(jnp.int32, (N_EXPERTS, R_ROUTE), 0)
    erow = erow_i.astype(F32)
    egrp = (erow_i >> (EXPERTS_PER_GROUP.bit_length() - 1)).astype(F32)
    row8 = lax.broadcasted_iota(jnp.int32, (SUBLANES, R_ROUTE), 0)
    row128 = lax.broadcasted_iota(jnp.int32, (ROUTER_PAD, R_ROUTE), 0)
    upper = (lax.broadcasted_iota(jnp.int32, (R_ROUTE, R_ROUTE), 0)
             < lax.broadcasted_iota(jnp.int32, (R_ROUTE, R_ROUTE), 1)).astype(BF16)

    def rows_of(base, vals):
        out = jnp.zeros(base.shape, F32)
        for r, v in vals:
            out = jnp.where(base == r, v, out)
        return out

    carry_sc[...] = jnp.zeros_like(carry_sc)

    def count_pass(i, carry_unused):
        rows = pl.ds(pl.multiple_of(i * R_ROUTE, R_ROUTE), R_ROUTE)
        lgt = lg_ref[rows, :].T
        g = lgt[0:N_GROUPS]
        ex = lgt[N_GROUPS:N_ROUTER]
        gmax = jnp.max(g, axis=0, keepdims=True)
        p_grp = 1.0 / jnp.sum(jnp.exp(g - gmax), axis=0, keepdims=True)
        grp = jnp.min(jnp.where(g == gmax, grow, float(N_GROUPS)), axis=0, keepdims=True)
        emask = egrp == grp
        e1 = jnp.max(jnp.where(emask, ex, NEG), axis=0, keepdims=True)
        x1 = jnp.min(jnp.where(emask & (ex == e1), erow, float(N_EXPERTS)), axis=0, keepdims=True)
        emask2 = emask & (erow != x1)
        e2 = jnp.max(jnp.where(emask2, ex, NEG), axis=0, keepdims=True)
        x2 = jnp.min(jnp.where(emask2 & (ex == e2), erow, float(N_EXPERTS)), axis=0, keepdims=True)
        t = jnp.exp(e2 - e1)
        w1 = p_grp / (1.0 + t)
        w2 = p_grp * t / (1.0 + t)
        oh1 = erow == x1
        oh2 = erow == x2
        onehot = jnp.concatenate([oh1.astype(BF16), oh2.astype(BF16)], axis=0)
        carry = carry_sc[:, 0:1]
        before = jnp.dot(onehot, upper, preferred_element_type=F32) + carry
        rank1 = jnp.sum(jnp.where(oh1, before[0:N_EXPERTS], 0.0), axis=0, keepdims=True)
        rank2 = jnp.sum(jnp.where(oh2, before[N_EXPERTS:], 0.0), axis=0, keepdims=True)
        total = carry + jnp.sum(onehot.astype(F32), axis=1, keepdims=True)
        carry_sc[...] = jnp.broadcast_to(total, carry_sc.shape)
        rect_sc[i] = rows_of(row8, ((REC_E, x1), (REC_E + 1, x2), (REC_POS, rank1), (REC_POS + 1, rank2)))
        rec_ref[rows, :] = rows_of(row128, ((REC_W, w1), (REC_W + 1, w2))).T
        return carry_unused

    lax.fori_loop(0, n_chunks, count_pass, 0)

    tot = carry_sc[:, 0:1]
    tot1 = [tot[e, 0] for e in range(N_EXPERTS)]
    cnt = [(tot1[e] + tot[N_EXPERTS + e, 0]).astype(jnp.int32) for e in range(N_EXPERTS)]
    first_blocks = _write_plan(cnt, plan_ref, nu_ref)
    row0 = [(fb * BM_MOE).astype(F32) for fb in first_blocks]

    def place_pass(i, carry_unused):
        rec = rect_sc[i]
        x1 = rec[REC_E:REC_E + 1]
        x2 = rec[REC_E + 1:REC_E + 2]
        p1 = rec[REC_POS:REC_POS + 1]
        p2 = rec[REC_POS + 1:REC_POS + 2]
        for e in range(N_EXPERTS):
            p1 = p1 + jnp.where(x1 == e, row0[e], 0.0)
            p2 = p2 + jnp.where(x2 == e, row0[e] + tot1[e], 0.0)
        rec = jnp.where(row8 == REC_POS, p1 * ROW_TILE, rec)
        rec = jnp.where(row8 == REC_POS + 1, p2 * ROW_TILE, rec)
        idx_ref[i] = rec.astype(jnp.int32)
        return carry_unused

    lax.fori_loop(0, n_chunks, place_pass, 0)


def _router(logits):
    n_chunks = T // R_ROUTE
    return pl.pallas_call(
        _router_body,
        name="router",
        grid=(1,),
        in_specs=[pl.BlockSpec((T, ROUTER_PAD), lambda i: (0, 0))],
        out_specs=[
            pl.BlockSpec((T, ROUTER_PAD), lambda i: (0, 0)),
            pl.BlockSpec((n_chunks, SUBLANES, R_ROUTE), lambda i: (0, 0, 0)),
            pl.BlockSpec(memory_space=pltpu.SMEM),
            pl.BlockSpec(memory_space=pltpu.SMEM),
        ],
        out_shape=[
            jax.ShapeDtypeStruct((T, ROUTER_PAD), F32),
            jax.ShapeDtypeStruct((n_chunks, SUBLANES, R_ROUTE), jnp.int32),
            jax.ShapeDtypeStruct((PLAN_ZERO + 1, N_BLK_MOE), jnp.int32),
            jax.ShapeDtypeStruct((1,), jnp.int32),
        ],
        scratch_shapes=[
            pltpu.VMEM((2 * N_EXPERTS, LANES), F32),
            pltpu.VMEM((n_chunks, SUBLANES, R_ROUTE), F32),
        ],
        compiler_params=_params(1),
    )(logits)


def _route(logits):
    rec, idx, plan, n_used = _router(logits)
    idx = idx.transpose(1, 0, 2).reshape(SUBLANES, T)
    eid = idx[REC_E:REC_E + TOP_K].reshape(-1)
    pos = idx[REC_POS:REC_POS + TOP_K].reshape(-1)
    order = jnp.argsort(eid, stable=True)
    tok_sorted = jnp.concatenate([((order % T) * ROW_TILE).astype(jnp.int32),
                                  jnp.zeros((BM_MOE,), jnp.int32)])
    return rec, pos, tok_sorted, plan, n_used


BLOCKS_PER_STEP = 4
N_STEPS_MOE = -(-(N_BLK_MOE + 1) // BLOCKS_PER_STEP)
GATHER_AHEAD = 2
X_SLOTS = GATHER_AHEAD + 1
WEIGHT_DMA_PRIORITY = 1
GATHER_DMA_PRIORITY = 0


def _moe_body(plan, n_used, tok, x_hbm, wg_hbm, wu_hbm, wd_hbm, o_ref,
              xb, wg_st, wu_st, wd_st, wg_bf, wu_bf, wd_bf, sem_x, sem_w):
    for h in range(BLOCKS_PER_STEP):
        rows = pl.ds(h * BM_MOE * ROW_TILE, BM_MOE * ROW_TILE)
        _moe_block(pl.program_id(0) * BLOCKS_PER_STEP + h, plan, n_used, tok, x_hbm, wg_hbm, wu_hbm,
                   wd_hbm, o_ref.at[rows], xb, wg_st, wu_st, wd_st, wg_bf, wu_bf, wd_bf, sem_x, sem_w)


def _moe_block(j, plan, n_used, tok, x_hbm, wg_hbm, wu_hbm, wd_hbm, o_ref,
               xb, wg_st, wu_st, wd_st, wg_bf, wu_bf, wd_bf, sem_x, sem_w):
    nu = n_used[0]

    def row_copy(src, slot, r):
        t8 = pl.multiple_of(tok[src + r], ROW_TILE)
        return pltpu.make_async_copy(x_hbm.at[pl.ds(t8, ROW_TILE)], xb.at[slot, _row_tile(r)],
                                     sem_x.at[slot])

    def block_wait(slot):
        pltpu.make_async_copy(x_hbm.at[pl.ds(0, BM_MOE * ROW_TILE)], xb.at[slot],
                              sem_x.at[slot]).wait()

    def weight_copies(e, ws):
        return (pltpu.make_async_copy(wg_hbm.at[e], wg_st.at[ws], sem_w.at[ws, 0]),
                pltpu.make_async_copy(wu_hbm.at[e], wu_st.at[ws], sem_w.at[ws, 1]),
                pltpu.make_async_copy(wd_hbm.at[e], wd_st.at[ws], sem_w.at[ws, 2]))

    @pl.when(j == 0)
    def _():
        for cp in weight_copies(plan[PLAN_E, 0], 0):
            cp.start(priority=WEIGHT_DMA_PRIORITY)
        for b in range(GATHER_AHEAD):
            src = plan[PLAN_SRC, b]

            def body(r, carry, src=src, b=b):
                row_copy(src, b, r).start(priority=GATHER_DMA_PRIORITY)
                return carry
            lax.fori_loop(0, BM_MOE, body, 0, unroll=8)

    @pl.when((j < nu) & (plan[PLAN_FIRST, jnp.minimum(j, N_BLK_MOE - 1)] == 1))
    def _():
        e = plan[PLAN_E, j]
        ws = plan[PLAN_WSLOT, j]
        e_next = plan[PLAN_NEXT_E, j]

        @pl.when(e_next != e)
        def _():
            for cp in weight_copies(e_next, 1 - ws):
                cp.start(priority=WEIGHT_DMA_PRIORITY)

        for cp in weight_copies(e, ws):
            cp.wait()
        wg_bf[...] = wg_st[ws].astype(BF16)
        wu_bf[...] = wu_st[ws].astype(BF16)
        wd_bf[...] = wd_st[ws].astype(BF16)

    @pl.when(j < nu)
    def _():
        slot = j % X_SLOTS
        next_slot = (j + GATHER_AHEAD) % X_SLOTS
        block_wait(slot)
        nsrc = plan[PLAN_SRC, jnp.minimum(j + GATHER_AHEAD, N_BLK_MOE - 1)]
        halves = [_load_packed_rows(xb.at[slot], BM_MOE, s) for s in range(ROW_TILE)]
        x = jnp.concatenate([lo for lo, _ in halves] + [hi for _, hi in halves], axis=1).astype(BF16)
        zero = plan[PLAN_ZERO, 0]
        group = BM_MOE // 8
        issued = [0]

        def gather_after(v):
            base = nsrc
            if v is not None:
                base = base + (lax.bitcast_convert_type(v[0:1, 0:1], jnp.int32)[0, 0] & zero)
            for r in range(issued[0], issued[0] + group):
                row_copy(base, next_slot, r).start(priority=GATHER_DMA_PRIORITY)
            issued[0] += group

        gather_after(None)
        gather_after(halves[ROW_TILE - 1][1])
        gu = []
        for c in range(D_EXPERT // MXU_N):
            cols = slice(c * MXU_N, (c + 1) * MXU_N)
            g = jnp.dot(x, wg_bf[:, cols], preferred_element_type=F32)
            gather_after(g)
            u = jnp.dot(x, wu_bf[:, cols], preferred_element_type=F32)
            gather_after(u)
            gu.append((g, u))
        acc = None
        for c, (g, u) in enumerate(gu):
            cols = slice(c * MXU_N, (c + 1) * MXU_N)
            h = (g * _sigmoid(g) * u).astype(BF16)
            part = jnp.dot(h, wd_bf[cols, :], preferred_element_type=F32)
            if c == 0:
                gather_after(part)
                gather_after(part[:, D_MODEL - LANES:])
            acc = part if acc is None else acc + part
        assert issued[0] == BM_MOE
        _store_packed_rows(o_ref, acc)

    @pl.when(j == nu)
    def _():
        for b in range(GATHER_AHEAD):
            block_wait((j + b) % X_SLOTS)

    @pl.when(j >= nu)
    def _():
        o_ref[...] = jnp.zeros_like(o_ref)


def _moe(plan, n_used, tok_sorted, xn_packed, w_g, w_u, w_d):
    grid_spec = pltpu.PrefetchScalarGridSpec(
        num_scalar_prefetch=3,
        grid=(N_STEPS_MOE,),
        in_specs=[pl.BlockSpec(memory_space=pl.ANY)] * 4,
        out_specs=pl.BlockSpec((BLOCKS_PER_STEP * BM_MOE * ROW_TILE, LANES),
                               lambda j, pn, nu, tk: (j, 0)),
        scratch_shapes=[
            pltpu.VMEM((X_SLOTS, BM_MOE * ROW_TILE, LANES), U32),
            pltpu.VMEM((2, D_MODEL, D_EXPERT), F32),
            pltpu.VMEM((2, D_MODEL, D_EXPERT), F32),
            pltpu.VMEM((2, D_EXPERT, D_MODEL), F32),
            pltpu.VMEM((D_MODEL, D_EXPERT), BF16),
            pltpu.VMEM((D_MODEL, D_EXPERT), BF16),
            pltpu.VMEM((D_EXPERT, D_MODEL), BF16),
            pltpu.SemaphoreType.DMA((X_SLOTS,)),
            pltpu.SemaphoreType.DMA((2, 3)),
        ],
    )
    return pl.pallas_call(
        _moe_body,
        name="experts",
        grid_spec=grid_spec,
        out_shape=jax.ShapeDtypeStruct(
            (N_STEPS_MOE * BLOCKS_PER_STEP * BM_MOE * ROW_TILE, LANES), U32),
        compiler_params=_params(1),
    )(plan, n_used, tok_sorted, xn_packed, w_g, w_u, w_d)


def _combine_body(pos, x1_ref, w_ref, gf_ref, o_hbm, y_ref, rb, sem, *, first_tile):
    m = pl.program_id(0)
    last = pl.num_programs(0) - 1

    def row_copy(base, slot, r, k):
        p8 = pl.multiple_of(pos[k * T + base + r], ROW_TILE)
        return pltpu.make_async_copy(o_hbm.at[pl.ds(p8, ROW_TILE)], rb.at[slot, k, _row_tile(r)],
                                     sem.at[slot])

    def tile_wait(slot):
        for k in range(TOP_K):
            pltpu.make_async_copy(o_hbm.at[pl.ds(0, TM_OUT * ROW_TILE)], rb.at[slot, k],
                                  sem.at[slot]).wait()

    def tile_base(mm):
        return (first_tile + mm) * TM_OUT

    @pl.when(m == 0)
    def _():
        for b in range(GATHER_AHEAD):
            base = tile_base(jnp.minimum(b, last))

            def body(r, carry, base=base, b=b):
                for k in range(TOP_K):
                    row_copy(base, b, r, k).start(priority=k)
                return carry
            lax.fori_loop(0, TM_OUT, body, 0, unroll=4)

    slot = m % X_SLOTS
    next_slot = (m + GATHER_AHEAD) % X_SLOTS
    tile_wait(slot)
    nbase = tile_base(jnp.minimum(m + GATHER_AHEAD, last))
    rows = TM_OUT // ROW_TILE
    w0 = w_ref[:, 0:1]
    w1 = w_ref[:, 1:2]
    ss = jnp.zeros((TM_OUT, 1), F32)
    for s in range(ROW_TILE):
        for r in range(s * rows, (s + 1) * rows):
            for k in range(TOP_K):
                row_copy(nbase, next_slot, r, k).start(priority=k)
        lo0, hi0 = _load_packed_rows(rb.at[slot, 0], TM_OUT, s)
        lo1, hi1 = _load_packed_rows(rb.at[slot, 1], TM_OUT, s)
        for off, r0, r1 in ((s * LANES, lo0, lo1), (HALF_D + s * LANES, hi0, hi1)):
            y = x1_ref[:, off:off + LANES] + w0 * r0 + w1 * r1
            ss = ss + jnp.sum(y * y, axis=-1, keepdims=True)
            y_ref[:, off:off + LANES] = y
    scale = lax.rsqrt(ss * (1.0 / D_MODEL) + EPS)
    y_ref[...] = y_ref[...] * scale * gf_ref[...]

    @pl.when(m == last)
    def _():
        for b in range(1, GATHER_AHEAD + 1):
            tile_wait((m + b) % X_SLOTS)


def _combine(pos, x1, wts, gf, out_sorted, first_tile, n_tiles):
    grid_spec = pltpu.PrefetchScalarGridSpec(
        num_scalar_prefetch=1,
        grid=(n_tiles,),
        in_specs=[
            pl.BlockSpec((TM_OUT, D_MODEL), lambda m, p: (first_tile + m, 0)),
            pl.BlockSpec((TM_OUT, ROUTER_PAD), lambda m, p: (first_tile + m, 0)),
            pl.BlockSpec((1, D_MODEL), lambda m, p: (0, 0)),
            pl.BlockSpec(memory_space=pl.ANY),
        ],
        out_specs=pl.BlockSpec((TM_OUT, D_MODEL), lambda m, p: (m, 0)),
        scratch_shapes=[
            pltpu.VMEM((X_SLOTS, TOP_K, TM_OUT * ROW_TILE, LANES), U32),
            pltpu.SemaphoreType.DMA((X_SLOTS,)),
        ],
    )
    return pl.pallas_call(
        functools.partial(_combine_body, first_tile=first_tile),
        name="combine",
        grid_spec=grid_spec,
        out_shape=jax.ShapeDtypeStruct((n_tiles * TM_OUT, D_MODEL), F32),
        compiler_params=_params(1),
    )(pos, x1, wts, gf, out_sorted)


def kernel(x_prompt, x_sample, cache_k, cache_v, state_conv, norm1_g, w_in, b_in, attn_sinks,
           w_attn_o, conv_dw, conv_dw_b, conv_ln_g, conv_ln_b, w_conv_o, w_out, norm2_g,
           w_router_group, b_router_group, w_router_expert, b_router_expert, w_e_gate, w_e_up,
           w_e_down, final_norm_g):
    x_p = x_prompt.reshape(T_P, D_MODEL)
    x_s = x_sample.reshape(T_S, D_MODEL)
    g1 = norm1_g[0][None, :]
    b1 = b_in[0][None, :]
    conv_consts = (conv_dw[0], conv_dw_b[0][None, :], conv_ln_g[0][None, :], conv_ln_b[0][None, :])
    hist_pad = jnp.pad(state_conv[0], ((0, 0), (HIST_PAD - HIST, 0), (0, 0)))
    q, kv, u, gates_p, c_p, gates_s, c_s = _in_proj(x_p, x_s, hist_pad, g1, w_in[0], b1, conv_consts)

    sinks = attn_sinks[0]
    a_p = _attn_prompt(sinks, q, kv)
    ck = cache_k[0].reshape(N_STREAMS, WINDOW, KV_WIDTH)
    cv = cache_v[0].reshape(N_STREAMS, WINDOW, KV_WIDTH)
    a_s = _attn_sample(sinks, q, kv, ck, cv)

    w_r = jnp.concatenate([w_router_group[0], w_router_expert[0]], axis=1)
    w_r_hi = w_r.astype(BF16)
    w_r_lo = (w_r - w_r_hi.astype(F32)).astype(BF16)
    w_r_cat = jnp.concatenate(
        [w_r_hi, w_r_lo, jnp.zeros((D_MODEL, ROUTER_PAD - 2 * N_ROUTER), BF16)], axis=1)
    b_r = jnp.concatenate([b_router_group[0], b_router_expert[0],
                           jnp.zeros((ROUTER_PAD - N_ROUTER,), F32)])[None, :]

    x1, xn2, logits = _mix(a_p, a_s, c_p, c_s, gates_p, gates_s, x_p, x_s,
                           w_attn_o[0], w_conv_o[0], w_out[0],
                           norm2_g[0][None, :], w_r_cat, b_r)

    wts, pos, tok_sorted, plan, n_used = _route(logits)
    out_sorted = _moe(plan, n_used, tok_sorted, xn2, w_e_gate[0], w_e_up[0], w_e_down[0])
    gf = final_norm_g[None, :]
    y_p = _combine(pos, x1, wts, gf, out_sorted, 0, T_P // TM_OUT)
    y_s = _combine(pos, x1, wts, gf, out_sorted, T_P // TM_OUT, T_S // TM_OUT)

    kv_shape = (1, -1, WINDOW, N_KV_HEADS, HEAD_DIM)
    new_k_prompt = kv[T_P - WINDOW:T_P, :KV_WIDTH].reshape(kv_shape)
    new_v_prompt = kv[T_P - WINDOW:T_P, KV_WIDTH:].reshape(kv_shape)
    new_conv_prompt = u[T_P - HIST:T_P].reshape(1, 1, HIST, CONV_CH)
    k_s = kv[T_P:, :KV_WIDTH].reshape(N_STREAMS, L_S, KV_WIDTH)
    v_s = kv[T_P:, KV_WIDTH:].reshape(N_STREAMS, L_S, KV_WIDTH)
    new_k_sample = jnp.concatenate([ck[:, L_S:], k_s], axis=1).reshape(kv_shape)
    new_v_sample = jnp.concatenate([cv[:, L_S:], v_s], axis=1).reshape(kv_shape)
    new_conv_sample = u[T_P:].reshape(N_STREAMS, L_S, CONV_CH)[:, L_S - HIST:].reshape(
        1, N_STREAMS, HIST, CONV_CH)

    return (y_p.reshape(1, T_P, D_MODEL), y_s.reshape(N_STREAMS, L_S, D_MODEL),
            new_k_prompt, new_v_prompt, new_conv_prompt,
            new_k_sample, new_v_sample, new_conv_sample)
```

```python
import functools

import numpy as np
import jax
import jax.numpy as jnp
from jax import lax
from jax.experimental import pallas as pl
from jax.experimental.pallas import tpu as pltpu

F32 = jnp.float32
BF16 = jnp.bfloat16
U32 = jnp.uint32

D_MODEL = 2048
T_P = 8192
N_STREAMS = 8
L_S = 64
T_S = N_STREAMS * L_S
T = T_P + T_S
CHUNK = 64
WINDOW = 128
HEAD_DIM = 64
N_Q_HEADS = 16
N_KV_HEADS = 4
GROUP = N_Q_HEADS // N_KV_HEADS
ATTN_WIDTH = N_Q_HEADS * HEAD_DIM
KV_WIDTH = N_KV_HEADS * HEAD_DIM
CONV_CH = 1024
CONV_WIDTH = 31
HIST = CONV_WIDTH - 1
HIST_PAD = 32
SUBLANES = 8
LANES = 128
ROW_TILE = SUBLANES
HALF_D = D_MODEL // 2
HI_MASK = np.uint32(0xFFFF0000)
OFF_KV = ATTN_WIDTH
OFF_GLU = OFF_KV + 2 * KV_WIDTH
OFF_GATES = OFF_GLU + 2 * CONV_CH
IN_WIDTH = OFF_GATES + 2 * D_MODEL
N_GROUPS = 8
EXPERTS_PER_GROUP = 4
N_EXPERTS = N_GROUPS * EXPERTS_PER_GROUP
TOP_K = 2
D_EXPERT = 512
N_ROUTER = N_GROUPS + N_EXPERTS
ROUTER_PAD = 128
EPS = 1e-6
NEG = -1e30

VMEM_LIMIT = 56 * 1024 * 1024
MXU_N = 256

TM_WIDE = 2048
TM_GLU = T // 4
TN_IN = 512
TN_GATES = TN_IN
N_GATE_TILES = 2 * D_MODEL // TN_GATES
ROW_CHUNK = 256
W_STAGE_ROWS = 256
TQ_ATTN = 512
PAIR = 2 * CHUNK
BF16_ROWS = 16
SUB_CONV = 32
PIECE_CONV = 32
CONV_LAG = 1
TM_MIX = 256
BM_MOE = 256
R_ROUTE = 256
TM_OUT = 256
N_SLOTS = T * TOP_K
N_BLK_MOE = -(-(N_SLOTS + N_EXPERTS * (BM_MOE - 1)) // BM_MOE)


def _sigmoid(x):
    return 1.0 / (1.0 + jnp.exp(-x))


def _params(n_axes):
    return pltpu.CompilerParams(dimension_semantics=("arbitrary",) * n_axes,
                                vmem_limit_bytes=VMEM_LIMIT)


def _store_packed_rows(ref, y):
    rows = y.shape[0]
    for s in range(ROW_TILE):
        lo = y[:, s * LANES:(s + 1) * LANES].astype(BF16).astype(F32)
        hi = y[:, HALF_D + s * LANES:HALF_D + (s + 1) * LANES].astype(BF16).astype(F32)
        word = (lax.bitcast_convert_type(hi, U32) & HI_MASK) | (lax.bitcast_convert_type(lo, U32) >> 16)
        ref[pl.ds(s, rows, stride=ROW_TILE), :] = word


def _row_tile(r):
    start = r * ROW_TILE
    return pl.ds(start if isinstance(r, int) else pl.multiple_of(start, ROW_TILE), ROW_TILE)


def _load_packed_rows(ref, rows, s):
    word = ref[pl.ds(s, rows, stride=ROW_TILE), :]
    lo = lax.bitcast_convert_type(word << 16, F32)
    hi = lax.bitcast_convert_type(word & HI_MASK, F32)
    return lo, hi


def _chunk_dot(xn, w_ref, b_ref, c):
    w = w_ref[:, c * MXU_N:(c + 1) * MXU_N].astype(BF16)
    return jnp.dot(xn, w, preferred_element_type=F32) + b_ref[:, c * MXU_N:(c + 1) * MXU_N]


def _stage_weights(jobs, stage, sem):
    width = stage.shape[-1]
    chunks = [(w_hbm, w_bf, r0) for w_hbm, w_bf in jobs
              for r0 in range(0, w_bf.shape[0], W_STAGE_ROWS)]

    def chunk_copy(i):
        w_hbm, _, r0 = chunks[i]
        return pltpu.make_async_copy(w_hbm.at[pl.ds(r0, W_STAGE_ROWS), pl.ds(0, width)],
                                     stage.at[i % 2], sem.at[i % 2])
    chunk_copy(0).start()
    for i, (_, w_bf, r0) in enumerate(chunks):
        if i + 1 < len(chunks):
            chunk_copy(i + 1).start()
        chunk_copy(i).wait()
        w_bf[r0:r0 + W_STAGE_ROWS, :] = stage[i % 2].astype(BF16)


def _qkv_body(xp_ref, xs_ref, g_ref, w_hbm, b_ref, xn_ref, q_ref, kv_ref, w_bf, stage, sem, *, tm):
    m = pl.program_id(0)

    @pl.when(m == 0)
    def _():
        _stage_weights([(w_hbm, w_bf)], stage, sem)

    is_sample = m >= T_P // tm
    for r0 in range(0, tm, ROW_CHUNK):
        rows = slice(r0, r0 + ROW_CHUNK)
        x = jnp.where(is_sample, xs_ref[rows, :], xp_ref[rows, :])
        ms = jnp.mean(x * x, axis=-1, keepdims=True)
        xn = (x * lax.rsqrt(ms + EPS) * g_ref[...]).astype(BF16)
        xn_ref[rows, :] = xn
        for c in range(OFF_GLU // MXU_N):
            cols = slice(c * MXU_N, (c + 1) * MXU_N)
            acc = jnp.dot(xn, w_bf[:, cols], preferred_element_type=F32) + b_ref[:, cols]
            if c < ATTN_WIDTH // MXU_N:
                q_ref[rows, cols] = (acc * (HEAD_DIM ** -0.5)).astype(BF16)
            else:
                kv_ref[rows, c * MXU_N - ATTN_WIDTH:(c + 1) * MXU_N - ATTN_WIDTH] = acc


def _qkv(x_p, x_s, norm_g, w_in, b_in):
    tm = T_S
    t = T
    n_p = T_P // tm
    return pl.pallas_call(
        functools.partial(_qkv_body, tm=tm),
        name="qkv",
        grid=(t // tm,),
        in_specs=[
            pl.BlockSpec((tm, D_MODEL), lambda m: (jnp.minimum(m, n_p - 1), 0)),
            pl.BlockSpec((tm, D_MODEL), lambda m: (0, 0)),
            pl.BlockSpec((1, D_MODEL), lambda m: (0, 0)),
            pl.BlockSpec(memory_space=pl.ANY),
            pl.BlockSpec((1, OFF_GLU), lambda m: (0, 0)),
        ],
        out_specs=[
            pl.BlockSpec((tm, D_MODEL), lambda m: (m, 0)),
            pl.BlockSpec((tm, ATTN_WIDTH), lambda m: (m, 0)),
            pl.BlockSpec((tm, 2 * KV_WIDTH), lambda m: (m, 0)),
        ],
        out_shape=[
            jax.ShapeDtypeStruct((t, D_MODEL), BF16),
            jax.ShapeDtypeStruct((t, ATTN_WIDTH), BF16),
            jax.ShapeDtypeStruct((t, 2 * KV_WIDTH), F32),
        ],
        scratch_shapes=[
            pltpu.VMEM((D_MODEL, OFF_GLU), BF16),
            pltpu.VMEM((2, W_STAGE_ROWS, OFF_GLU), F32),
            pltpu.SemaphoreType.DMA((2,)),
        ],
        compiler_params=_params(1),
    )(x_p, x_s, norm_g, w_in, b_in)


def _glu_body(xn_ref, wa_ref, ba_ref, wb_ref, bb_ref, u_ref):
    xn = xn_ref[...]
    for c in range(TN_IN // MXU_N):
        a = _chunk_dot(xn, wa_ref, ba_ref, c)
        b = _chunk_dot(xn, wb_ref, bb_ref, c)
        u_ref[:, c * MXU_N:(c + 1) * MXU_N] = a * _sigmoid(b)


def _glu(xn, w_in, b_in, tm):
    t = xn.shape[0]
    a0 = OFF_GLU // TN_IN
    b0 = (OFF_GLU + CONV_CH) // TN_IN
    return pl.pallas_call(
        _glu_body,
        name="glu",
        grid=(t // tm, CONV_CH // TN_IN),
        in_specs=[
            pl.BlockSpec((tm, D_MODEL), lambda m, n: (m, 0)),
            pl.BlockSpec((D_MODEL, TN_IN), lambda m, n: (0, a0 + n)),
            pl.BlockSpec((1, TN_IN), lambda m, n: (0, a0 + n)),
            pl.BlockSpec((D_MODEL, TN_IN), lambda m, n: (0, b0 + n)),
            pl.BlockSpec((1, TN_IN), lambda m, n: (0, b0 + n)),
        ],
        out_specs=pl.BlockSpec((tm, TN_IN), lambda m, n: (m, n)),
        out_shape=jax.ShapeDtypeStruct((t, CONV_CH), F32),
        compiler_params=_params(2),
    )(xn, w_in, b_in, w_in, b_in)


def _attn_prompt_body(sink_ref, q_ref, kvp_ref, kvc_ref, a_ref, at_sc):
    i = pl.program_id(0)
    kv = jnp.concatenate([kvp_ref[...], kvc_ref[...]], axis=0)
    k = kv[:, :KV_WIDTH].astype(BF16)
    vt = kv[:, KV_WIDTH:].T.astype(BF16)
    qt = q_ref[...].astype(F32).T.astype(BF16)
    n_cols = GROUP * PAIR
    n_keys = WINDOW + PAIR
    row = lax.broadcasted_iota(jnp.int32, (n_keys, n_cols), 0)
    col = lax.broadcasted_iota(jnp.int32, (n_keys, n_cols), 1)
    first = jnp.where((col & (PAIR - 1)) >= CHUNK, CHUNK, 0)
    gcol = lax.shift_right_logical(lax.broadcasted_iota(jnp.int32, (1, n_cols), 1),
                                   PAIR.bit_length() - 1)
    biases = []
    for p in range(TQ_ATTN // PAIR):
        pos = row + (i * TQ_ATTN + p * PAIR - WINDOW)
        ok = (row >= first) & (row < first + WINDOW + CHUNK) & (pos >= 0)
        biases.append(jnp.where(ok, 0.0, NEG))

    def scores(p, h):
        w0 = p * PAIR
        kh = k[w0:w0 + n_keys, h * HEAD_DIM:(h + 1) * HEAD_DIM]
        rhs = jnp.concatenate(
            [qt[(h * GROUP + g) * HEAD_DIM:(h * GROUP + g + 1) * HEAD_DIM, w0:w0 + PAIR]
             for g in range(GROUP)], axis=1)
        return jnp.dot(kh, rhs, preferred_element_type=F32) + biases[p]

    def finish(p, h, st):
        w0 = p * PAIR
        sink = jnp.full((1, n_cols), sink_ref[h * GROUP], F32)
        for g in range(1, GROUP):
            sink = jnp.where(gcol == g, sink_ref[h * GROUP + g], sink)
        m = jnp.maximum(jnp.max(st, axis=0, keepdims=True), sink)
        pt = jnp.exp(st - m)
        denom = jnp.sum(pt, axis=0, keepdims=True) + jnp.exp(sink - m)
        ot = jnp.dot(vt[h * HEAD_DIM:(h + 1) * HEAD_DIM, w0:w0 + n_keys], pt.astype(BF16),
                     preferred_element_type=F32) / denom
        for g in range(GROUP):
            r0 = (h * GROUP + g) * HEAD_DIM
            at_sc[r0:r0 + HEAD_DIM, w0:w0 + PAIR] = ot[:, g * PAIR:(g + 1) * PAIR]

    items = [(p, h) for p in range(TQ_ATTN // PAIR) for h in range(N_KV_HEADS)]
    st_next = scores(*items[0])
    for n, (p, h) in enumerate(items):
        st = st_next
        if n + 1 < len(items):
            st_next = scores(*items[n + 1])
        finish(p, h, st)
    a_ref[...] = at_sc[...].T.astype(BF16)


def _attn_prompt(sinks, q, kv):
    r = TQ_ATTN // WINDOW
    return pl.pallas_call(
        _attn_prompt_body,
        name="attn_prompt",
        grid=(T_P // TQ_ATTN,),
        in_specs=[
            pl.BlockSpec(memory_space=pltpu.SMEM),
            pl.BlockSpec((TQ_ATTN, ATTN_WIDTH), lambda i: (i, 0)),
            pl.BlockSpec((WINDOW, 2 * KV_WIDTH), lambda i: (jnp.maximum(i * r - 1, 0), 0)),
            pl.BlockSpec((TQ_ATTN, 2 * KV_WIDTH), lambda i: (i, 0)),
        ],
        out_specs=pl.BlockSpec((TQ_ATTN, ATTN_WIDTH), lambda i: (i, 0)),
        out_shape=jax.ShapeDtypeStruct((T_P, ATTN_WIDTH), BF16),
        scratch_shapes=[pltpu.VMEM((ATTN_WIDTH, TQ_ATTN), F32)],
        compiler_params=_params(1),
    )(sinks, q, kv, kv)


def _attn_sample_body(sink_ref, q_ref, kvc_ref, ck_ref, cv_ref, a_ref):
    kvc = kvc_ref[...]
    kw = jnp.concatenate([ck_ref[0], kvc[:, :KV_WIDTH]], axis=0).astype(BF16)
    vw = jnp.concatenate([cv_ref[0], kvc[:, KV_WIDTH:]], axis=0).astype(BF16)
    qc = q_ref[...]
    for h in range(N_KV_HEADS):
        kh = kw[:, h * HEAD_DIM:(h + 1) * HEAD_DIM]
        vh = vw[:, h * HEAD_DIM:(h + 1) * HEAD_DIM]
        qg = jnp.concatenate(
            [qc[:, (h * GROUP + g) * HEAD_DIM:(h * GROUP + g + 1) * HEAD_DIM] for g in range(GROUP)],
            axis=0)
        s = lax.dot_general(qg, kh, (((1,), (1,)), ((), ())), preferred_element_type=F32)
        sink = jnp.concatenate(
            [jnp.full((L_S, 1), sink_ref[h * GROUP + g], F32) for g in range(GROUP)], axis=0)
        m = jnp.maximum(jnp.max(s, axis=1, keepdims=True), sink)
        p = jnp.exp(s - m)
        denom = jnp.sum(p, axis=1, keepdims=True) + jnp.exp(sink - m)
        o = jnp.dot(p.astype(BF16), vh, preferred_element_type=F32) / denom
        for g in range(GROUP):
            c0 = (h * GROUP + g) * HEAD_DIM
            a_ref[:, c0:c0 + HEAD_DIM] = o[g * L_S:(g + 1) * L_S].astype(BF16)


def _attn_sample(sinks, q, kv, cache_k, cache_v):
    first = T_P // L_S
    return pl.pallas_call(
        _attn_sample_body,
        name="attn_sample",
        grid=(N_STREAMS,),
        in_specs=[
            pl.BlockSpec(memory_space=pltpu.SMEM),
            pl.BlockSpec((L_S, ATTN_WIDTH), lambda b: (first + b, 0)),
            pl.BlockSpec((L_S, 2 * KV_WIDTH), lambda b: (first + b, 0)),
            pl.BlockSpec((1, WINDOW, KV_WIDTH), lambda b: (b, 0, 0)),
            pl.BlockSpec((1, WINDOW, KV_WIDTH), lambda b: (b, 0, 0)),
        ],
        out_specs=pl.BlockSpec((L_S, ATTN_WIDTH), lambda b: (b, 0)),
        out_shape=jax.ShapeDtypeStruct((T_S, ATTN_WIDTH), BF16),
        compiler_params=_params(1),
    )(sinks, q, kv, cache_k, cache_v)


def _conv_shifts(prev, cur_ref, ext_sc, sh_sc, rows):
    ext_sc[0:HIST_PAD, :] = prev
    ext_sc[HIST_PAD:HIST_PAD + rows, :] = cur_ref[...]
    ext_sc[HIST_PAD + rows:, :] = jnp.zeros((SUBLANES, CONV_CH), F32)
    lead = HIST_PAD - HIST
    span = rows + HIST_PAD - SUBLANES
    for s in range(SUBLANES):
        sh_sc[s, 0:span, :] = ext_sc[lead + s:lead + s + span, :]


def _conv_rows(r0, w_ref, b_ref, lg_ref, lb_ref, o_ref, sh_sc):
    acc = jnp.zeros((SUB_CONV, CONV_CH), F32) + b_ref[...]
    for k in range(CONV_WIDTH):
        a, s = divmod(k, SUBLANES)
        r = r0 + a * SUBLANES
        acc = acc + w_ref[k:k + 1, :] * sh_sc[s, r:r + SUB_CONV, :]
    mu = jnp.mean(acc, axis=-1, keepdims=True)
    d = acc - mu
    var = jnp.mean(d * d, axis=-1, keepdims=True)
    yn = d * lax.rsqrt(var + EPS) * lg_ref[...] + lb_ref[...]
    out = yn * _sigmoid(yn)
    o_ref[r0:r0 + SUB_CONV, :] = out.astype(BF16)
    return out


def _gates_conv_body(xn_ref, w_ref, b_ref, prev_ref, cur_ref, cw_ref, cb_ref, lg_ref, lb_ref, zero_ref,
                     gate_ref, c_ref, ext_sc, sh_sc, *, rows, fresh):
    if fresh:
        step = pl.program_id(0) * N_GATE_TILES + pl.program_id(1)
        prev = jnp.where(step > 0, prev_ref[...], 0.0)
    else:
        prev = prev_ref[0]
    _conv_shifts(prev, cur_ref, ext_sc, sh_sc, rows)
    n_chunks = TN_GATES // MXU_N
    w = [w_ref[:, c * MXU_N:(c + 1) * MXU_N].astype(BF16) for c in range(n_chunks)]
    zeros = []
    for i in range(rows // PIECE_CONV):
        r = slice(i * ROW_CHUNK, (i + 1) * ROW_CHUNK)
        xr = xn_ref[r, :]
        if i >= CONV_LAG:
            zero16 = zeros[i - CONV_LAG]
            top = jnp.concatenate([xr[0:BF16_ROWS, 0:LANES] + zero16, xr[0:BF16_ROWS, LANES:]], axis=1)
            xr = jnp.concatenate([top, xr[BF16_ROWS:, :]], axis=0)
        accs = [jnp.dot(xr, w[c], preferred_element_type=F32) for c in range(n_chunks)]
        outs = [_conv_rows(r0, cw_ref, cb_ref, lg_ref, lb_ref, c_ref, sh_sc)
                for r0 in range(i * PIECE_CONV, (i + 1) * PIECE_CONV, SUB_CONV)]
        bits = [lax.bitcast_convert_type(o[g * SUBLANES:(g + 1) * SUBLANES, 0:LANES], jnp.int32)
                for o in outs for g in range(SUB_CONV // SUBLANES)]
        zero8 = lax.bitcast_convert_type(functools.reduce(jnp.bitwise_or, bits) & zero_ref[...], F32)
        zeros.append(jnp.concatenate([zero8, zero8], axis=0).astype(BF16))
        for c in range(n_chunks):
            cols = slice(c * MXU_N, (c + 1) * MXU_N)
            gate_ref[r, cols] = _sigmoid(accs[c] + b_ref[:, cols]).astype(BF16)


def _gates_conv(xn, w_in, b_in, hist, u, conv_consts, tm, rows, fresh):
    t = T_P if fresh else T_S
    xn_tile0 = 0 if fresh else T_P // tm
    u_tile0 = 0 if fresh else T_P // rows
    n0 = OFF_GATES // TN_GATES
    tile = lambda m, n: m * N_GATE_TILES + n
    if fresh:
        r = rows // HIST_PAD
        prev_spec = pl.BlockSpec((HIST_PAD, CONV_CH), lambda m, n: (jnp.maximum(tile(m, n) * r - 1, 0), 0))
    else:
        prev_spec = pl.BlockSpec((1, HIST_PAD, CONV_CH), lambda m, n: (tile(m, n), 0, 0))
    const_map = lambda m, n: (0, 0)
    return pl.pallas_call(
        functools.partial(_gates_conv_body, rows=rows, fresh=fresh),
        name="gates_conv",
        grid=(t // tm, N_GATE_TILES),
        in_specs=[
            pl.BlockSpec((tm, D_MODEL), lambda m, n: (xn_tile0 + m, 0)),
            pl.BlockSpec((D_MODEL, TN_GATES), lambda m, n: (0, n0 + n)),
            pl.BlockSpec((1, TN_GATES), lambda m, n: (0, n0 + n)),
            prev_spec,
            pl.BlockSpec((rows, CONV_CH), lambda m, n: (u_tile0 + tile(m, n), 0)),
            pl.BlockSpec((CONV_WIDTH, CONV_CH), const_map),
            pl.BlockSpec((1, CONV_CH), const_map),
            pl.BlockSpec((1, CONV_CH), const_map),
            pl.BlockSpec((1, CONV_CH), const_map),
            pl.BlockSpec((SUBLANES, LANES), const_map),
        ],
        out_specs=[
            pl.BlockSpec((tm, TN_GATES), lambda m, n: (m, n)),
            pl.BlockSpec((rows, CONV_CH), lambda m, n: (tile(m, n), 0)),
        ],
        out_shape=[
            jax.ShapeDtypeStruct((t, 2 * D_MODEL), BF16),
            jax.ShapeDtypeStruct((t, CONV_CH), BF16),
        ],
        scratch_shapes=[
            pltpu.VMEM((HIST_PAD + rows + SUBLANES, CONV_CH), F32),
            pltpu.VMEM((SUBLANES, HIST_PAD + rows, CONV_CH), F32),
        ],
        compiler_params=_params(2),
    )(xn, w_in, b_in, hist, u, *conv_consts, jnp.zeros((SUBLANES, LANES), jnp.int32))


def _in_proj(x_p, x_s, hist, norm_g, w_in, b_in, conv_consts):
    xn, q, kv = _qkv(x_p, x_s, norm_g, w_in, b_in)
    u = _glu(xn, w_in, b_in, TM_GLU)
    gates_p, c_p = _gates_conv(xn, w_in, b_in, u, u, conv_consts, TM_WIDE,
                               TM_WIDE // N_GATE_TILES, True)
    gates_s, c_s = _gates_conv(xn, w_in, b_in, hist, u, conv_consts, T_S,
                               T_S // N_GATE_TILES, False)
    return q, kv, u, gates_p, c_p, gates_s, c_s


def _mix_body(ap_ref, as_ref, cp_ref, cs_ref, gp_ref, gs_ref, xp_ref, xs_ref, wao_hbm, wco_hbm,
              wout_hbm, g2_ref, wr_ref, br_ref, x1_ref, xn_ref, lg_ref,
              wao_ref, wco_ref, wout_ref, stage, sem):
    m = pl.program_id(0)

    @pl.when(m == 0)
    def _():
        _stage_weights([(wao_hbm, wao_ref), (wco_hbm, wco_ref), (wout_hbm, wout_ref)], stage, sem)

    is_sample = m >= T_P // TM_MIX
    a = jnp.where(is_sample, as_ref[...], ap_ref[...])
    c = jnp.where(is_sample, cs_ref[...], cp_ref[...])
    gate = jnp.where(is_sample, gs_ref[...], gp_ref[...])
    x = jnp.where(is_sample, xs_ref[...], xp_ref[...])
    pa = jnp.dot(a, wao_ref[...], preferred_element_type=F32)
    pc = jnp.dot(c, wco_ref[...], preferred_element_type=F32)
    y = (gate[:, :D_MODEL].astype(F32) * pa + gate[:, D_MODEL:].astype(F32) * pc).astype(BF16)
    x1 = x + jnp.dot(y, wout_ref[...], preferred_element_type=F32)
    x1_ref[...] = x1
    ms = jnp.mean(x1 * x1, axis=-1, keepdims=True)
    xn = x1 * lax.rsqrt(ms + EPS) * g2_ref[...]
    _store_packed_rows(xn_ref, xn)
    r = jnp.dot(xn.astype(BF16), wr_ref[...], preferred_element_type=F32)
    lg_ref[...] = r + pltpu.roll(r, ROUTER_PAD - N_ROUTER, axis=1) + br_ref[...]


def _mix(a_p, a_s, c_p, c_s, g_p, g_s, x_p, x_s, wao, wco, wout, g2, wr, br):
    n_p = T_P // TM_MIX
    n_s = T_S // TM_MIX
    prompt_map = lambda m: (jnp.minimum(m, n_p - 1), 0)
    sample_map = lambda m: (jnp.clip(m - n_p, 0, n_s - 1), 0)
    row_map = lambda m: (m, 0)
    const_map = lambda m: (0, 0)

    def pair(width):
        return [pl.BlockSpec((TM_MIX, width), prompt_map), pl.BlockSpec((TM_MIX, width), sample_map)]

    return pl.pallas_call(
        _mix_body,
        name="mix",
        grid=(T // TM_MIX,),
        in_specs=pair(ATTN_WIDTH) + pair(CONV_CH) + pair(2 * D_MODEL) + pair(D_MODEL) + [
            pl.BlockSpec(memory_space=pl.ANY),
            pl.BlockSpec(memory_space=pl.ANY),
            pl.BlockSpec(memory_space=pl.ANY),
            pl.BlockSpec((1, D_MODEL), const_map),
            pl.BlockSpec((D_MODEL, ROUTER_PAD), const_map),
            pl.BlockSpec((1, ROUTER_PAD), const_map),
        ],
        out_specs=[
            pl.BlockSpec((TM_MIX, D_MODEL), row_map),
            pl.BlockSpec((TM_MIX * ROW_TILE, LANES), row_map),
            pl.BlockSpec((TM_MIX, ROUTER_PAD), row_map),
        ],
        out_shape=[
            jax.ShapeDtypeStruct((T, D_MODEL), F32),
            jax.ShapeDtypeStruct((T * ROW_TILE, LANES), U32),
            jax.ShapeDtypeStruct((T, ROUTER_PAD), F32),
        ],
        scratch_shapes=[
            pltpu.VMEM((ATTN_WIDTH, D_MODEL), BF16),
            pltpu.VMEM((CONV_CH, D_MODEL), BF16),
            pltpu.VMEM((D_MODEL, D_MODEL), BF16),
            pltpu.VMEM((2, W_STAGE_ROWS, D_MODEL), F32),
            pltpu.SemaphoreType.DMA((2,)),
        ],
        compiler_params=_params(1),
    )(a_p, a_s, c_p, c_s, g_p, g_s, x_p, x_s, wao, wco, wout, g2, wr, br)


REC_W, REC_E, REC_POS = 0, 2, 4


PLAN_E, PLAN_SRC, PLAN_FIRST, PLAN_WSLOT, PLAN_NEXT_E, PLAN_ZERO, PLAN_VALID = range(7)
N_PLAN_ROWS = 7


def _write_plan(cnt, plan_ref, nu_ref):
    shift = BM_MOE.bit_length() - 1
    nblk = [(c + (BM_MOE - 1)) >> shift for c in cnt]
    nxt = [None] * N_EXPERTS
    later = jnp.int32(-1)
    for e in reversed(range(N_EXPERTS)):
        nxt[e] = jnp.where(later >= 0, later, e)
        later = jnp.where(nblk[e] > 0, e, later)

    def clear(j, carry):
        for row in range(N_PLAN_ROWS):
            plan_ref[row, j] = 0
        return carry
    lax.fori_loop(0, N_BLK_MOE, clear, 0)

    first_blocks = []
    j0 = jnp.int32(0)
    row0 = jnp.int32(0)
    n_before = jnp.int32(0)
    for e in range(N_EXPERTS):
        def fill(b, carry, e=e, j0=j0, row0=row0, n_before=n_before):
            plan_ref[PLAN_E, j0 + b] = e
            plan_ref[PLAN_SRC, j0 + b] = row0 + b * BM_MOE
            plan_ref[PLAN_FIRST, j0 + b] = jnp.where(b == 0, 1, 0)
            plan_ref[PLAN_WSLOT, j0 + b] = n_before & 1
            plan_ref[PLAN_NEXT_E, j0 + b] = nxt[e]
            plan_ref[PLAN_VALID, j0 + b] = jnp.minimum(cnt[e] - b * BM_MOE, BM_MOE)
            return carry
        lax.fori_loop(0, nblk[e], fill, 0)
        first_blocks.append(j0)
        j0 = j0 + nblk[e]
        row0 = row0 + cnt[e]
        n_before = n_before + jnp.where(nblk[e] > 0, 1, 0)
    nu_ref[0] = j0
    return first_blocks


def _router_body(lg_ref, rec_ref, idx_ref, plan_ref, nu_ref, carry_sc, rect_sc):
    n_chunks = T // R_ROUTE
    grow = lax.broadcasted_iota(jnp.int32, (N_GROUPS, R_ROUTE), 0).astype(F32)
    erow_i = lax.broadcasted_iota(jnp.int32, (N_EXPERTS, R_ROUTE), 0)
    erow = erow_i.astype(F32)
    egrp = (erow_i >> (EXPERTS_PER_GROUP.bit_length() - 1)).astype(F32)
    row8 = lax.broadcasted_iota(jnp.int32, (SUBLANES, R_ROUTE), 0)
    row128 = lax.broadcasted_iota(jnp.int32, (ROUTER_PAD, R_ROUTE), 0)
    upper = (lax.broadcasted_iota(jnp.int32, (R_ROUTE, R_ROUTE), 0)
             < lax.broadcasted_iota(jnp.int32, (R_ROUTE, R_ROUTE), 1)).astype(BF16)

    def rows_of(base, vals):
        out = jnp.zeros(base.shape, F32)
        for r, v in vals:
            out = jnp.where(base == r, v, out)
        return out

    carry_sc[...] = jnp.zeros_like(carry_sc)

    def count_pass(i, carry_unused):
        rows = pl.ds(pl.multiple_of(i * R_ROUTE, R_ROUTE), R_ROUTE)
        lgt = lg_ref[rows, :].T
        g = lgt[0:N_GROUPS]
        ex = lgt[N_GROUPS:N_ROUTER]
        gmax = jnp.max(g, axis=0, keepdims=True)
        p_grp = 1.0 / jnp.sum(jnp.exp(g - gmax), axis=0, keepdims=True)
        grp = jnp.min(jnp.where(g == gmax, grow, float(N_GROUPS)), axis=0, keepdims=True)
        emask = egrp == grp
        e1 = jnp.max(jnp.where(emask, ex, NEG), axis=0, keepdims=True)
        x1 = jnp.min(jnp.where(emask & (ex == e1), erow, float(N_EXPERTS)), axis=0, keepdims=True)
        emask2 = emask & (erow != x1)
        e2 = jnp.max(jnp.where(emask2, ex, NEG), axis=0, keepdims=True)
        x2 = jnp.min(jnp.where(emask2 & (ex == e2), erow, float(N_EXPERTS)), axis=0, keepdims=True)
        t = jnp.exp(e2 - e1)
        w1 = p_grp / (1.0 + t)
        w2 = p_grp * t / (1.0 + t)
        oh1 = erow == x1
        oh2 = erow == x2
        onehot = jnp.concatenate([oh1.astype(BF16), oh2.astype(BF16)], axis=0)
        carry = carry_sc[:, 0:1]
        before = jnp.dot(onehot, upper, preferred_element_type=F32) + carry
        rank1 = jnp.sum(jnp.where(oh1, before[0:N_EXPERTS], 0.0), axis=0, keepdims=True)
        rank2 = jnp.sum(jnp.where(oh2, before[N_EXPERTS:], 0.0), axis=0, keepdims=True)
        total = carry + jnp.sum(onehot.astype(F32), axis=1, keepdims=True)
        carry_sc[...] = jnp.broadcast_to(total, carry_sc.shape)
        rect_sc[i] = rows_of(row8, ((REC_E, x1), (REC_E + 1, x2), (REC_POS, rank1), (REC_POS + 1, rank2)))
        rec_ref[rows, :] = rows_of(row128, ((REC_W, w1), (REC_W + 1, w2))).T
        return carry_unused

    lax.fori_loop(0, n_chunks, count_pass, 0)

    tot = carry_sc[:, 0:1]
    tot1 = [tot[e, 0] for e in range(N_EXPERTS)]
    cnt = [(tot1[e] + tot[N_EXPERTS + e, 0]).astype(jnp.int32) for e in range(N_EXPERTS)]
    first_blocks = _write_plan(cnt, plan_ref, nu_ref)
    row0 = [(fb * BM_MOE).astype(F32) for fb in first_blocks]

    def place_pass(i, carry_unused):
        rec = rect_sc[i]
        x1 = rec[REC_E:REC_E + 1]
        x2 = rec[REC_E + 1:REC_E + 2]
        p1 = rec[REC_POS:REC_POS + 1]
        p2 = rec[REC_POS + 1:REC_POS + 2]
        for e in range(N_EXPERTS):
            p1 = p1 + jnp.where(x1 == e, row0[e], 0.0)
            p2 = p2 + jnp.where(x2 == e, row0[e] + tot1[e], 0.0)
        rec = jnp.where(row8 == REC_POS, p1 * ROW_TILE, rec)
        rec = jnp.where(row8 == REC_POS + 1, p2 * ROW_TILE, rec)
        idx_ref[i] = rec.astype(jnp.int32)
        return carry_unused

    lax.fori_loop(0, n_chunks, place_pass, 0)


def _router(logits):
    n_chunks = T // R_ROUTE
    return pl.pallas_call(
        _router_body,
        name="router",
        grid=(1,),
        in_specs=[pl.BlockSpec((T, ROUTER_PAD), lambda i: (0, 0))],
        out_specs=[
            pl.BlockSpec((T, ROUTER_PAD), lambda i: (0, 0)),
            pl.BlockSpec((n_chunks, SUBLANES, R_ROUTE), lambda i: (0, 0, 0)),
            pl.BlockSpec(memory_space=pltpu.SMEM),
            pl.BlockSpec(memory_space=pltpu.SMEM),
        ],
        out_shape=[
            jax.ShapeDtypeStruct((T, ROUTER_PAD), F32),
            jax.ShapeDtypeStruct((n_chunks, SUBLANES, R_ROUTE), jnp.int32),
            jax.ShapeDtypeStruct((N_PLAN_ROWS, N_BLK_MOE), jnp.int32),
            jax.ShapeDtypeStruct((1,), jnp.int32),
        ],
        scratch_shapes=[
            pltpu.VMEM((2 * N_EXPERTS, LANES), F32),
            pltpu.VMEM((n_chunks, SUBLANES, R_ROUTE), F32),
        ],
        compiler_params=_params(1),
    )(logits)


def _route(logits):
    rec, idx, plan, n_used = _router(logits)
    idx = idx.transpose(1, 0, 2).reshape(SUBLANES, T)
    eid = idx[REC_E:REC_E + TOP_K].reshape(-1)
    pos = idx[REC_POS:REC_POS + TOP_K].reshape(-1)
    order = jnp.argsort(eid, stable=True)
    tok_sorted = jnp.concatenate([((order % T) * ROW_TILE).astype(jnp.int32),
                                  jnp.zeros((BM_MOE,), jnp.int32)])
    return rec, pos, tok_sorted, plan, n_used


BLOCKS_PER_STEP = 2
N_STEPS_MOE = -(-(N_BLK_MOE + 1) // BLOCKS_PER_STEP)
GATHER_AHEAD = 2
X_SLOTS = GATHER_AHEAD + 1
WEIGHT_DMA_PRIORITY = 1
GATHER_DMA_PRIORITY = 0


def _moe_body(plan, n_used, tok, x_hbm, wg_hbm, wu_hbm, wd_hbm, o_ref,
              xb, wg_st, wu_st, wd_st, wg_bf, wu_bf, wd_bf, sem_x, sem_w):
    for h in range(BLOCKS_PER_STEP):
        rows = pl.ds(h * BM_MOE * ROW_TILE, BM_MOE * ROW_TILE)
        _moe_block(pl.program_id(0) * BLOCKS_PER_STEP + h, plan, n_used, tok, x_hbm, wg_hbm, wu_hbm,
                   wd_hbm, o_ref.at[rows], xb, wg_st, wu_st, wd_st, wg_bf, wu_bf, wd_bf, sem_x, sem_w)


def _moe_block(j, plan, n_used, tok, x_hbm, wg_hbm, wu_hbm, wd_hbm, o_ref,
               xb, wg_st, wu_st, wd_st, wg_bf, wu_bf, wd_bf, sem_x, sem_w):
    nu = n_used[0]

    def row_copy(src, slot, r):
        t8 = pl.multiple_of(tok[src + r], ROW_TILE)
        return pltpu.make_async_copy(x_hbm.at[pl.ds(t8, ROW_TILE)], xb.at[slot, _row_tile(r)],
                                     sem_x.at[slot])

    def block_wait(slot):
        pltpu.make_async_copy(x_hbm.at[pl.ds(0, BM_MOE * ROW_TILE)], xb.at[slot],
                              sem_x.at[slot]).wait()

    def weight_copies(e, ws):
        return (pltpu.make_async_copy(wg_hbm.at[e], wg_st.at[ws], sem_w.at[ws, 0]),
                pltpu.make_async_copy(wu_hbm.at[e], wu_st.at[ws], sem_w.at[ws, 1]),
                pltpu.make_async_copy(wd_hbm.at[e], wd_st.at[ws], sem_w.at[ws, 2]))

    @pl.when(j == 0)
    def _():
        for cp in weight_copies(plan[PLAN_E, 0], 0):
            cp.start(priority=WEIGHT_DMA_PRIORITY)
        for b in range(GATHER_AHEAD):
            src = plan[PLAN_SRC, b]

            def body(r, carry, src=src, b=b):
                row_copy(src, b, r).start(priority=GATHER_DMA_PRIORITY)
                return carry
            lax.fori_loop(0, BM_MOE, body, 0, unroll=8)

    @pl.when((j < nu) & (plan[PLAN_FIRST, jnp.minimum(j, N_BLK_MOE - 1)] == 1))
    def _():
        e = plan[PLAN_E, j]
        ws = plan[PLAN_WSLOT, j]
        e_next = plan[PLAN_NEXT_E, j]

        @pl.when(e_next != e)
        def _():
            for cp in weight_copies(e_next, 1 - ws):
                cp.start(priority=WEIGHT_DMA_PRIORITY)

        for cp in weight_copies(e, ws):
            cp.wait()
        wg_bf[...] = wg_st[ws].astype(BF16)
        wu_bf[...] = wu_st[ws].astype(BF16)
        wd_bf[...] = wd_st[ws].astype(BF16)

    def compute(n_rows):
        slot = j % X_SLOTS
        next_slot = (j + GATHER_AHEAD) % X_SLOTS
        block_wait(slot)
        nsrc = plan[PLAN_SRC, jnp.minimum(j + GATHER_AHEAD, N_BLK_MOE - 1)]
        halves = [_load_packed_rows(xb.at[slot], n_rows, s) for s in range(ROW_TILE)]
        x = jnp.concatenate([lo for lo, _ in halves] + [hi for _, hi in halves], axis=1).astype(BF16)
        zero = plan[PLAN_ZERO, 0]
        group = BM_MOE // 8
        issued = [0]

        def gather_after(v):
            base = nsrc
            if v is not None:
                base = base + (lax.bitcast_convert_type(v[0:1, 0:1], jnp.int32)[0, 0] & zero)
            for r in range(issued[0], issued[0] + group):
                row_copy(base, next_slot, r).start(priority=GATHER_DMA_PRIORITY)
            issued[0] += group

        gather_after(None)
        gather_after(halves[ROW_TILE - 1][1])
        gu = []
        for c in range(D_EXPERT // MXU_N):
            cols = slice(c * MXU_N, (c + 1) * MXU_N)
            g = jnp.dot(x, wg_bf[:, cols], preferred_element_type=F32)
            gather_after(g)
            u = jnp.dot(x, wu_bf[:, cols], preferred_element_type=F32)
            gather_after(u)
            gu.append((g, u))
        acc = None
        for c, (g, u) in enumerate(gu):
            cols = slice(c * MXU_N, (c + 1) * MXU_N)
            h = (g * _sigmoid(g) * u).astype(BF16)
            part = jnp.dot(h, wd_bf[cols, :], preferred_element_type=F32)
            if c == 0:
                gather_after(part)
                gather_after(part[:, D_MODEL - LANES:])
            acc = part if acc is None else acc + part
        assert issued[0] == BM_MOE
        _store_packed_rows(o_ref, acc)
        if n_rows < BM_MOE:
            idle = pl.ds(n_rows * ROW_TILE, (BM_MOE - n_rows) * ROW_TILE)
            o_ref[idle, :] = jnp.zeros(((BM_MOE - n_rows) * ROW_TILE, LANES), U32)

    valid = plan[PLAN_VALID, jnp.minimum(j, N_BLK_MOE - 1)]

    @pl.when((j < nu) & (valid > BM_MOE // 2))
    def _():
        compute(BM_MOE)

    @pl.when((j < nu) & (valid <= BM_MOE // 2))
    def _():
        compute(BM_MOE // 2)

    @pl.when(j == nu)
    def _():
        for b in range(GATHER_AHEAD):
            block_wait((j + b) % X_SLOTS)

    @pl.when(j >= nu)
    def _():
        o_ref[...] = jnp.zeros_like(o_ref)


def _moe(plan, n_used, tok_sorted, xn_packed, w_g, w_u, w_d):
    grid_spec = pltpu.PrefetchScalarGridSpec(
        num_scalar_prefetch=3,
        grid=(N_STEPS_MOE,),
        in_specs=[pl.BlockSpec(memory_space=pl.ANY)] * 4,
        out_specs=pl.BlockSpec((BLOCKS_PER_STEP * BM_MOE * ROW_TILE, LANES),
                               lambda j, pn, nu, tk: (j, 0)),
        scratch_shapes=[
            pltpu.VMEM((X_SLOTS, BM_MOE * ROW_TILE, LANES), U32),
            pltpu.VMEM((2, D_MODEL, D_EXPERT), F32),
            pltpu.VMEM((2, D_MODEL, D_EXPERT), F32),
            pltpu.VMEM((2, D_EXPERT, D_MODEL), F32),
            pltpu.VMEM((D_MODEL, D_EXPERT), BF16),
            pltpu.VMEM((D_MODEL, D_EXPERT), BF16),
            pltpu.VMEM((D_EXPERT, D_MODEL), BF16),
            pltpu.SemaphoreType.DMA((X_SLOTS,)),
            pltpu.SemaphoreType.DMA((2, 3)),
        ],
    )
    return pl.pallas_call(
        _moe_body,
        name="experts",
        grid_spec=grid_spec,
        out_shape=jax.ShapeDtypeStruct(
            (N_STEPS_MOE * BLOCKS_PER_STEP * BM_MOE * ROW_TILE, LANES), U32),
        compiler_params=_params(1),
    )(plan, n_used, tok_sorted, xn_packed, w_g, w_u, w_d)


def _combine_body(pos, x1_ref, w_ref, gf_ref, o_hbm, y_ref, rb, sem, *, first_tile):
    m = pl.program_id(0)
    last = pl.num_programs(0) - 1

    def row_copy(base, slot, r, k):
        p8 = pl.multiple_of(pos[k * T + base + r], ROW_TILE)
        return pltpu.make_async_copy(o_hbm.at[pl.ds(p8, ROW_TILE)], rb.at[slot, k, _row_tile(r)],
                                     sem.at[slot])

    def tile_wait(slot):
        for k in range(TOP_K):
            pltpu.make_async_copy(o_hbm.at[pl.ds(0, TM_OUT * ROW_TILE)], rb.at[slot, k],
                                  sem.at[slot]).wait()

    def tile_base(mm):
        return (first_tile + mm) * TM_OUT

    @pl.when(m == 0)
    def _():
        for b in range(GATHER_AHEAD):
            base = tile_base(jnp.minimum(b, last))

            def body(r, carry, base=base, b=b):
                for k in range(TOP_K):
                    row_copy(base, b, r, k).start(priority=k)
                return carry
            lax.fori_loop(0, TM_OUT, body, 0, unroll=4)

    slot = m % X_SLOTS
    next_slot = (m + GATHER_AHEAD) % X_SLOTS
    tile_wait(slot)
    nbase = tile_base(jnp.minimum(m + GATHER_AHEAD, last))
    rows = TM_OUT // ROW_TILE
    w0 = w_ref[:, 0:1]
    w1 = w_ref[:, 1:2]
    ss = jnp.zeros((TM_OUT, 1), F32)
    for s in range(ROW_TILE):
        for r in range(s * rows, (s + 1) * rows):
            for k in range(TOP_K):
                row_copy(nbase, next_slot, r, k).start(priority=k)
        lo0, hi0 = _load_packed_rows(rb.at[slot, 0], TM_OUT, s)
        lo1, hi1 = _load_packed_rows(rb.at[slot, 1], TM_OUT, s)
        for off, r0, r1 in ((s * LANES, lo0, lo1), (HALF_D + s * LANES, hi0, hi1)):
            y = x1_ref[:, off:off + LANES] + w0 * r0 + w1 * r1
            ss = ss + jnp.sum(y * y, axis=-1, keepdims=True)
            y_ref[:, off:off + LANES] = y
    scale = lax.rsqrt(ss * (1.0 / D_MODEL) + EPS)
    y_ref[...] = y_ref[...] * scale * gf_ref[...]

    @pl.when(m == last)
    def _():
        for b in range(1, GATHER_AHEAD + 1):
            tile_wait((m + b) % X_SLOTS)


def _combine(pos, x1, wts, gf, out_sorted, first_tile, n_tiles):
    grid_spec = pltpu.PrefetchScalarGridSpec(
        num_scalar_prefetch=1,
        grid=(n_tiles,),
        in_specs=[
            pl.BlockSpec((TM_OUT, D_MODEL), lambda m, p: (first_tile + m, 0)),
            pl.BlockSpec((TM_OUT, ROUTER_PAD), lambda m, p: (first_tile + m, 0)),
            pl.BlockSpec((1, D_MODEL), lambda m, p: (0, 0)),
            pl.BlockSpec(memory_space=pl.ANY),
        ],
        out_specs=pl.BlockSpec((TM_OUT, D_MODEL), lambda m, p: (m, 0)),
        scratch_shapes=[
            pltpu.VMEM((X_SLOTS, TOP_K, TM_OUT * ROW_TILE, LANES), U32),
            pltpu.SemaphoreType.DMA((X_SLOTS,)),
        ],
    )
    return pl.pallas_call(
        functools.partial(_combine_body, first_tile=first_tile),
        name="combine",
        grid_spec=grid_spec,
        out_shape=jax.ShapeDtypeStruct((n_tiles * TM_OUT, D_MODEL), F32),
        compiler_params=_params(1),
    )(pos, x1, wts, gf, out_sorted)


def kernel(x_prompt, x_sample, cache_k, cache_v, state_conv, norm1_g, w_in, b_in, attn_sinks,
           w_attn_o, conv_dw, conv_dw_b, conv_ln_g, conv_ln_b, w_conv_o, w_out, norm2_g,
           w_router_group, b_router_group, w_router_expert, b_router_expert, w_e_gate, w_e_up,
           w_e_down, final_norm_g):
    x_p = x_prompt.reshape(T_P, D_MODEL)
    x_s = x_sample.reshape(T_S, D_MODEL)
    g1 = norm1_g[0][None, :]
    b1 = b_in[0][None, :]
    conv_consts = (conv_dw[0], conv_dw_b[0][None, :], conv_ln_g[0][None, :], conv_ln_b[0][None, :])
    hist_pad = jnp.pad(state_conv[0], ((0, 0), (HIST_PAD - HIST, 0), (0, 0)))
    q, kv, u, gates_p, c_p, gates_s, c_s = _in_proj(x_p, x_s, hist_pad, g1, w_in[0], b1, conv_consts)

    sinks = attn_sinks[0]
    a_p = _attn_prompt(sinks, q, kv)
    ck = cache_k[0].reshape(N_STREAMS, WINDOW, KV_WIDTH)
    cv = cache_v[0].reshape(N_STREAMS, WINDOW, KV_WIDTH)
    a_s = _attn_sample(sinks, q, kv, ck, cv)

    w_r = jnp.concatenate([w_router_group[0], w_router_expert[0]], axis=1)
    w_r_hi = w_r.astype(BF16)
    w_r_lo = (w_r - w_r_hi.astype(F32)).astype(BF16)
    w_r_cat = jnp.concatenate(
        [w_r_hi, w_r_lo, jnp.zeros((D_MODEL, ROUTER_PAD - 2 * N_ROUTER), BF16)], axis=1)
    b_r = jnp.concatenate([b_router_group[0], b_router_expert[0],
                           jnp.zeros((ROUTER_PAD - N_ROUTER,), F32)])[None, :]

    x1, xn2, logits = _mix(a_p, a_s, c_p, c_s, gates_p, gates_s, x_p, x_s,
                           w_attn_o[0], w_conv_o[0], w_out[0],
                           norm2_g[0][None, :], w_r_cat, b_r)

    wts, pos, tok_sorted, plan, n_used = _route(logits)
    out_sorted = _moe(plan, n_used, tok_sorted, xn2, w_e_gate[0], w_e_up[0], w_e_down[0])
    gf = final_norm_g[None, :]
    y_p = _combine(pos, x1, wts, gf, out_sorted, 0, T_P // TM_OUT)
    y_s = _combine(pos, x1, wts, gf, out_sorted, T_P // TM_OUT, T_S // TM_OUT)

    kv_shape = (1, -1, WINDOW, N_KV_HEADS, HEAD_DIM)
    new_k_prompt = kv[T_P - WINDOW:T_P, :KV_WIDTH].reshape(kv_shape)
    new_v_prompt = kv[T_P - WINDOW:T_P, KV_WIDTH:].reshape(kv_shape)
    new_conv_prompt = u[T_P - HIST:T_P].reshape(1, 1, HIST, CONV_CH)
    k_s = kv[T_P:, :KV_WIDTH].reshape(N_STREAMS, L_S, KV_WIDTH)
    v_s = kv[T_P:, KV_WIDTH:].reshape(N_STREAMS, L_S, KV_WIDTH)
    new_k_sample = jnp.concatenate([ck[:, L_S:], k_s], axis=1).reshape(kv_shape)
    new_v_sample = jnp.concatenate([cv[:, L_S:], v_s], axis=1).reshape(kv_shape)
    new_conv_sample = u[T_P:].reshape(N_STREAMS, L_S, CONV_CH)[:, L_S - HIST:].reshape(
        1, N_STREAMS, HIST, CONV_CH)

    return (y_p.reshape(1, T_P, D_MODEL), y_s.reshape(N_STREAMS, L_S, D_MODEL),
            new_k_prompt, new_v_prompt, new_conv_prompt,
            new_k_sample, new_v_sample, new_conv_sample)
```

```python
import functools

import numpy as np
import jax
import jax.numpy as jnp
from jax import lax
from jax.experimental import pallas as pl
from jax.experimental.pallas import tpu as pltpu

F32 = jnp.float32
BF16 = jnp.bfloat16
U32 = jnp.uint32

D_MODEL = 2048
T_P = 8192
N_STREAMS = 8
L_S = 64
T_S = N_STREAMS * L_S
T = T_P + T_S
CHUNK = 64
WINDOW = 128
HEAD_DIM = 64
N_Q_HEADS = 16
N_KV_HEADS = 4
GROUP = N_Q_HEADS // N_KV_HEADS
ATTN_WIDTH = N_Q_HEADS * HEAD_DIM
KV_WIDTH = N_KV_HEADS * HEAD_DIM
CONV_CH = 1024
CONV_WIDTH = 31
HIST = CONV_WIDTH - 1
HIST_PAD = 32
SUBLANES = 8
LANES = 128
ROW_TILE = SUBLANES
HALF_D = D_MODEL // 2
HI_MASK = np.uint32(0xFFFF0000)
OFF_KV = ATTN_WIDTH
OFF_GLU = OFF_KV + 2 * KV_WIDTH
OFF_GATES = OFF_GLU + 2 * CONV_CH
IN_WIDTH = OFF_GATES + 2 * D_MODEL
N_GROUPS = 8
EXPERTS_PER_GROUP = 4
N_EXPERTS = N_GROUPS * EXPERTS_PER_GROUP
TOP_K = 2
D_EXPERT = 512
N_ROUTER = N_GROUPS + N_EXPERTS
ROUTER_PAD = 128
EPS = 1e-6
NEG = -1e30

VMEM_LIMIT = 56 * 1024 * 1024
MXU_N = 256

TM_WIDE = 2048
TM_GLU = T // 4
TN_IN = 512
TN_GATES = TN_IN
N_GATE_TILES = 2 * D_MODEL // TN_GATES
ROW_CHUNK = 256
W_STAGE_ROWS = 256
TQ_ATTN = 512
PAIR = 2 * CHUNK
BF16_ROWS = 16
SUB_CONV = 32
PIECE_CONV = 32
CONV_LAG = 1
TM_MIX = 256
BM_MOE = 256
R_ROUTE = 256
TM_OUT = 256
N_SLOTS = T * TOP_K
N_BLK_MOE = -(-(N_SLOTS + N_EXPERTS * (BM_MOE - 1)) // BM_MOE)


def _sigmoid(x):
    return 1.0 / (1.0 + jnp.exp(-x))


def _params(n_axes):
    return pltpu.CompilerParams(dimension_semantics=("arbitrary",) * n_axes,
                                vmem_limit_bytes=VMEM_LIMIT)


def _store_packed_rows(ref, y):
    rows = y.shape[0]
    for s in range(ROW_TILE):
        lo = y[:, s * LANES:(s + 1) * LANES].astype(BF16).astype(F32)
        hi = y[:, HALF_D + s * LANES:HALF_D + (s + 1) * LANES].astype(BF16).astype(F32)
        word = (lax.bitcast_convert_type(hi, U32) & HI_MASK) | (lax.bitcast_convert_type(lo, U32) >> 16)
        ref[pl.ds(s, rows, stride=ROW_TILE), :] = word


def _row_tile(r):
    start = r * ROW_TILE
    return pl.ds(start if isinstance(r, int) else pl.multiple_of(start, ROW_TILE), ROW_TILE)


def _load_packed_rows(ref, rows, s):
    word = ref[pl.ds(s, rows, stride=ROW_TILE), :]
    lo = lax.bitcast_convert_type(word << 16, F32)
    hi = lax.bitcast_convert_type(word & HI_MASK, F32)
    return lo, hi


def _chunk_dot(xn, w_ref, b_ref, c):
    w = w_ref[:, c * MXU_N:(c + 1) * MXU_N].astype(BF16)
    return jnp.dot(xn, w, preferred_element_type=F32) + b_ref[:, c * MXU_N:(c + 1) * MXU_N]


def _stage_weights(jobs, stage, sem):
    width = stage.shape[-1]
    chunks = [(w_hbm, w_bf, r0) for w_hbm, w_bf in jobs
              for r0 in range(0, w_bf.shape[0], W_STAGE_ROWS)]

    def chunk_copy(i):
        w_hbm, _, r0 = chunks[i]
        return pltpu.make_async_copy(w_hbm.at[pl.ds(r0, W_STAGE_ROWS), pl.ds(0, width)],
                                     stage.at[i % 2], sem.at[i % 2])
    chunk_copy(0).start()
    for i, (_, w_bf, r0) in enumerate(chunks):
        if i + 1 < len(chunks):
            chunk_copy(i + 1).start()
        chunk_copy(i).wait()
        w_bf[r0:r0 + W_STAGE_ROWS, :] = stage[i % 2].astype(BF16)


def _qkv_body(xp_ref, xs_ref, g_ref, w_hbm, b_ref, xn_ref, q_ref, kv_ref, w_bf, stage, sem, *, tm):
    m = pl.program_id(0)

    @pl.when(m == 0)
    def _():
        _stage_weights([(w_hbm, w_bf)], stage, sem)

    is_sample = m >= T_P // tm
    for r0 in range(0, tm, ROW_CHUNK):
        rows = slice(r0, r0 + ROW_CHUNK)
        x = jnp.where(is_sample, xs_ref[rows, :], xp_ref[rows, :])
        ms = jnp.mean(x * x, axis=-1, keepdims=True)
        xn = (x * lax.rsqrt(ms + EPS) * g_ref[...]).astype(BF16)
        xn_ref[rows, :] = xn
        for c in range(OFF_GLU // MXU_N):
            cols = slice(c * MXU_N, (c + 1) * MXU_N)
            acc = jnp.dot(xn, w_bf[:, cols], preferred_element_type=F32) + b_ref[:, cols]
            if c < ATTN_WIDTH // MXU_N:
                q_ref[rows, cols] = (acc * (HEAD_DIM ** -0.5)).astype(BF16)
            else:
                kv_ref[rows, c * MXU_N - ATTN_WIDTH:(c + 1) * MXU_N - ATTN_WIDTH] = acc


def _qkv(x_p, x_s, norm_g, w_in, b_in):
    tm = T_S
    t = T
    n_p = T_P // tm
    return pl.pallas_call(
        functools.partial(_qkv_body, tm=tm),
        name="qkv",
        grid=(t // tm,),
        in_specs=[
            pl.BlockSpec((tm, D_MODEL), lambda m: (jnp.minimum(m, n_p - 1), 0)),
            pl.BlockSpec((tm, D_MODEL), lambda m: (0, 0)),
            pl.BlockSpec((1, D_MODEL), lambda m: (0, 0)),
            pl.BlockSpec(memory_space=pl.ANY),
            pl.BlockSpec((1, OFF_GLU), lambda m: (0, 0)),
        ],
        out_specs=[
            pl.BlockSpec((tm, D_MODEL), lambda m: (m, 0)),
            pl.BlockSpec((tm, ATTN_WIDTH), lambda m: (m, 0)),
            pl.BlockSpec((tm, 2 * KV_WIDTH), lambda m: (m, 0)),
        ],
        out_shape=[
            jax.ShapeDtypeStruct((t, D_MODEL), BF16),
            jax.ShapeDtypeStruct((t, ATTN_WIDTH), BF16),
            jax.ShapeDtypeStruct((t, 2 * KV_WIDTH), F32),
        ],
        scratch_shapes=[
            pltpu.VMEM((D_MODEL, OFF_GLU), BF16),
            pltpu.VMEM((2, W_STAGE_ROWS, OFF_GLU), F32),
            pltpu.SemaphoreType.DMA((2,)),
        ],
        compiler_params=_params(1),
    )(x_p, x_s, norm_g, w_in, b_in)


def _glu_body(xn_ref, wa_ref, ba_ref, wb_ref, bb_ref, u_ref):
    xn = xn_ref[...]
    for c in range(TN_IN // MXU_N):
        a = _chunk_dot(xn, wa_ref, ba_ref, c)
        b = _chunk_dot(xn, wb_ref, bb_ref, c)
        u_ref[:, c * MXU_N:(c + 1) * MXU_N] = a * _sigmoid(b)


def _glu(xn, w_in, b_in, tm):
    t = xn.shape[0]
    a0 = OFF_GLU // TN_IN
    b0 = (OFF_GLU + CONV_CH) // TN_IN
    return pl.pallas_call(
        _glu_body,
        name="glu",
        grid=(t // tm, CONV_CH // TN_IN),
        in_specs=[
            pl.BlockSpec((tm, D_MODEL), lambda m, n: (m, 0)),
            pl.BlockSpec((D_MODEL, TN_IN), lambda m, n: (0, a0 + n)),
            pl.BlockSpec((1, TN_IN), lambda m, n: (0, a0 + n)),
            pl.BlockSpec((D_MODEL, TN_IN), lambda m, n: (0, b0 + n)),
            pl.BlockSpec((1, TN_IN), lambda m, n: (0, b0 + n)),
        ],
        out_specs=pl.BlockSpec((tm, TN_IN), lambda m, n: (m, n)),
        out_shape=jax.ShapeDtypeStruct((t, CONV_CH), F32),
        compiler_params=_params(2),
    )(xn, w_in, b_in, w_in, b_in)


def _attn_prompt_body(sink_ref, q_ref, kvp_ref, kvc_ref, a_ref, at_sc):
    i = pl.program_id(0)
    kv = jnp.concatenate([kvp_ref[...], kvc_ref[...]], axis=0)
    k = kv[:, :KV_WIDTH].astype(BF16)
    vt = kv[:, KV_WIDTH:].T.astype(BF16)
    qt = q_ref[...].astype(F32).T.astype(BF16)
    n_cols = GROUP * PAIR
    n_keys = WINDOW + PAIR
    row = lax.broadcasted_iota(jnp.int32, (n_keys, n_cols), 0)
    col = lax.broadcasted_iota(jnp.int32, (n_keys, n_cols), 1)
    first = jnp.where((col & (PAIR - 1)) >= CHUNK, CHUNK, 0)
    gcol = lax.shift_right_logical(lax.broadcasted_iota(jnp.int32, (1, n_cols), 1),
                                   PAIR.bit_length() - 1)
    biases = []
    for p in range(TQ_ATTN // PAIR):
        pos = row + (i * TQ_ATTN + p * PAIR - WINDOW)
        ok = (row >= first) & (row < first + WINDOW + CHUNK) & (pos >= 0)
        biases.append(jnp.where(ok, 0.0, NEG))

    def scores(p, h):
        w0 = p * PAIR
        kh = k[w0:w0 + n_keys, h * HEAD_DIM:(h + 1) * HEAD_DIM]
        rhs = jnp.concatenate(
            [qt[(h * GROUP + g) * HEAD_DIM:(h * GROUP + g + 1) * HEAD_DIM, w0:w0 + PAIR]
             for g in range(GROUP)], axis=1)
        return jnp.dot(kh, rhs, preferred_element_type=F32) + biases[p]

    def finish(p, h, st):
        w0 = p * PAIR
        sink = jnp.full((1, n_cols), sink_ref[h * GROUP], F32)
        for g in range(1, GROUP):
            sink = jnp.where(gcol == g, sink_ref[h * GROUP + g], sink)
        m = jnp.maximum(jnp.max(st, axis=0, keepdims=True), sink)
        pt = jnp.exp(st - m)
        denom = jnp.sum(pt, axis=0, keepdims=True) + jnp.exp(sink - m)
        ot = jnp.dot(vt[h * HEAD_DIM:(h + 1) * HEAD_DIM, w0:w0 + n_keys], pt.astype(BF16),
                     preferred_element_type=F32) / denom
        for g in range(GROUP):
            r0 = (h * GROUP + g) * HEAD_DIM
            at_sc[r0:r0 + HEAD_DIM, w0:w0 + PAIR] = ot[:, g * PAIR:(g + 1) * PAIR]

    items = [(p, h) for p in range(TQ_ATTN // PAIR) for h in range(N_KV_HEADS)]
    st_next = scores(*items[0])
    for n, (p, h) in enumerate(items):
        st = st_next
        if n + 1 < len(items):
            st_next = scores(*items[n + 1])
        finish(p, h, st)
    a_ref[...] = at_sc[...].T.astype(BF16)


def _attn_prompt(sinks, q, kv):
    r = TQ_ATTN // WINDOW
    return pl.pallas_call(
        _attn_prompt_body,
        name="attn_prompt",
        grid=(T_P // TQ_ATTN,),
        in_specs=[
            pl.BlockSpec(memory_space=pltpu.SMEM),
            pl.BlockSpec((TQ_ATTN, ATTN_WIDTH), lambda i: (i, 0)),
            pl.BlockSpec((WINDOW, 2 * KV_WIDTH), lambda i: (jnp.maximum(i * r - 1, 0), 0)),
            pl.BlockSpec((TQ_ATTN, 2 * KV_WIDTH), lambda i: (i, 0)),
        ],
        out_specs=pl.BlockSpec((TQ_ATTN, ATTN_WIDTH), lambda i: (i, 0)),
        out_shape=jax.ShapeDtypeStruct((T_P, ATTN_WIDTH), BF16),
        scratch_shapes=[pltpu.VMEM((ATTN_WIDTH, TQ_ATTN), F32)],
        compiler_params=_params(1),
    )(sinks, q, kv, kv)


def _attn_sample_body(sink_ref, q_ref, kvc_ref, ck_ref, cv_ref, a_ref):
    def window(b):
        rows = slice(b * L_S, (b + 1) * L_S)
        kw = jnp.concatenate([ck_ref[b], kvc_ref[rows, :KV_WIDTH]], axis=0).astype(BF16)
        vw = jnp.concatenate([cv_ref[b], kvc_ref[rows, KV_WIDTH:]], axis=0).astype(BF16)
        return kw, vw

    def scores(b, h, kw):
        kh = kw[:, h * HEAD_DIM:(h + 1) * HEAD_DIM]
        qc = q_ref[b * L_S:(b + 1) * L_S, :]
        qg = jnp.concatenate(
            [qc[:, (h * GROUP + g) * HEAD_DIM:(h * GROUP + g + 1) * HEAD_DIM] for g in range(GROUP)],
            axis=0)
        return lax.dot_general(qg, kh, (((1,), (1,)), ((), ())), preferred_element_type=F32)

    def finish(b, h, vw, s):
        vh = vw[:, h * HEAD_DIM:(h + 1) * HEAD_DIM]
        sink = jnp.concatenate(
            [jnp.full((L_S, 1), sink_ref[h * GROUP + g], F32) for g in range(GROUP)], axis=0)
        m = jnp.maximum(jnp.max(s, axis=1, keepdims=True), sink)
        p = jnp.exp(s - m)
        denom = jnp.sum(p, axis=1, keepdims=True) + jnp.exp(sink - m)
        o = jnp.dot(p.astype(BF16), vh, preferred_element_type=F32) / denom
        for g in range(GROUP):
            c0 = (h * GROUP + g) * HEAD_DIM
            a_ref[b * L_S:(b + 1) * L_S, c0:c0 + HEAD_DIM] = o[g * L_S:(g + 1) * L_S].astype(BF16)

    windows = [window(b) for b in range(N_STREAMS)]
    items = [(b, h) for b in range(N_STREAMS) for h in range(N_KV_HEADS)]
    s_next = scores(*items[0], windows[0][0])
    for n, (b, h) in enumerate(items):
        s = s_next
        if n + 1 < len(items):
            nb, nh = items[n + 1]
            s_next = scores(nb, nh, windows[nb][0])
        finish(b, h, windows[b][1], s)


def _attn_sample(sinks, q, kv, cache_k, cache_v):
    first = T_P // T_S
    return pl.pallas_call(
        _attn_sample_body,
        name="attn_sample",
        grid=(1,),
        in_specs=[
            pl.BlockSpec(memory_space=pltpu.SMEM),
            pl.BlockSpec((T_S, ATTN_WIDTH), lambda i: (first, 0)),
            pl.BlockSpec((T_S, 2 * KV_WIDTH), lambda i: (first, 0)),
            pl.BlockSpec((N_STREAMS, WINDOW, KV_WIDTH), lambda i: (0, 0, 0)),
            pl.BlockSpec((N_STREAMS, WINDOW, KV_WIDTH), lambda i: (0, 0, 0)),
        ],
        out_specs=pl.BlockSpec((T_S, ATTN_WIDTH), lambda i: (0, 0)),
        out_shape=jax.ShapeDtypeStruct((T_S, ATTN_WIDTH), BF16),
        compiler_params=_params(1),
    )(sinks, q, kv, cache_k, cache_v)


def _conv_shifts(prev, cur_ref, ext_sc, sh_sc, rows):
    ext_sc[0:HIST_PAD, :] = prev
    ext_sc[HIST_PAD:HIST_PAD + rows, :] = cur_ref[...]
    ext_sc[HIST_PAD + rows:, :] = jnp.zeros((SUBLANES, CONV_CH), F32)
    lead = HIST_PAD - HIST
    span = rows + HIST_PAD - SUBLANES
    for s in range(SUBLANES):
        sh_sc[s, 0:span, :] = ext_sc[lead + s:lead + s + span, :]


def _conv_rows(r0, w_ref, b_ref, lg_ref, lb_ref, o_ref, sh_sc):
    acc = jnp.zeros((SUB_CONV, CONV_CH), F32) + b_ref[...]
    for k in range(CONV_WIDTH):
        a, s = divmod(k, SUBLANES)
        r = r0 + a * SUBLANES
        acc = acc + w_ref[k:k + 1, :] * sh_sc[s, r:r + SUB_CONV, :]
    mu = jnp.mean(acc, axis=-1, keepdims=True)
    d = acc - mu
    var = jnp.mean(d * d, axis=-1, keepdims=True)
    yn = d * lax.rsqrt(var + EPS) * lg_ref[...] + lb_ref[...]
    out = yn * _sigmoid(yn)
    o_ref[r0:r0 + SUB_CONV, :] = out.astype(BF16)
    return out


def _gates_conv_body(xn_ref, w_ref, b_ref, prev_ref, cur_ref, cw_ref, cb_ref, lg_ref, lb_ref, zero_ref,
                     gate_ref, c_ref, ext_sc, sh_sc, *, rows, fresh):
    if fresh:
        step = pl.program_id(0) * N_GATE_TILES + pl.program_id(1)
        prev = jnp.where(step > 0, prev_ref[...], 0.0)
    else:
        prev = prev_ref[0]
    _conv_shifts(prev, cur_ref, ext_sc, sh_sc, rows)
    n_chunks = TN_GATES // MXU_N
    w = [w_ref[:, c * MXU_N:(c + 1) * MXU_N].astype(BF16) for c in range(n_chunks)]
    zeros = []
    for i in range(rows // PIECE_CONV):
        r = slice(i * ROW_CHUNK, (i + 1) * ROW_CHUNK)
        xr = xn_ref[r, :]
        if i >= CONV_LAG:
            zero16 = zeros[i - CONV_LAG]
            top = jnp.concatenate([xr[0:BF16_ROWS, 0:LANES] + zero16, xr[0:BF16_ROWS, LANES:]], axis=1)
            xr = jnp.concatenate([top, xr[BF16_ROWS:, :]], axis=0)
        accs = [jnp.dot(xr, w[c], preferred_element_type=F32) for c in range(n_chunks)]
        outs = [_conv_rows(r0, cw_ref, cb_ref, lg_ref, lb_ref, c_ref, sh_sc)
                for r0 in range(i * PIECE_CONV, (i + 1) * PIECE_CONV, SUB_CONV)]
        bits = [lax.bitcast_convert_type(o[g * SUBLANES:(g + 1) * SUBLANES, 0:LANES], jnp.int32)
                for o in outs for g in range(SUB_CONV // SUBLANES)]
        zero8 = lax.bitcast_convert_type(functools.reduce(jnp.bitwise_or, bits) & zero_ref[...], F32)
        zeros.append(jnp.concatenate([zero8, zero8], axis=0).astype(BF16))
        for c in range(n_chunks):
            cols = slice(c * MXU_N, (c + 1) * MXU_N)
            gate_ref[r, cols] = _sigmoid(accs[c] + b_ref[:, cols]).astype(BF16)


def _gates_conv(xn, w_in, b_in, hist, u, conv_consts, tm, rows, fresh):
    t = T_P if fresh else T_S
    xn_tile0 = 0 if fresh else T_P // tm
    u_tile0 = 0 if fresh else T_P // rows
    n0 = OFF_GATES // TN_GATES
    tile = lambda m, n: m * N_GATE_TILES + n
    if fresh:
        r = rows // HIST_PAD
        prev_spec = pl.BlockSpec((HIST_PAD, CONV_CH), lambda m, n: (jnp.maximum(tile(m, n) * r - 1, 0), 0))
    else:
        prev_spec = pl.BlockSpec((1, HIST_PAD, CONV_CH), lambda m, n: (tile(m, n), 0, 0))
    const_map = lambda m, n: (0, 0)
    return pl.pallas_call(
        functools.partial(_gates_conv_body, rows=rows, fresh=fresh),
        name="gates_conv",
        grid=(t // tm, N_GATE_TILES),
        in_specs=[
            pl.BlockSpec((tm, D_MODEL), lambda m, n: (xn_tile0 + m, 0)),
            pl.BlockSpec((D_MODEL, TN_GATES), lambda m, n: (0, n0 + n)),
            pl.BlockSpec((1, TN_GATES), lambda m, n: (0, n0 + n)),
            prev_spec,
            pl.BlockSpec((rows, CONV_CH), lambda m, n: (u_tile0 + tile(m, n), 0)),
            pl.BlockSpec((CONV_WIDTH, CONV_CH), const_map),
            pl.BlockSpec((1, CONV_CH), const_map),
            pl.BlockSpec((1, CONV_CH), const_map),
            pl.BlockSpec((1, CONV_CH), const_map),
            pl.BlockSpec((SUBLANES, LANES), const_map),
        ],
        out_specs=[
            pl.BlockSpec((tm, TN_GATES), lambda m, n: (m, n)),
            pl.BlockSpec((rows, CONV_CH), lambda m, n: (tile(m, n), 0)),
        ],
        out_shape=[
            jax.ShapeDtypeStruct((t, 2 * D_MODEL), BF16),
            jax.ShapeDtypeStruct((t, CONV_CH), BF16),
        ],
        scratch_shapes=[
            pltpu.VMEM((HIST_PAD + rows + SUBLANES, CONV_CH), F32),
            pltpu.VMEM((SUBLANES, HIST_PAD + rows, CONV_CH), F32),
        ],
        compiler_params=_params(2),
    )(xn, w_in, b_in, hist, u, *conv_consts, jnp.zeros((SUBLANES, LANES), jnp.int32))


def _in_proj(x_p, x_s, hist, norm_g, w_in, b_in, conv_consts):
    xn, q, kv = _qkv(x_p, x_s, norm_g, w_in, b_in)
    u = _glu(xn, w_in, b_in, TM_GLU)
    gates_p, c_p = _gates_conv(xn, w_in, b_in, u, u, conv_consts, TM_WIDE,
                               TM_WIDE // N_GATE_TILES, True)
    gates_s, c_s = _gates_conv(xn, w_in, b_in, hist, u, conv_consts, T_S,
                               T_S // N_GATE_TILES, False)
    return q, kv, u, gates_p, c_p, gates_s, c_s


def _mix_body(ap_ref, as_ref, cp_ref, cs_ref, gp_ref, gs_ref, xp_ref, xs_ref, wao_hbm, wco_hbm,
              wout_hbm, g2_ref, wr_ref, br_ref, x1_ref, xn_ref, lg_ref,
              wao_ref, wco_ref, wout_ref, stage, sem):
    m = pl.program_id(0)

    @pl.when(m == 0)
    def _():
        _stage_weights([(wao_hbm, wao_ref), (wco_hbm, wco_ref), (wout_hbm, wout_ref)], stage, sem)

    is_sample = m >= T_P // TM_MIX
    a = jnp.where(is_sample, as_ref[...], ap_ref[...])
    c = jnp.where(is_sample, cs_ref[...], cp_ref[...])
    gate = jnp.where(is_sample, gs_ref[...], gp_ref[...])
    x = jnp.where(is_sample, xs_ref[...], xp_ref[...])
    pa = jnp.dot(a, wao_ref[...], preferred_element_type=F32)
    pc = jnp.dot(c, wco_ref[...], preferred_element_type=F32)
    y = (gate[:, :D_MODEL].astype(F32) * pa + gate[:, D_MODEL:].astype(F32) * pc).astype(BF16)
    x1 = x + jnp.dot(y, wout_ref[...], preferred_element_type=F32)
    x1_ref[...] = x1
    ms = jnp.mean(x1 * x1, axis=-1, keepdims=True)
    xn = x1 * lax.rsqrt(ms + EPS) * g2_ref[...]
    _store_packed_rows(xn_ref, xn)
    r = jnp.dot(xn.astype(BF16), wr_ref[...], preferred_element_type=F32)
    lg_ref[...] = r + pltpu.roll(r, ROUTER_PAD - N_ROUTER, axis=1) + br_ref[...]


def _mix(a_p, a_s, c_p, c_s, g_p, g_s, x_p, x_s, wao, wco, wout, g2, wr, br):
    n_p = T_P // TM_MIX
    n_s = T_S // TM_MIX
    prompt_map = lambda m: (jnp.minimum(m, n_p - 1), 0)
    sample_map = lambda m: (jnp.clip(m - n_p, 0, n_s - 1), 0)
    row_map = lambda m: (m, 0)
    const_map = lambda m: (0, 0)

    def pair(width):
        return [pl.BlockSpec((TM_MIX, width), prompt_map), pl.BlockSpec((TM_MIX, width), sample_map)]

    return pl.pallas_call(
        _mix_body,
        name="mix",
        grid=(T // TM_MIX,),
        in_specs=pair(ATTN_WIDTH) + pair(CONV_CH) + pair(2 * D_MODEL) + pair(D_MODEL) + [
            pl.BlockSpec(memory_space=pl.ANY),
            pl.BlockSpec(memory_space=pl.ANY),
            pl.BlockSpec(memory_space=pl.ANY),
            pl.BlockSpec((1, D_MODEL), const_map),
            pl.BlockSpec((D_MODEL, ROUTER_PAD), const_map),
            pl.BlockSpec((1, ROUTER_PAD), const_map),
        ],
        out_specs=[
            pl.BlockSpec((TM_MIX, D_MODEL), row_map),
            pl.BlockSpec((TM_MIX * ROW_TILE, LANES), row_map),
            pl.BlockSpec((TM_MIX, ROUTER_PAD), row_map),
        ],
        out_shape=[
            jax.ShapeDtypeStruct((T, D_MODEL), F32),
            jax.ShapeDtypeStruct((T * ROW_TILE, LANES), U32),
            jax.ShapeDtypeStruct((T, ROUTER_PAD), F32),
        ],
        scratch_shapes=[
            pltpu.VMEM((ATTN_WIDTH, D_MODEL), BF16),
            pltpu.VMEM((CONV_CH, D_MODEL), BF16),
            pltpu.VMEM((D_MODEL, D_MODEL), BF16),
            pltpu.VMEM((2, W_STAGE_ROWS, D_MODEL), F32),
            pltpu.SemaphoreType.DMA((2,)),
        ],
        compiler_params=_params(1),
    )(a_p, a_s, c_p, c_s, g_p, g_s, x_p, x_s, wao, wco, wout, g2, wr, br)


REC_W, REC_E, REC_POS = 0, 2, 4


PLAN_E, PLAN_SRC, PLAN_FIRST, PLAN_WSLOT, PLAN_NEXT_E, PLAN_ZERO, PLAN_VALID = range(7)
N_PLAN_ROWS = 7


def _write_plan(cnt, plan_ref, nu_ref):
    shift = BM_MOE.bit_length() - 1
    nblk = [(c + (BM_MOE - 1)) >> shift for c in cnt]
    nxt = [None] * N_EXPERTS
    later = jnp.int32(-1)
    for e in reversed(range(N_EXPERTS)):
        nxt[e] = jnp.where(later >= 0, later, e)
        later = jnp.where(nblk[e] > 0, e, later)

    def clear(j, carry):
        for row in range(N_PLAN_ROWS):
            plan_ref[row, j] = 0
        return carry
    lax.fori_loop(0, N_BLK_MOE, clear, 0)

    first_blocks = []
    j0 = jnp.int32(0)
    row0 = jnp.int32(0)
    n_before = jnp.int32(0)
    for e in range(N_EXPERTS):
        def fill(b, carry, e=e, j0=j0, row0=row0, n_before=n_before):
            plan_ref[PLAN_E, j0 + b] = e
            plan_ref[PLAN_SRC, j0 + b] = row0 + b * BM_MOE
            plan_ref[PLAN_FIRST, j0 + b] = jnp.where(b == 0, 1, 0)
            plan_ref[PLAN_WSLOT, j0 + b] = n_before & 1
            plan_ref[PLAN_NEXT_E, j0 + b] = nxt[e]
            plan_ref[PLAN_VALID, j0 + b] = jnp.minimum(cnt[e] - b * BM_MOE, BM_MOE)
            return carry
        lax.fori_loop(0, nblk[e], fill, 0)
        first_blocks.append(j0)
        j0 = j0 + nblk[e]
        row0 = row0 + cnt[e]
        n_before = n_before + jnp.where(nblk[e] > 0, 1, 0)
    nu_ref[0] = j0
    return first_blocks


def _router_body(lg_ref, rec_ref, idx_ref, plan_ref, nu_ref, carry_sc, rect_sc):
    n_chunks = T // R_ROUTE
    grow = lax.broadcasted_iota(jnp.int32, (N_GROUPS, R_ROUTE), 0).astype(F32)
    erow_i = lax.broadcasted_iota(jnp.int32, (N_EXPERTS, R_ROUTE), 0)
    erow = erow_i.astype(F32)
    egrp = (erow_i >> (EXPERTS_PER_GROUP.bit_length() - 1)).astype(F32)
    row8 = lax.broadcasted_iota(jnp.int32, (SUBLANES, R_ROUTE), 0)
    row128 = lax.broadcasted_iota(jnp.int32, (ROUTER_PAD, R_ROUTE), 0)
    upper = (lax.broadcasted_iota(jnp.int32, (R_ROUTE, R_ROUTE), 0)
             < lax.broadcasted_iota(jnp.int32, (R_ROUTE, R_ROUTE), 1)).astype(BF16)

    def rows_of(base, vals):
        out = jnp.zeros(base.shape, F32)
        for r, v in vals:
            out = jnp.where(base == r, v, out)
        return out

    carry_sc[...] = jnp.zeros_like(carry_sc)

    def count_pass(i, carry_unused):
        rows = pl.ds(pl.multiple_of(i * R_ROUTE, R_ROUTE), R_ROUTE)
        lgt = lg_ref[rows, :].T
        g = lgt[0:N_GROUPS]
        ex = lgt[N_GROUPS:N_ROUTER]
        gmax = jnp.max(g, axis=0, keepdims=True)
        p_grp = 1.0 / jnp.sum(jnp.exp(g - gmax), axis=0, keepdims=True)
        grp = jnp.min(jnp.where(g == gmax, grow, float(N_GROUPS)), axis=0, keepdims=True)
        emask = egrp == grp
        e1 = jnp.max(jnp.where(emask, ex, NEG), axis=0, keepdims=True)
        x1 = jnp.min(jnp.where(emask & (ex == e1), erow, float(N_EXPERTS)), axis=0, keepdims=True)
        emask2 = emask & (erow != x1)
        e2 = jnp.max(jnp.where(emask2, ex, NEG), axis=0, keepdims=True)
        x2 = jnp.min(jnp.where(emask2 & (ex == e2), erow, float(N_EXPERTS)), axis=0, keepdims=True)
        t = jnp.exp(e2 - e1)
        w1 = p_grp / (1.0 + t)
        w2 = p_grp * t / (1.0 + t)
        oh1 = erow == x1
        oh2 = erow == x2
        onehot = jnp.concatenate([oh1.astype(BF16), oh2.astype(BF16)], axis=0)
        carry = carry_sc[:, 0:1]
        before = jnp.dot(onehot, upper, preferred_element_type=F32) + carry
        rank1 = jnp.sum(jnp.where(oh1, before[0:N_EXPERTS], 0.0), axis=0, keepdims=True)
        rank2 = jnp.sum(jnp.where(oh2, before[N_EXPERTS:], 0.0), axis=0, keepdims=True)
        total = carry + jnp.sum(onehot.astype(F32), axis=1, keepdims=True)
        carry_sc[...] = jnp.broadcast_to(total, carry_sc.shape)
        rect_sc[i] = rows_of(row8, ((REC_E, x1), (REC_E + 1, x2), (REC_POS, rank1), (REC_POS + 1, rank2)))
        rec_ref[rows, :] = rows_of(row128, ((REC_W, w1), (REC_W + 1, w2))).T
        return carry_unused

    lax.fori_loop(0, n_chunks, count_pass, 0)

    tot = carry_sc[:, 0:1]
    tot1 = [tot[e, 0] for e in range(N_EXPERTS)]
    cnt = [(tot1[e] + tot[N_EXPERTS + e, 0]).astype(jnp.int32) for e in range(N_EXPERTS)]
    first_blocks = _write_plan(cnt, plan_ref, nu_ref)
    row0 = [(fb * BM_MOE).astype(F32) for fb in first_blocks]

    def place_pass(i, carry_unused):
        rec = rect_sc[i]
        x1 = rec[REC_E:REC_E + 1]
        x2 = rec[REC_E + 1:REC_E + 2]
        p1 = rec[REC_POS:REC_POS + 1]
        p2 = rec[REC_POS + 1:REC_POS + 2]
        for e in range(N_EXPERTS):
            p1 = p1 + jnp.where(x1 == e, row0[e], 0.0)
            p2 = p2 + jnp.where(x2 == e, row0[e] + tot1[e], 0.0)
        rec = jnp.where(row8 == REC_POS, p1 * ROW_TILE, rec)
        rec = jnp.where(row8 == REC_POS + 1, p2 * ROW_TILE, rec)
        idx_ref[i] = rec.astype(jnp.int32)
        return carry_unused

    lax.fori_loop(0, n_chunks, place_pass, 0)


def _router(logits):
    n_chunks = T // R_ROUTE
    return pl.pallas_call(
        _router_body,
        name="router",
        grid=(1,),
        in_specs=[pl.BlockSpec((T, ROUTER_PAD), lambda i: (0, 0))],
        out_specs=[
            pl.BlockSpec((T, ROUTER_PAD), lambda i: (0, 0)),
            pl.BlockSpec((n_chunks, SUBLANES, R_ROUTE), lambda i: (0, 0, 0)),
            pl.BlockSpec(memory_space=pltpu.SMEM),
            pl.BlockSpec(memory_space=pltpu.SMEM),
        ],
        out_shape=[
            jax.ShapeDtypeStruct((T, ROUTER_PAD), F32),
            jax.ShapeDtypeStruct((n_chunks, SUBLANES, R_ROUTE), jnp.int32),
            jax.ShapeDtypeStruct((N_PLAN_ROWS, N_BLK_MOE), jnp.int32),
            jax.ShapeDtypeStruct((1,), jnp.int32),
        ],
        scratch_shapes=[
            pltpu.VMEM((2 * N_EXPERTS, LANES), F32),
            pltpu.VMEM((n_chunks, SUBLANES, R_ROUTE), F32),
        ],
        compiler_params=_params(1),
    )(logits)


def _route(logits):
    rec, idx, plan, n_used = _router(logits)
    idx = idx.transpose(1, 0, 2).reshape(SUBLANES, T)
    eid = idx[REC_E:REC_E + TOP_K].reshape(-1)
    pos = idx[REC_POS:REC_POS + TOP_K].reshape(-1)
    order = jnp.argsort(eid, stable=True)
    tok_sorted = jnp.concatenate([((order % T) * ROW_TILE).astype(jnp.int32),
                                  jnp.zeros((BM_MOE,), jnp.int32)])
    return rec, pos, tok_sorted, plan, n_used


BLOCKS_PER_STEP = 2
N_STEPS_MOE = -(-(N_BLK_MOE + 1) // BLOCKS_PER_STEP)
GATHER_AHEAD = 2
X_SLOTS = GATHER_AHEAD + 1
WEIGHT_DMA_PRIORITY = 1
GATHER_DMA_PRIORITY = 0


def _moe_body(plan, n_used, tok, x_hbm, wg_hbm, wu_hbm, wd_hbm, o_ref,
              xb, wg_st, wu_st, wd_st, wg_bf, wu_bf, wd_bf, sem_x, sem_w):
    for h in range(BLOCKS_PER_STEP):
        rows = pl.ds(h * BM_MOE * ROW_TILE, BM_MOE * ROW_TILE)
        _moe_block(pl.program_id(0) * BLOCKS_PER_STEP + h, plan, n_used, tok, x_hbm, wg_hbm, wu_hbm,
                   wd_hbm, o_ref.at[rows], xb, wg_st, wu_st, wd_st, wg_bf, wu_bf, wd_bf, sem_x, sem_w)


def _moe_block(j, plan, n_used, tok, x_hbm, wg_hbm, wu_hbm, wd_hbm, o_ref,
               xb, wg_st, wu_st, wd_st, wg_bf, wu_bf, wd_bf, sem_x, sem_w):
    nu = n_used[0]

    def row_copy(src, slot, r):
        t8 = pl.multiple_of(tok[src + r], ROW_TILE)
        return pltpu.make_async_copy(x_hbm.at[pl.ds(t8, ROW_TILE)], xb.at[slot, _row_tile(r)],
                                     sem_x.at[slot])

    def block_wait(slot):
        pltpu.make_async_copy(x_hbm.at[pl.ds(0, BM_MOE * ROW_TILE)], xb.at[slot],
                              sem_x.at[slot]).wait()

    def weight_copies(e, ws):
        return (pltpu.make_async_copy(wg_hbm.at[e], wg_st.at[ws], sem_w.at[ws, 0]),
                pltpu.make_async_copy(wu_hbm.at[e], wu_st.at[ws], sem_w.at[ws, 1]),
                pltpu.make_async_copy(wd_hbm.at[e], wd_st.at[ws], sem_w.at[ws, 2]))

    @pl.when(j == 0)
    def _():
        for cp in weight_copies(plan[PLAN_E, 0], 0):
            cp.start(priority=WEIGHT_DMA_PRIORITY)
        for b in range(GATHER_AHEAD):
            src = plan[PLAN_SRC, b]

            def body(r, carry, src=src, b=b):
                row_copy(src, b, r).start(priority=GATHER_DMA_PRIORITY)
                return carry
            lax.fori_loop(0, BM_MOE, body, 0, unroll=8)

    @pl.when((j < nu) & (plan[PLAN_FIRST, jnp.minimum(j, N_BLK_MOE - 1)] == 1))
    def _():
        e = plan[PLAN_E, j]
        ws = plan[PLAN_WSLOT, j]
        e_next = plan[PLAN_NEXT_E, j]

        @pl.when(e_next != e)
        def _():
            for cp in weight_copies(e_next, 1 - ws):
                cp.start(priority=WEIGHT_DMA_PRIORITY)

        for cp in weight_copies(e, ws):
            cp.wait()
        wg_bf[...] = wg_st[ws].astype(BF16)
        wu_bf[...] = wu_st[ws].astype(BF16)
        wd_bf[...] = wd_st[ws].astype(BF16)

    def compute(n_rows):
        slot = j % X_SLOTS
        next_slot = (j + GATHER_AHEAD) % X_SLOTS
        block_wait(slot)
        nsrc = plan[PLAN_SRC, jnp.minimum(j + GATHER_AHEAD, N_BLK_MOE - 1)]
        halves = [_load_packed_rows(xb.at[slot], n_rows, s) for s in range(ROW_TILE)]
        x = jnp.concatenate([lo for lo, _ in halves] + [hi for _, hi in halves], axis=1).astype(BF16)
        zero = plan[PLAN_ZERO, 0]
        group = BM_MOE // 8
        issued = [0]

        def gather_after(v):
            base = nsrc
            if v is not None:
                base = base + (lax.bitcast_convert_type(v[0:1, 0:1], jnp.int32)[0, 0] & zero)
            for r in range(issued[0], issued[0] + group):
                row_copy(base, next_slot, r).start(priority=GATHER_DMA_PRIORITY)
            issued[0] += group

        gather_after(None)
        gather_after(halves[ROW_TILE - 1][1])
        gu = []
        for c in range(D_EXPERT // MXU_N):
            cols = slice(c * MXU_N, (c + 1) * MXU_N)
            g = jnp.dot(x, wg_bf[:, cols], preferred_element_type=F32)
            gather_after(g)
            u = jnp.dot(x, wu_bf[:, cols], preferred_element_type=F32)
            gather_after(u)
            gu.append((g, u))
        acc = None
        for c, (g, u) in enumerate(gu):
            cols = slice(c * MXU_N, (c + 1) * MXU_N)
            h = (g * _sigmoid(g) * u).astype(BF16)
            part = jnp.dot(h, wd_bf[cols, :], preferred_element_type=F32)
            if c == 0:
                gather_after(part)
                gather_after(part[:, D_MODEL - LANES:])
            acc = part if acc is None else acc + part
        assert issued[0] == BM_MOE
        _store_packed_rows(o_ref, acc)
        if n_rows < BM_MOE:
            idle = pl.ds(n_rows * ROW_TILE, (BM_MOE - n_rows) * ROW_TILE)
            o_ref[idle, :] = jnp.zeros(((BM_MOE - n_rows) * ROW_TILE, LANES), U32)

    valid = plan[PLAN_VALID, jnp.minimum(j, N_BLK_MOE - 1)]

    @pl.when((j < nu) & (valid > BM_MOE // 2))
    def _():
        compute(BM_MOE)

    @pl.when((j < nu) & (valid <= BM_MOE // 2))
    def _():
        compute(BM_MOE // 2)

    @pl.when(j == nu)
    def _():
        for b in range(GATHER_AHEAD):
            block_wait((j + b) % X_SLOTS)

    @pl.when(j >= nu)
    def _():
        o_ref[...] = jnp.zeros_like(o_ref)


def _moe(plan, n_used, tok_sorted, xn_packed, w_g, w_u, w_d):
    grid_spec = pltpu.PrefetchScalarGridSpec(
        num_scalar_prefetch=3,
        grid=(N_STEPS_MOE,),
        in_specs=[pl.BlockSpec(memory_space=pl.ANY)] * 4,
        out_specs=pl.BlockSpec((BLOCKS_PER_STEP * BM_MOE * ROW_TILE, LANES),
                               lambda j, pn, nu, tk: (j, 0)),
        scratch_shapes=[
            pltpu.VMEM((X_SLOTS, BM_MOE * ROW_TILE, LANES), U32),
            pltpu.VMEM((2, D_MODEL, D_EXPERT), F32),
            pltpu.VMEM((2, D_MODEL, D_EXPERT), F32),
            pltpu.VMEM((2, D_EXPERT, D_MODEL), F32),
            pltpu.VMEM((D_MODEL, D_EXPERT), BF16),
            pltpu.VMEM((D_MODEL, D_EXPERT), BF16),
            pltpu.VMEM((D_EXPERT, D_MODEL), BF16),
            pltpu.SemaphoreType.DMA((X_SLOTS,)),
            pltpu.SemaphoreType.DMA((2, 3)),
        ],
    )
    return pl.pallas_call(
        _moe_body,
        name="experts",
        grid_spec=grid_spec,
        out_shape=jax.ShapeDtypeStruct(
            (N_STEPS_MOE * BLOCKS_PER_STEP * BM_MOE * ROW_TILE, LANES), U32),
        compiler_params=_params(1),
    )(plan, n_used, tok_sorted, xn_packed, w_g, w_u, w_d)


def _combine_body(pos, x1_ref, w_ref, gf_ref, o_hbm, y_ref, rb, sem, *, first_tile):
    m = pl.program_id(0)
    last = pl.num_programs(0) - 1

    def row_copy(base, slot, r, k):
        p8 = pl.multiple_of(pos[k * T + base + r], ROW_TILE)
        return pltpu.make_async_copy(o_hbm.at[pl.ds(p8, ROW_TILE)], rb.at[slot, k, _row_tile(r)],
                                     sem.at[slot])

    def tile_wait(slot):
        for k in range(TOP_K):
            pltpu.make_async_copy(o_hbm.at[pl.ds(0, TM_OUT * ROW_TILE)], rb.at[slot, k],
                                  sem.at[slot]).wait()

    def tile_base(mm):
        return (first_tile + mm) * TM_OUT

    @pl.when(m == 0)
    def _():
        for b in range(GATHER_AHEAD):
            base = tile_base(jnp.minimum(b, last))

            def body(r, carry, base=base, b=b):
                for k in range(TOP_K):
                    row_copy(base, b, r, k).start(priority=k)
                return carry
            lax.fori_loop(0, TM_OUT, body, 0, unroll=4)

    slot = m % X_SLOTS
    next_slot = (m + GATHER_AHEAD) % X_SLOTS
    tile_wait(slot)
    nbase = tile_base(jnp.minimum(m + GATHER_AHEAD, last))
    rows = TM_OUT // ROW_TILE
    w0 = w_ref[:, 0:1]
    w1 = w_ref[:, 1:2]
    ss = jnp.zeros((TM_OUT, 1), F32)
    for s in range(ROW_TILE):
        for r in range(s * rows, (s + 1) * rows):
            for k in range(TOP_K):
                row_copy(nbase, next_slot, r, k).start(priority=k)
        lo0, hi0 = _load_packed_rows(rb.at[slot, 0], TM_OUT, s)
        lo1, hi1 = _load_packed_rows(rb.at[slot, 1], TM_OUT, s)
        for off, r0, r1 in ((s * LANES, lo0, lo1), (HALF_D + s * LANES, hi0, hi1)):
            y = x1_ref[:, off:off + LANES] + w0 * r0 + w1 * r1
            ss = ss + jnp.sum(y * y, axis=-1, keepdims=True)
            y_ref[:, off:off + LANES] = y
    scale = lax.rsqrt(ss * (1.0 / D_MODEL) + EPS)
    y_ref[...] = y_ref[...] * scale * gf_ref[...]

    @pl.when(m == last)
    def _():
        for b in range(1, GATHER_AHEAD + 1):
            tile_wait((m + b) % X_SLOTS)


def _combine(pos, x1, wts, gf, out_sorted, first_tile, n_tiles):
    grid_spec = pltpu.PrefetchScalarGridSpec(
        num_scalar_prefetch=1,
        grid=(n_tiles,),
        in_specs=[
            pl.BlockSpec((TM_OUT, D_MODEL), lambda m, p: (first_tile + m, 0)),
            pl.BlockSpec((TM_OUT, ROUTER_PAD), lambda m, p: (first_tile + m, 0)),
            pl.BlockSpec((1, D_MODEL), lambda m, p: (0, 0)),
            pl.BlockSpec(memory_space=pl.ANY),
        ],
        out_specs=pl.BlockSpec((TM_OUT, D_MODEL), lambda m, p: (m, 0)),
        scratch_shapes=[
            pltpu.VMEM((X_SLOTS, TOP_K, TM_OUT * ROW_TILE, LANES), U32),
            pltpu.SemaphoreType.DMA((X_SLOTS,)),
        ],
    )
    return pl.pallas_call(
        functools.partial(_combine_body, first_tile=first_tile),
        name="combine",
        grid_spec=grid_spec,
        out_shape=jax.ShapeDtypeStruct((n_tiles * TM_OUT, D_MODEL), F32),
        compiler_params=_params(1),
    )(pos, x1, wts, gf, out_sorted)


def kernel(x_prompt, x_sample, cache_k, cache_v, state_conv, norm1_g, w_in, b_in, attn_sinks,
           w_attn_o, conv_dw, conv_dw_b, conv_ln_g, conv_ln_b, w_conv_o, w_out, norm2_g,
           w_router_group, b_router_group, w_router_expert, b_router_expert, w_e_gate, w_e_up,
           w_e_down, final_norm_g):
    x_p = x_prompt.reshape(T_P, D_MODEL)
    x_s = x_sample.reshape(T_S, D_MODEL)
    g1 = norm1_g[0][None, :]
    b1 = b_in[0][None, :]
    conv_consts = (conv_dw[0], conv_dw_b[0][None, :], conv_ln_g[0][None, :], conv_ln_b[0][None, :])
    hist_pad = jnp.pad(state_conv[0], ((0, 0), (HIST_PAD - HIST, 0), (0, 0)))
    q, kv, u, gates_p, c_p, gates_s, c_s = _in_proj(x_p, x_s, hist_pad, g1, w_in[0], b1, conv_consts)

    sinks = attn_sinks[0]
    a_p = _attn_prompt(sinks, q, kv)
    ck = cache_k[0].reshape(N_STREAMS, WINDOW, KV_WIDTH)
    cv = cache_v[0].reshape(N_STREAMS, WINDOW, KV_WIDTH)
    a_s = _attn_sample(sinks, q, kv, ck, cv)

    w_r = jnp.concatenate([w_router_group[0], w_router_expert[0]], axis=1)
    w_r_hi = w_r.astype(BF16)
    w_r_lo = (w_r - w_r_hi.astype(F32)).astype(BF16)
    w_r_cat = jnp.concatenate(
        [w_r_hi, w_r_lo, jnp.zeros((D_MODEL, ROUTER_PAD - 2 * N_ROUTER), BF16)], axis=1)
    b_r = jnp.concatenate([b_router_group[0], b_router_expert[0],
                           jnp.zeros((ROUTER_PAD - N_ROUTER,), F32)])[None, :]

    x1, xn2, logits = _mix(a_p, a_s, c_p, c_s, gates_p, gates_s, x_p, x_s,
                           w_attn_o[0], w_conv_o[0], w_out[0],
                           norm2_g[0][None, :], w_r_cat, b_r)

    wts, pos, tok_sorted, plan, n_used = _route(logits)
    out_sorted = _moe(plan, n_used, tok_sorted, xn2, w_e_gate[0], w_e_up[0], w_e_down[0])
    gf = final_norm_g[None, :]
    y_p = _combine(pos, x1, wts, gf, out_sorted, 0, T_P // TM_OUT)
    y_s = _combine(pos, x1, wts, gf, out_sorted, T_P // TM_OUT, T_S // TM_OUT)

    kv_shape = (1, -1, WINDOW, N_KV_HEADS, HEAD_DIM)
    new_k_prompt = kv[T_P - WINDOW:T_P, :KV_WIDTH].reshape(kv_shape)
    new_v_prompt = kv[T_P - WINDOW:T_P, KV_WIDTH:].reshape(kv_shape)
    new_conv_prompt = u[T_P - HIST:T_P].reshape(1, 1, HIST, CONV_CH)
    k_s = kv[T_P:, :KV_WIDTH].reshape(N_STREAMS, L_S, KV_WIDTH)
    v_s = kv[T_P:, KV_WIDTH:].reshape(N_STREAMS, L_S, KV_WIDTH)
    new_k_sample = jnp.concatenate([ck[:, L_S:], k_s], axis=1).reshape(kv_shape)
    new_v_sample = jnp.concatenate([cv[:, L_S:], v_s], axis=1).reshape(kv_shape)
    new_conv_sample = u[T_P:].reshape(N_STREAMS, L_S, CONV_CH)[:, L_S - HIST:].reshape(
        1, N_STREAMS, HIST, CONV_CH)

    return (y_p.reshape(1, T_P, D_MODEL), y_s.reshape(N_STREAMS, L_S, D_MODEL),
            new_k_prompt, new_v_prompt, new_conv_prompt,
            new_k_sample, new_v_sample, new_conv_sample)
```

```python
import functools

import numpy as np
import jax
import jax.numpy as jnp
from jax import lax
from jax.experimental import pallas as pl
from jax.experimental.pallas import tpu as pltpu

F32 = jnp.float32
BF16 = jnp.bfloat16
U32 = jnp.uint32

D_MODEL = 2048
T_P = 8192
N_STREAMS = 8
L_S = 64
T_S = N_STREAMS * L_S
T = T_P + T_S
CHUNK = 64
WINDOW = 128
HEAD_DIM = 64
N_Q_HEADS = 16
N_KV_HEADS = 4
GROUP = N_Q_HEADS // N_KV_HEADS
ATTN_WIDTH = N_Q_HEADS * HEAD_DIM
KV_WIDTH = N_KV_HEADS * HEAD_DIM
CONV_CH = 1024
CONV_WIDTH = 31
HIST = CONV_WIDTH - 1
HIST_PAD = 32
SUBLANES = 8
LANES = 128
ROW_TILE = SUBLANES
HALF_D = D_MODEL // 2
HI_MASK = np.uint32(0xFFFF0000)
OFF_KV = ATTN_WIDTH
OFF_GLU = OFF_KV + 2 * KV_WIDTH
OFF_GATES = OFF_GLU + 2 * CONV_CH
IN_WIDTH = OFF_GATES + 2 * D_MODEL
N_GROUPS = 8
EXPERTS_PER_GROUP = 4
N_EXPERTS = N_GROUPS * EXPERTS_PER_GROUP
TOP_K = 2
D_EXPERT = 512
N_ROUTER = N_GROUPS + N_EXPERTS
ROUTER_PAD = 128
EPS = 1e-6
NEG = -1e30

VMEM_LIMIT = 56 * 1024 * 1024
MXU_N = 256

TM_WIDE = 2048
TM_GLU = T // 4
TN_IN = 512
TN_GATES = TN_IN
N_GATE_TILES = 2 * D_MODEL // TN_GATES
ROW_CHUNK = 256
W_STAGE_ROWS = 256
TQ_ATTN = 512
PAIR = 2 * CHUNK
BF16_ROWS = 16
SUB_CONV = 32
PIECE_CONV = 32
CONV_LAG = 1
TM_MIX = 256
MIX_COLS = 512
BM_MOE = 256
R_ROUTE = 256
TM_OUT = 256
N_SLOTS = T * TOP_K
N_BLK_MOE = -(-(N_SLOTS + N_EXPERTS * (BM_MOE - 1)) // BM_MOE)


def _sigmoid(x):
    return 1.0 / (1.0 + jnp.exp(-x))


def _params(n_axes):
    return pltpu.CompilerParams(dimension_semantics=("arbitrary",) * n_axes,
                                vmem_limit_bytes=VMEM_LIMIT)


def _store_packed_rows(ref, y):
    rows = y.shape[0]
    for s in range(ROW_TILE):
        lo = y[:, s * LANES:(s + 1) * LANES].astype(BF16).astype(F32)
        hi = y[:, HALF_D + s * LANES:HALF_D + (s + 1) * LANES].astype(BF16).astype(F32)
        word = (lax.bitcast_convert_type(hi, U32) & HI_MASK) | (lax.bitcast_convert_type(lo, U32) >> 16)
        ref[pl.ds(s, rows, stride=ROW_TILE), :] = word


def _row_tile(r):
    start = r * ROW_TILE
    return pl.ds(start if isinstance(r, int) else pl.multiple_of(start, ROW_TILE), ROW_TILE)


def _load_packed_rows(ref, rows, s):
    word = ref[pl.ds(s, rows, stride=ROW_TILE), :]
    lo = lax.bitcast_convert_type(word << 16, F32)
    hi = lax.bitcast_convert_type(word & HI_MASK, F32)
    return lo, hi


def _chunk_dot(xn, w_ref, b_ref, c):
    w = w_ref[:, c * MXU_N:(c + 1) * MXU_N].astype(BF16)
    return jnp.dot(xn, w, preferred_element_type=F32) + b_ref[:, c * MXU_N:(c + 1) * MXU_N]


def _stage_weights(jobs, stage, sem):
    width = stage.shape[-1]
    chunks = [(w_hbm, w_bf, r0) for w_hbm, w_bf in jobs
              for r0 in range(0, w_bf.shape[0], W_STAGE_ROWS)]

    def chunk_copy(i):
        w_hbm, _, r0 = chunks[i]
        return pltpu.make_async_copy(w_hbm.at[pl.ds(r0, W_STAGE_ROWS), pl.ds(0, width)],
                                     stage.at[i % 2], sem.at[i % 2])
    chunk_copy(0).start()
    for i, (_, w_bf, r0) in enumerate(chunks):
        if i + 1 < len(chunks):
            chunk_copy(i + 1).start()
        chunk_copy(i).wait()
        w_bf[r0:r0 + W_STAGE_ROWS, :] = stage[i % 2].astype(BF16)


def _qkv_body(xp_ref, xs_ref, g_ref, w_hbm, b_ref, xn_ref, q_ref, kv_ref, w_bf, stage, sem, *, tm):
    m = pl.program_id(0)

    @pl.when(m == 0)
    def _():
        _stage_weights([(w_hbm, w_bf)], stage, sem)

    is_sample = m >= T_P // tm
    for r0 in range(0, tm, ROW_CHUNK):
        rows = slice(r0, r0 + ROW_CHUNK)
        x = jnp.where(is_sample, xs_ref[rows, :], xp_ref[rows, :])
        ms = jnp.mean(x * x, axis=-1, keepdims=True)
        xn = (x * lax.rsqrt(ms + EPS) * g_ref[...]).astype(BF16)
        xn_ref[rows, :] = xn
        for c in range(OFF_GLU // MXU_N):
            cols = slice(c * MXU_N, (c + 1) * MXU_N)
            acc = jnp.dot(xn, w_bf[:, cols], preferred_element_type=F32) + b_ref[:, cols]
            if c < ATTN_WIDTH // MXU_N:
                q_ref[rows, cols] = (acc * (HEAD_DIM ** -0.5)).astype(BF16)
            else:
                kv_ref[rows, c * MXU_N - ATTN_WIDTH:(c + 1) * MXU_N - ATTN_WIDTH] = acc


def _qkv(x_p, x_s, norm_g, w_in, b_in):
    tm = T_S
    t = T
    n_p = T_P // tm
    return pl.pallas_call(
        functools.partial(_qkv_body, tm=tm),
        name="qkv",
        grid=(t // tm,),
        in_specs=[
            pl.BlockSpec((tm, D_MODEL), lambda m: (jnp.minimum(m, n_p - 1), 0)),
            pl.BlockSpec((tm, D_MODEL), lambda m: (0, 0)),
            pl.BlockSpec((1, D_MODEL), lambda m: (0, 0)),
            pl.BlockSpec(memory_space=pl.ANY),
            pl.BlockSpec((1, OFF_GLU), lambda m: (0, 0)),
        ],
        out_specs=[
            pl.BlockSpec((tm, D_MODEL), lambda m: (m, 0)),
            pl.BlockSpec((tm, ATTN_WIDTH), lambda m: (m, 0)),
            pl.BlockSpec((tm, 2 * KV_WIDTH), lambda m: (m, 0)),
        ],
        out_shape=[
            jax.ShapeDtypeStruct((t, D_MODEL), BF16),
            jax.ShapeDtypeStruct((t, ATTN_WIDTH), BF16),
            jax.ShapeDtypeStruct((t, 2 * KV_WIDTH), F32),
        ],
        scratch_shapes=[
            pltpu.VMEM((D_MODEL, OFF_GLU), BF16),
            pltpu.VMEM((2, W_STAGE_ROWS, OFF_GLU), F32),
            pltpu.SemaphoreType.DMA((2,)),
        ],
        compiler_params=_params(1),
    )(x_p, x_s, norm_g, w_in, b_in)


def _glu_body(xn_ref, wa_ref, ba_ref, wb_ref, bb_ref, u_ref):
    xn = xn_ref[...]
    for c in range(TN_IN // MXU_N):
        a = _chunk_dot(xn, wa_ref, ba_ref, c)
        b = _chunk_dot(xn, wb_ref, bb_ref, c)
        u_ref[:, c * MXU_N:(c + 1) * MXU_N] = a * _sigmoid(b)


def _glu(xn, w_in, b_in, tm):
    t = xn.shape[0]
    a0 = OFF_GLU // TN_IN
    b0 = (OFF_GLU + CONV_CH) // TN_IN
    return pl.pallas_call(
        _glu_body,
        name="glu",
        grid=(t // tm, CONV_CH // TN_IN),
        in_specs=[
            pl.BlockSpec((tm, D_MODEL), lambda m, n: (m, 0)),
            pl.BlockSpec((D_MODEL, TN_IN), lambda m, n: (0, a0 + n)),
            pl.BlockSpec((1, TN_IN), lambda m, n: (0, a0 + n)),
            pl.BlockSpec((D_MODEL, TN_IN), lambda m, n: (0, b0 + n)),
            pl.BlockSpec((1, TN_IN), lambda m, n: (0, b0 + n)),
        ],
        out_specs=pl.BlockSpec((tm, TN_IN), lambda m, n: (m, n)),
        out_shape=jax.ShapeDtypeStruct((t, CONV_CH), F32),
        compiler_params=_params(2),
    )(xn, w_in, b_in, w_in, b_in)


def _attn_prompt_body(sink_ref, q_ref, kvp_ref, kvc_ref, a_ref, at_sc):
    i = pl.program_id(0)
    kv = jnp.concatenate([kvp_ref[...], kvc_ref[...]], axis=0)
    k = kv[:, :KV_WIDTH].astype(BF16)
    vt = kv[:, KV_WIDTH:].T.astype(BF16)
    qt = q_ref[...].astype(F32).T.astype(BF16)
    n_cols = GROUP * PAIR
    n_keys = WINDOW + PAIR
    row = lax.broadcasted_iota(jnp.int32, (n_keys, n_cols), 0)
    col = lax.broadcasted_iota(jnp.int32, (n_keys, n_cols), 1)
    first = jnp.where((col & (PAIR - 1)) >= CHUNK, CHUNK, 0)
    gcol = lax.shift_right_logical(lax.broadcasted_iota(jnp.int32, (1, n_cols), 1),
                                   PAIR.bit_length() - 1)
    biases = []
    for p in range(TQ_ATTN // PAIR):
        pos = row + (i * TQ_ATTN + p * PAIR - WINDOW)
        ok = (row >= first) & (row < first + WINDOW + CHUNK) & (pos >= 0)
        biases.append(jnp.where(ok, 0.0, NEG))

    def scores(p, h):
        w0 = p * PAIR
        kh = k[w0:w0 + n_keys, h * HEAD_DIM:(h + 1) * HEAD_DIM]
        rhs = jnp.concatenate(
            [qt[(h * GROUP + g) * HEAD_DIM:(h * GROUP + g + 1) * HEAD_DIM, w0:w0 + PAIR]
             for g in range(GROUP)], axis=1)
        return jnp.dot(kh, rhs, preferred_element_type=F32) + biases[p]

    def finish(p, h, st):
        w0 = p * PAIR
        sink = jnp.full((1, n_cols), sink_ref[h * GROUP], F32)
        for g in range(1, GROUP):
            sink = jnp.where(gcol == g, sink_ref[h * GROUP + g], sink)
        m = jnp.maximum(jnp.max(st, axis=0, keepdims=True), sink)
        pt = jnp.exp(st - m)
        denom = jnp.sum(pt, axis=0, keepdims=True) + jnp.exp(sink - m)
        ot = jnp.dot(vt[h * HEAD_DIM:(h + 1) * HEAD_DIM, w0:w0 + n_keys], pt.astype(BF16),
                     preferred_element_type=F32) / denom
        for g in range(GROUP):
            r0 = (h * GROUP + g) * HEAD_DIM
            at_sc[r0:r0 + HEAD_DIM, w0:w0 + PAIR] = ot[:, g * PAIR:(g + 1) * PAIR]

    items = [(p, h) for p in range(TQ_ATTN // PAIR) for h in range(N_KV_HEADS)]
    st_next = scores(*items[0])
    for n, (p, h) in enumerate(items):
        st = st_next
        if n + 1 < len(items):
            st_next = scores(*items[n + 1])
        finish(p, h, st)
    a_ref[...] = at_sc[...].T.astype(BF16)


def _attn_prompt(sinks, q, kv):
    r = TQ_ATTN // WINDOW
    return pl.pallas_call(
        _attn_prompt_body,
        name="attn_prompt",
        grid=(T_P // TQ_ATTN,),
        in_specs=[
            pl.BlockSpec(memory_space=pltpu.SMEM),
            pl.BlockSpec((TQ_ATTN, ATTN_WIDTH), lambda i: (i, 0)),
            pl.BlockSpec((WINDOW, 2 * KV_WIDTH), lambda i: (jnp.maximum(i * r - 1, 0), 0)),
            pl.BlockSpec((TQ_ATTN, 2 * KV_WIDTH), lambda i: (i, 0)),
        ],
        out_specs=pl.BlockSpec((TQ_ATTN, ATTN_WIDTH), lambda i: (i, 0)),
        out_shape=jax.ShapeDtypeStruct((T_P, ATTN_WIDTH), BF16),
        scratch_shapes=[pltpu.VMEM((ATTN_WIDTH, TQ_ATTN), F32)],
        compiler_params=_params(1),
    )(sinks, q, kv, kv)


def _attn_sample_body(sink_ref, q_ref, kvc_ref, ck_ref, cv_ref, a_ref):
    kvc = kvc_ref[...]
    kw = jnp.concatenate([ck_ref[0], kvc[:, :KV_WIDTH]], axis=0).astype(BF16)
    vw = jnp.concatenate([cv_ref[0], kvc[:, KV_WIDTH:]], axis=0).astype(BF16)
    qc = q_ref[...]
    for h in range(N_KV_HEADS):
        kh = kw[:, h * HEAD_DIM:(h + 1) * HEAD_DIM]
        vh = vw[:, h * HEAD_DIM:(h + 1) * HEAD_DIM]
        qg = jnp.concatenate(
            [qc[:, (h * GROUP + g) * HEAD_DIM:(h * GROUP + g + 1) * HEAD_DIM] for g in range(GROUP)],
            axis=0)
        s = lax.dot_general(qg, kh, (((1,), (1,)), ((), ())), preferred_element_type=F32)
        sink = jnp.concatenate(
            [jnp.full((L_S, 1), sink_ref[h * GROUP + g], F32) for g in range(GROUP)], axis=0)
        m = jnp.maximum(jnp.max(s, axis=1, keepdims=True), sink)
        p = jnp.exp(s - m)
        denom = jnp.sum(p, axis=1, keepdims=True) + jnp.exp(sink - m)
        o = jnp.dot(p.astype(BF16), vh, preferred_element_type=F32) / denom
        for g in range(GROUP):
            c0 = (h * GROUP + g) * HEAD_DIM
            a_ref[:, c0:c0 + HEAD_DIM] = o[g * L_S:(g + 1) * L_S].astype(BF16)


def _attn_sample(sinks, q, kv, cache_k, cache_v):
    first = T_P // L_S
    return pl.pallas_call(
        _attn_sample_body,
        name="attn_sample",
        grid=(N_STREAMS,),
        in_specs=[
            pl.BlockSpec(memory_space=pltpu.SMEM),
            pl.BlockSpec((L_S, ATTN_WIDTH), lambda b: (first + b, 0)),
            pl.BlockSpec((L_S, 2 * KV_WIDTH), lambda b: (first + b, 0)),
            pl.BlockSpec((1, WINDOW, KV_WIDTH), lambda b: (b, 0, 0)),
            pl.BlockSpec((1, WINDOW, KV_WIDTH), lambda b: (b, 0, 0)),
        ],
        out_specs=pl.BlockSpec((L_S, ATTN_WIDTH), lambda b: (b, 0)),
        out_shape=jax.ShapeDtypeStruct((T_S, ATTN_WIDTH), BF16),
        compiler_params=_params(1),
    )(sinks, q, kv, cache_k, cache_v)


def _conv_shifts(prev, cur_ref, ext_sc, sh_sc, rows):
    ext_sc[0:HIST_PAD, :] = prev
    ext_sc[HIST_PAD:HIST_PAD + rows, :] = cur_ref[...]
    ext_sc[HIST_PAD + rows:, :] = jnp.zeros((SUBLANES, CONV_CH), F32)
    lead = HIST_PAD - HIST
    span = rows + HIST_PAD - SUBLANES
    for s in range(SUBLANES):
        sh_sc[s, 0:span, :] = ext_sc[lead + s:lead + s + span, :]


def _conv_rows(r0, w_ref, b_ref, lg_ref, lb_ref, o_ref, sh_sc):
    acc = jnp.zeros((SUB_CONV, CONV_CH), F32) + b_ref[...]
    for k in range(CONV_WIDTH):
        a, s = divmod(k, SUBLANES)
        r = r0 + a * SUBLANES
        acc = acc + w_ref[k:k + 1, :] * sh_sc[s, r:r + SUB_CONV, :]
    mu = jnp.mean(acc, axis=-1, keepdims=True)
    d = acc - mu
    var = jnp.mean(d * d, axis=-1, keepdims=True)
    yn = d * lax.rsqrt(var + EPS) * lg_ref[...] + lb_ref[...]
    out = yn * _sigmoid(yn)
    o_ref[r0:r0 + SUB_CONV, :] = out.astype(BF16)
    return out


def _gates_conv_body(xn_ref, w_ref, b_ref, prev_ref, cur_ref, cw_ref, cb_ref, lg_ref, lb_ref, zero_ref,
                     gate_ref, c_ref, ext_sc, sh_sc, *, rows, fresh):
    if fresh:
        step = pl.program_id(0) * N_GATE_TILES + pl.program_id(1)
        prev = jnp.where(step > 0, prev_ref[...], 0.0)
    else:
        prev = prev_ref[0]
    _conv_shifts(prev, cur_ref, ext_sc, sh_sc, rows)
    n_chunks = TN_GATES // MXU_N
    w = [w_ref[:, c * MXU_N:(c + 1) * MXU_N].astype(BF16) for c in range(n_chunks)]
    zeros = []
    for i in range(rows // PIECE_CONV):
        r = slice(i * ROW_CHUNK, (i + 1) * ROW_CHUNK)
        xr = xn_ref[r, :]
        if i >= CONV_LAG:
            zero16 = zeros[i - CONV_LAG]
            top = jnp.concatenate([xr[0:BF16_ROWS, 0:LANES] + zero16, xr[0:BF16_ROWS, LANES:]], axis=1)
            xr = jnp.concatenate([top, xr[BF16_ROWS:, :]], axis=0)
        accs = [jnp.dot(xr, w[c], preferred_element_type=F32) for c in range(n_chunks)]
        outs = [_conv_rows(r0, cw_ref, cb_ref, lg_ref, lb_ref, c_ref, sh_sc)
                for r0 in range(i * PIECE_CONV, (i + 1) * PIECE_CONV, SUB_CONV)]
        bits = [lax.bitcast_convert_type(o[g * SUBLANES:(g + 1) * SUBLANES, 0:LANES], jnp.int32)
                for o in outs for g in range(SUB_CONV // SUBLANES)]
        zero8 = lax.bitcast_convert_type(functools.reduce(jnp.bitwise_or, bits) & zero_ref[...], F32)
        zeros.append(jnp.concatenate([zero8, zero8], axis=0).astype(BF16))
        for c in range(n_chunks):
            cols = slice(c * MXU_N, (c + 1) * MXU_N)
            gate_ref[r, cols] = _sigmoid(accs[c] + b_ref[:, cols]).astype(BF16)


def _gates_conv(xn, w_in, b_in, hist, u, conv_consts, tm, rows, fresh):
    t = T_P if fresh else T_S
    xn_tile0 = 0 if fresh else T_P // tm
    u_tile0 = 0 if fresh else T_P // rows
    n0 = OFF_GATES // TN_GATES
    tile = lambda m, n: m * N_GATE_TILES + n
    if fresh:
        r = rows // HIST_PAD
        prev_spec = pl.BlockSpec((HIST_PAD, CONV_CH), lambda m, n: (jnp.maximum(tile(m, n) * r - 1, 0), 0))
    else:
        prev_spec = pl.BlockSpec((1, HIST_PAD, CONV_CH), lambda m, n: (tile(m, n), 0, 0))
    const_map = lambda m, n: (0, 0)
    return pl.pallas_call(
        functools.partial(_gates_conv_body, rows=rows, fresh=fresh),
        name="gates_conv",
        grid=(t // tm, N_GATE_TILES),
        in_specs=[
            pl.BlockSpec((tm, D_MODEL), lambda m, n: (xn_tile0 + m, 0)),
            pl.BlockSpec((D_MODEL, TN_GATES), lambda m, n: (0, n0 + n)),
            pl.BlockSpec((1, TN_GATES), lambda m, n: (0, n0 + n)),
            prev_spec,
            pl.BlockSpec((rows, CONV_CH), lambda m, n: (u_tile0 + tile(m, n), 0)),
            pl.BlockSpec((CONV_WIDTH, CONV_CH), const_map),
            pl.BlockSpec((1, CONV_CH), const_map),
            pl.BlockSpec((1, CONV_CH), const_map),
            pl.BlockSpec((1, CONV_CH), const_map),
            pl.BlockSpec((SUBLANES, LANES), const_map),
        ],
        out_specs=[
            pl.BlockSpec((tm, TN_GATES), lambda m, n: (m, n)),
            pl.BlockSpec((rows, CONV_CH), lambda m, n: (tile(m, n), 0)),
        ],
        out_shape=[
            jax.ShapeDtypeStruct((t, 2 * D_MODEL), BF16),
            jax.ShapeDtypeStruct((t, CONV_CH), BF16),
        ],
        scratch_shapes=[
            pltpu.VMEM((HIST_PAD + rows + SUBLANES, CONV_CH), F32),
            pltpu.VMEM((SUBLANES, HIST_PAD + rows, CONV_CH), F32),
        ],
        compiler_params=_params(2),
    )(xn, w_in, b_in, hist, u, *conv_consts, jnp.zeros((SUBLANES, LANES), jnp.int32))


def _in_proj(x_p, x_s, hist, norm_g, w_in, b_in, conv_consts):
    xn, q, kv = _qkv(x_p, x_s, norm_g, w_in, b_in)
    u = _glu(xn, w_in, b_in, TM_GLU)
    gates_p, c_p = _gates_conv(xn, w_in, b_in, u, u, conv_consts, TM_WIDE,
                               TM_WIDE // N_GATE_TILES, True)
    gates_s, c_s = _gates_conv(xn, w_in, b_in, hist, u, conv_consts, T_S,
                               T_S // N_GATE_TILES, False)
    return q, kv, u, gates_p, c_p, gates_s, c_s


def _mix_body(ap_ref, as_ref, cp_ref, cs_ref, gp_ref, gs_ref, xp_ref, xs_ref, wao_hbm, wco_hbm,
              wout_hbm, g2_ref, wr_ref, br_ref, x1_ref, xn_ref, lg_ref,
              wao_ref, wco_ref, wout_ref, stage, sem):
    m = pl.program_id(0)

    @pl.when(m == 0)
    def _():
        _stage_weights([(wao_hbm, wao_ref), (wco_hbm, wco_ref), (wout_hbm, wout_ref)], stage, sem)

    is_sample = m >= T_P // TM_MIX
    a = jnp.where(is_sample, as_ref[...], ap_ref[...])
    c = jnp.where(is_sample, cs_ref[...], cp_ref[...])
    gate = jnp.where(is_sample, gs_ref[...], gp_ref[...])
    x = jnp.where(is_sample, xs_ref[...], xp_ref[...])
    x1 = x
    for c0 in range(0, D_MODEL, MIX_COLS):
        cols = slice(c0, c0 + MIX_COLS)
        pa = jnp.dot(a, wao_ref[:, cols], preferred_element_type=F32)
        pc = jnp.dot(c, wco_ref[:, cols], preferred_element_type=F32)
        y = (gate[:, cols].astype(F32) * pa
             + gate[:, D_MODEL + c0:D_MODEL + c0 + MIX_COLS].astype(F32) * pc).astype(BF16)
        x1 = x1 + jnp.dot(y, wout_ref[cols, :], preferred_element_type=F32)
    x1_ref[...] = x1
    ms = jnp.mean(x1 * x1, axis=-1, keepdims=True)
    xn = x1 * lax.rsqrt(ms + EPS) * g2_ref[...]
    _store_packed_rows(xn_ref, xn)
    r = jnp.dot(xn.astype(BF16), wr_ref[...], preferred_element_type=F32)
    lg_ref[...] = r + pltpu.roll(r, ROUTER_PAD - N_ROUTER, axis=1) + br_ref[...]


def _mix(a_p, a_s, c_p, c_s, g_p, g_s, x_p, x_s, wao, wco, wout, g2, wr, br):
    n_p = T_P // TM_MIX
    n_s = T_S // TM_MIX
    prompt_map = lambda m: (jnp.minimum(m, n_p - 1), 0)
    sample_map = lambda m: (jnp.clip(m - n_p, 0, n_s - 1), 0)
    row_map = lambda m: (m, 0)
    const_map = lambda m: (0, 0)

    def pair(width):
        return [pl.BlockSpec((TM_MIX, width), prompt_map), pl.BlockSpec((TM_MIX, width), sample_map)]

    return pl.pallas_call(
        _mix_body,
        name="mix",
        grid=(T // TM_MIX,),
        in_specs=pair(ATTN_WIDTH) + pair(CONV_CH) + pair(2 * D_MODEL) + pair(D_MODEL) + [
            pl.BlockSpec(memory_space=pl.ANY),
            pl.BlockSpec(memory_space=pl.ANY),
            pl.BlockSpec(memory_space=pl.ANY),
            pl.BlockSpec((1, D_MODEL), const_map),
            pl.BlockSpec((D_MODEL, ROUTER_PAD), const_map),
            pl.BlockSpec((1, ROUTER_PAD), const_map),
        ],
        out_specs=[
            pl.BlockSpec((TM_MIX, D_MODEL), row_map),
            pl.BlockSpec((TM_MIX * ROW_TILE, LANES), row_map),
            pl.BlockSpec((TM_MIX, ROUTER_PAD), row_map),
        ],
        out_shape=[
            jax.ShapeDtypeStruct((T, D_MODEL), F32),
            jax.ShapeDtypeStruct((T * ROW_TILE, LANES), U32),
            jax.ShapeDtypeStruct((T, ROUTER_PAD), F32),
        ],
        scratch_shapes=[
            pltpu.VMEM((ATTN_WIDTH, D_MODEL), BF16),
            pltpu.VMEM((CONV_CH, D_MODEL), BF16),
            pltpu.VMEM((D_MODEL, D_MODEL), BF16),
            pltpu.VMEM((2, W_STAGE_ROWS, D_MODEL), F32),
            pltpu.SemaphoreType.DMA((2,)),
        ],
        compiler_params=_params(1),
    )(a_p, a_s, c_p, c_s, g_p, g_s, x_p, x_s, wao, wco, wout, g2, wr, br)


REC_W, REC_E, REC_POS = 0, 2, 4


PLAN_E, PLAN_SRC, PLAN_FIRST, PLAN_WSLOT, PLAN_NEXT_E, PLAN_ZERO, PLAN_VALID = range(7)
N_PLAN_ROWS = 7


def _write_plan(cnt, plan_ref, nu_ref):
    shift = BM_MOE.bit_length() - 1
    nblk = [(c + (BM_MOE - 1)) >> shift for c in cnt]
    nxt = [None] * N_EXPERTS
    later = jnp.int32(-1)
    for e in reversed(range(N_EXPERTS)):
        nxt[e] = jnp.where(later >= 0, later, e)
        later = jnp.where(nblk[e] > 0, e, later)

    def clear(j, carry):
        for row in range(N_PLAN_ROWS):
            plan_ref[row, j] = 0
        return carry
    lax.fori_loop(0, N_BLK_MOE, clear, 0)

    first_blocks = []
    j0 = jnp.int32(0)
    row0 = jnp.int32(0)
    n_before = jnp.int32(0)
    for e in range(N_EXPERTS):
        def fill(b, carry, e=e, j0=j0, row0=row0, n_before=n_before):
            plan_ref[PLAN_E, j0 + b] = e
            plan_ref[PLAN_SRC, j0 + b] = row0 + b * BM_MOE
            plan_ref[PLAN_FIRST, j0 + b] = jnp.where(b == 0, 1, 0)
            plan_ref[PLAN_WSLOT, j0 + b] = n_before & 1
            plan_ref[PLAN_NEXT_E, j0 + b] = nxt[e]
            plan_ref[PLAN_VALID, j0 + b] = jnp.minimum(cnt[e] - b * BM_MOE, BM_MOE)
            return carry
        lax.fori_loop(0, nblk[e], fill, 0)
        first_blocks.append(j0)
        j0 = j0 + nblk[e]
        row0 = row0 + cnt[e]
        n_before = n_before + jnp.where(nblk[e] > 0, 1, 0)
    nu_ref[0] = j0
    return first_blocks


def _router_body(lg_ref, rec_ref, idx_ref, plan_ref, nu_ref, carry_sc, rect_sc):
    n_chunks = T // R_ROUTE
    grow = lax.broadcasted_iota(jnp.int32, (N_GROUPS, R_ROUTE), 0).astype(F32)
    erow_i = lax.broadcasted_iota(jnp.int32, (N_EXPERTS, R_ROUTE), 0)
    erow = erow_i.astype(F32)
    egrp = (erow_i >> (EXPERTS_PER_GROUP.bit_length() - 1)).astype(F32)
    row8 = lax.broadcasted_iota(jnp.int32, (SUBLANES, R_ROUTE), 0)
    row128 = lax.broadcasted_iota(jnp.int32, (ROUTER_PAD, R_ROUTE), 0)
    upper = (lax.broadcasted_iota(jnp.int32, (R_ROUTE, R_ROUTE), 0)
             < lax.broadcasted_iota(jnp.int32, (R_ROUTE, R_ROUTE), 1)).astype(BF16)

    def rows_of(base, vals):
        out = jnp.zeros(base.shape, F32)
        for r, v in vals:
            out = jnp.where(base == r, v, out)
        return out

    carry_sc[...] = jnp.zeros_like(carry_sc)

    def count_pass(i, carry_unused):
        rows = pl.ds(pl.multiple_of(i * R_ROUTE, R_ROUTE), R_ROUTE)
        lgt = lg_ref[rows, :].T
        g = lgt[0:N_GROUPS]
        ex = lgt[N_GROUPS:N_ROUTER]
        gmax = jnp.max(g, axis=0, keepdims=True)
        p_grp = 1.0 / jnp.sum(jnp.exp(g - gmax), axis=0, keepdims=True)
        grp = jnp.min(jnp.where(g == gmax, grow, float(N_GROUPS)), axis=0, keepdims=True)
        emask = egrp == grp
        e1 = jnp.max(jnp.where(emask, ex, NEG), axis=0, keepdims=True)
        x1 = jnp.min(jnp.where(emask & (ex == e1), erow, float(N_EXPERTS)), axis=0, keepdims=True)
        emask2 = emask & (erow != x1)
        e2 = jnp.max(jnp.where(emask2, ex, NEG), axis=0, keepdims=True)
        x2 = jnp.min(jnp.where(emask2 & (ex == e2), erow, float(N_EXPERTS)), axis=0, keepdims=True)
        t = jnp.exp(e2 - e1)
        w1 = p_grp / (1.0 + t)
        w2 = p_grp * t / (1.0 + t)
        oh1 = erow == x1
        oh2 = erow == x2
        onehot = jnp.concatenate([oh1.astype(BF16), oh2.astype(BF16)], axis=0)
        carry = carry_sc[:, 0:1]
        before = jnp.dot(onehot, upper, preferred_element_type=F32) + carry
        rank1 = jnp.sum(jnp.where(oh1, before[0:N_EXPERTS], 0.0), axis=0, keepdims=True)
        rank2 = jnp.sum(jnp.where(oh2, before[N_EXPERTS:], 0.0), axis=0, keepdims=True)
        total = carry + jnp.sum(onehot.astype(F32), axis=1, keepdims=True)
        carry_sc[...] = jnp.broadcast_to(total, carry_sc.shape)
        rect_sc[i] = rows_of(row8, ((REC_E, x1), (REC_E + 1, x2), (REC_POS, rank1), (REC_POS + 1, rank2)))
        rec_ref[rows, :] = rows_of(row128, ((REC_W, w1), (REC_W + 1, w2))).T
        return carry_unused

    lax.fori_loop(0, n_chunks, count_pass, 0)

    tot = carry_sc[:, 0:1]
    tot1 = [tot[e, 0] for e in range(N_EXPERTS)]
    cnt = [(tot1[e] + tot[N_EXPERTS + e, 0]).astype(jnp.int32) for e in range(N_EXPERTS)]
    first_blocks = _write_plan(cnt, plan_ref, nu_ref)
    row0 = [(fb * BM_MOE).astype(F32) for fb in first_blocks]

    def place_pass(i, carry_unused):
        rec = rect_sc[i]
        x1 = rec[REC_E:REC_E + 1]
        x2 = rec[REC_E + 1:REC_E + 2]
        p1 = rec[REC_POS:REC_POS + 1]
        p2 = rec[REC_POS + 1:REC_POS + 2]
        for e in range(N_EXPERTS):
            p1 = p1 + jnp.where(x1 == e, row0[e], 0.0)
            p2 = p2 + jnp.where(x2 == e, row0[e] + tot1[e], 0.0)
        rec = jnp.where(row8 == REC_POS, p1 * ROW_TILE, rec)
        rec = jnp.where(row8 == REC_POS + 1, p2 * ROW_TILE, rec)
        idx_ref[i] = rec.astype(jnp.int32)
        return carry_unused

    lax.fori_loop(0, n_chunks, place_pass, 0)


def _router(logits):
    n_chunks = T // R_ROUTE
    return pl.pallas_call(
        _router_body,
        name="router",
        grid=(1,),
        in_specs=[pl.BlockSpec((T, ROUTER_PAD), lambda i: (0, 0))],
        out_specs=[
            pl.BlockSpec((T, ROUTER_PAD), lambda i: (0, 0)),
            pl.BlockSpec((n_chunks, SUBLANES, R_ROUTE), lambda i: (0, 0, 0)),
            pl.BlockSpec(memory_space=pltpu.SMEM),
            pl.BlockSpec(memory_space=pltpu.SMEM),
        ],
        out_shape=[
            jax.ShapeDtypeStruct((T, ROUTER_PAD), F32),
            jax.ShapeDtypeStruct((n_chunks, SUBLANES, R_ROUTE), jnp.int32),
            jax.ShapeDtypeStruct((N_PLAN_ROWS, N_BLK_MOE), jnp.int32),
            jax.ShapeDtypeStruct((1,), jnp.int32),
        ],
        scratch_shapes=[
            pltpu.VMEM((2 * N_EXPERTS, LANES), F32),
            pltpu.VMEM((n_chunks, SUBLANES, R_ROUTE), F32),
        ],
        compiler_params=_params(1),
    )(logits)


def _route(logits):
    rec, idx, plan, n_used = _router(logits)
    idx = idx.transpose(1, 0, 2).reshape(SUBLANES, T)
    eid = idx[REC_E:REC_E + TOP_K].reshape(-1)
    pos = idx[REC_POS:REC_POS + TOP_K].reshape(-1)
    order = jnp.argsort(eid, stable=True)
    tok_sorted = jnp.concatenate([((order % T) * ROW_TILE).astype(jnp.int32),
                                  jnp.zeros((BM_MOE,), jnp.int32)])
    return rec, pos, tok_sorted, plan, n_used


BLOCKS_PER_STEP = 2
N_STEPS_MOE = -(-(N_BLK_MOE + 1) // BLOCKS_PER_STEP)
GATHER_AHEAD = 2
X_SLOTS = GATHER_AHEAD + 1
WEIGHT_DMA_PRIORITY = 1
GATHER_DMA_PRIORITY = 0


def _moe_body(plan, n_used, tok, x_hbm, wg_hbm, wu_hbm, wd_hbm, o_ref,
              xb, wg_st, wu_st, wd_st, wg_bf, wu_bf, wd_bf, sem_x, sem_w):
    for h in range(BLOCKS_PER_STEP):
        rows = pl.ds(h * BM_MOE * ROW_TILE, BM_MOE * ROW_TILE)
        _moe_block(pl.program_id(0) * BLOCKS_PER_STEP + h, plan, n_used, tok, x_hbm, wg_hbm, wu_hbm,
                   wd_hbm, o_ref.at[rows], xb, wg_st, wu_st, wd_st, wg_bf, wu_bf, wd_bf, sem_x, sem_w)


def _moe_block(j, plan, n_used, tok, x_hbm, wg_hbm, wu_hbm, wd_hbm, o_ref,
               xb, wg_st, wu_st, wd_st, wg_bf, wu_bf, wd_bf, sem_x, sem_w):
    nu = n_used[0]

    def row_copy(src, slot, r):
        t8 = pl.multiple_of(tok[src + r], ROW_TILE)
        return pltpu.make_async_copy(x_hbm.at[pl.ds(t8, ROW_TILE)], xb.at[slot, _row_tile(r)],
                                     sem_x.at[slot])

    def block_wait(slot):
        pltpu.make_async_copy(x_hbm.at[pl.ds(0, BM_MOE * ROW_TILE)], xb.at[slot],
                              sem_x.at[slot]).wait()

    def weight_copies(e, ws):
        return (pltpu.make_async_copy(wg_hbm.at[e], wg_st.at[ws], sem_w.at[ws, 0]),
                pltpu.make_async_copy(wu_hbm.at[e], wu_st.at[ws], sem_w.at[ws, 1]),
                pltpu.make_async_copy(wd_hbm.at[e], wd_st.at[ws], sem_w.at[ws, 2]))

    @pl.when(j == 0)
    def _():
        for cp in weight_copies(plan[PLAN_E, 0], 0):
            cp.start(priority=WEIGHT_DMA_PRIORITY)
        for b in range(GATHER_AHEAD):
            src = plan[PLAN_SRC, b]

            def body(r, carry, src=src, b=b):
                row_copy(src, b, r).start(priority=GATHER_DMA_PRIORITY)
                return carry
            lax.fori_loop(0, BM_MOE, body, 0, unroll=8)

    @pl.when((j < nu) & (plan[PLAN_FIRST, jnp.minimum(j, N_BLK_MOE - 1)] == 1))
    def _():
        e = plan[PLAN_E, j]
        ws = plan[PLAN_WSLOT, j]
        e_next = plan[PLAN_NEXT_E, j]

        @pl.when(e_next != e)
        def _():
            for cp in weight_copies(e_next, 1 - ws):
                cp.start(priority=WEIGHT_DMA_PRIORITY)

        for cp in weight_copies(e, ws):
            cp.wait()
        wg_bf[...] = wg_st[ws].astype(BF16)
        wu_bf[...] = wu_st[ws].astype(BF16)
        wd_bf[...] = wd_st[ws].astype(BF16)

    def compute(n_rows):
        slot = j % X_SLOTS
        next_slot = (j + GATHER_AHEAD) % X_SLOTS
        block_wait(slot)
        nsrc = plan[PLAN_SRC, jnp.minimum(j + GATHER_AHEAD, N_BLK_MOE - 1)]
        halves = [_load_packed_rows(xb.at[slot], n_rows, s) for s in range(ROW_TILE)]
        x = jnp.concatenate([lo for lo, _ in halves] + [hi for _, hi in halves], axis=1).astype(BF16)
        zero = plan[PLAN_ZERO, 0]
        group = BM_MOE // 8
        issued = [0]

        def gather_after(v):
            base = nsrc
            if v is not None:
                base = base + (lax.bitcast_convert_type(v[0:1, 0:1], jnp.int32)[0, 0] & zero)
            for r in range(issued[0], issued[0] + group):
                row_copy(base, next_slot, r).start(priority=GATHER_DMA_PRIORITY)
            issued[0] += group

        gather_after(None)
        gather_after(halves[ROW_TILE - 1][1])
        gu = []
        for c in range(D_EXPERT // MXU_N):
            cols = slice(c * MXU_N, (c + 1) * MXU_N)
            g = jnp.dot(x, wg_bf[:, cols], preferred_element_type=F32)
            gather_after(g)
            u = jnp.dot(x, wu_bf[:, cols], preferred_element_type=F32)
            gather_after(u)
            gu.append((g, u))
        acc = None
        for c, (g, u) in enumerate(gu):
            cols = slice(c * MXU_N, (c + 1) * MXU_N)
            h = (g * _sigmoid(g) * u).astype(BF16)
            part = jnp.dot(h, wd_bf[cols, :], preferred_element_type=F32)
            if c == 0:
                gather_after(part)
                gather_after(part[:, D_MODEL - LANES:])
            acc = part if acc is None else acc + part
        assert issued[0] == BM_MOE
        _store_packed_rows(o_ref, acc)
        if n_rows < BM_MOE:
            idle = pl.ds(n_rows * ROW_TILE, (BM_MOE - n_rows) * ROW_TILE)
            o_ref[idle, :] = jnp.zeros(((BM_MOE - n_rows) * ROW_TILE, LANES), U32)

    valid = plan[PLAN_VALID, jnp.minimum(j, N_BLK_MOE - 1)]

    @pl.when((j < nu) & (valid > BM_MOE // 2))
    def _():
        compute(BM_MOE)

    @pl.when((j < nu) & (valid <= BM_MOE // 2))
    def _():
        compute(BM_MOE // 2)

    @pl.when(j == nu)
    def _():
        for b in range(GATHER_AHEAD):
            block_wait((j + b) % X_SLOTS)

    @pl.when(j >= nu)
    def _():
        o_ref[...] = jnp.zeros_like(o_ref)


def _moe(plan, n_used, tok_sorted, xn_packed, w_g, w_u, w_d):
    grid_spec = pltpu.PrefetchScalarGridSpec(
        num_scalar_prefetch=3,
        grid=(N_STEPS_MOE,),
        in_specs=[pl.BlockSpec(memory_space=pl.ANY)] * 4,
        out_specs=pl.BlockSpec((BLOCKS_PER_STEP * BM_MOE * ROW_TILE, LANES),
                               lambda j, pn, nu, tk: (j, 0)),
        scratch_shapes=[
            pltpu.VMEM((X_SLOTS, BM_MOE * ROW_TILE, LANES), U32),
            pltpu.VMEM((2, D_MODEL, D_EXPERT), F32),
            pltpu.VMEM((2, D_MODEL, D_EXPERT), F32),
            pltpu.VMEM((2, D_EXPERT, D_MODEL), F32),
            pltpu.VMEM((D_MODEL, D_EXPERT), BF16),
            pltpu.VMEM((D_MODEL, D_EXPERT), BF16),
            pltpu.VMEM((D_EXPERT, D_MODEL), BF16),
            pltpu.SemaphoreType.DMA((X_SLOTS,)),
            pltpu.SemaphoreType.DMA((2, 3)),
        ],
    )
    return pl.pallas_call(
        _moe_body,
        name="experts",
        grid_spec=grid_spec,
        out_shape=jax.ShapeDtypeStruct(
            (N_STEPS_MOE * BLOCKS_PER_STEP * BM_MOE * ROW_TILE, LANES), U32),
        compiler_params=_params(1),
    )(plan, n_used, tok_sorted, xn_packed, w_g, w_u, w_d)


def _combine_body(pos, x1_ref, w_ref, gf_ref, o_hbm, y_ref, rb, sem, *, first_tile):
    m = pl.program_id(0)
    last = pl.num_programs(0) - 1

    def row_copy(base, slot, r, k):
        p8 = pl.multiple_of(pos[k * T + base + r], ROW_TILE)
        return pltpu.make_async_copy(o_hbm.at[pl.ds(p8, ROW_TILE)], rb.at[slot, k, _row_tile(r)],
                                     sem.at[slot])

    def tile_wait(slot):
        for k in range(TOP_K):
            pltpu.make_async_copy(o_hbm.at[pl.ds(0, TM_OUT * ROW_TILE)], rb.at[slot, k],
                                  sem.at[slot]).wait()

    def tile_base(mm):
        return (first_tile + mm) * TM_OUT

    @pl.when(m == 0)
    def _():
        for b in range(GATHER_AHEAD):
            base = tile_base(jnp.minimum(b, last))

            def body(r, carry, base=base, b=b):
                for k in range(TOP_K):
                    row_copy(base, b, r, k).start(priority=k)
                return carry
            lax.fori_loop(0, TM_OUT, body, 0, unroll=4)

    slot = m % X_SLOTS
    next_slot = (m + GATHER_AHEAD) % X_SLOTS
    tile_wait(slot)
    nbase = tile_base(jnp.minimum(m + GATHER_AHEAD, last))
    rows = TM_OUT // ROW_TILE
    w0 = w_ref[:, 0:1]
    w1 = w_ref[:, 1:2]
    ss = jnp.zeros((TM_OUT, 1), F32)
    for s in range(ROW_TILE):
        for r in range(s * rows, (s + 1) * rows):
            for k in range(TOP_K):
                row_copy(nbase, next_slot, r, k).start(priority=k)
        lo0, hi0 = _load_packed_rows(rb.at[slot, 0], TM_OUT, s)
        lo1, hi1 = _load_packed_rows(rb.at[slot, 1], TM_OUT, s)
        for off, r0, r1 in ((s * LANES, lo0, lo1), (HALF_D + s * LANES, hi0, hi1)):
            y = x1_ref[:, off:off + LANES] + w0 * r0 + w1 * r1
            ss = ss + jnp.sum(y * y, axis=-1, keepdims=True)
            y_ref[:, off:off + LANES] = y
    scale = lax.rsqrt(ss * (1.0 / D_MODEL) + EPS)
    y_ref[...] = y_ref[...] * scale * gf_ref[...]

    @pl.when(m == last)
    def _():
        for b in range(1, GATHER_AHEAD + 1):
            tile_wait((m + b) % X_SLOTS)


def _combine(pos, x1, wts, gf, out_sorted, first_tile, n_tiles):
    grid_spec = pltpu.PrefetchScalarGridSpec(
        num_scalar_prefetch=1,
        grid=(n_tiles,),
        in_specs=[
            pl.BlockSpec((TM_OUT, D_MODEL), lambda m, p: (first_tile + m, 0)),
            pl.BlockSpec((TM_OUT, ROUTER_PAD), lambda m, p: (first_tile + m, 0)),
            pl.BlockSpec((1, D_MODEL), lambda m, p: (0, 0)),
            pl.BlockSpec(memory_space=pl.ANY),
        ],
        out_specs=pl.BlockSpec((TM_OUT, D_MODEL), lambda m, p: (m, 0)),
        scratch_shapes=[
            pltpu.VMEM((X_SLOTS, TOP_K, TM_OUT * ROW_TILE, LANES), U32),
            pltpu.SemaphoreType.DMA((X_SLOTS,)),
        ],
    )
    return pl.pallas_call(
        functools.partial(_combine_body, first_tile=first_tile),
        name="combine",
        grid_spec=grid_spec,
        out_shape=jax.ShapeDtypeStruct((n_tiles * TM_OUT, D_MODEL), F32),
        compiler_params=_params(1),
    )(pos, x1, wts, gf, out_sorted)


def kernel(x_prompt, x_sample, cache_k, cache_v, state_conv, norm1_g, w_in, b_in, attn_sinks,
           w_attn_o, conv_dw, conv_dw_b, conv_ln_g, conv_ln_b, w_conv_o, w_out, norm2_g,
           w_router_group, b_router_group, w_router_expert, b_router_expert, w_e_gate, w_e_up,
           w_e_down, final_norm_g):
    x_p = x_prompt.reshape(T_P, D_MODEL)
    x_s = x_sample.reshape(T_S, D_MODEL)
    g1 = norm1_g[0][None, :]
    b1 = b_in[0][None, :]
    conv_consts = (conv_dw[0], conv_dw_b[0][None, :], conv_ln_g[0][None, :], conv_ln_b[0][None, :])
    hist_pad = jnp.pad(state_conv[0], ((0, 0), (HIST_PAD - HIST, 0), (0, 0)))
    q, kv, u, gates_p, c_p, gates_s, c_s = _in_proj(x_p, x_s, hist_pad, g1, w_in[0], b1, conv_consts)

    sinks = attn_sinks[0]
    a_p = _attn_prompt(sinks, q, kv)
    ck = cache_k[0].reshape(N_STREAMS, WINDOW, KV_WIDTH)
    cv = cache_v[0].reshape(N_STREAMS, WINDOW, KV_WIDTH)
    a_s = _attn_sample(sinks, q, kv, ck, cv)

    w_r = jnp.concatenate([w_router_group[0], w_router_expert[0]], axis=1)
    w_r_hi = w_r.astype(BF16)
    w_r_lo = (w_r - w_r_hi.astype(F32)).astype(BF16)
    w_r_cat = jnp.concatenate(
        [w_r_hi, w_r_lo, jnp.zeros((D_MODEL, ROUTER_PAD - 2 * N_ROUTER), BF16)], axis=1)
    b_r = jnp.concatenate([b_router_group[0], b_router_expert[0],
                           jnp.zeros((ROUTER_PAD - N_ROUTER,), F32)])[None, :]

    x1, xn2, logits = _mix(a_p, a_s, c_p, c_s, gates_p, gates_s, x_p, x_s,
                           w_attn_o[0], w_conv_o[0], w_out[0],
                           norm2_g[0][None, :], w_r_cat, b_r)

    wts, pos, tok_sorted, plan, n_used = _route(logits)
    out_sorted = _moe(plan, n_used, tok_sorted, xn2, w_e_gate[0], w_e_up[0], w_e_down[0])
    gf = final_norm_g[None, :]
    y_p = _combine(pos, x1, wts, gf, out_sorted, 0, T_P // TM_OUT)
    y_s = _combine(pos, x1, wts, gf, out_sorted, T_P // TM_OUT, T_S // TM_OUT)

    kv_shape = (1, -1, WINDOW, N_KV_HEADS, HEAD_DIM)
    new_k_prompt = kv[T_P - WINDOW:T_P, :KV_WIDTH].reshape(kv_shape)
    new_v_prompt = kv[T_P - WINDOW:T_P, KV_WIDTH:].reshape(kv_shape)
    new_conv_prompt = u[T_P - HIST:T_P].reshape(1, 1, HIST, CONV_CH)
    k_s = kv[T_P:, :KV_WIDTH].reshape(N_STREAMS, L_S, KV_WIDTH)
    v_s = kv[T_P:, KV_WIDTH:].reshape(N_STREAMS, L_S, KV_WIDTH)
    new_k_sample = jnp.concatenate([ck[:, L_S:], k_s], axis=1).reshape(kv_shape)
    new_v_sample = jnp.concatenate([cv[:, L_S:], v_s], axis=1).reshape(kv_shape)
    new_conv_sample = u[T_P:].reshape(N_STREAMS, L_S, CONV_CH)[:, L_S - HIST:].reshape(
        1, N_STREAMS, HIST, CONV_CH)

    return (y_p.reshape(1, T_P, D_MODEL), y_s.reshape(N_STREAMS, L_S, D_MODEL),
            new_k_prompt, new_v_prompt, new_conv_prompt,
            new_k_sample, new_v_sample, new_conv_sample)
```
